```python
import math
import jax, jax.numpy as jnp
from jax import lax
import numpy as np

D_MODEL = 2048
BATCH = 4
SEQ = 4096
DEPTH = 1

CTX_LEN = 256
GRID_W = 64
D_MIX = D_MODEL
CM_WIDTH = D_MIX // 2
CM_HEADS = 8
CM_HEAD_DIM = CM_WIDTH // CM_HEADS
CM_CHUNK = 128
CM_CHUNK_ROWS = CM_CHUNK // GRID_W
GLA_WIDTH = D_MIX - CM_WIDTH
GLA_HEADS = 4
GLA_DV = GLA_WIDTH // GLA_HEADS
GLA_DK = GLA_DV // 2
GLA_KEY_WIDTH = GLA_HEADS * GLA_DK
GLA_GATE_RANK = 16
GLA_GATE_NORMALIZER = 16.0
GLA_CHUNK = 64
N_EXPERTS = 32
TOP_K = 4
D_EXPERT = D_MODEL
SWIGLU_LIMIT = 7.0
SWIGLU_ALPHA = 1.702
MOE_BLOCK = 256
N_MOD = 6
DEEPNORM_ALPHA = (2 * DEPTH) ** 0.25
DEEPNORM_BETA = (8 * DEPTH) ** -0.25
LN_EPS = 1e-5
RMS_EPS = 1e-6
D_IN = 2 * CM_WIDTH + 2 * GLA_KEY_WIDTH + 2 * GLA_WIDTH + 2 * GLA_GATE_RANK
IN_SPLITS = (CM_WIDTH,
             2 * CM_WIDTH,
             2 * CM_WIDTH + GLA_KEY_WIDTH,
             2 * CM_WIDTH + 2 * GLA_KEY_WIDTH,
             2 * CM_WIDTH + 2 * GLA_KEY_WIDTH + GLA_WIDTH,
             2 * CM_WIDTH + 2 * GLA_KEY_WIDTH + 2 * GLA_WIDTH,
             2 * CM_WIDTH + 2 * GLA_KEY_WIDTH + 2 * GLA_WIDTH + GLA_GATE_RANK)

kernel_name = 'hybrid_chunkmlp_gla_moe_prefix'


def layer_norm(t, g, b):
    tf = t.astype(jnp.float32)
    mu = jnp.mean(tf, axis=-1, keepdims=True)
    var = jnp.mean(jnp.square(tf - mu), axis=-1, keepdims=True)
    return ((tf - mu) * lax.rsqrt(var + LN_EPS)).astype(t.dtype) * g + b


def modulate(t, shift, scale):
    return t * (1 + scale) + shift


def post_norm(h, y, g, b):
    return layer_norm(DEEPNORM_ALPHA * h + y, g, b)


def split_heads(t, n_heads):
    b, l, _ = t.shape
    return t.reshape(b, l, n_heads, -1).transpose(0, 2, 1, 3)


def flip_seq(t):
    return jnp.flip(t, axis=2)


def chunk_spatial_gate(u, v, n_chunks, norm_g, norm_b, w_s, b_s):
    b, l, _ = v.shape
    v = layer_norm(v, norm_g, norm_b).reshape(b, n_chunks, CM_CHUNK, CM_HEADS, CM_HEAD_DIM)
    s = jnp.einsum('hpq,bnqhe->bnphe', w_s, v) + b_s.T[:, :, None]
    return u * s.reshape(b, l, CM_WIDTH)


def gla_inputs(qp, kp, vp, lr_f, lr_b, w_gk_f, b_gk_f, w_gk_b, b_gk_b):
    q = split_heads(qp, GLA_HEADS) * (GLA_DK ** -0.5)
    k = split_heads(kp, GLA_HEADS)
    v = split_heads(vp, GLA_HEADS)
    g_f = jax.nn.log_sigmoid((lr_f @ w_gk_f + b_gk_f).astype(jnp.float32)) / GLA_GATE_NORMALIZER
    g_b = jax.nn.log_sigmoid((lr_b @ w_gk_b + b_gk_b).astype(jnp.float32)) / GLA_GATE_NORMALIZER
    return q, k, v, split_heads(g_f, GLA_HEADS), split_heads(g_b, GLA_HEADS)


def gla_chunked(q, k, v, g, s0):
    bsz, nh, l, dk = q.shape
    dv = v.shape[-1]
    n = l // GLA_CHUNK
    q = q.reshape(bsz, nh, n, GLA_CHUNK, dk).astype(jnp.float32)
    k = k.reshape(bsz, nh, n, GLA_CHUNK, dk).astype(jnp.float32)
    v = v.reshape(bsz, nh, n, GLA_CHUNK, dv).astype(jnp.float32)
    bcum = jnp.cumsum(g.reshape(bsz, nh, n, GLA_CHUNK, dk), axis=3)
    b_last = bcum[:, :, :, -1:, :]
    qe = q * jnp.exp(bcum)
    ke = k * jnp.exp(-bcum)
    kd = k * jnp.exp(b_last - bcum)
    mask = jnp.tril(jnp.ones((GLA_CHUNK, GLA_CHUNK), dtype=bool))
    att = jnp.where(mask, jnp.einsum('bhntd,bhnsd->bhnts', qe, ke), 0.0)
    o_intra = jnp.einsum('bhnts,bhnse->bhnte', att, v)
    u = jnp.einsum('bhnsd,bhnse->bhnde', kd, v)
    decay = jnp.exp(b_last[:, :, :, 0, :])

    def step(s, inp):
        d, uu = inp
        return d[..., None] * s + uu, s

    s_fin, s_prev = lax.scan(step, s0, (jnp.moveaxis(decay, 2, 0), jnp.moveaxis(u, 2, 0)))
    s_prev = jnp.moveaxis(s_prev, 0, 2)
    o = o_intra + jnp.einsum('bhntd,bhnde->bhnte', qe, s_prev)
    return o.reshape(bsz, nh, l, dv), s_fin


def gla_bidirectional(q, k, v, g_f, g_b, s0_f, s0_b):
    o_f, s_f = gla_chunked(q, k, v, g_f, s0_f)
    o_b, s_b = gla_chunked(flip_seq(q), flip_seq(k), flip_seq(v), flip_seq(g_b), s0_b)
    return o_f + flip_seq(o_b), s_f, s_b


def gla_output(o, gate, norm_g):
    o = o * lax.rsqrt(jnp.mean(jnp.square(o), axis=-1, keepdims=True) + RMS_EPS) * norm_g
    b, nh, l, dv = o.shape
    o = o.transpose(0, 2, 1, 3).reshape(b, l, nh * dv)
    return (o * jax.nn.silu(gate.astype(jnp.float32))).astype(gate.dtype)


def moe(h, w_r, b_r, w_gu, b_gu, w_dn, b_dn):
    n, d = h.shape
    logits = (h @ w_r + b_r).astype(jnp.float32)
    top_val, top_idx = lax.top_k(logits, TOP_K)
    top_w = jax.nn.softmax(top_val, axis=-1)
    m = n * TOP_K
    e_flat = top_idx.reshape(-1).astype(jnp.int32)
    w_flat = top_w.reshape(-1)
    t_flat = jnp.arange(m, dtype=jnp.int32) // TOP_K
    order = jnp.argsort(e_flat)
    e_sorted = e_flat[order]
    counts = jnp.bincount(e_flat, length=N_EXPERTS).astype(jnp.int32)
    starts = jnp.cumsum(counts) - counts
    padded = (counts + MOE_BLOCK - 1) // MOE_BLOCK * MOE_BLOCK
    pad_ends = jnp.cumsum(padded)
    pad_starts = pad_ends - padded
    dest = pad_starts[e_sorted] + jnp.arange(m, dtype=jnp.int32) - starts[e_sorted]
    n_blocks = -(-(m + N_EXPERTS * (MOE_BLOCK - 1)) // MOE_BLOCK)
    n_slots = n_blocks * MOE_BLOCK
    slot_tok = jnp.zeros((n_slots,), jnp.int32).at[dest].set(t_flat[order])
    slot_w = jnp.zeros((n_slots,), jnp.float32).at[dest].set(w_flat[order])
    blk_start = jnp.arange(n_blocks, dtype=jnp.int32) * MOE_BLOCK
    blk_exp = jnp.minimum(jnp.searchsorted(pad_ends, blk_start, side='right'), N_EXPERTS - 1)

    def expert_block(args):
        tok, e = args
        xb = h[tok]
        gu = xb @ w_gu[e] + b_gu[e]
        gate, up = jnp.split(gu, 2, axis=-1)
        gate = jnp.minimum(gate, SWIGLU_LIMIT)
        up = jnp.clip(up, -SWIGLU_LIMIT, SWIGLU_LIMIT)
        act = (up + 1) * gate * jax.nn.sigmoid(SWIGLU_ALPHA * gate)
        return act @ w_dn[e] + b_dn[e]

    yb = lax.map(expert_block, (slot_tok.reshape(n_blocks, MOE_BLOCK), blk_exp))
    contrib = yb.reshape(n_slots, d).astype(jnp.float32) * slot_w[:, None]
    y = jnp.zeros((n, d), jnp.float32).at[slot_tok].add(contrib)
    return y.astype(h.dtype)


def setup_inputs(seed: int = 0) -> dict:
    key = jax.random.key(seed)
    ks = jax.random.split(key, 32)

    def nrm(k, shape, scale):
        return jax.random.normal(k, shape, jnp.float32) * scale

    return {
        'x': nrm(ks[0], (BATCH, SEQ, D_MODEL), 1.0),
        'c': nrm(ks[1], (BATCH, D_MODEL), 1.0),
        'ctx': nrm(ks[2], (BATCH, CTX_LEN, D_MODEL), 1.0),
        'c_ctx': nrm(ks[3], (D_MODEL,), 1.0),
        'ln_in_g': 1.0 + nrm(ks[4], (D_MODEL,), 0.01),
        'ln_in_b': nrm(ks[5], (D_MODEL,), 0.01),
        'w_ada': nrm(ks[6], (DEPTH, D_MODEL, N_MOD * D_MODEL), D_MODEL ** -0.5),
        'b_ada': nrm(ks[7], (DEPTH, N_MOD * D_MODEL), 0.01),
        'w_in': nrm(ks[8], (DEPTH, D_MODEL, D_IN), D_MODEL ** -0.5),
        'cm_norm_g': 1.0 + nrm(ks[9], (DEPTH, CM_WIDTH), 0.01),
        'cm_norm_b': nrm(ks[10], (DEPTH, CM_WIDTH), 0.01),
        'cm_w_s': nrm(ks[11], (DEPTH, CM_HEADS, CM_CHUNK, CM_CHUNK), CM_CHUNK ** -0.5),
        'cm_b_s': 1.0 + nrm(ks[12], (DEPTH, CM_HEADS, CM_CHUNK), 0.01),
        'gla_w_gk_f': nrm(ks[13], (DEPTH, GLA_GATE_RANK, GLA_KEY_WIDTH), GLA_GATE_RANK ** -0.5),
        'gla_b_gk_f': nrm(ks[14], (DEPTH, GLA_KEY_WIDTH), 0.1),
        'gla_w_gk_b': nrm(ks[15], (DEPTH, GLA_GATE_RANK, GLA_KEY_WIDTH), GLA_GATE_RANK ** -0.5),
        'gla_b_gk_b': nrm(ks[16], (DEPTH, GLA_KEY_WIDTH), 0.1),
        'gla_norm_g': 1.0 + nrm(ks[17], (DEPTH, GLA_DV), 0.01),
        'w_out': nrm(ks[18], (DEPTH, D_MIX, D_MODEL), D_MIX ** -0.5 * DEEPNORM_BETA),
        'ln1_g': 1.0 + nrm(ks[19], (DEPTH, D_MODEL), 0.01),
        'ln1_b': nrm(ks[20], (DEPTH, D_MODEL), 0.01),
        'w_router': nrm(ks[21], (DEPTH, D_MODEL, N_EXPERTS), D_MODEL ** -0.5),
        'b_router': nrm(ks[22], (DEPTH, N_EXPERTS), 0.01),
        'w_gate_up': nrm(ks[23], (DEPTH, N_EXPERTS, D_MODEL, 2 * D_EXPERT), D_MODEL ** -0.5),
        'b_gate_up': nrm(ks[24], (DEPTH, N_EXPERTS, 2 * D_EXPERT), 0.01),
        'w_down': nrm(ks[25], (DEPTH, N_EXPERTS, D_EXPERT, D_MODEL), D_EXPERT ** -0.5 * DEEPNORM_BETA),
        'b_down': nrm(ks[26], (DEPTH, N_EXPERTS, D_MODEL), 0.01),
        'ln2_g': 1.0 + nrm(ks[27], (DEPTH, D_MODEL), 0.01),
        'ln2_b': nrm(ks[28], (DEPTH, D_MODEL), 0.01),
    }


def reference(x, c, ctx, c_ctx, ln_in_g, ln_in_b, w_ada, b_ada, w_in, cm_norm_g, cm_norm_b,
              cm_w_s, cm_b_s, gla_w_gk_f, gla_b_gk_f, gla_w_gk_b, gla_b_gk_b, gla_norm_g,
              w_out, ln1_g, ln1_b, w_router, b_router, w_gate_up, b_gate_up, w_down, b_down,
              ln2_g, ln2_b):
    bsz, n_lat, d = x.shape
    rows = n_lat // GRID_W
    n_chunks_x = rows // CM_CHUNK_ROWS
    n_chunks_c = ctx.shape[1] // CM_CHUNK
    h_x = layer_norm(x, ln_in_g, ln_in_b)
    h_c = layer_norm(ctx, ln_in_g, ln_in_b)
    s_zero = jnp.zeros((bsz, GLA_HEADS, GLA_DK, GLA_DV), jnp.float32)

    for l in range(DEPTH):
        last = l == DEPTH - 1
        mod_x = jax.nn.silu(c) @ w_ada[l] + b_ada[l]
        mod_c = jax.nn.silu(c_ctx) @ w_ada[l] + b_ada[l]
        sh1x, sc1x, g1x, sh2x, sc2x, g2x = jnp.split(mod_x[:, None, :], N_MOD, axis=-1)
        sh1c, sc1c, g1c, sh2c, sc2c, g2c = jnp.split(mod_c, N_MOD, axis=-1)

        px = modulate(h_x, sh1x, sc1x) @ w_in[l]
        pc = modulate(h_c, sh1c, sc1c) @ w_in[l]
        ux, vx, qx, kx, vvx, gox, lfx, lbx = jnp.split(px, IN_SPLITS, axis=-1)
        uc, vc, qc, kc, vvc, goc, lfc, lbc = jnp.split(pc, IN_SPLITS, axis=-1)

        cm_x = chunk_spatial_gate(jax.nn.gelu(ux, approximate=False), jax.nn.gelu(vx, approximate=False),
                                  n_chunks_x, cm_norm_g[l], cm_norm_b[l], cm_w_s[l], cm_b_s[l])

        q_c, k_c, v_c, gf_c, gb_c = gla_inputs(qc, kc, vvc, lfc, lbc, gla_w_gk_f[l], gla_b_gk_f[l],
                                               gla_w_gk_b[l], gla_b_gk_b[l])
        q_x, k_x, v_x, gf_x, gb_x = gla_inputs(qx, kx, vvx, lfx, lbx, gla_w_gk_f[l], gla_b_gk_f[l],
                                               gla_w_gk_b[l], gla_b_gk_b[l])
        o_c, s_cf, s_cb = gla_bidirectional(q_c, k_c, v_c, gf_c, gb_c, s_zero, s_zero)
        o_x, _, _ = gla_bidirectional(q_x, k_x, v_x, gf_x, gb_x, s_cf, s_cb)
        gla_x = gla_output(o_x, gox, gla_norm_g[l])

        y_x = jnp.concatenate([cm_x, gla_x], axis=-1) @ w_out[l]
        h_x = post_norm(h_x, g1x * y_x, ln1_g[l], ln1_b[l])
        f_x = moe(modulate(h_x, sh2x, sc2x).reshape(-1, d), w_router[l], b_router[l],
                  w_gate_up[l], b_gate_up[l], w_down[l], b_down[l]).reshape(bsz, n_lat, d)
        h_x = post_norm(h_x, g2x * f_x, ln2_g[l], ln2_b[l])

        if not last:
            cm_c = chunk_spatial_gate(jax.nn.gelu(uc, approximate=False), jax.nn.gelu(vc, approximate=False),
                                      n_chunks_c, cm_norm_g[l], cm_norm_b[l], cm_w_s[l], cm_b_s[l])
            gla_c = gla_output(o_c, goc, gla_norm_g[l])
            y_c = jnp.concatenate([cm_c, gla_c], axis=-1) @ w_out[l]
            h_c = post_norm(h_c, g1c * y_c, ln1_g[l], ln1_b[l])
            f_c = moe(modulate(h_c, sh2c, sc2c).reshape(-1, d), w_router[l], b_router[l],
                      w_gate_up[l], b_gate_up[l], w_down[l], b_down[l]).reshape(h_c.shape)
            h_c = post_norm(h_c, g2c * f_c, ln2_g[l], ln2_b[l])

    return h_x
```

```python
import math

import jax
import jax.numpy as jnp
from jax import lax
from jax.experimental import pallas as pl
from jax.experimental.pallas import tpu as pltpu

F32 = jnp.float32
BF16 = jnp.bfloat16
U32 = jnp.uint32
I32 = jnp.int32

CM_CHUNK = 128
CM_HEADS = 8
GLA_HEADS = 4
GLA_DK = 128
GLA_DV = 256
GLA_CHUNK = 64
GLA_RANK = 16
GLA_GATE_NORMALIZER = 16.0
N_EXPERTS = 32
TOP_K = 4
MOE_BLOCK = 256
SWIGLU_LIMIT = 7.0
SWIGLU_ALPHA = 1.702
N_MOD = 6
DEEPNORM_ALPHA = 2.0 ** 0.25
LN_EPS = 1e-5
RMS_EPS = 1e-6

LANES = 128
VMEM_LIMIT = 56 * 1024 * 1024


def _cparams(n_axes, vmem=VMEM_LIMIT):
    return pltpu.CompilerParams(dimension_semantics=("arbitrary",) * n_axes,
                                vmem_limit_bytes=vmem)


def _layer_norm(t, g, b):
    mu = jnp.mean(t, axis=-1, keepdims=True)
    d = t - mu
    var = jnp.mean(d * d, axis=-1, keepdims=True)
    return d * lax.rsqrt(var + LN_EPS) * g + b


def _gelu(t):
    return 0.5 * t * (1.0 + lax.erf(t * (1.0 / math.sqrt(2.0))))


def _silu(t):
    return t * jax.nn.sigmoid(t)


def _ones_where(mask, dtype):
    return jnp.where(mask, 1.0, 0.0).astype(dtype)


def _dot(a, b):
    return jnp.dot(a, b, preferred_element_type=F32)


def _dot_nt(a, b):
    return lax.dot_general(a, b, (((1,), (1,)), ((), ())), preferred_element_type=F32)


def _dot_tn(a, b):
    return lax.dot_general(a, b, (((0,), (0,)), ((), ())), preferred_element_type=F32)


def _ada_body(c_ref, w_ref, b_ref, o_ref):
    a = _silu(c_ref[...]).astype(BF16)
    o_ref[...] = _dot(a, w_ref[...].astype(BF16)) + b_ref[...]


def _ada(cc, w, b):
    rows, d = cc.shape
    n = w.shape[1]
    tn = 1024
    return pl.pallas_call(
        _ada_body,
        grid=(n // tn,),
        in_specs=[pl.BlockSpec((rows, d), lambda j: (0, 0)),
                  pl.BlockSpec((d, tn), lambda j: (0, j)),
                  pl.BlockSpec((1, tn), lambda j: (0, j))],
        out_specs=pl.BlockSpec((rows, tn), lambda j: (0, j)),
        out_shape=jax.ShapeDtypeStruct((rows, n), F32),
        compiler_params=_cparams(1),
        name="ada",
    )(cc, w, b)


def _inproj_body(x_ref, g_ref, b_ref, sh_ref, sc_ref, w_ref, wlr_ref, o_ref, olr_ref, hm_ref):
    @pl.when(pl.program_id(1) == 0)
    def _():
        h = _layer_norm(x_ref[...], g_ref[...], b_ref[...])
        hm = (h * (1.0 + sc_ref[...]) + sh_ref[...]).astype(BF16)
        hm_ref[...] = hm
        olr_ref[...] = _dot(hm, wlr_ref[...])

    o_ref[...] = _dot(hm_ref[...], w_ref[...])


def _inproj(x2, mod3, mod_row, ln_g, ln_b, w_main, w_lr, col0, ncols, tm):
    r, d = x2.shape
    tn = 1024
    return pl.pallas_call(
        _inproj_body,
        grid=(r // tm, ncols),
        in_specs=[pl.BlockSpec((tm, d), lambda i, j: (i, 0)),
                  pl.BlockSpec((1, d), lambda i, j: (0, 0)),
                  pl.BlockSpec((1, d), lambda i, j: (0, 0)),
                  pl.BlockSpec((None, 1, d), lambda i, j: (mod_row(i), 0, 0)),
                  pl.BlockSpec((None, 1, d), lambda i, j: (mod_row(i), 0, 1)),
                  pl.BlockSpec((d, tn), lambda i, j: (0, col0 + j)),
                  pl.BlockSpec((d, LANES), lambda i, j: (0, 0))],
        out_specs=[pl.BlockSpec((tm, tn), lambda i, j: (i, j)),
                   pl.BlockSpec((tm, LANES), lambda i, j: (i, 0))],
        out_shape=[jax.ShapeDtypeStruct((r, ncols * tn), F32),
                   jax.ShapeDtypeStruct((r, LANES), F32)],
        scratch_shapes=[pltpu.VMEM((tm, d), BF16)],
        compiler_params=_cparams(2),
        name="inproj",
    )(x2, ln_g, ln_b, mod3, mod3, w_main, w_lr)


def _cm_body(u_ref, v_ref, ng_ref, nb_ref, ws_ref, bs_ref, o_ref):
    tm = u_ref.shape[0]
    u = _gelu(u_ref[...])
    v = _gelu(v_ref[...])
    vb = _layer_norm(v, ng_ref[...], nb_ref[...]).astype(BF16)
    hd = CM_CHUNK
    for c in range(tm // CM_CHUNK):
        rows = slice(c * CM_CHUNK, (c + 1) * CM_CHUNK)
        for h in range(CM_HEADS):
            cols = slice(h * hd, (h + 1) * hd)
            s = _dot(ws_ref[h], vb[rows, cols]) + bs_ref[:, cols]
            o_ref[rows, cols] = (u[rows, cols] * s).astype(BF16)


def _chunk_mlp(p, ng, nb, ws, bs, tm):
    r = p.shape[0]
    w = CM_HEADS * CM_CHUNK
    return pl.pallas_call(
        _cm_body,
        grid=(r // tm,),
        in_specs=[pl.BlockSpec((tm, w), lambda i: (i, 0)),
                  pl.BlockSpec((tm, w), lambda i: (i, 1)),
                  pl.BlockSpec((1, w), lambda i: (0, 0)),
                  pl.BlockSpec((1, w), lambda i: (0, 0)),
                  pl.BlockSpec((CM_HEADS, CM_CHUNK, CM_CHUNK), lambda i: (0, 0, 0)),
                  pl.BlockSpec((CM_CHUNK, w), lambda i: (0, 0))],
        out_specs=pl.BlockSpec((tm, w), lambda i: (i, 0)),
        out_shape=jax.ShapeDtypeStruct((r, w), BF16),
        compiler_params=_cparams(1),
        name="chunk_mlp",
    )(p, p, ng, nb, ws, bs)


GLA_TILE = 256


def _gla_tile(q, k, v, lr, wg, bg, st_ref, forward, need_o):
    t = k.shape[0]
    n_chunks = t // GLA_CHUNK
    z = _dot(lr.astype(BF16), wg) + bg
    g = jax.nn.log_sigmoid(z) * (1.0 / GLA_GATE_NORMALIZER)
    r_id = lax.broadcasted_iota(I32, (t, t), 0)
    c_id = lax.broadcasted_iota(I32, (t, t), 1)
    shift = GLA_CHUNK.bit_length() - 1
    same = (r_id >> shift) == (c_id >> shift)
    lower = same & (c_id <= r_id)
    tri = _ones_where(lower, BF16)
    g_hi = g.astype(BF16)
    g_lo = (g - g_hi.astype(F32)).astype(BF16)
    csum = _dot(tri, g_hi) + _dot(tri, g_lo)
    g3 = g.reshape(n_chunks, GLA_CHUNK, GLA_DK)
    tot = jnp.broadcast_to(jnp.sum(g3, axis=1, keepdims=True), g3.shape).reshape(t, GLA_DK)
    bcum = csum if forward else tot - csum + g
    kd = (k * jnp.exp(tot - bcum)).astype(BF16)
    decay = jnp.exp(tot)
    vb = v.astype(BF16)
    o = None
    if need_o:
        qe = ((q * (GLA_DK ** -0.5)) * jnp.exp(bcum)).astype(BF16)
        ke = (k * jnp.exp(-bcum)).astype(BF16)
        att = _dot_nt(qe, ke)
        mask = lower if forward else same & (c_id >= r_id)
        att = jnp.where(mask, att, 0.0).astype(BF16)
        o = _dot(att, vb)
    outs = [None] * n_chunks
    order = range(n_chunks) if forward else range(n_chunks - 1, -1, -1)
    for c in order:
        rows = slice(c * GLA_CHUNK, (c + 1) * GLA_CHUNK)
        s_t = st_ref[...]
        if need_o:
            outs[c] = o[rows] + _dot_nt(qe[rows], s_t.astype(BF16))
        u_t = _dot_tn(vb[rows], kd[rows])
        st_ref[...] = s_t * decay[c * GLA_CHUNK:c * GLA_CHUNK + 1, :] + u_t
    if need_o:
        return jnp.concatenate(outs, axis=0)
    return None


def _gla_body(q_ref, k_ref, v_ref, go_ref, lr_ref, kc_ref, vc_ref, lrc_ref,
              wgf_ref, bgf_ref, wgb_ref, bgb_ref, ng_ref, out_ref, o_scr, sf_ref, sb_ref):
    t = GLA_TILE
    n_x = q_ref.shape[0] // t
    n_c = kc_ref.shape[0] // t
    half = n_x // 2
    sf_ref[...] = jnp.zeros_like(sf_ref)
    sb_ref[...] = jnp.zeros_like(sb_ref)
    wgf, bgf, wgb, bgb = wgf_ref[...], bgf_ref[...], wgb_ref[...], bgb_ref[...]

    for i in range(n_c):
        rf = slice(i * t, (i + 1) * t)
        rb = slice((n_c - 1 - i) * t, (n_c - i) * t)
        _gla_tile(None, kc_ref[rf], vc_ref[rf], lrc_ref[rf], wgf, bgf, sf_ref, True, False)
        _gla_tile(None, kc_ref[rb], vc_ref[rb], lrc_ref[rb], wgb, bgb, sb_ref, False, False)

    def tile_out(i, forward):
        rows = pl.ds(pl.multiple_of(i * t, t), t)
        if forward:
            return rows, _gla_tile(q_ref[rows], k_ref[rows], v_ref[rows], lr_ref[rows],
                                   wgf, bgf, sf_ref, True, True)
        return rows, _gla_tile(q_ref[rows], k_ref[rows], v_ref[rows], lr_ref[rows],
                               wgb, bgb, sb_ref, False, True)

    def first_half(i, carry):
        rows, o = tile_out(i, True)
        o_scr[rows] = o
        rows, o = tile_out(n_x - 1 - i, False)
        o_scr[rows] = o
        return carry

    def finish(rows, o):
        o = o + o_scr[rows]
        ms = jnp.mean(o * o, axis=-1, keepdims=True)
        on = o * lax.rsqrt(ms + RMS_EPS) * ng_ref[...]
        out_ref[rows] = (on * _silu(go_ref[rows])).astype(BF16)

    def second_half(i, carry):
        rows, o = tile_out(i, True)
        finish(rows, o)
        rows, o = tile_out(n_x - 1 - i, False)
        finish(rows, o)
        return carry

    lax.fori_loop(0, half, first_half, 0)
    lax.fori_loop(half, n_x, second_half, 0)


def _gla(p3, lr3, pc3, lrc3, wgf, bgf, wgb, bgb, ng):
    bsz, l, _ = p3.shape
    lc = pc3.shape[1]
    dk, dv = GLA_DK, GLA_DV
    q0, k0 = 2048 // dk, 2560 // dk
    v0, go0 = 3072 // dv, 4096 // dv
    kc0, vc0 = 512 // dk, 1024 // dv
    return pl.pallas_call(
        _gla_body,
        grid=(bsz, GLA_HEADS),
        in_specs=[pl.BlockSpec((None, l, dk), lambda b, h: (b, 0, q0 + h)),
                  pl.BlockSpec((None, l, dk), lambda b, h: (b, 0, k0 + h)),
                  pl.BlockSpec((None, l, dv), lambda b, h: (b, 0, v0 + h)),
                  pl.BlockSpec((None, l, dv), lambda b, h: (b, 0, go0 + h)),
                  pl.BlockSpec((None, l, LANES), lambda b, h: (b, 0, 0)),
                  pl.BlockSpec((None, lc, dk), lambda b, h: (b, 0, kc0 + h)),
                  pl.BlockSpec((None, lc, dv), lambda b, h: (b, 0, vc0 + h)),
                  pl.BlockSpec((None, lc, LANES), lambda b, h: (b, 0, 0)),
                  pl.BlockSpec((LANES, dk), lambda b, h: (0, h)),
                  pl.BlockSpec((1, dk), lambda b, h: (0, h)),
                  pl.BlockSpec((LANES, dk), lambda b, h: (0, h)),
                  pl.BlockSpec((1, dk), lambda b, h: (0, h)),
                  pl.BlockSpec((1, dv), lambda b, h: (0, 0))],
        out_specs=pl.BlockSpec((None, l, dv), lambda b, h: (b, 0, h)),
        out_shape=jax.ShapeDtypeStruct((bsz, l, GLA_HEADS * dv), BF16),
        scratch_shapes=[pltpu.VMEM((l, dv), F32),
                        pltpu.VMEM((dv, dk), F32),
                        pltpu.VMEM((dv, dk), F32)],
        compiler_params=_cparams(2),
        name="gla",
    )(p3, p3, p3, p3, lr3, pc3, pc3, lrc3, wgf, bgf, wgb, bgb, ng)


def _pack_bf16_pair(lo, hi):
    lo_b = lax.bitcast_convert_type(lo.astype(BF16).astype(F32), U32)
    hi_b = lax.bitcast_convert_type(hi.astype(BF16).astype(F32), U32)
    return hi_b | (lo_b >> 16)


def _unpack_bf16_pair(p):
    lo = lax.bitcast_convert_type(p << 16, F32).astype(BF16)
    hi = lax.bitcast_convert_type(p & jnp.uint32(0xFFFF0000), F32).astype(BF16)
    return lo, hi


def _post_body(cm_ref, gla_ref, x_ref, lng_ref, lnb_ref, g1_ref, sh2_ref, sc2_ref,
               wo_ref, l1g_ref, l1b_ref, wr_ref, br_ref, h1_ref, hp_ref, lg_ref):
    half = cm_ref.shape[1]
    y = _dot(cm_ref[...], wo_ref[:half, :]) + _dot(gla_ref[...], wo_ref[half:, :])
    hx = _layer_norm(x_ref[...], lng_ref[...], lnb_ref[...])
    h1 = _layer_norm(DEEPNORM_ALPHA * hx + g1_ref[...] * y, l1g_ref[...], l1b_ref[...])
    h1_ref[...] = h1
    hm = h1 * (1.0 + sc2_ref[...]) + sh2_ref[...]
    d2 = hm.shape[1] // 2
    hp_ref[...] = _pack_bf16_pair(hm[:, :d2], hm[:, d2:])
    lg_ref[...] = _dot(hm.astype(BF16), wr_ref[...]) + br_ref[...]


def _post_attn(cm, gla, x2, mod3, rows_per_batch, ln_g, ln_b, w_out, l1g, l1b, w_r, b_r, tm):
    r, d = x2.shape
    half = d // 2
    row = lambda i: (i * tm) // rows_per_batch
    full = lambda shape: pl.BlockSpec(shape, lambda i: (0,) * len(shape))
    return pl.pallas_call(
        _post_body,
        grid=(r // tm,),
        in_specs=[pl.BlockSpec((tm, half), lambda i: (i, 0)),
                  pl.BlockSpec((tm, half), lambda i: (i, 0)),
                  pl.BlockSpec((tm, d), lambda i: (i, 0)),
                  full((1, d)), full((1, d)),
                  pl.BlockSpec((None, 1, d), lambda i: (row(i), 0, 2)),
                  pl.BlockSpec((None, 1, d), lambda i: (row(i), 0, 3)),
                  pl.BlockSpec((None, 1, d), lambda i: (row(i), 0, 4)),
                  full((d, d)), full((1, d)), full((1, d)),
                  full((d, LANES)), full((1, LANES))],
        out_specs=[pl.BlockSpec((tm, d), lambda i: (i, 0)),
                   pl.BlockSpec((tm, half), lambda i: (i, 0)),
                   pl.BlockSpec((tm, LANES), lambda i: (i, 0))],
        out_shape=[jax.ShapeDtypeStruct((r, d), F32),
                   jax.ShapeDtypeStruct((r, half), U32),
                   jax.ShapeDtypeStruct((r, LANES), F32)],
        compiler_params=_cparams(1),
        name="post_attn",
    )(cm, gla, x2, ln_g, ln_b, mod3, mod3, mod3, w_out, l1g, l1b, w_r, b_r)


def _route_body(lg_ref, idx_ref, wt_ref, rank_ref, cnt_ref, carry_ref):
    i = pl.program_id(0)
    t = lg_ref.shape[0]

    @pl.when(i == 0)
    def _():
        carry_ref[...] = jnp.zeros_like(carry_ref)

    lane = lax.broadcasted_iota(I32, (t, LANES), 1)
    lane_f = lane.astype(F32)
    neg = jnp.float32(-jnp.inf)
    l = jnp.where(lane < N_EXPERTS, lg_ref[...], neg)
    tops, onehots, idxs = [], [], []
    for _ in range(TOP_K):
        m = jnp.max(l, axis=-1, keepdims=True)
        idx = jnp.min(jnp.where(l == m, lane_f, float(LANES)), axis=-1, keepdims=True).astype(I32)
        oh = lane == idx
        l = jnp.where(oh, neg, l)
        tops.append(m)
        idxs.append(idx)
        onehots.append(oh)
    exps = [jnp.exp(m - tops[0]) for m in tops]
    denom = exps[0] + exps[1] + exps[2] + exps[3]
    sel = _ones_where(onehots[0] | onehots[1] | onehots[2] | onehots[3], F32)
    r_id = lax.broadcasted_iota(I32, (t, t), 0)
    c_id = lax.broadcasted_iota(I32, (t, t), 1)
    strict = _ones_where(c_id < r_id, BF16)
    before = _dot(strict, sel.astype(BF16)) + carry_ref[0:1, :]
    idx_out = jnp.zeros((t, LANES), I32)
    wt_out = jnp.zeros((t, LANES), F32)
    rank_out = jnp.zeros((t, LANES), I32)
    for k in range(TOP_K):
        rk = jnp.sum(jnp.where(onehots[k], before, 0.0), axis=-1, keepdims=True).astype(I32)
        idx_out = jnp.where(lane == k, idxs[k], idx_out)
        wt_out = jnp.where(lane == k, exps[k] / denom, wt_out)
        rank_out = jnp.where(lane == k, rk, rank_out)
    idx_ref[...] = idx_out
    wt_ref[...] = wt_out
    rank_ref[...] = rank_out
    total = carry_ref[0:1, :] + jnp.sum(sel, axis=0, keepdims=True)
    carry_ref[...] = jnp.broadcast_to(total, carry_ref.shape)
    cnt_ref[...] = jnp.broadcast_to(total, cnt_ref.shape)


def _route(logits, tm):
    n = logits.shape[0]
    blk = pl.BlockSpec((tm, LANES), lambda i: (i, 0))
    return pl.pallas_call(
        _route_body,
        grid=(n // tm,),
        in_specs=[blk],
        out_specs=[blk, blk, blk, pl.BlockSpec((8, LANES), lambda i: (0, 0))],
        out_shape=[jax.ShapeDtypeStruct((n, LANES), I32),
                   jax.ShapeDtypeStruct((n, LANES), F32),
                   jax.ShapeDtypeStruct((n, LANES), I32),
                   jax.ShapeDtypeStruct((8, LANES), F32)],
        scratch_shapes=[pltpu.VMEM((8, LANES), F32)],
        compiler_params=_cparams(1),
        name="route",
    )(logits)


def _dispatch_body(dest_ref, hp_ref, xs_in_ref, xs_ref, sem):
    del xs_in_ref
    t = hp_ref.shape[0]

    def issue(tok, carry):
        for k in range(TOP_K):
            d = dest_ref[0, tok * TOP_K + k]
            pltpu.make_async_copy(hp_ref.at[pl.ds(tok, 1)], xs_ref.at[pl.ds(d, 1)], sem).start()
        return carry

    lax.fori_loop(0, t, issue, 0)
    for _ in range(TOP_K):
        pltpu.make_async_copy(hp_ref, xs_ref.at[pl.ds(0, t)], sem).wait()


def _dispatch(dest, hp, n_slots, tm):
    n, w = hp.shape
    dest3 = dest.reshape(n // tm, 1, tm * TOP_K)
    xs0 = jnp.zeros((n_slots, w), U32)
    return pl.pallas_call(
        _dispatch_body,
        grid=(n // tm,),
        in_specs=[pl.BlockSpec((None, 1, tm * TOP_K), lambda i: (i, 0, 0), memory_space=pltpu.SMEM),
                  pl.BlockSpec((tm, w), lambda i: (i, 0)),
                  pl.BlockSpec(memory_space=pl.ANY)],
        out_specs=pl.BlockSpec(memory_space=pl.ANY),
        out_shape=jax.ShapeDtypeStruct((n_slots, w), U32),
        scratch_shapes=[pltpu.SemaphoreType.DMA(())],
        input_output_aliases={2: 0},
        compiler_params=_cparams(1),
        name="dispatch",
    )(dest3, hp, xs0)


EXPERT_TM = 1024
EXPERT_TN = 256


def _expert_body(sbe_ref, sbx_ref, sbr_ref, x_ref, wg_ref, wu_ref, bg_ref, bu_ref, wd_ref, bd_ref,
                 o_ref, xb_ref):
    del sbe_ref, sbx_ref
    s = pl.program_id(0)
    j = pl.program_id(1)
    n_rows = sbr_ref[s]
    d2 = x_ref.shape[1]
    sub = MOE_BLOCK

    def compute():
        wg = wg_ref[...].astype(BF16)
        wu = wu_ref[...].astype(BF16)
        wd = wd_ref[...].astype(BF16)
        for i in range(EXPERT_TM // sub):
            rows = slice(i * sub, (i + 1) * sub)

            @pl.when(n_rows > i * sub)
            def _(rows=rows):
                @pl.when(j == 0)
                def _():
                    lo, hi = _unpack_bf16_pair(x_ref[rows, :])
                    xb_ref[rows, :d2] = lo
                    xb_ref[rows, d2:] = hi

                xb = xb_ref[rows, :]
                gate = jnp.minimum(_dot(xb, wg) + bg_ref[...], SWIGLU_LIMIT)
                up = jnp.clip(_dot(xb, wu) + bu_ref[...], -SWIGLU_LIMIT, SWIGLU_LIMIT)
                act = ((up + 1.0) * gate * jax.nn.sigmoid(SWIGLU_ALPHA * gate)).astype(BF16)
                y = _dot(act, wd)

                @pl.when(j == 0)
                def _():
                    o_ref[rows, :] = y + bd_ref[...]

                @pl.when(j > 0)
                def _():
                    o_ref[rows, :] += y

    pl.when(n_rows > 0)(compute)

    for i in range(EXPERT_TM // sub):
        @pl.when((n_rows <= i * sub) & (j == 0))
        def _(i=i):
            o_ref[i * sub:(i + 1) * sub, :] = jnp.zeros((sub, o_ref.shape[1]), F32)


def _experts(sb_e, sb_x, sb_rows, xs, w_gu, b_gu3, w_dn, b_dn3, n_steps):
    d2 = xs.shape[1]
    d = 2 * d2
    de = w_dn.shape[1]
    tn = EXPERT_TN
    nj = de // tn
    tm = EXPERT_TM

    def jj(j, nr, s):
        return jnp.where(nr[s] > 0, j, nj - 1)

    return pl.pallas_call(
        _expert_body,
        grid_spec=pltpu.PrefetchScalarGridSpec(
            num_scalar_prefetch=3,
            grid=(n_steps, nj),
            in_specs=[
                pl.BlockSpec((tm, d2), lambda s, j, e, xi, nr: (xi[s], 0)),
                pl.BlockSpec((None, d, tn), lambda s, j, e, xi, nr: (e[s], 0, jj(j, nr, s))),
                pl.BlockSpec((None, d, tn), lambda s, j, e, xi, nr: (e[s], 0, nj + jj(j, nr, s))),
                pl.BlockSpec((None, 1, tn), lambda s, j, e, xi, nr: (e[s], 0, jj(j, nr, s))),
                pl.BlockSpec((None, 1, tn), lambda s, j, e, xi, nr: (e[s], 0, nj + jj(j, nr, s))),
                pl.BlockSpec((None, tn, d), lambda s, j, e, xi, nr: (e[s], jj(j, nr, s), 0)),
                pl.BlockSpec((None, 1, d), lambda s, j, e, xi, nr: (e[s], 0, 0)),
            ],
            out_specs=pl.BlockSpec((tm, d), lambda s, j, e, xi, nr: (s, 0)),
            scratch_shapes=[pltpu.VMEM((tm, d), BF16)]),
        out_shape=jax.ShapeDtypeStruct((n_steps * tm, d), F32),
        compiler_params=_cparams(2),
        name="experts",
    )(sb_e, sb_x, sb_rows, xs, w_gu, w_gu, b_gu3, b_gu3, w_dn, b_dn3)


def _combine_body(dest_ref, wt_ref, h1_ref, g2_ref, l2g_ref, l2b_ref, y_ref, out_ref, buf_ref, sem):
    t = h1_ref.shape[0]

    def issue(tok, carry):
        for k in range(TOP_K):
            d = dest_ref[0, tok * TOP_K + k]
            pltpu.make_async_copy(y_ref.at[pl.ds(d, 1)], buf_ref.at[k, pl.ds(tok, 1)], sem).start()
        return carry

    lax.fori_loop(0, t, issue, 0)
    for k in range(TOP_K):
        pltpu.make_async_copy(y_ref.at[pl.ds(0, t)], buf_ref.at[k], sem).wait()
    wt = wt_ref[...]
    f = buf_ref[0] * wt[:, 0:1]
    for k in range(1, TOP_K):
        f = f + buf_ref[k] * wt[:, k:k + 1]
    out_ref[...] = _layer_norm(DEEPNORM_ALPHA * h1_ref[...] + g2_ref[...] * f, l2g_ref[...], l2b_ref[...])


def _combine(dest, wt, h1, mod3, rows_per_batch, l2g, l2b, y, tm):
    n, d = h1.shape
    dest3 = dest.reshape(n // tm, 1, tm * TOP_K)
    row = lambda i: (i * tm) // rows_per_batch
    return pl.pallas_call(
        _combine_body,
        grid=(n // tm,),
        in_specs=[pl.BlockSpec((None, 1, tm * TOP_K), lambda i: (i, 0, 0), memory_space=pltpu.SMEM),
                  pl.BlockSpec((tm, LANES), lambda i: (i, 0)),
                  pl.BlockSpec((tm, d), lambda i: (i, 0)),
                  pl.BlockSpec((None, 1, d), lambda i: (row(i), 0, 5)),
                  pl.BlockSpec((1, d), lambda i: (0, 0)),
                  pl.BlockSpec((1, d), lambda i: (0, 0)),
                  pl.BlockSpec(memory_space=pl.ANY)],
        out_specs=pl.BlockSpec((tm, d), lambda i: (i, 0)),
        out_shape=jax.ShapeDtypeStruct((n, d), F32),
        scratch_shapes=[pltpu.VMEM((TOP_K, tm, d), F32), pltpu.SemaphoreType.DMA(())],
        compiler_params=_cparams(1),
        name="combine",
    )(dest3, wt, h1, mod3, l2g, l2b, y)


def _pick_tile(n, pref):
    t = pref
    while n % t:
        t //= 2
    return t


def kernel(x, c, ctx, c_ctx, ln_in_g, ln_in_b, w_ada, b_ada, w_in, cm_norm_g, cm_norm_b, cm_w_s, cm_b_s,
           gla_w_gk_f, gla_b_gk_f, gla_w_gk_b, gla_b_gk_b, gla_norm_g, w_out, ln1_g, ln1_b,
           w_router, b_router, w_gate_up, b_gate_up, w_down, b_down, ln2_g, ln2_b):
    bsz, l, d = x.shape
    lc = ctx.shape[1]
    n, nc = bsz * l, bsz * lc
    assert w_ada.shape[0] == 1, "single-layer configuration"
    assert bsz + 1 <= 8 and l % (2 * GLA_TILE) == 0 and lc % GLA_TILE == 0
    row = lambda v: v.reshape(1, -1)

    cc = jnp.concatenate([c, c_ctx[None, :], jnp.zeros((8 - bsz - 1, d), F32)], axis=0)
    mod3 = _ada(cc, w_ada[0], row(b_ada[0])).reshape(8, 1, N_MOD * d)

    w_in_b = w_in[0].astype(BF16)
    n_main = 5 * 1024
    w_main = w_in_b[:, :n_main]
    w_lr = jnp.pad(w_in_b[:, n_main:], ((0, 0), (0, LANES - 2 * GLA_RANK)))
    x2 = x.reshape(n, d)
    tm_x = _pick_tile(l, 512)
    p, lr = _inproj(x2, mod3, lambda i: (i * tm_x) // l, row(ln_in_g), row(ln_in_b), w_main, w_lr, 0, 5, tm_x)
    tm_c = _pick_tile(nc, 512)
    pc, lrc = _inproj(ctx.reshape(nc, d), mod3, lambda i: bsz, row(ln_in_g), row(ln_in_b),
                      w_main, w_lr, 2, 2, tm_c)

    bs_tile = jnp.repeat(cm_b_s[0].T, CM_CHUNK, axis=1)
    cm = _chunk_mlp(p, row(cm_norm_g[0]), row(cm_norm_b[0]), cm_w_s[0].astype(BF16), bs_tile, tm_x)

    kw = GLA_HEADS * GLA_DK
    wgf = jnp.zeros((LANES, kw), BF16).at[:GLA_RANK].set(gla_w_gk_f[0].astype(BF16))
    wgb = jnp.zeros((LANES, kw), BF16).at[GLA_RANK:2 * GLA_RANK].set(gla_w_gk_b[0].astype(BF16))
    gla = _gla(p.reshape(bsz, l, -1), lr.reshape(bsz, l, LANES), pc.reshape(bsz, lc, -1),
               lrc.reshape(bsz, lc, LANES), wgf, row(gla_b_gk_f[0]), wgb, row(gla_b_gk_b[0]),
               row(gla_norm_g[0])).reshape(n, -1)

    w_r = jnp.pad(w_router[0], ((0, 0), (0, LANES - N_EXPERTS))).astype(BF16)
    b_r = jnp.pad(b_router[0], (0, LANES - N_EXPERTS)).reshape(1, LANES)
    h1, hp, logits = _post_attn(cm, gla, x2, mod3, l, row(ln_in_g), row(ln_in_b), w_out[0].astype(BF16),
                                row(ln1_g[0]), row(ln1_b[0]), w_r, b_r, tm_x)

    idx, wt, rank, cnt = _route(logits, _pick_tile(n, 1024))
    counts = cnt[0, :N_EXPERTS].astype(I32)
    tm_e = EXPERT_TM
    n_steps = (n * TOP_K + N_EXPERTS * (tm_e - 1)) // tm_e
    steps_e = (counts + tm_e - 1) // tm_e
    step_end = jnp.cumsum(steps_e)
    step_start = step_end - steps_e
    n_used = step_end[-1]
    dest = ((step_start * tm_e)[idx[:, :TOP_K]] + rank[:, :TOP_K]).reshape(-1)
    sid = jnp.arange(n_steps, dtype=I32)
    used = sid < n_used
    last = jnp.maximum(n_used - 1, 0)
    sid_c = jnp.minimum(sid, last)
    sb_e = jnp.minimum(jnp.searchsorted(step_end, sid_c, side="right"), N_EXPERTS - 1).astype(I32)
    sb_rows = jnp.where(used, jnp.minimum(counts[sb_e] - (sid - step_start[sb_e]) * tm_e, tm_e), 0).astype(I32)
    sb_x = sid_c.astype(I32)

    xs = _dispatch(dest, hp, n_steps * tm_e, _pick_tile(n, 512))
    ys = _experts(sb_e, sb_x, sb_rows, xs, w_gate_up[0], b_gate_up[0].reshape(N_EXPERTS, 1, -1),
                  w_down[0], b_down[0].reshape(N_EXPERTS, 1, -1), n_steps)
    out = _combine(dest, wt, h1, mod3, l, row(ln2_g[0]), row(ln2_b[0]), ys, _pick_tile(n, 256))
    return out.reshape(bsz, l, d)
```

```python
import math

import jax
import jax.numpy as jnp
from jax import lax
from jax.experimental import pallas as pl
from jax.experimental.pallas import tpu as pltpu

F32 = jnp.float32
BF16 = jnp.bfloat16
U32 = jnp.uint32
I32 = jnp.int32

CM_CHUNK = 128
CM_HEADS = 8
GLA_HEADS = 4
GLA_DK = 128
GLA_DV = 256
GLA_CHUNK = 64
GLA_RANK = 16
GLA_GATE_NORMALIZER = 16.0
N_EXPERTS = 32
TOP_K = 4
MOE_BLOCK = 256
SWIGLU_LIMIT = 7.0
SWIGLU_ALPHA = 1.702
N_MOD = 6
DEEPNORM_ALPHA = 2.0 ** 0.25
LN_EPS = 1e-5
RMS_EPS = 1e-6

LANES = 128
VMEM_LIMIT = 56 * 1024 * 1024


def _cparams(n_axes, vmem=VMEM_LIMIT):
    return pltpu.CompilerParams(dimension_semantics=("arbitrary",) * n_axes,
                                vmem_limit_bytes=vmem)


def _layer_norm(t, g, b):
    mu = jnp.mean(t, axis=-1, keepdims=True)
    d = t - mu
    var = jnp.mean(d * d, axis=-1, keepdims=True)
    return d * lax.rsqrt(var + LN_EPS) * g + b


def _gelu(t):
    return 0.5 * t * (1.0 + lax.erf(t * (1.0 / math.sqrt(2.0))))


def _silu(t):
    return t * jax.nn.sigmoid(t)


def _ones_where(mask, dtype):
    return jnp.where(mask, 1.0, 0.0).astype(dtype)


def _dot(a, b):
    return jnp.dot(a, b, preferred_element_type=F32)


def _dot_nt(a, b):
    return lax.dot_general(a, b, (((1,), (1,)), ((), ())), preferred_element_type=F32)


def _dot_tn(a, b):
    return lax.dot_general(a, b, (((0,), (0,)), ((), ())), preferred_element_type=F32)


def _ada_body(c_ref, w_ref, b_ref, o_ref):
    a = _silu(c_ref[...]).astype(BF16)
    o_ref[...] = _dot(a, w_ref[...].astype(BF16)) + b_ref[...]


def _ada(cc, w, b):
    rows, d = cc.shape
    n = w.shape[1]
    tn = 1024
    return pl.pallas_call(
        _ada_body,
        grid=(n // tn,),
        in_specs=[pl.BlockSpec((rows, d), lambda j: (0, 0)),
                  pl.BlockSpec((d, tn), lambda j: (0, j)),
                  pl.BlockSpec((1, tn), lambda j: (0, j))],
        out_specs=pl.BlockSpec((rows, tn), lambda j: (0, j)),
        out_shape=jax.ShapeDtypeStruct((rows, n), F32),
        compiler_params=_cparams(1),
        name="ada",
    )(cc, w, b)


def _inproj_body(x_ref, g_ref, b_ref, sh_ref, sc_ref, w_ref, wlr_ref, o_ref, olr_ref, hm_ref):
    @pl.when(pl.program_id(1) == 0)
    def _():
        h = _layer_norm(x_ref[...], g_ref[...], b_ref[...])
        hm = (h * (1.0 + sc_ref[...]) + sh_ref[...]).astype(BF16)
        hm_ref[...] = hm
        olr_ref[...] = _dot(hm, wlr_ref[...])

    o_ref[...] = _dot(hm_ref[...], w_ref[...])


def _inproj(x2, mod3, mod_row, ln_g, ln_b, w_main, w_lr, col0, ncols, tm):
    r, d = x2.shape
    tn = 1024
    return pl.pallas_call(
        _inproj_body,
        grid=(r // tm, ncols),
        in_specs=[pl.BlockSpec((tm, d), lambda i, j: (i, 0)),
                  pl.BlockSpec((1, d), lambda i, j: (0, 0)),
                  pl.BlockSpec((1, d), lambda i, j: (0, 0)),
                  pl.BlockSpec((None, 1, d), lambda i, j: (mod_row(i), 0, 0)),
                  pl.BlockSpec((None, 1, d), lambda i, j: (mod_row(i), 0, 1)),
                  pl.BlockSpec((d, tn), lambda i, j: (0, col0 + j)),
                  pl.BlockSpec((d, LANES), lambda i, j: (0, 0))],
        out_specs=[pl.BlockSpec((tm, tn), lambda i, j: (i, j)),
                   pl.BlockSpec((tm, LANES), lambda i, j: (i, 0))],
        out_shape=[jax.ShapeDtypeStruct((r, ncols * tn), F32),
                   jax.ShapeDtypeStruct((r, LANES), F32)],
        scratch_shapes=[pltpu.VMEM((tm, d), BF16)],
        compiler_params=_cparams(2),
        name="inproj",
    )(x2, ln_g, ln_b, mod3, mod3, w_main, w_lr)


def _cm_body(u_ref, v_ref, ng_ref, nb_ref, ws_ref, bs_ref, o_ref):
    tm = u_ref.shape[0]
    u = _gelu(u_ref[...])
    v = _gelu(v_ref[...])
    vb = _layer_norm(v, ng_ref[...], nb_ref[...]).astype(BF16)
    hd = CM_CHUNK
    for c in range(tm // CM_CHUNK):
        rows = slice(c * CM_CHUNK, (c + 1) * CM_CHUNK)
        for h in range(CM_HEADS):
            cols = slice(h * hd, (h + 1) * hd)
            s = _dot(ws_ref[h], vb[rows, cols]) + bs_ref[:, cols]
            o_ref[rows, cols] = (u[rows, cols] * s).astype(BF16)


def _chunk_mlp(p, ng, nb, ws, bs, tm):
    r = p.shape[0]
    w = CM_HEADS * CM_CHUNK
    return pl.pallas_call(
        _cm_body,
        grid=(r // tm,),
        in_specs=[pl.BlockSpec((tm, w), lambda i: (i, 0)),
                  pl.BlockSpec((tm, w), lambda i: (i, 1)),
                  pl.BlockSpec((1, w), lambda i: (0, 0)),
                  pl.BlockSpec((1, w), lambda i: (0, 0)),
                  pl.BlockSpec((CM_HEADS, CM_CHUNK, CM_CHUNK), lambda i: (0, 0, 0)),
                  pl.BlockSpec((CM_CHUNK, w), lambda i: (0, 0))],
        out_specs=pl.BlockSpec((tm, w), lambda i: (i, 0)),
        out_shape=jax.ShapeDtypeStruct((r, w), BF16),
        compiler_params=_cparams(1),
        name="chunk_mlp",
    )(p, p, ng, nb, ws, bs)


GLA_TILE = 256


def _gla_tile(q, k, v, lr, wg, bg, st_ref, forward, need_o):
    t = k.shape[0]
    n_chunks = t // GLA_CHUNK
    z = _dot(lr.astype(BF16), wg) + bg
    g = jax.nn.log_sigmoid(z) * (1.0 / GLA_GATE_NORMALIZER)
    r_id = lax.broadcasted_iota(I32, (t, t), 0)
    c_id = lax.broadcasted_iota(I32, (t, t), 1)
    shift = GLA_CHUNK.bit_length() - 1
    same = (r_id >> shift) == (c_id >> shift)
    lower = same & (c_id <= r_id)
    tri = _ones_where(lower, BF16)
    g_hi = g.astype(BF16)
    g_lo = (g - g_hi.astype(F32)).astype(BF16)
    csum = _dot(tri, g_hi) + _dot(tri, g_lo)
    g3 = g.reshape(n_chunks, GLA_CHUNK, GLA_DK)
    tot = jnp.broadcast_to(jnp.sum(g3, axis=1, keepdims=True), g3.shape).reshape(t, GLA_DK)
    bcum = csum if forward else tot - csum + g
    kd = (k * jnp.exp(tot - bcum)).astype(BF16)
    decay = jnp.exp(tot)
    vb = v.astype(BF16)
    o = None
    if need_o:
        qe = ((q * (GLA_DK ** -0.5)) * jnp.exp(bcum)).astype(BF16)
        ke = (k * jnp.exp(-bcum)).astype(BF16)
        att = _dot_nt(qe, ke)
        mask = lower if forward else same & (c_id >= r_id)
        att = jnp.where(mask, att, 0.0).astype(BF16)
        o = _dot(att, vb)
    outs = [None] * n_chunks
    order = range(n_chunks) if forward else range(n_chunks - 1, -1, -1)
    for c in order:
        rows = slice(c * GLA_CHUNK, (c + 1) * GLA_CHUNK)
        s_t = st_ref[...]
        if need_o:
            outs[c] = o[rows] + _dot_nt(qe[rows], s_t.astype(BF16))
        u_t = _dot_tn(vb[rows], kd[rows])
        st_ref[...] = s_t * decay[c * GLA_CHUNK:c * GLA_CHUNK + 1, :] + u_t
    if need_o:
        return jnp.concatenate(outs, axis=0)
    return None


def _gla_body(q_ref, k_ref, v_ref, go_ref, lr_ref, kc_ref, vc_ref, lrc_ref,
              wgf_ref, bgf_ref, wgb_ref, bgb_ref, ng_ref, out_ref, o_scr, sf_ref, sb_ref):
    t = GLA_TILE
    n_x = q_ref.shape[0] // t
    n_c = kc_ref.shape[0] // t
    half = n_x // 2
    sf_ref[...] = jnp.zeros_like(sf_ref)
    sb_ref[...] = jnp.zeros_like(sb_ref)
    wgf, bgf, wgb, bgb = wgf_ref[...], bgf_ref[...], wgb_ref[...], bgb_ref[...]

    for i in range(n_c):
        rf = slice(i * t, (i + 1) * t)
        rb = slice((n_c - 1 - i) * t, (n_c - i) * t)
        _gla_tile(None, kc_ref[rf], vc_ref[rf], lrc_ref[rf], wgf, bgf, sf_ref, True, False)
        _gla_tile(None, kc_ref[rb], vc_ref[rb], lrc_ref[rb], wgb, bgb, sb_ref, False, False)

    def tile_out(i, forward):
        rows = pl.ds(pl.multiple_of(i * t, t), t)
        if forward:
            return rows, _gla_tile(q_ref[rows], k_ref[rows], v_ref[rows], lr_ref[rows],
                                   wgf, bgf, sf_ref, True, True)
        return rows, _gla_tile(q_ref[rows], k_ref[rows], v_ref[rows], lr_ref[rows],
                               wgb, bgb, sb_ref, False, True)

    def first_half(i, carry):
        rows, o = tile_out(i, True)
        o_scr[rows] = o
        rows, o = tile_out(n_x - 1 - i, False)
        o_scr[rows] = o
        return carry

    def finish(rows, o):
        o = o + o_scr[rows]
        ms = jnp.mean(o * o, axis=-1, keepdims=True)
        on = o * lax.rsqrt(ms + RMS_EPS) * ng_ref[...]
        out_ref[rows] = (on * _silu(go_ref[rows])).astype(BF16)

    def second_half(i, carry):
        rows, o = tile_out(i, True)
        finish(rows, o)
        rows, o = tile_out(n_x - 1 - i, False)
        finish(rows, o)
        return carry

    lax.fori_loop(0, half, first_half, 0)
    lax.fori_loop(half, n_x, second_half, 0)


def _gla(p3, lr3, pc3, lrc3, wgf, bgf, wgb, bgb, ng):
    bsz, l, _ = p3.shape
    lc = pc3.shape[1]
    dk, dv = GLA_DK, GLA_DV
    q0, k0 = 2048 // dk, 2560 // dk
    v0, go0 = 3072 // dv, 4096 // dv
    kc0, vc0 = 512 // dk, 1024 // dv
    return pl.pallas_call(
        _gla_body,
        grid=(bsz, GLA_HEADS),
        in_specs=[pl.BlockSpec((None, l, dk), lambda b, h: (b, 0, q0 + h)),
                  pl.BlockSpec((None, l, dk), lambda b, h: (b, 0, k0 + h)),
                  pl.BlockSpec((None, l, dv), lambda b, h: (b, 0, v0 + h)),
                  pl.BlockSpec((None, l, dv), lambda b, h: (b, 0, go0 + h)),
                  pl.BlockSpec((None, l, LANES), lambda b, h: (b, 0, 0)),
                  pl.BlockSpec((None, lc, dk), lambda b, h: (b, 0, kc0 + h)),
                  pl.BlockSpec((None, lc, dv), lambda b, h: (b, 0, vc0 + h)),
                  pl.BlockSpec((None, lc, LANES), lambda b, h: (b, 0, 0)),
                  pl.BlockSpec((LANES, dk), lambda b, h: (0, h)),
                  pl.BlockSpec((1, dk), lambda b, h: (0, h)),
                  pl.BlockSpec((LANES, dk), lambda b, h: (0, h)),
                  pl.BlockSpec((1, dk), lambda b, h: (0, h)),
                  pl.BlockSpec((1, dv), lambda b, h: (0, 0))],
        out_specs=pl.BlockSpec((None, l, dv), lambda b, h: (b, 0, h)),
        out_shape=jax.ShapeDtypeStruct((bsz, l, GLA_HEADS * dv), BF16),
        scratch_shapes=[pltpu.VMEM((l, dv), F32),
                        pltpu.VMEM((dv, dk), F32),
                        pltpu.VMEM((dv, dk), F32)],
        compiler_params=_cparams(2),
        name="gla",
    )(p3, p3, p3, p3, lr3, pc3, pc3, lrc3, wgf, bgf, wgb, bgb, ng)


def _pack_bf16_pair(lo, hi):
    lo_b = lax.bitcast_convert_type(lo.astype(BF16).astype(F32), U32)
    hi_b = lax.bitcast_convert_type(hi.astype(BF16).astype(F32), U32)
    return hi_b | (lo_b >> 16)


def _unpack_bf16_pair(p):
    lo = lax.bitcast_convert_type(p << 16, F32).astype(BF16)
    hi = lax.bitcast_convert_type(p & jnp.uint32(0xFFFF0000), F32).astype(BF16)
    return lo, hi


def _post_body(cm_ref, gla_ref, x_ref, lng_ref, lnb_ref, g1_ref, sh2_ref, sc2_ref,
               wo_ref, l1g_ref, l1b_ref, wr_ref, br_ref, h1_ref, hp_ref, lg_ref):
    half = cm_ref.shape[1]
    y = _dot(cm_ref[...], wo_ref[:half, :]) + _dot(gla_ref[...], wo_ref[half:, :])
    hx = _layer_norm(x_ref[...], lng_ref[...], lnb_ref[...])
    h1 = _layer_norm(DEEPNORM_ALPHA * hx + g1_ref[...] * y, l1g_ref[...], l1b_ref[...])
    h1_ref[...] = h1
    hm = h1 * (1.0 + sc2_ref[...]) + sh2_ref[...]
    d2 = hm.shape[1] // 2
    hp_ref[...] = _pack_bf16_pair(hm[:, :d2], hm[:, d2:])
    lg_ref[...] = _dot(hm.astype(BF16), wr_ref[...]) + br_ref[...]


def _post_attn(cm, gla, x2, mod3, rows_per_batch, ln_g, ln_b, w_out, l1g, l1b, w_r, b_r, tm):
    r, d = x2.shape
    half = d // 2
    row = lambda i: (i * tm) // rows_per_batch
    full = lambda shape: pl.BlockSpec(shape, lambda i: (0,) * len(shape))
    return pl.pallas_call(
        _post_body,
        grid=(r // tm,),
        in_specs=[pl.BlockSpec((tm, half), lambda i: (i, 0)),
                  pl.BlockSpec((tm, half), lambda i: (i, 0)),
                  pl.BlockSpec((tm, d), lambda i: (i, 0)),
                  full((1, d)), full((1, d)),
                  pl.BlockSpec((None, 1, d), lambda i: (row(i), 0, 2)),
                  pl.BlockSpec((None, 1, d), lambda i: (row(i), 0, 3)),
                  pl.BlockSpec((None, 1, d), lambda i: (row(i), 0, 4)),
                  full((d, d)), full((1, d)), full((1, d)),
                  full((d, LANES)), full((1, LANES))],
        out_specs=[pl.BlockSpec((tm, d), lambda i: (i, 0)),
                   pl.BlockSpec((tm, half), lambda i: (i, 0)),
                   pl.BlockSpec((tm, LANES), lambda i: (i, 0))],
        out_shape=[jax.ShapeDtypeStruct((r, d), F32),
                   jax.ShapeDtypeStruct((r, half), U32),
                   jax.ShapeDtypeStruct((r, LANES), F32)],
        compiler_params=_cparams(1),
        name="post_attn",
    )(cm, gla, x2, ln_g, ln_b, mod3, mod3, mod3, w_out, l1g, l1b, w_r, b_r)


def _route_body(lg_ref, idx_ref, wt_ref, rank_ref, cnt_ref, carry_ref):
    i = pl.program_id(0)
    t = lg_ref.shape[0]

    @pl.when(i == 0)
    def _():
        carry_ref[...] = jnp.zeros_like(carry_ref)

    lane = lax.broadcasted_iota(I32, (t, LANES), 1)
    lane_f = lane.astype(F32)
    neg = jnp.float32(-jnp.inf)
    l = jnp.where(lane < N_EXPERTS, lg_ref[...], neg)
    tops, onehots, idxs = [], [], []
    for _ in range(TOP_K):
        m = jnp.max(l, axis=-1, keepdims=True)
        idx = jnp.min(jnp.where(l == m, lane_f, float(LANES)), axis=-1, keepdims=True).astype(I32)
        oh = lane == idx
        l = jnp.where(oh, neg, l)
        tops.append(m)
        idxs.append(idx)
        onehots.append(oh)
    exps = [jnp.exp(m - tops[0]) for m in tops]
    denom = exps[0] + exps[1] + exps[2] + exps[3]
    sel = _ones_where(onehots[0] | onehots[1] | onehots[2] | onehots[3], F32)
    r_id = lax.broadcasted_iota(I32, (t, t), 0)
    c_id = lax.broadcasted_iota(I32, (t, t), 1)
    strict = _ones_where(c_id < r_id, BF16)
    before = _dot(strict, sel.astype(BF16)) + carry_ref[0:1, :]
    idx_out = jnp.zeros((t, LANES), I32)
    wt_out = jnp.zeros((t, LANES), F32)
    rank_out = jnp.zeros((t, LANES), I32)
    for k in range(TOP_K):
        rk = jnp.sum(jnp.where(onehots[k], before, 0.0), axis=-1, keepdims=True).astype(I32)
        idx_out = jnp.where(lane == k, idxs[k], idx_out)
        wt_out = jnp.where(lane == k, exps[k] / denom, wt_out)
        rank_out = jnp.where(lane == k, rk, rank_out)
    idx_ref[...] = idx_out
    wt_ref[...] = wt_out
    rank_ref[...] = rank_out
    total = carry_ref[0:1, :] + jnp.sum(sel, axis=0, keepdims=True)
    carry_ref[...] = jnp.broadcast_to(total, carry_ref.shape)
    cnt_ref[...] = jnp.broadcast_to(total, cnt_ref.shape)


def _route(logits, tm):
    n = logits.shape[0]
    blk = pl.BlockSpec((tm, LANES), lambda i: (i, 0))
    return pl.pallas_call(
        _route_body,
        grid=(n // tm,),
        in_specs=[blk],
        out_specs=[blk, blk, blk, pl.BlockSpec((8, LANES), lambda i: (0, 0))],
        out_shape=[jax.ShapeDtypeStruct((n, LANES), I32),
                   jax.ShapeDtypeStruct((n, LANES), F32),
                   jax.ShapeDtypeStruct((n, LANES), I32),
                   jax.ShapeDtypeStruct((8, LANES), F32)],
        scratch_shapes=[pltpu.VMEM((8, LANES), F32)],
        compiler_params=_cparams(1),
        name="route",
    )(logits)


def _dispatch_body(dest_ref, hp_ref, xs_in_ref, xs_ref, sem):
    del xs_in_ref
    t = hp_ref.shape[0]

    def issue(tok, carry):
        for k in range(TOP_K):
            d = dest_ref[0, tok * TOP_K + k]
            pltpu.make_async_copy(hp_ref.at[pl.ds(tok, 1)], xs_ref.at[pl.ds(d, 1)], sem).start()
        return carry

    lax.fori_loop(0, t, issue, 0)
    for _ in range(TOP_K):
        pltpu.make_async_copy(hp_ref, xs_ref.at[pl.ds(0, t)], sem).wait()


def _dispatch(dest, hp, n_slots, tm):
    n, w = hp.shape
    dest3 = dest.reshape(n // tm, 1, tm * TOP_K)
    xs0 = jnp.zeros((n_slots, w), U32)
    return pl.pallas_call(
        _dispatch_body,
        grid=(n // tm,),
        in_specs=[pl.BlockSpec((None, 1, tm * TOP_K), lambda i: (i, 0, 0), memory_space=pltpu.SMEM),
                  pl.BlockSpec((tm, w), lambda i: (i, 0)),
                  pl.BlockSpec(memory_space=pl.ANY)],
        out_specs=pl.BlockSpec(memory_space=pl.ANY),
        out_shape=jax.ShapeDtypeStruct((n_slots, w), U32),
        scratch_shapes=[pltpu.SemaphoreType.DMA(())],
        input_output_aliases={2: 0},
        compiler_params=_cparams(1),
        name="dispatch",
    )(dest3, hp, xs0)


PASS_SUBS = 9
EXPERT_TN = 256


def _expert_body(meta_ref, pe_ref, pr_ref, pn_ref, xs_hbm, wg_ref, wu_ref, bg_ref, bu_ref, wd_ref, bd_ref,
                 y_hbm, xraw, xb, h_ref, ybuf, zbuf, sem_x, sem_y, sem_z):
    del pe_ref
    sub = MOE_BLOCK
    tn = EXPERT_TN
    nj = h_ref.shape[0]
    nc = y_hbm.shape[1] // tn
    d2 = xraw.shape[1]
    p = pl.program_id(0)
    s = pl.program_id(1)
    n_pass = meta_ref[0]
    nsub = pn_ref[p]
    row0 = pr_ref[p]

    def x_copy(i, pp):
        r = pl.multiple_of((pr_ref[pp] + i) * sub, sub)
        return pltpu.make_async_copy(xs_hbm.at[pl.ds(r, sub)], xraw.at[pl.ds(i * sub, sub)], sem_x)

    def y_copy(i, r0, c, slot):
        r = pl.multiple_of((r0 + i) * sub, sub)
        col = pl.multiple_of(c * tn, tn)
        return pltpu.make_async_copy(ybuf.at[slot, pl.ds(i * sub, sub), :],
                                     y_hbm.at[pl.ds(r, sub), pl.ds(col, tn)], sem_y.at[slot])

    def z_copy(b, c):
        r = pl.multiple_of(b * sub, sub)
        return pltpu.make_async_copy(zbuf, y_hbm.at[pl.ds(r, sub), pl.ds(c * tn, tn)], sem_z)

    def for_subs(count, fn):
        for i in range(PASS_SUBS):
            pl.when(i < count)(lambda i=i: fn(i))

    def for_tail(fn):
        def body(b, carry):
            for c in range(nc):
                fn(b, c)
            return carry
        lax.fori_loop(meta_ref[1], y_hbm.shape[0] // sub, body, 0)

    @pl.when(s == 0)
    def _():
        @pl.when(p == 0)
        def _():
            for_subs(nsub, lambda i: x_copy(i, p).start())
            zbuf[...] = jnp.zeros_like(zbuf)
            for_tail(lambda b, c: z_copy(b, c).start())

        for_subs(nsub, lambda i: x_copy(i, p).wait())

    @pl.when((s == 1) & (p + 1 < n_pass))
    def _():
        for_subs(pn_ref[p + 1], lambda i: x_copy(i, p + 1).start())

    def for_groups(fn):
        k = 1 << (PASS_SUBS.bit_length() - 1)
        while k:
            start = pl.multiple_of((nsub & (-2 * k)) * sub, sub)
            pl.when((nsub & k) != 0)(lambda start=start, k=k: fn(pl.ds(start, k * sub)))
            k //= 2

    @pl.when(s == 0)
    def _():
        def unpack(i):
            rows = slice(i * sub, (i + 1) * sub)
            lo, hi = _unpack_bf16_pair(xraw[rows, :])
            xb[rows, :d2] = lo
            xb[rows, d2:] = hi

        for_subs(nsub, unpack)

    @pl.when(s < nj)
    def _():
        def group(rows):
            x = xb[rows, :]
            gate = jnp.minimum(_dot(x, wg_ref[...].astype(BF16)) + bg_ref[...], SWIGLU_LIMIT)
            up = jnp.clip(_dot(x, wu_ref[...].astype(BF16)) + bu_ref[...], -SWIGLU_LIMIT, SWIGLU_LIMIT)
            h_ref[s, rows, :] = ((up + 1.0) * gate * jax.nn.sigmoid(SWIGLU_ALPHA * gate)).astype(BF16)

        for_groups(group)

    @pl.when(s >= nj)
    def _():
        c = s - nj
        slot = c % 2

        @pl.when(c >= 2)
        def _():
            for_subs(nsub, lambda i: y_copy(i, row0, c - 2, slot).wait())

        @pl.when((c < 2) & (p > 0))
        def _():
            for_subs(pn_ref[p - 1], lambda i: y_copy(i, pr_ref[p - 1], nc - 2 + c, slot).wait())

        def group(rows):
            hx = jnp.concatenate([h_ref[j, rows, :] for j in range(nj)], axis=1)
            ybuf[slot, rows, :] = _dot(hx, wd_ref[...].astype(BF16)) + bd_ref[...]

        for_groups(group)
        for_subs(nsub, lambda i: y_copy(i, row0, c, slot).start())

        @pl.when((p == n_pass - 1) & (c == nc - 1))
        def _():
            for_subs(nsub, lambda i: y_copy(i, row0, c - 1, 1 - slot).wait())
            for_subs(nsub, lambda i: y_copy(i, row0, c, slot).wait())
            for_tail(lambda b, cc: z_copy(b, cc).wait())


def _experts(meta, pass_e, pass_row0, pass_nsub, xs, w_gu, b_gu3, w_dn, b_dn3):
    n_slots, d2 = xs.shape
    d = 2 * d2
    de = w_dn.shape[1]
    tn = EXPERT_TN
    nj = de // tn
    nc = d // tn
    rmax = PASS_SUBS * MOE_BLOCK

    def first(s):
        return jnp.minimum(s, nj - 1)

    def second(s):
        return jnp.maximum(s - nj, 0)

    return pl.pallas_call(
        _expert_body,
        grid_spec=pltpu.PrefetchScalarGridSpec(
            num_scalar_prefetch=4,
            grid=(meta[0], nj + nc),
            in_specs=[
                pl.BlockSpec(memory_space=pl.ANY),
                pl.BlockSpec((None, d, tn), lambda p, s, m, e, r, n: (e[p], 0, first(s))),
                pl.BlockSpec((None, d, tn), lambda p, s, m, e, r, n: (e[p], 0, nj + first(s))),
                pl.BlockSpec((None, 1, tn), lambda p, s, m, e, r, n: (e[p], 0, first(s))),
                pl.BlockSpec((None, 1, tn), lambda p, s, m, e, r, n: (e[p], 0, nj + first(s))),
                pl.BlockSpec((None, de, tn), lambda p, s, m, e, r, n: (e[p], 0, second(s))),
                pl.BlockSpec((None, 1, tn), lambda p, s, m, e, r, n: (e[p], 0, second(s))),
            ],
            out_specs=pl.BlockSpec(memory_space=pl.ANY),
            scratch_shapes=[pltpu.VMEM((rmax, d2), U32),
                            pltpu.VMEM((rmax, d), BF16),
                            pltpu.VMEM((nj, rmax, tn), BF16),
                            pltpu.VMEM((2, rmax, tn), F32),
                            pltpu.VMEM((MOE_BLOCK, tn), F32),
                            pltpu.SemaphoreType.DMA(()),
                            pltpu.SemaphoreType.DMA((2,)),
                            pltpu.SemaphoreType.DMA(())]),
        out_shape=jax.ShapeDtypeStruct((n_slots, d), F32),
        compiler_params=_cparams(2),
        name="experts",
    )(meta, pass_e, pass_row0, pass_nsub, xs, w_gu, w_gu, b_gu3, b_gu3, w_dn, b_dn3)


def _combine_body(dest_ref, wt_ref, h1_ref, g2_ref, l2g_ref, l2b_ref, y_ref, out_ref, buf_ref, sem):
    t = h1_ref.shape[0]

    def issue(tok, carry):
        for k in range(TOP_K):
            d = dest_ref[0, tok * TOP_K + k]
            pltpu.make_async_copy(y_ref.at[pl.ds(d, 1)], buf_ref.at[k, pl.ds(tok, 1)], sem).start()
        return carry

    lax.fori_loop(0, t, issue, 0)
    for k in range(TOP_K):
        pltpu.make_async_copy(y_ref.at[pl.ds(0, t)], buf_ref.at[k], sem).wait()
    wt = wt_ref[...]
    f = buf_ref[0] * wt[:, 0:1]
    for k in range(1, TOP_K):
        f = f + buf_ref[k] * wt[:, k:k + 1]
    out_ref[...] = _layer_norm(DEEPNORM_ALPHA * h1_ref[...] + g2_ref[...] * f, l2g_ref[...], l2b_ref[...])


def _combine(dest, wt, h1, mod3, rows_per_batch, l2g, l2b, y, tm):
    n, d = h1.shape
    dest3 = dest.reshape(n // tm, 1, tm * TOP_K)
    row = lambda i: (i * tm) // rows_per_batch
    return pl.pallas_call(
        _combine_body,
        grid=(n // tm,),
        in_specs=[pl.BlockSpec((None, 1, tm * TOP_K), lambda i: (i, 0, 0), memory_space=pltpu.SMEM),
                  pl.BlockSpec((tm, LANES), lambda i: (i, 0)),
                  pl.BlockSpec((tm, d), lambda i: (i, 0)),
                  pl.BlockSpec((None, 1, d), lambda i: (row(i), 0, 5)),
                  pl.BlockSpec((1, d), lambda i: (0, 0)),
                  pl.BlockSpec((1, d), lambda i: (0, 0)),
                  pl.BlockSpec(memory_space=pl.ANY)],
        out_specs=pl.BlockSpec((tm, d), lambda i: (i, 0)),
        out_shape=jax.ShapeDtypeStruct((n, d), F32),
        scratch_shapes=[pltpu.VMEM((TOP_K, tm, d), F32), pltpu.SemaphoreType.DMA(())],
        compiler_params=_cparams(1),
        name="combine",
    )(dest3, wt, h1, mod3, l2g, l2b, y)


def _pick_tile(n, pref):
    t = pref
    while n % t:
        t //= 2
    return t


def kernel(x, c, ctx, c_ctx, ln_in_g, ln_in_b, w_ada, b_ada, w_in, cm_norm_g, cm_norm_b, cm_w_s, cm_b_s,
           gla_w_gk_f, gla_b_gk_f, gla_w_gk_b, gla_b_gk_b, gla_norm_g, w_out, ln1_g, ln1_b,
           w_router, b_router, w_gate_up, b_gate_up, w_down, b_down, ln2_g, ln2_b):
    bsz, l, d = x.shape
    lc = ctx.shape[1]
    n, nc = bsz * l, bsz * lc
    assert w_ada.shape[0] == 1, "single-layer configuration"
    assert bsz + 1 <= 8 and l % (2 * GLA_TILE) == 0 and lc % GLA_TILE == 0
    row = lambda v: v.reshape(1, -1)

    cc = jnp.concatenate([c, c_ctx[None, :], jnp.zeros((8 - bsz - 1, d), F32)], axis=0)
    mod3 = _ada(cc, w_ada[0], row(b_ada[0])).reshape(8, 1, N_MOD * d)

    w_in_b = w_in[0].astype(BF16)
    n_main = 5 * 1024
    w_main = w_in_b[:, :n_main]
    w_lr = jnp.pad(w_in_b[:, n_main:], ((0, 0), (0, LANES - 2 * GLA_RANK)))
    x2 = x.reshape(n, d)
    tm_x = _pick_tile(l, 512)
    p, lr = _inproj(x2, mod3, lambda i: (i * tm_x) // l, row(ln_in_g), row(ln_in_b), w_main, w_lr, 0, 5, tm_x)
    tm_c = _pick_tile(nc, 512)
    pc, lrc = _inproj(ctx.reshape(nc, d), mod3, lambda i: bsz, row(ln_in_g), row(ln_in_b),
                      w_main, w_lr, 2, 2, tm_c)

    bs_tile = jnp.repeat(cm_b_s[0].T, CM_CHUNK, axis=1)
    cm = _chunk_mlp(p, row(cm_norm_g[0]), row(cm_norm_b[0]), cm_w_s[0].astype(BF16), bs_tile, tm_x)

    kw = GLA_HEADS * GLA_DK
    wgf = jnp.zeros((LANES, kw), BF16).at[:GLA_RANK].set(gla_w_gk_f[0].astype(BF16))
    wgb = jnp.zeros((LANES, kw), BF16).at[GLA_RANK:2 * GLA_RANK].set(gla_w_gk_b[0].astype(BF16))
    gla = _gla(p.reshape(bsz, l, -1), lr.reshape(bsz, l, LANES), pc.reshape(bsz, lc, -1),
               lrc.reshape(bsz, lc, LANES), wgf, row(gla_b_gk_f[0]), wgb, row(gla_b_gk_b[0]),
               row(gla_norm_g[0])).reshape(n, -1)

    w_r = jnp.pad(w_router[0], ((0, 0), (0, LANES - N_EXPERTS))).astype(BF16)
    b_r = jnp.pad(b_router[0], (0, LANES - N_EXPERTS)).reshape(1, LANES)
    h1, hp, logits = _post_attn(cm, gla, x2, mod3, l, row(ln_in_g), row(ln_in_b), w_out[0].astype(BF16),
                                row(ln1_g[0]), row(ln1_b[0]), w_r, b_r, tm_x)

    idx, wt, rank, cnt = _route(logits, _pick_tile(n, 1024))
    counts = cnt[0, :N_EXPERTS].astype(I32)
    n_blocks = (n * TOP_K + N_EXPERTS * (MOE_BLOCK - 1)) // MOE_BLOCK
    blocks_e = (counts + MOE_BLOCK - 1) // MOE_BLOCK
    blk_end = jnp.cumsum(blocks_e)
    blk_start = blk_end - blocks_e
    dest = ((blk_start * MOE_BLOCK)[idx[:, :TOP_K]] + rank[:, :TOP_K]).reshape(-1)
    n_pass_max = n_blocks // PASS_SUBS + N_EXPERTS
    pass_cnt = (blocks_e + PASS_SUBS - 1) // PASS_SUBS
    pass_end = jnp.cumsum(pass_cnt)
    pass_start = pass_end - pass_cnt
    pid = jnp.arange(n_pass_max, dtype=I32)
    pass_e = jnp.minimum(jnp.searchsorted(pass_end, pid, side="right"), N_EXPERTS - 1).astype(I32)
    local = pid - pass_start[pass_e]
    pass_row0 = (blk_start[pass_e] + local * PASS_SUBS).astype(I32)
    pass_nsub = jnp.clip(blocks_e[pass_e] - local * PASS_SUBS, 0, PASS_SUBS).astype(I32)
    meta = jnp.stack([pass_end[-1], blk_end[-1]]).astype(I32)

    xs = _dispatch(dest, hp, n_blocks * MOE_BLOCK, _pick_tile(n, 512))
    ys = _experts(meta, pass_e, pass_row0, pass_nsub, xs, w_gate_up[0], b_gate_up[0].reshape(N_EXPERTS, 1, -1),
                  w_down[0], b_down[0].reshape(N_EXPERTS, 1, -1))
    out = _combine(dest, wt, h1, mod3, l, row(ln2_g[0]), row(ln2_b[0]), ys, _pick_tile(n, 256))
    return out.reshape(bsz, l, d)
```

```python
import math

import jax
import jax.numpy as jnp
from jax import lax
from jax.experimental import pallas as pl
from jax.experimental.pallas import tpu as pltpu

F32 = jnp.float32
BF16 = jnp.bfloat16
U32 = jnp.uint32
I32 = jnp.int32

CM_CHUNK = 128
CM_HEADS = 8
GLA_HEADS = 4
GLA_DK = 128
GLA_DV = 256
GLA_CHUNK = 64
GLA_RANK = 16
GLA_GATE_NORMALIZER = 16.0
N_EXPERTS = 32
TOP_K = 4
MOE_BLOCK = 256
SWIGLU_LIMIT = 7.0
SWIGLU_ALPHA = 1.702
N_MOD = 6
DEEPNORM_ALPHA = 2.0 ** 0.25
LN_EPS = 1e-5
RMS_EPS = 1e-6

LANES = 128
VMEM_LIMIT = 56 * 1024 * 1024


def _cparams(n_axes, vmem=VMEM_LIMIT):
    return pltpu.CompilerParams(dimension_semantics=("arbitrary",) * n_axes,
                                vmem_limit_bytes=vmem)


def _layer_norm(t, g, b):
    mu = jnp.mean(t, axis=-1, keepdims=True)
    d = t - mu
    var = jnp.mean(d * d, axis=-1, keepdims=True)
    return d * lax.rsqrt(var + LN_EPS) * g + b


def _gelu(t):
    return 0.5 * t * (1.0 + lax.erf(t * (1.0 / math.sqrt(2.0))))


def _silu(t):
    return t * jax.nn.sigmoid(t)


def _ones_where(mask, dtype):
    return jnp.where(mask, 1.0, 0.0).astype(dtype)


def _dot(a, b):
    return jnp.dot(a, b, preferred_element_type=F32)


def _dot_nt(a, b):
    return lax.dot_general(a, b, (((1,), (1,)), ((), ())), preferred_element_type=F32)


def _dot_tn(a, b):
    return lax.dot_general(a, b, (((0,), (0,)), ((), ())), preferred_element_type=F32)


def _ada_body(c_ref, w_ref, b_ref, o_ref):
    a = _silu(c_ref[...]).astype(BF16)
    o_ref[...] = _dot(a, w_ref[...].astype(BF16)) + b_ref[...]


def _ada(cc, w, b):
    rows, d = cc.shape
    n = w.shape[1]
    tn = 1024
    return pl.pallas_call(
        _ada_body,
        grid=(n // tn,),
        in_specs=[pl.BlockSpec((rows, d), lambda j: (0, 0)),
                  pl.BlockSpec((d, tn), lambda j: (0, j)),
                  pl.BlockSpec((1, tn), lambda j: (0, j))],
        out_specs=pl.BlockSpec((rows, tn), lambda j: (0, j)),
        out_shape=jax.ShapeDtypeStruct((rows, n), F32),
        compiler_params=_cparams(1),
        name="ada",
    )(cc, w, b)


def _inproj_body(x_ref, g_ref, b_ref, sh_ref, sc_ref, w_ref, wlr_ref, o_ref, olr_ref, hm_ref):
    @pl.when(pl.program_id(1) == 0)
    def _():
        h = _layer_norm(x_ref[...], g_ref[...], b_ref[...])
        hm = (h * (1.0 + sc_ref[...]) + sh_ref[...]).astype(BF16)
        hm_ref[...] = hm
        olr_ref[...] = _dot(hm, wlr_ref[...])

    o_ref[...] = _dot(hm_ref[...], w_ref[...])


def _inproj(x2, mod3, mod_row, ln_g, ln_b, w_main, w_lr, col0, ncols, tm):
    r, d = x2.shape
    tn = 1024
    return pl.pallas_call(
        _inproj_body,
        grid=(r // tm, ncols),
        in_specs=[pl.BlockSpec((tm, d), lambda i, j: (i, 0)),
                  pl.BlockSpec((1, d), lambda i, j: (0, 0)),
                  pl.BlockSpec((1, d), lambda i, j: (0, 0)),
                  pl.BlockSpec((None, 1, d), lambda i, j: (mod_row(i), 0, 0)),
                  pl.BlockSpec((None, 1, d), lambda i, j: (mod_row(i), 0, 1)),
                  pl.BlockSpec((d, tn), lambda i, j: (0, col0 + j)),
                  pl.BlockSpec((d, LANES), lambda i, j: (0, 0))],
        out_specs=[pl.BlockSpec((tm, tn), lambda i, j: (i, j)),
                   pl.BlockSpec((tm, LANES), lambda i, j: (i, 0))],
        out_shape=[jax.ShapeDtypeStruct((r, ncols * tn), F32),
                   jax.ShapeDtypeStruct((r, LANES), F32)],
        scratch_shapes=[pltpu.VMEM((tm, d), BF16)],
        compiler_params=_cparams(2),
        name="inproj",
    )(x2, ln_g, ln_b, mod3, mod3, w_main, w_lr)


def _cm_body(u_ref, v_ref, ng_ref, nb_ref, ws_ref, bs_ref, o_ref):
    tm = u_ref.shape[0]
    u = _gelu(u_ref[...])
    v = _gelu(v_ref[...])
    vb = _layer_norm(v, ng_ref[...], nb_ref[...]).astype(BF16)
    hd = CM_CHUNK
    for c in range(tm // CM_CHUNK):
        rows = slice(c * CM_CHUNK, (c + 1) * CM_CHUNK)
        for h in range(CM_HEADS):
            cols = slice(h * hd, (h + 1) * hd)
            s = _dot(ws_ref[h], vb[rows, cols]) + bs_ref[:, cols]
            o_ref[rows, cols] = (u[rows, cols] * s).astype(BF16)


def _chunk_mlp(p, ng, nb, ws, bs, tm):
    r = p.shape[0]
    w = CM_HEADS * CM_CHUNK
    return pl.pallas_call(
        _cm_body,
        grid=(r // tm,),
        in_specs=[pl.BlockSpec((tm, w), lambda i: (i, 0)),
                  pl.BlockSpec((tm, w), lambda i: (i, 1)),
                  pl.BlockSpec((1, w), lambda i: (0, 0)),
                  pl.BlockSpec((1, w), lambda i: (0, 0)),
                  pl.BlockSpec((CM_HEADS, CM_CHUNK, CM_CHUNK), lambda i: (0, 0, 0)),
                  pl.BlockSpec((CM_CHUNK, w), lambda i: (0, 0))],
        out_specs=pl.BlockSpec((tm, w), lambda i: (i, 0)),
        out_shape=jax.ShapeDtypeStruct((r, w), BF16),
        compiler_params=_cparams(1),
        name="chunk_mlp",
    )(p, p, ng, nb, ws, bs)


GLA_TILE = 256


def _gla_tile(q, k, v, lr, wg, bg, st_ref, forward, need_o):
    t = k.shape[0]
    n_chunks = t // GLA_CHUNK
    z = _dot(lr.astype(BF16), wg) + bg
    g = jax.nn.log_sigmoid(z) * (1.0 / GLA_GATE_NORMALIZER)
    r_id = lax.broadcasted_iota(I32, (t, t), 0)
    c_id = lax.broadcasted_iota(I32, (t, t), 1)
    shift = GLA_CHUNK.bit_length() - 1
    same = (r_id >> shift) == (c_id >> shift)
    lower = same & (c_id <= r_id)
    tri = _ones_where(lower, BF16)
    g_hi = g.astype(BF16)
    g_lo = (g - g_hi.astype(F32)).astype(BF16)
    csum = _dot(tri, g_hi) + _dot(tri, g_lo)
    g3 = g.reshape(n_chunks, GLA_CHUNK, GLA_DK)
    tot = jnp.broadcast_to(jnp.sum(g3, axis=1, keepdims=True), g3.shape).reshape(t, GLA_DK)
    bcum = csum if forward else tot - csum + g
    kd = (k * jnp.exp(tot - bcum)).astype(BF16)
    decay = jnp.exp(tot)
    vb = v.astype(BF16)
    o = None
    if need_o:
        qe = ((q * (GLA_DK ** -0.5)) * jnp.exp(bcum)).astype(BF16)
        ke = (k * jnp.exp(-bcum)).astype(BF16)
        att = _dot_nt(qe, ke)
        mask = lower if forward else same & (c_id >= r_id)
        att = jnp.where(mask, att, 0.0).astype(BF16)
        o = _dot(att, vb)
    outs = [None] * n_chunks
    order = range(n_chunks) if forward else range(n_chunks - 1, -1, -1)
    for c in order:
        rows = slice(c * GLA_CHUNK, (c + 1) * GLA_CHUNK)
        s_t = st_ref[...]
        if need_o:
            outs[c] = o[rows] + _dot_nt(qe[rows], s_t.astype(BF16))
        u_t = _dot_tn(vb[rows], kd[rows])
        st_ref[...] = s_t * decay[c * GLA_CHUNK:c * GLA_CHUNK + 1, :] + u_t
    if need_o:
        return jnp.concatenate(outs, axis=0)
    return None


def _gla_body(q_ref, k_ref, v_ref, go_ref, lr_ref, kc_ref, vc_ref, lrc_ref,
              wgf_ref, bgf_ref, wgb_ref, bgb_ref, ng_ref, out_ref, o_scr, sf_ref, sb_ref):
    t = GLA_TILE
    n_x = q_ref.shape[0] // t
    n_c = kc_ref.shape[0] // t
    half = n_x // 2
    sf_ref[...] = jnp.zeros_like(sf_ref)
    sb_ref[...] = jnp.zeros_like(sb_ref)
    wgf, bgf, wgb, bgb = wgf_ref[...], bgf_ref[...], wgb_ref[...], bgb_ref[...]

    for i in range(n_c):
        rf = slice(i * t, (i + 1) * t)
        rb = slice((n_c - 1 - i) * t, (n_c - i) * t)
        _gla_tile(None, kc_ref[rf], vc_ref[rf], lrc_ref[rf], wgf, bgf, sf_ref, True, False)
        _gla_tile(None, kc_ref[rb], vc_ref[rb], lrc_ref[rb], wgb, bgb, sb_ref, False, False)

    def tile_out(i, forward):
        rows = pl.ds(pl.multiple_of(i * t, t), t)
        if forward:
            return rows, _gla_tile(q_ref[rows], k_ref[rows], v_ref[rows], lr_ref[rows],
                                   wgf, bgf, sf_ref, True, True)
        return rows, _gla_tile(q_ref[rows], k_ref[rows], v_ref[rows], lr_ref[rows],
                               wgb, bgb, sb_ref, False, True)

    def first_half(i, carry):
        rows, o = tile_out(i, True)
        o_scr[rows] = o
        rows, o = tile_out(n_x - 1 - i, False)
        o_scr[rows] = o
        return carry

    def finish(rows, o):
        o = o + o_scr[rows]
        ms = jnp.mean(o * o, axis=-1, keepdims=True)
        on = o * lax.rsqrt(ms + RMS_EPS) * ng_ref[...]
        out_ref[rows] = (on * _silu(go_ref[rows])).astype(BF16)

    def second_half(i, carry):
        rows, o = tile_out(i, True)
        finish(rows, o)
        rows, o = tile_out(n_x - 1 - i, False)
        finish(rows, o)
        return carry

    lax.fori_loop(0, half, first_half, 0)
    lax.fori_loop(half, n_x, second_half, 0)


def _gla(p3, lr3, pc3, lrc3, wgf, bgf, wgb, bgb, ng):
    bsz, l, _ = p3.shape
    lc = pc3.shape[1]
    dk, dv = GLA_DK, GLA_DV
    q0, k0 = 2048 // dk, 2560 // dk
    v0, go0 = 3072 // dv, 4096 // dv
    kc0, vc0 = 512 // dk, 1024 // dv
    return pl.pallas_call(
        _gla_body,
        grid=(bsz, GLA_HEADS),
        in_specs=[pl.BlockSpec((None, l, dk), lambda b, h: (b, 0, q0 + h)),
                  pl.BlockSpec((None, l, dk), lambda b, h: (b, 0, k0 + h)),
                  pl.BlockSpec((None, l, dv), lambda b, h: (b, 0, v0 + h)),
                  pl.BlockSpec((None, l, dv), lambda b, h: (b, 0, go0 + h)),
                  pl.BlockSpec((None, l, LANES), lambda b, h: (b, 0, 0)),
                  pl.BlockSpec((None, lc, dk), lambda b, h: (b, 0, kc0 + h)),
                  pl.BlockSpec((None, lc, dv), lambda b, h: (b, 0, vc0 + h)),
                  pl.BlockSpec((None, lc, LANES), lambda b, h: (b, 0, 0)),
                  pl.BlockSpec((LANES, dk), lambda b, h: (0, h)),
                  pl.BlockSpec((1, dk), lambda b, h: (0, h)),
                  pl.BlockSpec((LANES, dk), lambda b, h: (0, h)),
                  pl.BlockSpec((1, dk), lambda b, h: (0, h)),
                  pl.BlockSpec((1, dv), lambda b, h: (0, 0))],
        out_specs=pl.BlockSpec((None, l, dv), lambda b, h: (b, 0, h)),
        out_shape=jax.ShapeDtypeStruct((bsz, l, GLA_HEADS * dv), BF16),
        scratch_shapes=[pltpu.VMEM((l, dv), F32),
                        pltpu.VMEM((dv, dk), F32),
                        pltpu.VMEM((dv, dk), F32)],
        compiler_params=_cparams(2),
        name="gla",
    )(p3, p3, p3, p3, lr3, pc3, pc3, lrc3, wgf, bgf, wgb, bgb, ng)


def _pack_bf16_pair(lo, hi):
    lo_b = lax.bitcast_convert_type(lo.astype(BF16).astype(F32), U32)
    hi_b = lax.bitcast_convert_type(hi.astype(BF16).astype(F32), U32)
    return hi_b | (lo_b >> 16)


def _unpack_bf16_pair(p):
    lo = lax.bitcast_convert_type(p << 16, F32).astype(BF16)
    hi = lax.bitcast_convert_type(p & jnp.uint32(0xFFFF0000), F32).astype(BF16)
    return lo, hi


def _post_body(cm_ref, gla_ref, x_ref, lng_ref, lnb_ref, g1_ref, sh2_ref, sc2_ref,
               wo_ref, l1g_ref, l1b_ref, wr_ref, br_ref, h1_ref, hp_ref, lg_ref):
    half = cm_ref.shape[1]
    y = _dot(cm_ref[...], wo_ref[:half, :]) + _dot(gla_ref[...], wo_ref[half:, :])
    hx = _layer_norm(x_ref[...], lng_ref[...], lnb_ref[...])
    h1 = _layer_norm(DEEPNORM_ALPHA * hx + g1_ref[...] * y, l1g_ref[...], l1b_ref[...])
    h1_ref[...] = h1
    hm = h1 * (1.0 + sc2_ref[...]) + sh2_ref[...]
    d2 = hm.shape[1] // 2
    hp_ref[...] = _pack_bf16_pair(hm[:, :d2], hm[:, d2:])
    lg_ref[...] = _dot(hm.astype(BF16), wr_ref[...]) + br_ref[...]


def _post_attn(cm, gla, x2, mod3, rows_per_batch, ln_g, ln_b, w_out, l1g, l1b, w_r, b_r, tm):
    r, d = x2.shape
    half = d // 2
    row = lambda i: (i * tm) // rows_per_batch
    full = lambda shape: pl.BlockSpec(shape, lambda i: (0,) * len(shape))
    return pl.pallas_call(
        _post_body,
        grid=(r // tm,),
        in_specs=[pl.BlockSpec((tm, half), lambda i: (i, 0)),
                  pl.BlockSpec((tm, half), lambda i: (i, 0)),
                  pl.BlockSpec((tm, d), lambda i: (i, 0)),
                  full((1, d)), full((1, d)),
                  pl.BlockSpec((None, 1, d), lambda i: (row(i), 0, 2)),
                  pl.BlockSpec((None, 1, d), lambda i: (row(i), 0, 3)),
                  pl.BlockSpec((None, 1, d), lambda i: (row(i), 0, 4)),
                  full((d, d)), full((1, d)), full((1, d)),
                  full((d, LANES)), full((1, LANES))],
        out_specs=[pl.BlockSpec((tm, d), lambda i: (i, 0)),
                   pl.BlockSpec((tm, half), lambda i: (i, 0)),
                   pl.BlockSpec((tm, LANES), lambda i: (i, 0))],
        out_shape=[jax.ShapeDtypeStruct((r, d), F32),
                   jax.ShapeDtypeStruct((r, half), U32),
                   jax.ShapeDtypeStruct((r, LANES), F32)],
        compiler_params=_cparams(1),
        name="post_attn",
    )(cm, gla, x2, ln_g, ln_b, mod3, mod3, mod3, w_out, l1g, l1b, w_r, b_r)


def _route_body(lg_ref, idx_ref, wt_ref, rank_ref, cnt_ref, carry_ref):
    i = pl.program_id(0)
    t = lg_ref.shape[0]

    @pl.when(i == 0)
    def _():
        carry_ref[...] = jnp.zeros_like(carry_ref)

    lane = lax.broadcasted_iota(I32, (t, LANES), 1)
    lane_f = lane.astype(F32)
    neg = jnp.float32(-jnp.inf)
    l = jnp.where(lane < N_EXPERTS, lg_ref[...], neg)
    tops, onehots, idxs = [], [], []
    for _ in range(TOP_K):
        m = jnp.max(l, axis=-1, keepdims=True)
        idx = jnp.min(jnp.where(l == m, lane_f, float(LANES)), axis=-1, keepdims=True).astype(I32)
        oh = lane == idx
        l = jnp.where(oh, neg, l)
        tops.append(m)
        idxs.append(idx)
        onehots.append(oh)
    exps = [jnp.exp(m - tops[0]) for m in tops]
    denom = exps[0] + exps[1] + exps[2] + exps[3]
    sel = _ones_where(onehots[0] | onehots[1] | onehots[2] | onehots[3], F32)
    r_id = lax.broadcasted_iota(I32, (t, t), 0)
    c_id = lax.broadcasted_iota(I32, (t, t), 1)
    strict = _ones_where(c_id < r_id, BF16)
    before = _dot(strict, sel.astype(BF16)) + carry_ref[0:1, :]
    idx_out = jnp.zeros((t, LANES), I32)
    wt_out = jnp.zeros((t, LANES), F32)
    rank_out = jnp.zeros((t, LANES), I32)
    for k in range(TOP_K):
        rk = jnp.sum(jnp.where(onehots[k], before, 0.0), axis=-1, keepdims=True).astype(I32)
        idx_out = jnp.where(lane == k, idxs[k], idx_out)
        wt_out = jnp.where(lane == k, exps[k] / denom, wt_out)
        rank_out = jnp.where(lane == k, rk, rank_out)
    idx_ref[...] = idx_out
    wt_ref[...] = wt_out
    rank_ref[...] = rank_out
    total = carry_ref[0:1, :] + jnp.sum(sel, axis=0, keepdims=True)
    carry_ref[...] = jnp.broadcast_to(total, carry_ref.shape)
    cnt_ref[...] = jnp.broadcast_to(total, cnt_ref.shape)


def _route(logits, tm):
    n = logits.shape[0]
    blk = pl.BlockSpec((tm, LANES), lambda i: (i, 0))
    return pl.pallas_call(
        _route_body,
        grid=(n // tm,),
        in_specs=[blk],
        out_specs=[blk, blk, blk, pl.BlockSpec((8, LANES), lambda i: (0, 0))],
        out_shape=[jax.ShapeDtypeStruct((n, LANES), I32),
                   jax.ShapeDtypeStruct((n, LANES), F32),
                   jax.ShapeDtypeStruct((n, LANES), I32),
                   jax.ShapeDtypeStruct((8, LANES), F32)],
        scratch_shapes=[pltpu.VMEM((8, LANES), F32)],
        compiler_params=_cparams(1),
        name="route",
    )(logits)


def _dispatch_body(dest_ref, hp_ref, xs_in_ref, xs_ref, sem):
    del xs_in_ref
    t = hp_ref.shape[0]

    def issue(tok, carry):
        for k in range(TOP_K):
            d = dest_ref[0, tok * TOP_K + k]
            pltpu.make_async_copy(hp_ref.at[pl.ds(tok, 1)], xs_ref.at[pl.ds(d, 1)], sem).start()
        return carry

    lax.fori_loop(0, t, issue, 0)
    for _ in range(TOP_K):
        pltpu.make_async_copy(hp_ref, xs_ref.at[pl.ds(0, t)], sem).wait()


def _dispatch(dest, hp, n_slots, tm):
    n, w = hp.shape
    dest3 = dest.reshape(n // tm, 1, tm * TOP_K)
    xs0 = jnp.zeros((n_slots, w), U32)
    return pl.pallas_call(
        _dispatch_body,
        grid=(n // tm,),
        in_specs=[pl.BlockSpec((None, 1, tm * TOP_K), lambda i: (i, 0, 0), memory_space=pltpu.SMEM),
                  pl.BlockSpec((tm, w), lambda i: (i, 0)),
                  pl.BlockSpec(memory_space=pl.ANY)],
        out_specs=pl.BlockSpec(memory_space=pl.ANY),
        out_shape=jax.ShapeDtypeStruct((n_slots, w), U32),
        scratch_shapes=[pltpu.SemaphoreType.DMA(())],
        input_output_aliases={2: 0},
        compiler_params=_cparams(1),
        name="dispatch",
    )(dest3, hp, xs0)


PASS_SUBS = 9
EXPERT_TN = 256


def _expert_body_grid_tiles(meta_ref, pe_ref, pr_ref, pn_ref, xs_hbm, wg_ref, wu_ref, bg_ref, bu_ref, wd_ref, bd_ref,
                            y_hbm, xraw, xb, h_ref, ybuf, zbuf, sem_x, sem_y, sem_z):
    del pe_ref
    sub = MOE_BLOCK
    tn = EXPERT_TN
    nj = h_ref.shape[0]
    nc = y_hbm.shape[1] // tn
    d2 = xraw.shape[1]
    p = pl.program_id(0)
    s = pl.program_id(1)
    n_pass = meta_ref[0]
    nsub = pn_ref[p]
    row0 = pr_ref[p]

    def x_copy(i, pp):
        r = pl.multiple_of((pr_ref[pp] + i) * sub, sub)
        return pltpu.make_async_copy(xs_hbm.at[pl.ds(r, sub)], xraw.at[pl.ds(i * sub, sub)], sem_x)

    def y_copy(i, r0, c, slot):
        r = pl.multiple_of((r0 + i) * sub, sub)
        col = pl.multiple_of(c * tn, tn)
        return pltpu.make_async_copy(ybuf.at[slot, pl.ds(i * sub, sub), :],
                                     y_hbm.at[pl.ds(r, sub), pl.ds(col, tn)], sem_y.at[slot])

    def z_copy(b, c):
        r = pl.multiple_of(b * sub, sub)
        return pltpu.make_async_copy(zbuf, y_hbm.at[pl.ds(r, sub), pl.ds(c * tn, tn)], sem_z)

    def for_subs(count, fn):
        for i in range(PASS_SUBS):
            pl.when(i < count)(lambda i=i: fn(i))

    def for_tail(fn):
        def body(b, carry):
            for c in range(nc):
                fn(b, c)
            return carry
        lax.fori_loop(meta_ref[1], y_hbm.shape[0] // sub, body, 0)

    @pl.when(s == 0)
    def _():
        @pl.when(p == 0)
        def _():
            for_subs(nsub, lambda i: x_copy(i, p).start())
            zbuf[...] = jnp.zeros_like(zbuf)
            for_tail(lambda b, c: z_copy(b, c).start())

        for_subs(nsub, lambda i: x_copy(i, p).wait())

    @pl.when((s == 1) & (p + 1 < n_pass))
    def _():
        for_subs(pn_ref[p + 1], lambda i: x_copy(i, p + 1).start())

    def for_groups(fn):
        k = 1 << (PASS_SUBS.bit_length() - 1)
        while k:
            start = pl.multiple_of((nsub & (-2 * k)) * sub, sub)
            pl.when((nsub & k) != 0)(lambda start=start, k=k: fn(pl.ds(start, k * sub)))
            k //= 2

    @pl.when(s == 0)
    def _():
        def unpack(i):
            rows = slice(i * sub, (i + 1) * sub)
            lo, hi = _unpack_bf16_pair(xraw[rows, :])
            xb[rows, :d2] = lo
            xb[rows, d2:] = hi

        for_subs(nsub, unpack)

    @pl.when(s < nj)
    def _():
        def group(rows):
            x = xb[rows, :]
            gate = jnp.minimum(_dot(x, wg_ref[...].astype(BF16)) + bg_ref[...], SWIGLU_LIMIT)
            up = jnp.clip(_dot(x, wu_ref[...].astype(BF16)) + bu_ref[...], -SWIGLU_LIMIT, SWIGLU_LIMIT)
            h_ref[s, rows, :] = ((up + 1.0) * gate * jax.nn.sigmoid(SWIGLU_ALPHA * gate)).astype(BF16)

        for_groups(group)

    @pl.when(s >= nj)
    def _():
        c = s - nj
        slot = c % 2

        @pl.when(c >= 2)
        def _():
            for_subs(nsub, lambda i: y_copy(i, row0, c - 2, slot).wait())

        @pl.when((c < 2) & (p > 0))
        def _():
            for_subs(pn_ref[p - 1], lambda i: y_copy(i, pr_ref[p - 1], nc - 2 + c, slot).wait())

        def group(rows):
            hx = jnp.concatenate([h_ref[j, rows, :] for j in range(nj)], axis=1)
            ybuf[slot, rows, :] = _dot(hx, wd_ref[...].astype(BF16)) + bd_ref[...]

        for_groups(group)
        for_subs(nsub, lambda i: y_copy(i, row0, c, slot).start())

        @pl.when((p == n_pass - 1) & (c == nc - 1))
        def _():
            for_subs(nsub, lambda i: y_copy(i, row0, c - 1, 1 - slot).wait())
            for_subs(nsub, lambda i: y_copy(i, row0, c, slot).wait())
            for_tail(lambda b, cc: z_copy(b, cc).wait())


def _experts_grid_tiles(meta, pass_e, pass_row0, pass_nsub, xs, w_gu, b_gu3, w_dn, b_dn3):
    n_slots, d2 = xs.shape
    d = 2 * d2
    de = w_dn.shape[1]
    tn = EXPERT_TN
    nj = de // tn
    nc = d // tn
    rmax = PASS_SUBS * MOE_BLOCK

    def first(s):
        return jnp.minimum(s, nj - 1)

    def second(s):
        return jnp.maximum(s - nj, 0)

    return pl.pallas_call(
        _expert_body,
        grid_spec=pltpu.PrefetchScalarGridSpec(
            num_scalar_prefetch=4,
            grid=(meta[0], nj + nc),
            in_specs=[
                pl.BlockSpec(memory_space=pl.ANY),
                pl.BlockSpec((None, d, tn), lambda p, s, m, e, r, n: (e[p], 0, first(s))),
                pl.BlockSpec((None, d, tn), lambda p, s, m, e, r, n: (e[p], 0, nj + first(s))),
                pl.BlockSpec((None, 1, tn), lambda p, s, m, e, r, n: (e[p], 0, first(s))),
                pl.BlockSpec((None, 1, tn), lambda p, s, m, e, r, n: (e[p], 0, nj + first(s))),
                pl.BlockSpec((None, de, tn), lambda p, s, m, e, r, n: (e[p], 0, second(s))),
                pl.BlockSpec((None, 1, tn), lambda p, s, m, e, r, n: (e[p], 0, second(s))),
            ],
            out_specs=pl.BlockSpec(memory_space=pl.ANY),
            scratch_shapes=[pltpu.VMEM((rmax, d2), U32),
                            pltpu.VMEM((rmax, d), BF16),
                            pltpu.VMEM((nj, rmax, tn), BF16),
                            pltpu.VMEM((2, rmax, tn), F32),
                            pltpu.VMEM((MOE_BLOCK, tn), F32),
                            pltpu.SemaphoreType.DMA(()),
                            pltpu.SemaphoreType.DMA((2,)),
                            pltpu.SemaphoreType.DMA(())]),
        out_shape=jax.ShapeDtypeStruct((n_slots, d), F32),
        compiler_params=_cparams(2),
        name="experts",
    )(meta, pass_e, pass_row0, pass_nsub, xs, w_gu, w_gu, b_gu3, b_gu3, w_dn, b_dn3)


W_SLOTS = 3
W_AHEAD = 2


def _expert_body(meta_ref, pe_ref, pr_ref, pn_ref, xs_hbm, wgu_hbm, wdn_hbm, bgu_ref, bdn_ref,
                 y_hbm, xraw, xb, h_ref, wg_buf, wu_buf, wd_buf, ybuf, zbuf,
                 sem_x, sem_y, sem_z, sem_w, sem_d):
    sub = MOE_BLOCK
    tn = EXPERT_TN
    nj = h_ref.shape[0]
    nc = y_hbm.shape[1] // tn
    de = nj * tn
    d2 = xraw.shape[1]
    p = pl.program_id(0)
    n_pass = meta_ref[0]
    nsub = pn_ref[p]
    row0 = pr_ref[p]

    def x_copy(i, pp):
        r = pl.multiple_of((pr_ref[pp] + i) * sub, sub)
        return pltpu.make_async_copy(xs_hbm.at[pl.ds(r, sub)], xraw.at[pl.ds(i * sub, sub)], sem_x)

    def y_copy(i, r0, c, slot):
        r = pl.multiple_of((r0 + i) * sub, sub)
        col = pl.multiple_of(c * tn, tn)
        return pltpu.make_async_copy(ybuf.at[slot, pl.ds(i * sub, sub), :],
                                     y_hbm.at[pl.ds(r, sub), pl.ds(col, tn)], sem_y.at[slot])

    def z_copy(b, c):
        r = pl.multiple_of(b * sub, sub)
        return pltpu.make_async_copy(zbuf, y_hbm.at[pl.ds(r, sub), pl.ds(c * tn, tn)], sem_z)

    def w1_copies(pp, j):
        e = pe_ref[pp]
        slot = j % W_SLOTS
        col = pl.multiple_of(j * tn, tn)
        return (pltpu.make_async_copy(wgu_hbm.at[e, :, pl.ds(col, tn)], wg_buf.at[slot], sem_w.at[slot]),
                pltpu.make_async_copy(wgu_hbm.at[e, :, pl.ds(de + col, tn)], wu_buf.at[slot], sem_w.at[slot]))

    def w2_copy(pp, c):
        slot = c % W_SLOTS
        col = pl.multiple_of(c * tn, tn)
        return pltpu.make_async_copy(wdn_hbm.at[pe_ref[pp], :, pl.ds(col, tn)], wd_buf.at[slot], sem_d.at[slot])

    def start_w1(pp, j):
        for cp in w1_copies(pp, j):
            cp.start()

    def for_subs(count, fn):
        for i in range(PASS_SUBS):
            pl.when(i < count)(lambda i=i: fn(i))

    def for_tail(fn):
        def body(b, carry):
            for c in range(nc):
                fn(b, c)
            return carry
        lax.fori_loop(meta_ref[1], y_hbm.shape[0] // sub, body, 0)

    def for_groups(fn):
        k = 1 << (PASS_SUBS.bit_length() - 1)
        while k:
            start = pl.multiple_of((nsub & (-2 * k)) * sub, sub)
            pl.when((nsub & k) != 0)(lambda start=start, k=k: fn(pl.ds(start, k * sub)))
            k //= 2

    @pl.when(p == 0)
    def _():
        for_subs(nsub, lambda i: x_copy(i, p).start())
        for j in range(W_AHEAD):
            start_w1(p, j)
        zbuf[...] = jnp.zeros_like(zbuf)
        for_tail(lambda b, c: z_copy(b, c).start())

    for_subs(nsub, lambda i: x_copy(i, p).wait())

    def unpack(i):
        rows = slice(i * sub, (i + 1) * sub)
        lo, hi = _unpack_bf16_pair(xraw[rows, :])
        xb[rows, :d2] = lo
        xb[rows, d2:] = hi

    for_subs(nsub, unpack)

    @pl.when(p + 1 < n_pass)
    def _():
        for_subs(pn_ref[p + 1], lambda i: x_copy(i, p + 1).start())

    def first_step(j, carry):
        slot = j % W_SLOTS
        for cp in w1_copies(p, j):
            cp.wait()
        nxt = j + W_AHEAD
        pl.when(nxt < nj)(lambda: start_w1(p, nxt))
        pl.when(nxt >= nj)(lambda: w2_copy(p, nxt - nj).start())

        def group(rows):
            x = xb[rows, :]
            gate = jnp.minimum(_dot(x, wg_buf[slot].astype(BF16)) + bgu_ref[j], SWIGLU_LIMIT)
            up = jnp.clip(_dot(x, wu_buf[slot].astype(BF16)) + bgu_ref[nj + j], -SWIGLU_LIMIT, SWIGLU_LIMIT)
            h_ref[j, rows, :] = ((up + 1.0) * gate * jax.nn.sigmoid(SWIGLU_ALPHA * gate)).astype(BF16)

        for_groups(group)
        return carry

    lax.fori_loop(0, nj, first_step, 0)

    def second_step(c, carry):
        wslot = c % W_SLOTS
        slot = c % 2
        w2_copy(p, c).wait()
        nxt = c + W_AHEAD
        pl.when(nxt < nc)(lambda: w2_copy(p, nxt).start())
        pl.when((nxt >= nc) & (p + 1 < n_pass))(lambda: start_w1(p + 1, nxt - nc))

        @pl.when(c >= 2)
        def _():
            for_subs(nsub, lambda i: y_copy(i, row0, c - 2, slot).wait())

        @pl.when((c < 2) & (p > 0))
        def _():
            for_subs(pn_ref[p - 1], lambda i: y_copy(i, pr_ref[p - 1], nc - 2 + c, slot).wait())

        def group(rows):
            hx = jnp.concatenate([h_ref[j, rows, :] for j in range(nj)], axis=1)
            ybuf[slot, rows, :] = _dot(hx, wd_buf[wslot].astype(BF16)) + bdn_ref[c]

        for_groups(group)
        for_subs(nsub, lambda i: y_copy(i, row0, c, slot).start())
        return carry

    lax.fori_loop(0, nc, second_step, 0)

    @pl.when(p == n_pass - 1)
    def _():
        for c in (nc - 2, nc - 1):
            for_subs(nsub, lambda i, c=c: y_copy(i, row0, c, c % 2).wait())
        for_tail(lambda b, cc: z_copy(b, cc).wait())


def _experts(meta, pass_e, pass_row0, pass_nsub, xs, w_gu, b_gu, w_dn, b_dn):
    n_slots, d2 = xs.shape
    d = 2 * d2
    n_e, de = w_dn.shape[0], w_dn.shape[1]
    tn = EXPERT_TN
    nj = de // tn
    nc = d // tn
    assert W_AHEAD < W_SLOTS and W_AHEAD <= min(nj, nc) and nc % 2 == 0
    rmax = PASS_SUBS * MOE_BLOCK
    return pl.pallas_call(
        _expert_body,
        grid_spec=pltpu.PrefetchScalarGridSpec(
            num_scalar_prefetch=4,
            grid=(meta[0],),
            in_specs=[
                pl.BlockSpec(memory_space=pl.ANY),
                pl.BlockSpec(memory_space=pl.ANY),
                pl.BlockSpec(memory_space=pl.ANY),
                pl.BlockSpec((None, 2 * nj, 1, tn), lambda p, m, e, r, n: (e[p], 0, 0, 0)),
                pl.BlockSpec((None, nc, 1, tn), lambda p, m, e, r, n: (e[p], 0, 0, 0)),
            ],
            out_specs=pl.BlockSpec(memory_space=pl.ANY),
            scratch_shapes=[pltpu.VMEM((rmax, d2), U32),
                            pltpu.VMEM((rmax, d), BF16),
                            pltpu.VMEM((nj, rmax, tn), BF16),
                            pltpu.VMEM((W_SLOTS, d, tn), F32),
                            pltpu.VMEM((W_SLOTS, d, tn), F32),
                            pltpu.VMEM((W_SLOTS, de, tn), F32),
                            pltpu.VMEM((2, rmax, tn), F32),
                            pltpu.VMEM((MOE_BLOCK, tn), F32),
                            pltpu.SemaphoreType.DMA(()),
                            pltpu.SemaphoreType.DMA((2,)),
                            pltpu.SemaphoreType.DMA(()),
                            pltpu.SemaphoreType.DMA((W_SLOTS,)),
                            pltpu.SemaphoreType.DMA((W_SLOTS,))]),
        out_shape=jax.ShapeDtypeStruct((n_slots, d), F32),
        compiler_params=_cparams(1),
        name="experts",
    )(meta, pass_e, pass_row0, pass_nsub, xs, w_gu, w_dn,
      b_gu.reshape(n_e, 2 * nj, 1, tn), b_dn.reshape(n_e, nc, 1, tn))


def _combine_body(dest_ref, wt_ref, h1_ref, g2_ref, l2g_ref, l2b_ref, y_ref, out_ref, buf_ref, sem):
    t = h1_ref.shape[0]

    def issue(tok, carry):
        for k in range(TOP_K):
            d = dest_ref[0, tok * TOP_K + k]
            pltpu.make_async_copy(y_ref.at[pl.ds(d, 1)], buf_ref.at[k, pl.ds(tok, 1)], sem).start()
        return carry

    lax.fori_loop(0, t, issue, 0)
    for k in range(TOP_K):
        pltpu.make_async_copy(y_ref.at[pl.ds(0, t)], buf_ref.at[k], sem).wait()
    wt = wt_ref[...]
    f = buf_ref[0] * wt[:, 0:1]
    for k in range(1, TOP_K):
        f = f + buf_ref[k] * wt[:, k:k + 1]
    out_ref[...] = _layer_norm(DEEPNORM_ALPHA * h1_ref[...] + g2_ref[...] * f, l2g_ref[...], l2b_ref[...])


def _combine(dest, wt, h1, mod3, rows_per_batch, l2g, l2b, y, tm):
    n, d = h1.shape
    dest3 = dest.reshape(n // tm, 1, tm * TOP_K)
    row = lambda i: (i * tm) // rows_per_batch
    return pl.pallas_call(
        _combine_body,
        grid=(n // tm,),
        in_specs=[pl.BlockSpec((None, 1, tm * TOP_K), lambda i: (i, 0, 0), memory_space=pltpu.SMEM),
                  pl.BlockSpec((tm, LANES), lambda i: (i, 0)),
                  pl.BlockSpec((tm, d), lambda i: (i, 0)),
                  pl.BlockSpec((None, 1, d), lambda i: (row(i), 0, 5)),
                  pl.BlockSpec((1, d), lambda i: (0, 0)),
                  pl.BlockSpec((1, d), lambda i: (0, 0)),
                  pl.BlockSpec(memory_space=pl.ANY)],
        out_specs=pl.BlockSpec((tm, d), lambda i: (i, 0)),
        out_shape=jax.ShapeDtypeStruct((n, d), F32),
        scratch_shapes=[pltpu.VMEM((TOP_K, tm, d), F32), pltpu.SemaphoreType.DMA(())],
        compiler_params=_cparams(1),
        name="combine",
    )(dest3, wt, h1, mod3, l2g, l2b, y)


def _pick_tile(n, pref):
    t = pref
    while n % t:
        t //= 2
    return t


def kernel(x, c, ctx, c_ctx, ln_in_g, ln_in_b, w_ada, b_ada, w_in, cm_norm_g, cm_norm_b, cm_w_s, cm_b_s,
           gla_w_gk_f, gla_b_gk_f, gla_w_gk_b, gla_b_gk_b, gla_norm_g, w_out, ln1_g, ln1_b,
           w_router, b_router, w_gate_up, b_gate_up, w_down, b_down, ln2_g, ln2_b):
    bsz, l, d = x.shape
    lc = ctx.shape[1]
    n, nc = bsz * l, bsz * lc
    assert w_ada.shape[0] == 1, "single-layer configuration"
    assert bsz + 1 <= 8 and l % (2 * GLA_TILE) == 0 and lc % GLA_TILE == 0
    row = lambda v: v.reshape(1, -1)

    cc = jnp.concatenate([c, c_ctx[None, :], jnp.zeros((8 - bsz - 1, d), F32)], axis=0)
    mod3 = _ada(cc, w_ada[0], row(b_ada[0])).reshape(8, 1, N_MOD * d)

    w_in_b = w_in[0].astype(BF16)
    n_main = 5 * 1024
    w_main = w_in_b[:, :n_main]
    w_lr = jnp.pad(w_in_b[:, n_main:], ((0, 0), (0, LANES - 2 * GLA_RANK)))
    x2 = x.reshape(n, d)
    tm_x = _pick_tile(l, 512)
    p, lr = _inproj(x2, mod3, lambda i: (i * tm_x) // l, row(ln_in_g), row(ln_in_b), w_main, w_lr, 0, 5, tm_x)
    tm_c = _pick_tile(nc, 512)
    pc, lrc = _inproj(ctx.reshape(nc, d), mod3, lambda i: bsz, row(ln_in_g), row(ln_in_b),
                      w_main, w_lr, 2, 2, tm_c)

    bs_tile = jnp.repeat(cm_b_s[0].T, CM_CHUNK, axis=1)
    cm = _chunk_mlp(p, row(cm_norm_g[0]), row(cm_norm_b[0]), cm_w_s[0].astype(BF16), bs_tile, tm_x)

    kw = GLA_HEADS * GLA_DK
    wgf = jnp.zeros((LANES, kw), BF16).at[:GLA_RANK].set(gla_w_gk_f[0].astype(BF16))
    wgb = jnp.zeros((LANES, kw), BF16).at[GLA_RANK:2 * GLA_RANK].set(gla_w_gk_b[0].astype(BF16))
    gla = _gla(p.reshape(bsz, l, -1), lr.reshape(bsz, l, LANES), pc.reshape(bsz, lc, -1),
               lrc.reshape(bsz, lc, LANES), wgf, row(gla_b_gk_f[0]), wgb, row(gla_b_gk_b[0]),
               row(gla_norm_g[0])).reshape(n, -1)

    w_r = jnp.pad(w_router[0], ((0, 0), (0, LANES - N_EXPERTS))).astype(BF16)
    b_r = jnp.pad(b_router[0], (0, LANES - N_EXPERTS)).reshape(1, LANES)
    h1, hp, logits = _post_attn(cm, gla, x2, mod3, l, row(ln_in_g), row(ln_in_b), w_out[0].astype(BF16),
                                row(ln1_g[0]), row(ln1_b[0]), w_r, b_r, tm_x)

    idx, wt, rank, cnt = _route(logits, _pick_tile(n, 1024))
    counts = cnt[0, :N_EXPERTS].astype(I32)
    n_blocks = (n * TOP_K + N_EXPERTS * (MOE_BLOCK - 1)) // MOE_BLOCK
    blocks_e = (counts + MOE_BLOCK - 1) // MOE_BLOCK
    blk_end = jnp.cumsum(blocks_e)
    blk_start = blk_end - blocks_e
    dest = ((blk_start * MOE_BLOCK)[idx[:, :TOP_K]] + rank[:, :TOP_K]).reshape(-1)
    n_pass_max = n_blocks // PASS_SUBS + N_EXPERTS
    pass_cnt = (blocks_e + PASS_SUBS - 1) // PASS_SUBS
    pass_end = jnp.cumsum(pass_cnt)
    pass_start = pass_end - pass_cnt
    pid = jnp.arange(n_pass_max, dtype=I32)
    pass_e = jnp.minimum(jnp.searchsorted(pass_end, pid, side="right"), N_EXPERTS - 1).astype(I32)
    local = pid - pass_start[pass_e]
    pass_row0 = (blk_start[pass_e] + local * PASS_SUBS).astype(I32)
    pass_nsub = jnp.clip(blocks_e[pass_e] - local * PASS_SUBS, 0, PASS_SUBS).astype(I32)
    meta = jnp.stack([pass_end[-1], blk_end[-1]]).astype(I32)

    xs = _dispatch(dest, hp, n_blocks * MOE_BLOCK, _pick_tile(n, 512))
    ys = _experts(meta, pass_e, pass_row0, pass_nsub, xs, w_gate_up[0], b_gate_up[0], w_down[0], b_down[0])
    out = _combine(dest, wt, h1, mod3, l, row(ln2_g[0]), row(ln2_b[0]), ys, _pick_tile(n, 256))
    return out.reshape(bsz, l, d)
```

```python
import math

import jax
import jax.numpy as jnp
from jax import lax
from jax.experimental import pallas as pl
from jax.experimental.pallas import tpu as pltpu

F32 = jnp.float32
BF16 = jnp.bfloat16
U32 = jnp.uint32
I32 = jnp.int32

CM_CHUNK = 128
CM_HEADS = 8
GLA_HEADS = 4
GLA_DK = 128
GLA_DV = 256
GLA_CHUNK = 64
GLA_RANK = 16
GLA_GATE_NORMALIZER = 16.0
N_EXPERTS = 32
TOP_K = 4
MOE_BLOCK = 256
SWIGLU_LIMIT = 7.0
SWIGLU_ALPHA = 1.702
N_MOD = 6
DEEPNORM_ALPHA = 2.0 ** 0.25
LN_EPS = 1e-5
RMS_EPS = 1e-6

LANES = 128
VMEM_LIMIT = 56 * 1024 * 1024


def _cparams(n_axes, vmem=VMEM_LIMIT):
    return pltpu.CompilerParams(dimension_semantics=("arbitrary",) * n_axes,
                                vmem_limit_bytes=vmem)


def _layer_norm(t, g, b):
    mu = jnp.mean(t, axis=-1, keepdims=True)
    d = t - mu
    var = jnp.mean(d * d, axis=-1, keepdims=True)
    return d * lax.rsqrt(var + LN_EPS) * g + b


def _gelu(t):
    return 0.5 * t * (1.0 + lax.erf(t * (1.0 / math.sqrt(2.0))))


def _silu(t):
    return t * jax.nn.sigmoid(t)


def _ones_where(mask, dtype):
    return jnp.where(mask, 1.0, 0.0).astype(dtype)


def _dot(a, b):
    return jnp.dot(a, b, preferred_element_type=F32)


def _dot_nt(a, b):
    return lax.dot_general(a, b, (((1,), (1,)), ((), ())), preferred_element_type=F32)


def _dot_tn(a, b):
    return lax.dot_general(a, b, (((0,), (0,)), ((), ())), preferred_element_type=F32)


def _ada_body(c_ref, w_ref, b_ref, o_ref):
    a = _silu(c_ref[...]).astype(BF16)
    o_ref[...] = _dot(a, w_ref[...].astype(BF16)) + b_ref[...]


def _ada(cc, w, b):
    rows, d = cc.shape
    n = w.shape[1]
    tn = 1024
    return pl.pallas_call(
        _ada_body,
        grid=(n // tn,),
        in_specs=[pl.BlockSpec((rows, d), lambda j: (0, 0)),
                  pl.BlockSpec((d, tn), lambda j: (0, j)),
                  pl.BlockSpec((1, tn), lambda j: (0, j))],
        out_specs=pl.BlockSpec((rows, tn), lambda j: (0, j)),
        out_shape=jax.ShapeDtypeStruct((rows, n), F32),
        compiler_params=_cparams(1),
        name="ada",
    )(cc, w, b)


def _inproj_body(x_ref, g_ref, b_ref, sh_ref, sc_ref, w_ref, wlr_ref, o_ref, olr_ref, hm_ref):
    @pl.when(pl.program_id(1) == 0)
    def _():
        h = _layer_norm(x_ref[...], g_ref[...], b_ref[...])
        hm = (h * (1.0 + sc_ref[...]) + sh_ref[...]).astype(BF16)
        hm_ref[...] = hm
        olr_ref[...] = _dot(hm, wlr_ref[...])

    o_ref[...] = _dot(hm_ref[...], w_ref[...])


def _inproj(x2, mod3, mod_row, ln_g, ln_b, w_main, w_lr, col0, ncols, tm, tn):
    r, d = x2.shape
    return pl.pallas_call(
        _inproj_body,
        grid=(r // tm, ncols),
        in_specs=[pl.BlockSpec((tm, d), lambda i, j: (i, 0)),
                  pl.BlockSpec((1, d), lambda i, j: (0, 0)),
                  pl.BlockSpec((1, d), lambda i, j: (0, 0)),
                  pl.BlockSpec((None, 1, d), lambda i, j: (mod_row(i), 0, 0)),
                  pl.BlockSpec((None, 1, d), lambda i, j: (mod_row(i), 0, 1)),
                  pl.BlockSpec((d, tn), lambda i, j: (0, col0 + j)),
                  pl.BlockSpec((d, LANES), lambda i, j: (0, 0))],
        out_specs=[pl.BlockSpec((tm, tn), lambda i, j: (i, j)),
                   pl.BlockSpec((tm, LANES), lambda i, j: (i, 0))],
        out_shape=[jax.ShapeDtypeStruct((r, ncols * tn), F32),
                   jax.ShapeDtypeStruct((r, LANES), F32)],
        scratch_shapes=[pltpu.VMEM((tm, d), BF16)],
        compiler_params=_cparams(2),
        name="inproj",
    )(x2, ln_g, ln_b, mod3, mod3, w_main, w_lr)


def _cm_body(u_ref, v_ref, ng_ref, nb_ref, ws_ref, bs_ref, o_ref):
    tm = u_ref.shape[0]
    u = _gelu(u_ref[...])
    v = _gelu(v_ref[...])
    vb = _layer_norm(v, ng_ref[...], nb_ref[...]).astype(BF16)
    hd = CM_CHUNK
    for c in range(tm // CM_CHUNK):
        rows = slice(c * CM_CHUNK, (c + 1) * CM_CHUNK)
        for h in range(CM_HEADS):
            cols = slice(h * hd, (h + 1) * hd)
            s = _dot(ws_ref[h], vb[rows, cols]) + bs_ref[:, cols]
            o_ref[rows, cols] = (u[rows, cols] * s).astype(BF16)


def _chunk_mlp(p, ng, nb, ws, bs, tm):
    r = p.shape[0]
    w = CM_HEADS * CM_CHUNK
    return pl.pallas_call(
        _cm_body,
        grid=(r // tm,),
        in_specs=[pl.BlockSpec((tm, w), lambda i: (i, 0)),
                  pl.BlockSpec((tm, w), lambda i: (i, 1)),
                  pl.BlockSpec((1, w), lambda i: (0, 0)),
                  pl.BlockSpec((1, w), lambda i: (0, 0)),
                  pl.BlockSpec((CM_HEADS, CM_CHUNK, CM_CHUNK), lambda i: (0, 0, 0)),
                  pl.BlockSpec((CM_CHUNK, w), lambda i: (0, 0))],
        out_specs=pl.BlockSpec((tm, w), lambda i: (i, 0)),
        out_shape=jax.ShapeDtypeStruct((r, w), BF16),
        compiler_params=_cparams(1),
        name="chunk_mlp",
    )(p, p, ng, nb, ws, bs)


GLA_TILE = 256


def _gla_tile(q, k, v, lr, wg, bg, st_ref, forward, need_o):
    t = k.shape[0]
    n_chunks = t // GLA_CHUNK
    z = _dot(lr.astype(BF16), wg) + bg
    g = jax.nn.log_sigmoid(z) * (1.0 / GLA_GATE_NORMALIZER)
    r_id = lax.broadcasted_iota(I32, (t, t), 0)
    c_id = lax.broadcasted_iota(I32, (t, t), 1)
    shift = GLA_CHUNK.bit_length() - 1
    same = (r_id >> shift) == (c_id >> shift)
    lower = same & (c_id <= r_id)
    tri = _ones_where(lower, BF16)
    g_hi = g.astype(BF16)
    g_lo = (g - g_hi.astype(F32)).astype(BF16)
    csum = _dot(tri, g_hi) + _dot(tri, g_lo)
    g3 = g.reshape(n_chunks, GLA_CHUNK, GLA_DK)
    tot = jnp.broadcast_to(jnp.sum(g3, axis=1, keepdims=True), g3.shape).reshape(t, GLA_DK)
    bcum = csum if forward else tot - csum + g
    kd = (k * jnp.exp(tot - bcum)).astype(BF16)
    decay = jnp.exp(tot)
    vb = v.astype(BF16)
    o = None
    if need_o:
        qe = ((q * (GLA_DK ** -0.5)) * jnp.exp(bcum)).astype(BF16)
        ke = (k * jnp.exp(-bcum)).astype(BF16)
        att = _dot_nt(qe, ke)
        mask = lower if forward else same & (c_id >= r_id)
        att = jnp.where(mask, att, 0.0).astype(BF16)
        o = _dot(att, vb)
    outs = [None] * n_chunks
    order = range(n_chunks) if forward else range(n_chunks - 1, -1, -1)
    for c in order:
        rows = slice(c * GLA_CHUNK, (c + 1) * GLA_CHUNK)
        s_t = st_ref[...]
        if need_o:
            outs[c] = o[rows] + _dot_nt(qe[rows], s_t.astype(BF16))
        u_t = _dot_tn(vb[rows], kd[rows])
        st_ref[...] = s_t * decay[c * GLA_CHUNK:c * GLA_CHUNK + 1, :] + u_t
    if need_o:
        return jnp.concatenate(outs, axis=0)
    return None


def _gla_body(q_ref, k_ref, v_ref, go_ref, lr_ref, kc_ref, vc_ref, lrc_ref,
              wgf_ref, bgf_ref, wgb_ref, bgb_ref, ng_ref, out_ref, o_scr, sf_ref, sb_ref):
    t = GLA_TILE
    n_x = q_ref.shape[0] // t
    n_c = kc_ref.shape[0] // t
    half = n_x // 2
    sf_ref[...] = jnp.zeros_like(sf_ref)
    sb_ref[...] = jnp.zeros_like(sb_ref)
    wgf, bgf, wgb, bgb = wgf_ref[...], bgf_ref[...], wgb_ref[...], bgb_ref[...]

    for i in range(n_c):
        rf = slice(i * t, (i + 1) * t)
        rb = slice((n_c - 1 - i) * t, (n_c - i) * t)
        _gla_tile(None, kc_ref[rf], vc_ref[rf], lrc_ref[rf], wgf, bgf, sf_ref, True, False)
        _gla_tile(None, kc_ref[rb], vc_ref[rb], lrc_ref[rb], wgb, bgb, sb_ref, False, False)

    def tile_out(i, forward):
        rows = pl.ds(pl.multiple_of(i * t, t), t)
        if forward:
            return rows, _gla_tile(q_ref[rows], k_ref[rows], v_ref[rows], lr_ref[rows],
                                   wgf, bgf, sf_ref, True, True)
        return rows, _gla_tile(q_ref[rows], k_ref[rows], v_ref[rows], lr_ref[rows],
                               wgb, bgb, sb_ref, False, True)

    def first_half(i, carry):
        rows, o = tile_out(i, True)
        o_scr[rows] = o
        rows, o = tile_out(n_x - 1 - i, False)
        o_scr[rows] = o
        return carry

    def finish(rows, o):
        o = o + o_scr[rows]
        ms = jnp.mean(o * o, axis=-1, keepdims=True)
        on = o * lax.rsqrt(ms + RMS_EPS) * ng_ref[...]
        out_ref[rows] = (on * _silu(go_ref[rows])).astype(BF16)

    def second_half(i, carry):
        rows, o = tile_out(i, True)
        finish(rows, o)
        rows, o = tile_out(n_x - 1 - i, False)
        finish(rows, o)
        return carry

    lax.fori_loop(0, half, first_half, 0)
    lax.fori_loop(half, n_x, second_half, 0)


def _gla(p3, lr3, pc3, lrc3, wgf, bgf, wgb, bgb, ng):
    bsz, l, _ = p3.shape
    lc = pc3.shape[1]
    dk, dv = GLA_DK, GLA_DV
    q0, k0 = 2048 // dk, 2560 // dk
    v0, go0 = 3072 // dv, 4096 // dv
    kc0, vc0 = 512 // dk, 1024 // dv
    return pl.pallas_call(
        _gla_body,
        grid=(bsz, GLA_HEADS),
        in_specs=[pl.BlockSpec((None, l, dk), lambda b, h: (b, 0, q0 + h)),
                  pl.BlockSpec((None, l, dk), lambda b, h: (b, 0, k0 + h)),
                  pl.BlockSpec((None, l, dv), lambda b, h: (b, 0, v0 + h)),
                  pl.BlockSpec((None, l, dv), lambda b, h: (b, 0, go0 + h)),
                  pl.BlockSpec((None, l, LANES), lambda b, h: (b, 0, 0)),
                  pl.BlockSpec((None, lc, dk), lambda b, h: (b, 0, kc0 + h)),
                  pl.BlockSpec((None, lc, dv), lambda b, h: (b, 0, vc0 + h)),
                  pl.BlockSpec((None, lc, LANES), lambda b, h: (b, 0, 0)),
                  pl.BlockSpec((LANES, dk), lambda b, h: (0, h)),
                  pl.BlockSpec((1, dk), lambda b, h: (0, h)),
                  pl.BlockSpec((LANES, dk), lambda b, h: (0, h)),
                  pl.BlockSpec((1, dk), lambda b, h: (0, h)),
                  pl.BlockSpec((1, dv), lambda b, h: (0, 0))],
        out_specs=pl.BlockSpec((None, l, dv), lambda b, h: (b, 0, h)),
        out_shape=jax.ShapeDtypeStruct((bsz, l, GLA_HEADS * dv), BF16),
        scratch_shapes=[pltpu.VMEM((l, dv), F32),
                        pltpu.VMEM((dv, dk), F32),
                        pltpu.VMEM((dv, dk), F32)],
        compiler_params=_cparams(2),
        name="gla",
    )(p3, p3, p3, p3, lr3, pc3, pc3, lrc3, wgf, bgf, wgb, bgb, ng)


def _pack_bf16_pair(lo, hi):
    lo_b = lax.bitcast_convert_type(lo.astype(BF16).astype(F32), U32)
    hi_b = lax.bitcast_convert_type(hi.astype(BF16).astype(F32), U32)
    return hi_b | (lo_b >> 16)


def _unpack_bf16_pair(p):
    lo = lax.bitcast_convert_type(p << 16, F32).astype(BF16)
    hi = lax.bitcast_convert_type(p & jnp.uint32(0xFFFF0000), F32).astype(BF16)
    return lo, hi


def _post_body(cm_ref, gla_ref, x_ref, lng_ref, lnb_ref, g1_ref, sh2_ref, sc2_ref,
               wo_ref, l1g_ref, l1b_ref, wr_ref, br_ref, h1_ref, hp_ref, lg_ref):
    half = cm_ref.shape[1]
    y = _dot(cm_ref[...], wo_ref[:half, :]) + _dot(gla_ref[...], wo_ref[half:, :])
    hx = _layer_norm(x_ref[...], lng_ref[...], lnb_ref[...])
    h1 = _layer_norm(DEEPNORM_ALPHA * hx + g1_ref[...] * y, l1g_ref[...], l1b_ref[...])
    h1_ref[...] = h1
    hm = h1 * (1.0 + sc2_ref[...]) + sh2_ref[...]
    d2 = hm.shape[1] // 2
    hp_ref[...] = _pack_bf16_pair(hm[:, :d2], hm[:, d2:])
    lg_ref[...] = _dot(hm.astype(BF16), wr_ref[...]) + br_ref[...]


def _post_attn(cm, gla, x2, mod3, rows_per_batch, ln_g, ln_b, w_out, l1g, l1b, w_r, b_r, tm):
    r, d = x2.shape
    half = d // 2
    row = lambda i: (i * tm) // rows_per_batch
    full = lambda shape: pl.BlockSpec(shape, lambda i: (0,) * len(shape))
    return pl.pallas_call(
        _post_body,
        grid=(r // tm,),
        in_specs=[pl.BlockSpec((tm, half), lambda i: (i, 0)),
                  pl.BlockSpec((tm, half), lambda i: (i, 0)),
                  pl.BlockSpec((tm, d), lambda i: (i, 0)),
                  full((1, d)), full((1, d)),
                  pl.BlockSpec((None, 1, d), lambda i: (row(i), 0, 2)),
                  pl.BlockSpec((None, 1, d), lambda i: (row(i), 0, 3)),
                  pl.BlockSpec((None, 1, d), lambda i: (row(i), 0, 4)),
                  full((d, d)), full((1, d)), full((1, d)),
                  full((d, LANES)), full((1, LANES))],
        out_specs=[pl.BlockSpec((tm, d), lambda i: (i, 0)),
                   pl.BlockSpec((tm, half), lambda i: (i, 0)),
                   pl.BlockSpec((tm, LANES), lambda i: (i, 0))],
        out_shape=[jax.ShapeDtypeStruct((r, d), F32),
                   jax.ShapeDtypeStruct((r, half), U32),
                   jax.ShapeDtypeStruct((r, LANES), F32)],
        compiler_params=_cparams(1),
        name="post_attn",
    )(cm, gla, x2, ln_g, ln_b, mod3, mod3, mod3, w_out, l1g, l1b, w_r, b_r)


def _route_body(lg_ref, idx_ref, wt_ref, rank_ref, cnt_ref, carry_ref):
    i = pl.program_id(0)
    t = lg_ref.shape[0]

    @pl.when(i == 0)
    def _():
        carry_ref[...] = jnp.zeros_like(carry_ref)

    lane = lax.broadcasted_iota(I32, (t, LANES), 1)
    lane_f = lane.astype(F32)
    neg = jnp.float32(-jnp.inf)
    l = jnp.where(lane < N_EXPERTS, lg_ref[...], neg)
    tops, onehots, idxs = [], [], []
    for _ in range(TOP_K):
        m = jnp.max(l, axis=-1, keepdims=True)
        idx = jnp.min(jnp.where(l == m, lane_f, float(LANES)), axis=-1, keepdims=True).astype(I32)
        oh = lane == idx
        l = jnp.where(oh, neg, l)
        tops.append(m)
        idxs.append(idx)
        onehots.append(oh)
    exps = [jnp.exp(m - tops[0]) for m in tops]
    denom = exps[0] + exps[1] + exps[2] + exps[3]
    sel = _ones_where(onehots[0] | onehots[1] | onehots[2] | onehots[3], F32)
    r_id = lax.broadcasted_iota(I32, (t, t), 0)
    c_id = lax.broadcasted_iota(I32, (t, t), 1)
    strict = _ones_where(c_id < r_id, BF16)
    before = _dot(strict, sel.astype(BF16)) + carry_ref[0:1, :]
    idx_out = jnp.zeros((t, LANES), I32)
    wt_out = jnp.zeros((t, LANES), F32)
    rank_out = jnp.zeros((t, LANES), I32)
    for k in range(TOP_K):
        rk = jnp.sum(jnp.where(onehots[k], before, 0.0), axis=-1, keepdims=True).astype(I32)
        idx_out = jnp.where(lane == k, idxs[k], idx_out)
        wt_out = jnp.where(lane == k, exps[k] / denom, wt_out)
        rank_out = jnp.where(lane == k, rk, rank_out)
    idx_ref[...] = idx_out
    wt_ref[...] = wt_out
    rank_ref[...] = rank_out
    total = carry_ref[0:1, :] + jnp.sum(sel, axis=0, keepdims=True)
    carry_ref[...] = jnp.broadcast_to(total, carry_ref.shape)
    cnt_ref[...] = jnp.broadcast_to(total, cnt_ref.shape)


def _route(logits, tm):
    n = logits.shape[0]
    blk = pl.BlockSpec((tm, LANES), lambda i: (i, 0))
    return pl.pallas_call(
        _route_body,
        grid=(n // tm,),
        in_specs=[blk],
        out_specs=[blk, blk, blk, pl.BlockSpec((8, LANES), lambda i: (0, 0))],
        out_shape=[jax.ShapeDtypeStruct((n, LANES), I32),
                   jax.ShapeDtypeStruct((n, LANES), F32),
                   jax.ShapeDtypeStruct((n, LANES), I32),
                   jax.ShapeDtypeStruct((8, LANES), F32)],
        scratch_shapes=[pltpu.VMEM((8, LANES), F32)],
        compiler_params=_cparams(1),
        name="route",
    )(logits)


def _dispatch_body(nz_ref, zl_ref, dest_ref, hp_ref, xs_ref, zero_ref, sem, zsem):
    t = hp_ref.shape[0]
    unroll = 4

    @pl.when(pl.program_id(0) == 0)
    def _():
        zero_ref[...] = jnp.zeros_like(zero_ref)

        def z_copy(b):
            r = pl.multiple_of(zl_ref[b] * MOE_BLOCK, MOE_BLOCK)
            return pltpu.make_async_copy(zero_ref, xs_ref.at[pl.ds(r, MOE_BLOCK)], zsem)

        def start(b, carry):
            z_copy(b).start()
            return carry

        def wait(b, carry):
            z_copy(b).wait()
            return carry

        lax.fori_loop(0, nz_ref[0], start, 0)
        lax.fori_loop(0, nz_ref[0], wait, 0)

    def issue(it, carry):
        for u in range(unroll):
            tok = it * unroll + u
            for k in range(TOP_K):
                d = dest_ref[0, tok * TOP_K + k]
                pltpu.make_async_copy(hp_ref.at[pl.ds(tok, 1)], xs_ref.at[pl.ds(d, 1)], sem).start()
        return carry

    lax.fori_loop(0, t // unroll, issue, 0)
    for _ in range(TOP_K):
        pltpu.make_async_copy(hp_ref, xs_ref.at[pl.ds(0, t)], sem).wait()


def _dispatch(n_zero, zero_blocks, dest, hp, n_slots, tm):
    n, w = hp.shape
    dest3 = dest.reshape(n // tm, 1, tm * TOP_K)
    return pl.pallas_call(
        _dispatch_body,
        grid_spec=pltpu.PrefetchScalarGridSpec(
            num_scalar_prefetch=2,
            grid=(n // tm,),
            in_specs=[pl.BlockSpec((None, 1, tm * TOP_K), lambda i, nz, zl: (i, 0, 0), memory_space=pltpu.SMEM),
                      pl.BlockSpec((tm, w), lambda i, nz, zl: (i, 0))],
            out_specs=pl.BlockSpec(memory_space=pl.ANY),
            scratch_shapes=[pltpu.VMEM((MOE_BLOCK, w), U32),
                            pltpu.SemaphoreType.DMA(()),
                            pltpu.SemaphoreType.DMA(())]),
        out_shape=jax.ShapeDtypeStruct((n_slots, w), U32),
        compiler_params=_cparams(1),
        name="dispatch",
    )(n_zero, zero_blocks, dest3, hp)


PASS_SUBS = 9
EXPERT_TN = 256


def _expert_body_grid_tiles(meta_ref, pe_ref, pr_ref, pn_ref, xs_hbm, wg_ref, wu_ref, bg_ref, bu_ref, wd_ref, bd_ref,
                            y_hbm, xraw, xb, h_ref, ybuf, zbuf, sem_x, sem_y, sem_z):
    del pe_ref
    sub = MOE_BLOCK
    tn = EXPERT_TN
    nj = h_ref.shape[0]
    nc = y_hbm.shape[1] // tn
    d2 = xraw.shape[1]
    p = pl.program_id(0)
    s = pl.program_id(1)
    n_pass = meta_ref[0]
    nsub = pn_ref[p]
    row0 = pr_ref[p]

    def x_copy(i, pp):
        r = pl.multiple_of((pr_ref[pp] + i) * sub, sub)
        return pltpu.make_async_copy(xs_hbm.at[pl.ds(r, sub)], xraw.at[pl.ds(i * sub, sub)], sem_x)

    def y_copy(i, r0, c, slot):
        r = pl.multiple_of((r0 + i) * sub, sub)
        col = pl.multiple_of(c * tn, tn)
        return pltpu.make_async_copy(ybuf.at[slot, pl.ds(i * sub, sub), :],
                                     y_hbm.at[pl.ds(r, sub), pl.ds(col, tn)], sem_y.at[slot])

    def z_copy(b, c):
        r = pl.multiple_of(b * sub, sub)
        return pltpu.make_async_copy(zbuf, y_hbm.at[pl.ds(r, sub), pl.ds(c * tn, tn)], sem_z)

    def for_subs(count, fn):
        for i in range(PASS_SUBS):
            pl.when(i < count)(lambda i=i: fn(i))

    def for_tail(fn):
        def body(b, carry):
            for c in range(nc):
                fn(b, c)
            return carry
        lax.fori_loop(meta_ref[1], y_hbm.shape[0] // sub, body, 0)

    @pl.when(s == 0)
    def _():
        @pl.when(p == 0)
        def _():
            for_subs(nsub, lambda i: x_copy(i, p).start())
            zbuf[...] = jnp.zeros_like(zbuf)
            for_tail(lambda b, c: z_copy(b, c).start())

        for_subs(nsub, lambda i: x_copy(i, p).wait())

    @pl.when((s == 1) & (p + 1 < n_pass))
    def _():
        for_subs(pn_ref[p + 1], lambda i: x_copy(i, p + 1).start())

    def for_groups(fn):
        k = 1 << (PASS_SUBS.bit_length() - 1)
        while k:
            start = pl.multiple_of((nsub & (-2 * k)) * sub, sub)
            pl.when((nsub & k) != 0)(lambda start=start, k=k: fn(pl.ds(start, k * sub)))
            k //= 2

    @pl.when(s == 0)
    def _():
        def unpack(i):
            rows = slice(i * sub, (i + 1) * sub)
            lo, hi = _unpack_bf16_pair(xraw[rows, :])
            xb[rows, :d2] = lo
            xb[rows, d2:] = hi

        for_subs(nsub, unpack)

    @pl.when(s < nj)
    def _():
        def group(rows):
            x = xb[rows, :]
            gate = jnp.minimum(_dot(x, wg_ref[...].astype(BF16)) + bg_ref[...], SWIGLU_LIMIT)
            up = jnp.clip(_dot(x, wu_ref[...].astype(BF16)) + bu_ref[...], -SWIGLU_LIMIT, SWIGLU_LIMIT)
            h_ref[s, rows, :] = ((up + 1.0) * gate * jax.nn.sigmoid(SWIGLU_ALPHA * gate)).astype(BF16)

        for_groups(group)

    @pl.when(s >= nj)
    def _():
        c = s - nj
        slot = c % 2

        @pl.when(c >= 2)
        def _():
            for_subs(nsub, lambda i: y_copy(i, row0, c - 2, slot).wait())

        @pl.when((c < 2) & (p > 0))
        def _():
            for_subs(pn_ref[p - 1], lambda i: y_copy(i, pr_ref[p - 1], nc - 2 + c, slot).wait())

        def group(rows):
            hx = jnp.concatenate([h_ref[j, rows, :] for j in range(nj)], axis=1)
            ybuf[slot, rows, :] = _dot(hx, wd_ref[...].astype(BF16)) + bd_ref[...]

        for_groups(group)
        for_subs(nsub, lambda i: y_copy(i, row0, c, slot).start())

        @pl.when((p == n_pass - 1) & (c == nc - 1))
        def _():
            for_subs(nsub, lambda i: y_copy(i, row0, c - 1, 1 - slot).wait())
            for_subs(nsub, lambda i: y_copy(i, row0, c, slot).wait())
            for_tail(lambda b, cc: z_copy(b, cc).wait())


def _experts_grid_tiles(meta, pass_e, pass_row0, pass_nsub, xs, w_gu, b_gu3, w_dn, b_dn3):
    n_slots, d2 = xs.shape
    d = 2 * d2
    de = w_dn.shape[1]
    tn = EXPERT_TN
    nj = de // tn
    nc = d // tn
    rmax = PASS_SUBS * MOE_BLOCK

    def first(s):
        return jnp.minimum(s, nj - 1)

    def second(s):
        return jnp.maximum(s - nj, 0)

    return pl.pallas_call(
        _expert_body,
        grid_spec=pltpu.PrefetchScalarGridSpec(
            num_scalar_prefetch=4,
            grid=(meta[0], nj + nc),
            in_specs=[
                pl.BlockSpec(memory_space=pl.ANY),
                pl.BlockSpec((None, d, tn), lambda p, s, m, e, r, n: (e[p], 0, first(s))),
                pl.BlockSpec((None, d, tn), lambda p, s, m, e, r, n: (e[p], 0, nj + first(s))),
                pl.BlockSpec((None, 1, tn), lambda p, s, m, e, r, n: (e[p], 0, first(s))),
                pl.BlockSpec((None, 1, tn), lambda p, s, m, e, r, n: (e[p], 0, nj + first(s))),
                pl.BlockSpec((None, de, tn), lambda p, s, m, e, r, n: (e[p], 0, second(s))),
                pl.BlockSpec((None, 1, tn), lambda p, s, m, e, r, n: (e[p], 0, second(s))),
            ],
            out_specs=pl.BlockSpec(memory_space=pl.ANY),
            scratch_shapes=[pltpu.VMEM((rmax, d2), U32),
                            pltpu.VMEM((rmax, d), BF16),
                            pltpu.VMEM((nj, rmax, tn), BF16),
                            pltpu.VMEM((2, rmax, tn), F32),
                            pltpu.VMEM((MOE_BLOCK, tn), F32),
                            pltpu.SemaphoreType.DMA(()),
                            pltpu.SemaphoreType.DMA((2,)),
                            pltpu.SemaphoreType.DMA(())]),
        out_shape=jax.ShapeDtypeStruct((n_slots, d), F32),
        compiler_params=_cparams(2),
        name="experts",
    )(meta, pass_e, pass_row0, pass_nsub, xs, w_gu, w_gu, b_gu3, b_gu3, w_dn, b_dn3)


W_SLOTS = 3
W_AHEAD = 2


def _expert_body(meta_ref, pe_ref, pr_ref, pn_ref, xs_hbm, wgu_hbm, wdn_hbm, bgu_ref, bdn_ref,
                 y_hbm, xraw, xb, h_ref, wg_buf, wu_buf, wd_buf, ybuf, zbuf,
                 sem_x, sem_y, sem_z, sem_w, sem_d):
    sub = MOE_BLOCK
    tn = EXPERT_TN
    nj = h_ref.shape[0]
    nc = y_hbm.shape[1] // tn
    de = nj * tn
    d2 = xraw.shape[1]
    p = pl.program_id(0)
    n_pass = meta_ref[0]
    nsub = pn_ref[p]
    row0 = pr_ref[p]

    def x_copy(i, pp):
        r = pl.multiple_of((pr_ref[pp] + i) * sub, sub)
        return pltpu.make_async_copy(xs_hbm.at[pl.ds(r, sub)], xraw.at[pl.ds(i * sub, sub)], sem_x)

    def y_copy(i, r0, c, slot):
        r = pl.multiple_of((r0 + i) * sub, sub)
        col = pl.multiple_of(c * tn, tn)
        return pltpu.make_async_copy(ybuf.at[slot, pl.ds(i * sub, sub), :],
                                     y_hbm.at[pl.ds(r, sub), pl.ds(col, tn)], sem_y.at[slot])

    def z_copy(b, c):
        r = pl.multiple_of(b * sub, sub)
        return pltpu.make_async_copy(zbuf, y_hbm.at[pl.ds(r, sub), pl.ds(c * tn, tn)], sem_z)

    def w1_copies(pp, j):
        e = pe_ref[pp]
        slot = j % W_SLOTS
        col = pl.multiple_of(j * tn, tn)
        return (pltpu.make_async_copy(wgu_hbm.at[e, :, pl.ds(col, tn)], wg_buf.at[slot], sem_w.at[slot]),
                pltpu.make_async_copy(wgu_hbm.at[e, :, pl.ds(de + col, tn)], wu_buf.at[slot], sem_w.at[slot]))

    def w2_copy(pp, c):
        slot = c % W_SLOTS
        col = pl.multiple_of(c * tn, tn)
        return pltpu.make_async_copy(wdn_hbm.at[pe_ref[pp], :, pl.ds(col, tn)], wd_buf.at[slot], sem_d.at[slot])

    def start_w1(pp, j):
        for cp in w1_copies(pp, j):
            cp.start()

    def for_subs(count, fn):
        for i in range(PASS_SUBS):
            pl.when(i < count)(lambda i=i: fn(i))

    def for_tail(fn):
        def body(b, carry):
            for c in range(nc):
                fn(b, c)
            return carry
        lax.fori_loop(meta_ref[1], y_hbm.shape[0] // sub, body, 0)

    def for_groups(fn):
        k = 1 << (PASS_SUBS.bit_length() - 1)
        while k:
            start = pl.multiple_of((nsub & (-2 * k)) * sub, sub)
            pl.when((nsub & k) != 0)(lambda start=start, k=k: fn(pl.ds(start, k * sub)))
            k //= 2

    @pl.when(p == 0)
    def _():
        for_subs(nsub, lambda i: x_copy(i, p).start())
        for j in range(W_AHEAD):
            start_w1(p, j)
        zbuf[...] = jnp.zeros_like(zbuf)
        for_tail(lambda b, c: z_copy(b, c).start())

    for_subs(nsub, lambda i: x_copy(i, p).wait())

    def unpack(i):
        rows = slice(i * sub, (i + 1) * sub)
        lo, hi = _unpack_bf16_pair(xraw[rows, :])
        xb[rows, :d2] = lo
        xb[rows, d2:] = hi

    for_subs(nsub, unpack)

    @pl.when(p + 1 < n_pass)
    def _():
        for_subs(pn_ref[p + 1], lambda i: x_copy(i, p + 1).start())

    def first_step(j, carry):
        slot = j % W_SLOTS
        for cp in w1_copies(p, j):
            cp.wait()
        nxt = j + W_AHEAD
        pl.when(nxt < nj)(lambda: start_w1(p, nxt))
        pl.when(nxt >= nj)(lambda: w2_copy(p, nxt - nj).start())

        def group(rows):
            x = xb[rows, :]
            gate = jnp.minimum(_dot(x, wg_buf[slot].astype(BF16)) + bgu_ref[j], SWIGLU_LIMIT)
            up = jnp.clip(_dot(x, wu_buf[slot].astype(BF16)) + bgu_ref[nj + j], -SWIGLU_LIMIT, SWIGLU_LIMIT)
            h_ref[j, rows, :] = ((up + 1.0) * gate * jax.nn.sigmoid(SWIGLU_ALPHA * gate)).astype(BF16)

        for_groups(group)
        return carry

    lax.fori_loop(0, nj, first_step, 0)

    def second_step(c, carry):
        wslot = c % W_SLOTS
        slot = c % 2
        w2_copy(p, c).wait()
        nxt = c + W_AHEAD
        pl.when(nxt < nc)(lambda: w2_copy(p, nxt).start())
        pl.when((nxt >= nc) & (p + 1 < n_pass))(lambda: start_w1(p + 1, nxt - nc))

        @pl.when(c >= 2)
        def _():
            for_subs(nsub, lambda i: y_copy(i, row0, c - 2, slot).wait())

        @pl.when((c < 2) & (p > 0))
        def _():
            for_subs(pn_ref[p - 1], lambda i: y_copy(i, pr_ref[p - 1], nc - 2 + c, slot).wait())

        def group(rows):
            hx = jnp.concatenate([h_ref[j, rows, :] for j in range(nj)], axis=1)
            ybuf[slot, rows, :] = _dot(hx, wd_buf[wslot].astype(BF16)) + bdn_ref[c]

        for_groups(group)
        for_subs(nsub, lambda i: y_copy(i, row0, c, slot).start())
        return carry

    lax.fori_loop(0, nc, second_step, 0)

    @pl.when(p == n_pass - 1)
    def _():
        for c in (nc - 2, nc - 1):
            for_subs(nsub, lambda i, c=c: y_copy(i, row0, c, c % 2).wait())
        for_tail(lambda b, cc: z_copy(b, cc).wait())


def _experts(meta, pass_e, pass_row0, pass_nsub, xs, w_gu, b_gu, w_dn, b_dn):
    n_slots, d2 = xs.shape
    d = 2 * d2
    n_e, de = w_dn.shape[0], w_dn.shape[1]
    tn = EXPERT_TN
    nj = de // tn
    nc = d // tn
    assert W_AHEAD < W_SLOTS and W_AHEAD <= min(nj, nc) and nc % 2 == 0
    rmax = PASS_SUBS * MOE_BLOCK
    return pl.pallas_call(
        _expert_body,
        grid_spec=pltpu.PrefetchScalarGridSpec(
            num_scalar_prefetch=4,
            grid=(meta[0],),
            in_specs=[
                pl.BlockSpec(memory_space=pl.ANY),
                pl.BlockSpec(memory_space=pl.ANY),
                pl.BlockSpec(memory_space=pl.ANY),
                pl.BlockSpec((None, 2 * nj, 1, tn), lambda p, m, e, r, n: (e[p], 0, 0, 0)),
                pl.BlockSpec((None, nc, 1, tn), lambda p, m, e, r, n: (e[p], 0, 0, 0)),
            ],
            out_specs=pl.BlockSpec(memory_space=pl.ANY),
            scratch_shapes=[pltpu.VMEM((rmax, d2), U32),
                            pltpu.VMEM((rmax, d), BF16),
                            pltpu.VMEM((nj, rmax, tn), BF16),
                            pltpu.VMEM((W_SLOTS, d, tn), F32),
                            pltpu.VMEM((W_SLOTS, d, tn), F32),
                            pltpu.VMEM((W_SLOTS, de, tn), F32),
                            pltpu.VMEM((2, rmax, tn), F32),
                            pltpu.VMEM((MOE_BLOCK, tn), F32),
                            pltpu.SemaphoreType.DMA(()),
                            pltpu.SemaphoreType.DMA((2,)),
                            pltpu.SemaphoreType.DMA(()),
                            pltpu.SemaphoreType.DMA((W_SLOTS,)),
                            pltpu.SemaphoreType.DMA((W_SLOTS,))]),
        out_shape=jax.ShapeDtypeStruct((n_slots, d), F32),
        compiler_params=_cparams(1),
        name="experts",
    )(meta, pass_e, pass_row0, pass_nsub, xs, w_gu, w_dn,
      b_gu.reshape(n_e, 2 * nj, 1, tn), b_dn.reshape(n_e, nc, 1, tn))


ROW_DMA_UNROLL = 2
COMBINE_CHUNK = 64


def _combine_body(dcur_ref, dnxt_ref, wt_ref, h1_ref, g2_ref, l2g_ref, l2b_ref, y_ref, out_ref, buf_ref, sem):
    i = pl.program_id(0)
    n_tiles = pl.num_programs(0)
    t = h1_ref.shape[0]
    slot = i % 2

    def gather(dest_ref, sl):
        def body(it, carry):
            for u in range(ROW_DMA_UNROLL):
                tok = it * ROW_DMA_UNROLL + u
                for k in range(TOP_K):
                    d = dest_ref[0, tok * TOP_K + k]
                    pltpu.make_async_copy(y_ref.at[pl.ds(d, 1)], buf_ref.at[sl, k, pl.ds(tok, 1)],
                                          sem.at[sl]).start()
            return carry
        lax.fori_loop(0, t // ROW_DMA_UNROLL, body, 0)

    pl.when(i == 0)(lambda: gather(dcur_ref, 0))
    pl.when(i + 1 < n_tiles)(lambda: gather(dnxt_ref, 1 - slot))
    for k in range(TOP_K):
        pltpu.make_async_copy(y_ref.at[pl.ds(0, t)], buf_ref.at[slot, k], sem.at[slot]).wait()
    for r in range(0, t, COMBINE_CHUNK):
        rows = slice(r, r + COMBINE_CHUNK)
        wt = wt_ref[rows, :]
        f = buf_ref[slot, 0, rows, :] * wt[:, 0:1]
        for k in range(1, TOP_K):
            f = f + buf_ref[slot, k, rows, :] * wt[:, k:k + 1]
        out_ref[rows, :] = _layer_norm(DEEPNORM_ALPHA * h1_ref[rows, :] + g2_ref[...] * f,
                                       l2g_ref[...], l2b_ref[...])


def _combine(dest, wt, h1, mod3, rows_per_batch, l2g, l2b, y, tm):
    n, d = h1.shape
    n_tiles = n // tm
    dest3 = dest.reshape(n_tiles, 1, tm * TOP_K)
    row = lambda i: (i * tm) // rows_per_batch
    return pl.pallas_call(
        _combine_body,
        grid=(n_tiles,),
        in_specs=[pl.BlockSpec((None, 1, tm * TOP_K), lambda i: (i, 0, 0), memory_space=pltpu.SMEM),
                  pl.BlockSpec((None, 1, tm * TOP_K), lambda i: (jnp.minimum(i + 1, n_tiles - 1), 0, 0),
                               memory_space=pltpu.SMEM),
                  pl.BlockSpec((tm, LANES), lambda i: (i, 0)),
                  pl.BlockSpec((tm, d), lambda i: (i, 0)),
                  pl.BlockSpec((None, 1, d), lambda i: (row(i), 0, 5)),
                  pl.BlockSpec((1, d), lambda i: (0, 0)),
                  pl.BlockSpec((1, d), lambda i: (0, 0)),
                  pl.BlockSpec(memory_space=pl.ANY)],
        out_specs=pl.BlockSpec((tm, d), lambda i: (i, 0)),
        out_shape=jax.ShapeDtypeStruct((n, d), F32),
        scratch_shapes=[pltpu.VMEM((2, TOP_K, tm, d), F32), pltpu.SemaphoreType.DMA((2,))],
        compiler_params=_cparams(1),
        name="combine",
    )(dest3, dest3, wt, h1, mod3, l2g, l2b, y)


def _pick_tile(n, pref):
    t = pref
    while n % t:
        t //= 2
    return t


def kernel(x, c, ctx, c_ctx, ln_in_g, ln_in_b, w_ada, b_ada, w_in, cm_norm_g, cm_norm_b, cm_w_s, cm_b_s,
           gla_w_gk_f, gla_b_gk_f, gla_w_gk_b, gla_b_gk_b, gla_norm_g, w_out, ln1_g, ln1_b,
           w_router, b_router, w_gate_up, b_gate_up, w_down, b_down, ln2_g, ln2_b):
    bsz, l, d = x.shape
    lc = ctx.shape[1]
    n, nc = bsz * l, bsz * lc
    assert w_ada.shape[0] == 1, "single-layer configuration"
    assert bsz + 1 <= 8 and l % (2 * GLA_TILE) == 0 and lc % GLA_TILE == 0
    row = lambda v: v.reshape(1, -1)

    cc = jnp.concatenate([c, c_ctx[None, :], jnp.zeros((8 - bsz - 1, d), F32)], axis=0)
    mod3 = _ada(cc, w_ada[0], row(b_ada[0])).reshape(8, 1, N_MOD * d)

    w_in_b = w_in[0].astype(BF16)
    n_main = 5 * 1024
    w_main = w_in_b[:, :n_main]
    w_lr = jnp.pad(w_in_b[:, n_main:], ((0, 0), (0, LANES - 2 * GLA_RANK)))
    x2 = x.reshape(n, d)
    tm_x = _pick_tile(l, 512)
    p, lr = _inproj(x2, mod3, lambda i: (i * tm_x) // l, row(ln_in_g), row(ln_in_b), w_main, w_lr, 0, 2, tm_x,
                    n_main // 2)
    tm_c = _pick_tile(nc, 512)
    pc, lrc = _inproj(ctx.reshape(nc, d), mod3, lambda i: bsz, row(ln_in_g), row(ln_in_b),
                      w_main, w_lr, 2, 2, tm_c, 1024)

    bs_tile = jnp.repeat(cm_b_s[0].T, CM_CHUNK, axis=1)
    cm = _chunk_mlp(p, row(cm_norm_g[0]), row(cm_norm_b[0]), cm_w_s[0].astype(BF16), bs_tile, tm_x)

    kw = GLA_HEADS * GLA_DK
    wgf = jnp.zeros((LANES, kw), BF16).at[:GLA_RANK].set(gla_w_gk_f[0].astype(BF16))
    wgb = jnp.zeros((LANES, kw), BF16).at[GLA_RANK:2 * GLA_RANK].set(gla_w_gk_b[0].astype(BF16))
    gla = _gla(p.reshape(bsz, l, -1), lr.reshape(bsz, l, LANES), pc.reshape(bsz, lc, -1),
               lrc.reshape(bsz, lc, LANES), wgf, row(gla_b_gk_f[0]), wgb, row(gla_b_gk_b[0]),
               row(gla_norm_g[0])).reshape(n, -1)

    w_r = jnp.pad(w_router[0], ((0, 0), (0, LANES - N_EXPERTS))).astype(BF16)
    b_r = jnp.pad(b_router[0], (0, LANES - N_EXPERTS)).reshape(1, LANES)
    h1, hp, logits = _post_attn(cm, gla, x2, mod3, l, row(ln_in_g), row(ln_in_b), w_out[0].astype(BF16),
                                row(ln1_g[0]), row(ln1_b[0]), w_r, b_r, tm_x)

    idx, wt, rank, cnt = _route(logits, _pick_tile(n, 1024))
    counts = cnt[0, :N_EXPERTS].astype(I32)
    n_blocks = (n * TOP_K + N_EXPERTS * (MOE_BLOCK - 1)) // MOE_BLOCK
    blocks_e = (counts + MOE_BLOCK - 1) // MOE_BLOCK
    blk_end = jnp.cumsum(blocks_e)
    blk_start = blk_end - blocks_e
    dest = ((blk_start * MOE_BLOCK)[idx[:, :TOP_K]] + rank[:, :TOP_K]).reshape(-1)
    n_pass_max = n_blocks // PASS_SUBS + N_EXPERTS
    pass_cnt = (blocks_e + PASS_SUBS - 1) // PASS_SUBS
    pass_end = jnp.cumsum(pass_cnt)
    pass_start = pass_end - pass_cnt
    pid = jnp.arange(n_pass_max, dtype=I32)
    pass_e = jnp.minimum(jnp.searchsorted(pass_end, pid, side="right"), N_EXPERTS - 1).astype(I32)
    local = pid - pass_start[pass_e]
    pass_row0 = (blk_start[pass_e] + local * PASS_SUBS).astype(I32)
    pass_nsub = jnp.clip(blocks_e[pass_e] - local * PASS_SUBS, 0, PASS_SUBS).astype(I32)
    meta = jnp.stack([pass_end[-1], blk_end[-1]]).astype(I32)

    ar = jnp.arange(N_EXPERTS, dtype=I32)
    cand = jnp.concatenate([jnp.where(counts % MOE_BLOCK != 0, blk_end - 1, -1),
                            jnp.where(blk_end[-1] + ar < n_blocks, blk_end[-1] + ar, -1)])
    zero_blocks = cand[jnp.argsort(cand < 0, stable=True)].astype(I32)
    n_zero = jnp.sum(cand >= 0).astype(I32).reshape(1)

    xs = _dispatch(n_zero, zero_blocks, dest, hp, n_blocks * MOE_BLOCK, _pick_tile(n, 512))
    ys = _experts(meta, pass_e, pass_row0, pass_nsub, xs, w_gate_up[0], b_gate_up[0], w_down[0], b_down[0])
    out = _combine(dest, wt, h1, mod3, l, row(ln2_g[0]), row(ln2_b[0]), ys, _pick_tile(n, 512))
    return out.reshape(bsz, l, d)
```

```python
import math

import jax
import jax.numpy as jnp
from jax import lax
from jax.experimental import pallas as pl
from jax.experimental.pallas import tpu as pltpu

F32 = jnp.float32
BF16 = jnp.bfloat16
U32 = jnp.uint32
I32 = jnp.int32

CM_CHUNK = 128
CM_HEADS = 8
GLA_HEADS = 4
GLA_DK = 128
GLA_DV = 256
GLA_CHUNK = 64
GLA_RANK = 16
GLA_GATE_NORMALIZER = 16.0
N_EXPERTS = 32
TOP_K = 4
MOE_BLOCK = 256
SWIGLU_LIMIT = 7.0
SWIGLU_ALPHA = 1.702
N_MOD = 6
DEEPNORM_ALPHA = 2.0 ** 0.25
LN_EPS = 1e-5
RMS_EPS = 1e-6

LANES = 128
VMEM_LIMIT = 56 * 1024 * 1024


def _cparams(n_axes, vmem=VMEM_LIMIT):
    return pltpu.CompilerParams(dimension_semantics=("arbitrary",) * n_axes,
                                vmem_limit_bytes=vmem)


def _layer_norm(t, g, b):
    mu = jnp.mean(t, axis=-1, keepdims=True)
    d = t - mu
    var = jnp.mean(d * d, axis=-1, keepdims=True)
    return d * lax.rsqrt(var + LN_EPS) * g + b


def _gelu(t):
    return 0.5 * t * (1.0 + lax.erf(t * (1.0 / math.sqrt(2.0))))


def _silu(t):
    return t * jax.nn.sigmoid(t)


def _ones_where(mask, dtype):
    return jnp.where(mask, 1.0, 0.0).astype(dtype)


def _dot(a, b):
    return jnp.dot(a, b, preferred_element_type=F32)


def _dot_nt(a, b):
    return lax.dot_general(a, b, (((1,), (1,)), ((), ())), preferred_element_type=F32)


def _dot_tn(a, b):
    return lax.dot_general(a, b, (((0,), (0,)), ((), ())), preferred_element_type=F32)


def _ada_body(c_ref, w_ref, b_ref, o_ref):
    a = _silu(c_ref[...]).astype(BF16)
    o_ref[...] = _dot(a, w_ref[...].astype(BF16)) + b_ref[...]


def _ada(cc, w, b):
    rows, d = cc.shape
    n = w.shape[1]
    tn = 1024
    return pl.pallas_call(
        _ada_body,
        grid=(n // tn,),
        in_specs=[pl.BlockSpec((rows, d), lambda j: (0, 0)),
                  pl.BlockSpec((d, tn), lambda j: (0, j)),
                  pl.BlockSpec((1, tn), lambda j: (0, j))],
        out_specs=pl.BlockSpec((rows, tn), lambda j: (0, j)),
        out_shape=jax.ShapeDtypeStruct((rows, n), F32),
        compiler_params=_cparams(1),
        name="ada",
    )(cc, w, b)


def _inproj_body(x_ref, g_ref, b_ref, sh_ref, sc_ref, w_ref, wlr_ref, o_ref, olr_ref, hm_ref):
    @pl.when(pl.program_id(1) == 0)
    def _():
        h = _layer_norm(x_ref[...], g_ref[...], b_ref[...])
        hm = (h * (1.0 + sc_ref[...]) + sh_ref[...]).astype(BF16)
        hm_ref[...] = hm
        olr_ref[...] = _dot(hm, wlr_ref[...])

    o_ref[...] = _dot(hm_ref[...], w_ref[...])


def _inproj(x2, mod3, mod_row, ln_g, ln_b, w_main, w_lr, col0, ncols, tm, tn):
    r, d = x2.shape
    return pl.pallas_call(
        _inproj_body,
        grid=(r // tm, ncols),
        in_specs=[pl.BlockSpec((tm, d), lambda i, j: (i, 0)),
                  pl.BlockSpec((1, d), lambda i, j: (0, 0)),
                  pl.BlockSpec((1, d), lambda i, j: (0, 0)),
                  pl.BlockSpec((None, 1, d), lambda i, j: (mod_row(i), 0, 0)),
                  pl.BlockSpec((None, 1, d), lambda i, j: (mod_row(i), 0, 1)),
                  pl.BlockSpec((d, tn), lambda i, j: (0, col0 + j)),
                  pl.BlockSpec((d, LANES), lambda i, j: (0, 0))],
        out_specs=[pl.BlockSpec((tm, tn), lambda i, j: (i, j)),
                   pl.BlockSpec((tm, LANES), lambda i, j: (i, 0))],
        out_shape=[jax.ShapeDtypeStruct((r, ncols * tn), F32),
                   jax.ShapeDtypeStruct((r, LANES), F32)],
        scratch_shapes=[pltpu.VMEM((tm, d), BF16)],
        compiler_params=_cparams(2),
        name="inproj",
    )(x2, ln_g, ln_b, mod3, mod3, w_main, w_lr)


def _cm_body(u_ref, v_ref, ng_ref, nb_ref, ws_ref, bs_ref, o_ref):
    tm = u_ref.shape[0]
    u = _gelu(u_ref[...])
    v = _gelu(v_ref[...])
    vb = _layer_norm(v, ng_ref[...], nb_ref[...]).astype(BF16)
    hd = CM_CHUNK
    for c in range(tm // CM_CHUNK):
        rows = slice(c * CM_CHUNK, (c + 1) * CM_CHUNK)
        for h in range(CM_HEADS):
            cols = slice(h * hd, (h + 1) * hd)
            s = _dot(ws_ref[h], vb[rows, cols]) + bs_ref[:, cols]
            o_ref[rows, cols] = (u[rows, cols] * s).astype(BF16)


def _chunk_mlp(p, ng, nb, ws, bs, tm):
    r = p.shape[0]
    w = CM_HEADS * CM_CHUNK
    return pl.pallas_call(
        _cm_body,
        grid=(r // tm,),
        in_specs=[pl.BlockSpec((tm, w), lambda i: (i, 0)),
                  pl.BlockSpec((tm, w), lambda i: (i, 1)),
                  pl.BlockSpec((1, w), lambda i: (0, 0)),
                  pl.BlockSpec((1, w), lambda i: (0, 0)),
                  pl.BlockSpec((CM_HEADS, CM_CHUNK, CM_CHUNK), lambda i: (0, 0, 0)),
                  pl.BlockSpec((CM_CHUNK, w), lambda i: (0, 0))],
        out_specs=pl.BlockSpec((tm, w), lambda i: (i, 0)),
        out_shape=jax.ShapeDtypeStruct((r, w), BF16),
        compiler_params=_cparams(1),
        name="chunk_mlp",
    )(p, p, ng, nb, ws, bs)


ROW_CHUNK = 256


def _inproj_cm_body(x_ref, g_ref, b_ref, sh_ref, sc_ref, wuv_ref, wr_ref, wlr_ref, ng_ref, nb_ref, ws_ref, bs_ref,
                    cm_ref, p_ref, lr_ref):
    tm = x_ref.shape[0]
    half = wuv_ref.shape[1] // 2
    hd = CM_CHUNK
    for r in range(0, tm, ROW_CHUNK):
        rows = slice(r, r + ROW_CHUNK)
        h = _layer_norm(x_ref[rows, :], g_ref[...], b_ref[...])
        hm = (h * (1.0 + sc_ref[...]) + sh_ref[...]).astype(BF16)
        uv = _dot(hm, wuv_ref[...])
        u = _gelu(uv[:, :half])
        vb = _layer_norm(_gelu(uv[:, half:]), ng_ref[...], nb_ref[...]).astype(BF16)
        for c in range(ROW_CHUNK // CM_CHUNK):
            crow = slice(c * CM_CHUNK, (c + 1) * CM_CHUNK)
            orow = slice(r + c * CM_CHUNK, r + (c + 1) * CM_CHUNK)
            for hh in range(CM_HEADS):
                cols = slice(hh * hd, (hh + 1) * hd)
                s = _dot(ws_ref[hh], vb[crow, cols]) + bs_ref[:, cols]
                cm_ref[orow, cols] = (u[crow, cols] * s).astype(BF16)
        p_ref[rows, :] = _dot(hm, wr_ref[...])
        lr_ref[rows, :] = _dot(hm, wlr_ref[...])


def _inproj_cm(x2, mod3, rows_per_batch, ln_g, ln_b, w_uv, w_rest, w_lr, ng, nb, ws, bs, tm):
    r, d = x2.shape
    n_uv, n_rest = w_uv.shape[1], w_rest.shape[1]
    row = lambda i: (i * tm) // rows_per_batch

    def const(shape):
        return pl.BlockSpec(shape, lambda i: (0,) * len(shape), pipeline_mode=pl.Buffered(1))

    return pl.pallas_call(
        _inproj_cm_body,
        grid=(r // tm,),
        in_specs=[pl.BlockSpec((tm, d), lambda i: (i, 0)),
                  const((1, d)), const((1, d)),
                  pl.BlockSpec((None, 1, d), lambda i: (row(i), 0, 0)),
                  pl.BlockSpec((None, 1, d), lambda i: (row(i), 0, 1)),
                  const((d, n_uv)), const((d, n_rest)), const((d, LANES)),
                  const((1, n_uv // 2)), const((1, n_uv // 2)),
                  const((CM_HEADS, CM_CHUNK, CM_CHUNK)), const((CM_CHUNK, n_uv // 2))],
        out_specs=[pl.BlockSpec((tm, n_uv // 2), lambda i: (i, 0)),
                   pl.BlockSpec((tm, n_rest), lambda i: (i, 0)),
                   pl.BlockSpec((tm, LANES), lambda i: (i, 0))],
        out_shape=[jax.ShapeDtypeStruct((r, n_uv // 2), BF16),
                   jax.ShapeDtypeStruct((r, n_rest), F32),
                   jax.ShapeDtypeStruct((r, LANES), F32)],
        compiler_params=_cparams(1),
        name="inproj_cm",
    )(x2, ln_g, ln_b, mod3, mod3, w_uv, w_rest, w_lr, ng, nb, ws, bs)


GLA_TILE = 256
GLA_TILES_PER_ITER = 2


def _gla_tile(q, k, v, lr, wg, bg, st_ref, forward, need_o):
    t = k.shape[0]
    n_chunks = t // GLA_CHUNK
    z = _dot(lr.astype(BF16), wg) + bg
    g = jax.nn.log_sigmoid(z) * (1.0 / GLA_GATE_NORMALIZER)
    r_id = lax.broadcasted_iota(I32, (t, t), 0)
    c_id = lax.broadcasted_iota(I32, (t, t), 1)
    shift = GLA_CHUNK.bit_length() - 1
    same = (r_id >> shift) == (c_id >> shift)
    lower = same & (c_id <= r_id)
    tri = _ones_where(lower, BF16)
    g_hi = g.astype(BF16)
    g_lo = (g - g_hi.astype(F32)).astype(BF16)
    csum = _dot(tri, g_hi) + _dot(tri, g_lo)
    g3 = g.reshape(n_chunks, GLA_CHUNK, GLA_DK)
    tot = jnp.broadcast_to(jnp.sum(g3, axis=1, keepdims=True), g3.shape).reshape(t, GLA_DK)
    bcum = csum if forward else tot - csum + g
    kd = (k * jnp.exp(tot - bcum)).astype(BF16)
    decay = jnp.exp(tot)
    vb = v.astype(BF16)
    o = None
    if need_o:
        qe = ((q * (GLA_DK ** -0.5)) * jnp.exp(bcum)).astype(BF16)
        ke = (k * jnp.exp(-bcum)).astype(BF16)
        att = _dot_nt(qe, ke)
        mask = lower if forward else same & (c_id >= r_id)
        att = jnp.where(mask, att, 0.0).astype(BF16)
        o = _dot(att, vb)
    outs = [None] * n_chunks
    order = range(n_chunks) if forward else range(n_chunks - 1, -1, -1)
    for c in order:
        rows = slice(c * GLA_CHUNK, (c + 1) * GLA_CHUNK)
        s_t = st_ref[...]
        if need_o:
            outs[c] = o[rows] + _dot_nt(qe[rows], s_t.astype(BF16))
        u_t = _dot_tn(vb[rows], kd[rows])
        st_ref[...] = s_t * decay[c * GLA_CHUNK:c * GLA_CHUNK + 1, :] + u_t
    if need_o:
        return jnp.concatenate(outs, axis=0)
    return None


def _gla_body(q_ref, k_ref, v_ref, go_ref, lr_ref, kc_ref, vc_ref, lrc_ref,
              wgf_ref, bgf_ref, wgb_ref, bgb_ref, ng_ref, out_ref, o_scr, sf_ref, sb_ref):
    t = GLA_TILE
    n_x = q_ref.shape[0] // t
    n_c = kc_ref.shape[0] // t
    half = n_x // 2
    sf_ref[...] = jnp.zeros_like(sf_ref)
    sb_ref[...] = jnp.zeros_like(sb_ref)
    wgf, bgf, wgb, bgb = wgf_ref[...], bgf_ref[...], wgb_ref[...], bgb_ref[...]

    for i in range(n_c):
        rf = slice(i * t, (i + 1) * t)
        rb = slice((n_c - 1 - i) * t, (n_c - i) * t)
        _gla_tile(None, kc_ref[rf], vc_ref[rf], lrc_ref[rf], wgf, bgf, sf_ref, True, False)
        _gla_tile(None, kc_ref[rb], vc_ref[rb], lrc_ref[rb], wgb, bgb, sb_ref, False, False)

    def tile_out(i, forward):
        rows = pl.ds(pl.multiple_of(i * t, t), t)
        if forward:
            return rows, _gla_tile(q_ref[rows], k_ref[rows], v_ref[rows], lr_ref[rows],
                                   wgf, bgf, sf_ref, True, True)
        return rows, _gla_tile(q_ref[rows], k_ref[rows], v_ref[rows], lr_ref[rows],
                               wgb, bgb, sb_ref, False, True)

    def finish(rows, o):
        o = o + o_scr[rows]
        ms = jnp.mean(o * o, axis=-1, keepdims=True)
        on = o * lax.rsqrt(ms + RMS_EPS) * ng_ref[...]
        out_ref[rows] = (on * _silu(go_ref[rows])).astype(BF16)

    def keep(rows, o):
        o_scr[rows] = o

    u = math.gcd(GLA_TILES_PER_ITER, half)

    def make_step(sink):
        def step(it, carry):
            for w in range(u):
                i = it * u + w
                sink(*tile_out(i, True))
                sink(*tile_out(n_x - 1 - i, False))
            return carry
        return step

    lax.fori_loop(0, half // u, make_step(keep), 0)
    lax.fori_loop(half // u, n_x // u, make_step(finish), 0)


def _gla(p3, lr3, pc3, lrc3, wgf, bgf, wgb, bgb, ng):
    bsz, l, _ = p3.shape
    lc = pc3.shape[1]
    dk, dv = GLA_DK, GLA_DV
    kw = GLA_HEADS * dk
    q0, k0 = 0, kw // dk
    v0, go0 = 2 * kw // dv, (2 * kw + GLA_HEADS * dv) // dv
    kc0, vc0 = k0, v0
    return pl.pallas_call(
        _gla_body,
        grid=(bsz, GLA_HEADS),
        in_specs=[pl.BlockSpec((None, l, dk), lambda b, h: (b, 0, q0 + h)),
                  pl.BlockSpec((None, l, dk), lambda b, h: (b, 0, k0 + h)),
                  pl.BlockSpec((None, l, dv), lambda b, h: (b, 0, v0 + h)),
                  pl.BlockSpec((None, l, dv), lambda b, h: (b, 0, go0 + h)),
                  pl.BlockSpec((None, l, LANES), lambda b, h: (b, 0, 0)),
                  pl.BlockSpec((None, lc, dk), lambda b, h: (b, 0, kc0 + h)),
                  pl.BlockSpec((None, lc, dv), lambda b, h: (b, 0, vc0 + h)),
                  pl.BlockSpec((None, lc, LANES), lambda b, h: (b, 0, 0)),
                  pl.BlockSpec((LANES, dk), lambda b, h: (0, h)),
                  pl.BlockSpec((1, dk), lambda b, h: (0, h)),
                  pl.BlockSpec((LANES, dk), lambda b, h: (0, h)),
                  pl.BlockSpec((1, dk), lambda b, h: (0, h)),
                  pl.BlockSpec((1, dv), lambda b, h: (0, 0))],
        out_specs=pl.BlockSpec((None, l, dv), lambda b, h: (b, 0, h)),
        out_shape=jax.ShapeDtypeStruct((bsz, l, GLA_HEADS * dv), BF16),
        scratch_shapes=[pltpu.VMEM((l, dv), F32),
                        pltpu.VMEM((dv, dk), F32),
                        pltpu.VMEM((dv, dk), F32)],
        compiler_params=_cparams(2),
        name="gla",
    )(p3, p3, p3, p3, lr3, pc3, pc3, lrc3, wgf, bgf, wgb, bgb, ng)


def _pack_bf16_pair(lo, hi):
    lo_b = lax.bitcast_convert_type(lo.astype(BF16).astype(F32), U32)
    hi_b = lax.bitcast_convert_type(hi.astype(BF16).astype(F32), U32)
    return hi_b | (lo_b >> 16)


def _unpack_bf16_pair(p):
    lo = lax.bitcast_convert_type(p << 16, F32).astype(BF16)
    hi = lax.bitcast_convert_type(p & jnp.uint32(0xFFFF0000), F32).astype(BF16)
    return lo, hi


def _post_body(cm_ref, gla_ref, x_ref, lng_ref, lnb_ref, g1_ref, sh2_ref, sc2_ref,
               wo_ref, l1g_ref, l1b_ref, wr_ref, br_ref, h1_ref, hp_ref, lg_ref):
    half = cm_ref.shape[1]
    d2 = x_ref.shape[1] // 2
    for r in range(0, x_ref.shape[0], ROW_CHUNK):
        rows = slice(r, r + ROW_CHUNK)
        y = _dot(cm_ref[rows, :], wo_ref[:half, :]) + _dot(gla_ref[rows, :], wo_ref[half:, :])
        hx = _layer_norm(x_ref[rows, :], lng_ref[...], lnb_ref[...])
        h1 = _layer_norm(DEEPNORM_ALPHA * hx + g1_ref[...] * y, l1g_ref[...], l1b_ref[...])
        h1_ref[rows, :] = h1
        hm = h1 * (1.0 + sc2_ref[...]) + sh2_ref[...]
        hp_ref[rows, :] = _pack_bf16_pair(hm[:, :d2], hm[:, d2:])
        lg_ref[rows, :] = _dot(hm.astype(BF16), wr_ref[...]) + br_ref[...]


def _post_attn(cm, gla, x2, mod3, rows_per_batch, ln_g, ln_b, w_out, l1g, l1b, w_r, b_r, tm):
    r, d = x2.shape
    half = d // 2
    row = lambda i: (i * tm) // rows_per_batch
    full = lambda shape: pl.BlockSpec(shape, lambda i: (0,) * len(shape))
    return pl.pallas_call(
        _post_body,
        grid=(r // tm,),
        in_specs=[pl.BlockSpec((tm, half), lambda i: (i, 0)),
                  pl.BlockSpec((tm, half), lambda i: (i, 0)),
                  pl.BlockSpec((tm, d), lambda i: (i, 0)),
                  full((1, d)), full((1, d)),
                  pl.BlockSpec((None, 1, d), lambda i: (row(i), 0, 2)),
                  pl.BlockSpec((None, 1, d), lambda i: (row(i), 0, 3)),
                  pl.BlockSpec((None, 1, d), lambda i: (row(i), 0, 4)),
                  full((d, d)), full((1, d)), full((1, d)),
                  full((d, LANES)), full((1, LANES))],
        out_specs=[pl.BlockSpec((tm, d), lambda i: (i, 0)),
                   pl.BlockSpec((tm, half), lambda i: (i, 0)),
                   pl.BlockSpec((tm, LANES), lambda i: (i, 0))],
        out_shape=[jax.ShapeDtypeStruct((r, d), F32),
                   jax.ShapeDtypeStruct((r, half), U32),
                   jax.ShapeDtypeStruct((r, LANES), F32)],
        compiler_params=_cparams(1),
        name="post_attn",
    )(cm, gla, x2, ln_g, ln_b, mod3, mod3, mod3, w_out, l1g, l1b, w_r, b_r)


def _route_body(lg_ref, idx_ref, wt_ref, rank_ref, cnt_ref, carry_ref):
    i = pl.program_id(0)
    t = lg_ref.shape[0]

    @pl.when(i == 0)
    def _():
        carry_ref[...] = jnp.zeros_like(carry_ref)

    lane = lax.broadcasted_iota(I32, (t, LANES), 1)
    lane_f = lane.astype(F32)
    neg = jnp.float32(-jnp.inf)
    l = jnp.where(lane < N_EXPERTS, lg_ref[...], neg)
    tops, onehots, idxs = [], [], []
    for _ in range(TOP_K):
        m = jnp.max(l, axis=-1, keepdims=True)
        idx = jnp.min(jnp.where(l == m, lane_f, float(LANES)), axis=-1, keepdims=True).astype(I32)
        oh = lane == idx
        l = jnp.where(oh, neg, l)
        tops.append(m)
        idxs.append(idx)
        onehots.append(oh)
    exps = [jnp.exp(m - tops[0]) for m in tops]
    denom = exps[0] + exps[1] + exps[2] + exps[3]
    sel = _ones_where(onehots[0] | onehots[1] | onehots[2] | onehots[3], F32)
    r_id = lax.broadcasted_iota(I32, (t, t), 0)
    c_id = lax.broadcasted_iota(I32, (t, t), 1)
    strict = _ones_where(c_id < r_id, BF16)
    before = _dot(strict, sel.astype(BF16)) + carry_ref[0:1, :]
    idx_out = jnp.zeros((t, LANES), I32)
    wt_out = jnp.zeros((t, LANES), F32)
    rank_out = jnp.zeros((t, LANES), I32)
    for k in range(TOP_K):
        rk = jnp.sum(jnp.where(onehots[k], before, 0.0), axis=-1, keepdims=True).astype(I32)
        idx_out = jnp.where(lane == k, idxs[k], idx_out)
        wt_out = jnp.where(lane == k, exps[k] / denom, wt_out)
        rank_out = jnp.where(lane == k, rk, rank_out)
    idx_ref[...] = idx_out
    wt_ref[...] = wt_out
    rank_ref[...] = rank_out
    total = carry_ref[0:1, :] + jnp.sum(sel, axis=0, keepdims=True)
    carry_ref[...] = jnp.broadcast_to(total, carry_ref.shape)
    cnt_ref[...] = jnp.broadcast_to(total, cnt_ref.shape)


def _route(logits, tm):
    n = logits.shape[0]
    blk = pl.BlockSpec((tm, LANES), lambda i: (i, 0))
    return pl.pallas_call(
        _route_body,
        grid=(n // tm,),
        in_specs=[blk],
        out_specs=[blk, blk, blk, pl.BlockSpec((8, LANES), lambda i: (0, 0))],
        out_shape=[jax.ShapeDtypeStruct((n, LANES), I32),
                   jax.ShapeDtypeStruct((n, LANES), F32),
                   jax.ShapeDtypeStruct((n, LANES), I32),
                   jax.ShapeDtypeStruct((8, LANES), F32)],
        scratch_shapes=[pltpu.VMEM((8, LANES), F32)],
        compiler_params=_cparams(1),
        name="route",
    )(logits)


def _dispatch_body(nz_ref, zl_ref, dest_ref, hp_ref, xs_ref, zero_ref, sem, zsem):
    t = hp_ref.shape[0]
    unroll = 4

    @pl.when(pl.program_id(0) == 0)
    def _():
        zero_ref[...] = jnp.zeros_like(zero_ref)

        def z_copy(b):
            r = pl.multiple_of(zl_ref[b] * MOE_BLOCK, MOE_BLOCK)
            return pltpu.make_async_copy(zero_ref, xs_ref.at[pl.ds(r, MOE_BLOCK)], zsem)

        def start(b, carry):
            z_copy(b).start()
            return carry

        def wait(b, carry):
            z_copy(b).wait()
            return carry

        lax.fori_loop(0, nz_ref[0], start, 0)
        lax.fori_loop(0, nz_ref[0], wait, 0)

    def issue(it, carry):
        for u in range(unroll):
            tok = it * unroll + u
            for k in range(TOP_K):
                d = dest_ref[0, tok * TOP_K + k]
                pltpu.make_async_copy(hp_ref.at[pl.ds(tok, 1)], xs_ref.at[pl.ds(d, 1)], sem).start()
        return carry

    lax.fori_loop(0, t // unroll, issue, 0)
    for _ in range(TOP_K):
        pltpu.make_async_copy(hp_ref, xs_ref.at[pl.ds(0, t)], sem).wait()


def _dispatch(n_zero, zero_blocks, dest, hp, n_slots, tm):
    n, w = hp.shape
    dest3 = dest.reshape(n // tm, 1, tm * TOP_K)
    return pl.pallas_call(
        _dispatch_body,
        grid_spec=pltpu.PrefetchScalarGridSpec(
            num_scalar_prefetch=2,
            grid=(n // tm,),
            in_specs=[pl.BlockSpec((None, 1, tm * TOP_K), lambda i, nz, zl: (i, 0, 0), memory_space=pltpu.SMEM),
                      pl.BlockSpec((tm, w), lambda i, nz, zl: (i, 0))],
            out_specs=pl.BlockSpec(memory_space=pl.ANY),
            scratch_shapes=[pltpu.VMEM((MOE_BLOCK, w), U32),
                            pltpu.SemaphoreType.DMA(()),
                            pltpu.SemaphoreType.DMA(())]),
        out_shape=jax.ShapeDtypeStruct((n_slots, w), U32),
        compiler_params=_cparams(1),
        name="dispatch",
    )(n_zero, zero_blocks, dest3, hp)


PASS_SUBS = 9
EXPERT_TN = 256


def _expert_body_grid_tiles(meta_ref, pe_ref, pr_ref, pn_ref, xs_hbm, wg_ref, wu_ref, bg_ref, bu_ref, wd_ref, bd_ref,
                            y_hbm, xraw, xb, h_ref, ybuf, zbuf, sem_x, sem_y, sem_z):
    del pe_ref
    sub = MOE_BLOCK
    tn = EXPERT_TN
    nj = h_ref.shape[0]
    nc = y_hbm.shape[1] // tn
    d2 = xraw.shape[1]
    p = pl.program_id(0)
    s = pl.program_id(1)
    n_pass = meta_ref[0]
    nsub = pn_ref[p]
    row0 = pr_ref[p]

    def x_copy(i, pp):
        r = pl.multiple_of((pr_ref[pp] + i) * sub, sub)
        return pltpu.make_async_copy(xs_hbm.at[pl.ds(r, sub)], xraw.at[pl.ds(i * sub, sub)], sem_x)

    def y_copy(i, r0, c, slot):
        r = pl.multiple_of((r0 + i) * sub, sub)
        col = pl.multiple_of(c * tn, tn)
        return pltpu.make_async_copy(ybuf.at[slot, pl.ds(i * sub, sub), :],
                                     y_hbm.at[pl.ds(r, sub), pl.ds(col, tn)], sem_y.at[slot])

    def z_copy(b, c):
        r = pl.multiple_of(b * sub, sub)
        return pltpu.make_async_copy(zbuf, y_hbm.at[pl.ds(r, sub), pl.ds(c * tn, tn)], sem_z)

    def for_subs(count, fn):
        for i in range(PASS_SUBS):
            pl.when(i < count)(lambda i=i: fn(i))

    def for_tail(fn):
        def body(b, carry):
            for c in range(nc):
                fn(b, c)
            return carry
        lax.fori_loop(meta_ref[1], y_hbm.shape[0] // sub, body, 0)

    @pl.when(s == 0)
    def _():
        @pl.when(p == 0)
        def _():
            for_subs(nsub, lambda i: x_copy(i, p).start())
            zbuf[...] = jnp.zeros_like(zbuf)
            for_tail(lambda b, c: z_copy(b, c).start())

        for_subs(nsub, lambda i: x_copy(i, p).wait())

    @pl.when((s == 1) & (p + 1 < n_pass))
    def _():
        for_subs(pn_ref[p + 1], lambda i: x_copy(i, p + 1).start())

    def for_groups(fn):
        k = 1 << (PASS_SUBS.bit_length() - 1)
        while k:
            start = pl.multiple_of((nsub & (-2 * k)) * sub, sub)
            pl.when((nsub & k) != 0)(lambda start=start, k=k: fn(start, k * sub))
            k //= 2

    @pl.when(s == 0)
    def _():
        def unpack(i):
            rows = slice(i * sub, (i + 1) * sub)
            lo, hi = _unpack_bf16_pair(xraw[rows, :])
            xb[rows, :d2] = lo
            xb[rows, d2:] = hi

        for_subs(nsub, unpack)

    @pl.when(s < nj)
    def _():
        def group(rows):
            x = xb[rows, :]
            gate = jnp.minimum(_dot(x, wg_ref[...].astype(BF16)) + bg_ref[...], SWIGLU_LIMIT)
            up = jnp.clip(_dot(x, wu_ref[...].astype(BF16)) + bu_ref[...], -SWIGLU_LIMIT, SWIGLU_LIMIT)
            h_ref[s, rows, :] = ((up + 1.0) * gate * jax.nn.sigmoid(SWIGLU_ALPHA * gate)).astype(BF16)

        for_groups(group)

    @pl.when(s >= nj)
    def _():
        c = s - nj
        slot = c % 2

        @pl.when(c >= 2)
        def _():
            for_subs(nsub, lambda i: y_copy(i, row0, c - 2, slot).wait())

        @pl.when((c < 2) & (p > 0))
        def _():
            for_subs(pn_ref[p - 1], lambda i: y_copy(i, pr_ref[p - 1], nc - 2 + c, slot).wait())

        def group(rows):
            hx = jnp.concatenate([h_ref[j, rows, :] for j in range(nj)], axis=1)
            ybuf[slot, rows, :] = _dot(hx, wd_ref[...].astype(BF16)) + bd_ref[...]

        for_groups(group)
        for_subs(nsub, lambda i: y_copy(i, row0, c, slot).start())

        @pl.when((p == n_pass - 1) & (c == nc - 1))
        def _():
            for_subs(nsub, lambda i: y_copy(i, row0, c - 1, 1 - slot).wait())
            for_subs(nsub, lambda i: y_copy(i, row0, c, slot).wait())
            for_tail(lambda b, cc: z_copy(b, cc).wait())


def _experts_grid_tiles(meta, pass_e, pass_row0, pass_nsub, xs, w_gu, b_gu3, w_dn, b_dn3):
    n_slots, d2 = xs.shape
    d = 2 * d2
    de = w_dn.shape[1]
    tn = EXPERT_TN
    nj = de // tn
    nc = d // tn
    rmax = PASS_SUBS * MOE_BLOCK

    def first(s):
        return jnp.minimum(s, nj - 1)

    def second(s):
        return jnp.maximum(s - nj, 0)

    return pl.pallas_call(
        _expert_body,
        grid_spec=pltpu.PrefetchScalarGridSpec(
            num_scalar_prefetch=4,
            grid=(meta[0], nj + nc),
            in_specs=[
                pl.BlockSpec(memory_space=pl.ANY),
                pl.BlockSpec((None, d, tn), lambda p, s, m, e, r, n: (e[p], 0, first(s))),
                pl.BlockSpec((None, d, tn), lambda p, s, m, e, r, n: (e[p], 0, nj + first(s))),
                pl.BlockSpec((None, 1, tn), lambda p, s, m, e, r, n: (e[p], 0, first(s))),
                pl.BlockSpec((None, 1, tn), lambda p, s, m, e, r, n: (e[p], 0, nj + first(s))),
                pl.BlockSpec((None, de, tn), lambda p, s, m, e, r, n: (e[p], 0, second(s))),
                pl.BlockSpec((None, 1, tn), lambda p, s, m, e, r, n: (e[p], 0, second(s))),
            ],
            out_specs=pl.BlockSpec(memory_space=pl.ANY),
            scratch_shapes=[pltpu.VMEM((rmax, d2), U32),
                            pltpu.VMEM((rmax, d), BF16),
                            pltpu.VMEM((nj, rmax, tn), BF16),
                            pltpu.VMEM((2, rmax, tn), F32),
                            pltpu.VMEM((MOE_BLOCK, tn), F32),
                            pltpu.SemaphoreType.DMA(()),
                            pltpu.SemaphoreType.DMA((2,)),
                            pltpu.SemaphoreType.DMA(())]),
        out_shape=jax.ShapeDtypeStruct((n_slots, d), F32),
        compiler_params=_cparams(2),
        name="experts",
    )(meta, pass_e, pass_row0, pass_nsub, xs, w_gu, w_gu, b_gu3, b_gu3, w_dn, b_dn3)


DOT_ROWS = 256
W_SLOTS = 3
W_AHEAD = 2


def _expert_body(meta_ref, pe_ref, pr_ref, pn_ref, xs_hbm, wgu_hbm, wdn_hbm, bgu_ref, bdn_ref,
                 y_hbm, xraw, xb, h_ref, wg_buf, wu_buf, wd_buf, ybuf, zbuf,
                 sem_x, sem_y, sem_z, sem_w, sem_d):
    sub = MOE_BLOCK
    tn = EXPERT_TN
    nj = h_ref.shape[0]
    nc = y_hbm.shape[1] // tn
    de = nj * tn
    d2 = xraw.shape[1]
    p = pl.program_id(0)
    n_pass = meta_ref[0]
    nsub = pn_ref[p]
    row0 = pr_ref[p]

    def x_copy(i, pp):
        r = pl.multiple_of((pr_ref[pp] + i) * sub, sub)
        return pltpu.make_async_copy(xs_hbm.at[pl.ds(r, sub)], xraw.at[pl.ds(i * sub, sub)], sem_x)

    def y_copy(i, r0, c, slot):
        r = pl.multiple_of((r0 + i) * sub, sub)
        col = pl.multiple_of(c * tn, tn)
        return pltpu.make_async_copy(ybuf.at[slot, pl.ds(i * sub, sub), :],
                                     y_hbm.at[pl.ds(r, sub), pl.ds(col, tn)], sem_y.at[slot])

    def z_copy(b, c):
        r = pl.multiple_of(b * sub, sub)
        return pltpu.make_async_copy(zbuf, y_hbm.at[pl.ds(r, sub), pl.ds(c * tn, tn)], sem_z)

    def w1_copies(pp, j):
        e = pe_ref[pp]
        slot = j % W_SLOTS
        col = pl.multiple_of(j * tn, tn)
        return (pltpu.make_async_copy(wgu_hbm.at[e, :, pl.ds(col, tn)], wg_buf.at[slot], sem_w.at[slot]),
                pltpu.make_async_copy(wgu_hbm.at[e, :, pl.ds(de + col, tn)], wu_buf.at[slot], sem_w.at[slot]))

    def w2_copy(pp, c):
        slot = c % W_SLOTS
        col = pl.multiple_of(c * tn, tn)
        return pltpu.make_async_copy(wdn_hbm.at[pe_ref[pp], :, pl.ds(col, tn)], wd_buf.at[slot], sem_d.at[slot])

    def start_w1(pp, j):
        for cp in w1_copies(pp, j):
            cp.start()

    def for_subs(count, fn):
        for i in range(PASS_SUBS):
            pl.when(i < count)(lambda i=i: fn(i))

    def for_tail(fn):
        def body(b, carry):
            for c in range(nc):
                fn(b, c)
            return carry
        lax.fori_loop(meta_ref[1], y_hbm.shape[0] // sub, body, 0)

    def for_groups(fn):
        k = 1 << (PASS_SUBS.bit_length() - 1)
        while k:
            start = pl.multiple_of((nsub & (-2 * k)) * sub, sub)
            pl.when((nsub & k) != 0)(lambda start=start, k=k: fn(start, k * sub))
            k //= 2

    @pl.when(p == 0)
    def _():
        for_subs(nsub, lambda i: x_copy(i, p).start())
        for j in range(W_AHEAD):
            start_w1(p, j)
        zbuf[...] = jnp.zeros_like(zbuf)
        for_tail(lambda b, c: z_copy(b, c).start())

    for_subs(nsub, lambda i: x_copy(i, p).wait())

    def unpack(i):
        rows = slice(i * sub, (i + 1) * sub)
        lo, hi = _unpack_bf16_pair(xraw[rows, :])
        xb[rows, :d2] = lo
        xb[rows, d2:] = hi

    for_subs(nsub, unpack)

    @pl.when(p + 1 < n_pass)
    def _():
        for_subs(pn_ref[p + 1], lambda i: x_copy(i, p + 1).start())

    def first_step(j, carry):
        slot = j % W_SLOTS
        for cp in w1_copies(p, j):
            cp.wait()
        nxt = j + W_AHEAD
        pl.when(nxt < nj)(lambda: start_w1(p, nxt))
        pl.when(nxt >= nj)(lambda: w2_copy(p, nxt - nj).start())

        def group(start, size):
            wg = wg_buf[slot].astype(BF16)
            wu = wu_buf[slot].astype(BF16)
            for r in range(0, size, DOT_ROWS):
                rows = pl.ds(start + r, DOT_ROWS)
                x = xb[rows, :]
                gate = jnp.minimum(_dot(x, wg) + bgu_ref[j], SWIGLU_LIMIT)
                up = jnp.clip(_dot(x, wu) + bgu_ref[nj + j], -SWIGLU_LIMIT, SWIGLU_LIMIT)
                h_ref[j, rows, :] = ((up + 1.0) * gate * jax.nn.sigmoid(SWIGLU_ALPHA * gate)).astype(BF16)

        for_groups(group)
        return carry

    lax.fori_loop(0, nj, first_step, 0)

    def second_step(c, carry):
        wslot = c % W_SLOTS
        slot = c % 2
        w2_copy(p, c).wait()
        nxt = c + W_AHEAD
        pl.when(nxt < nc)(lambda: w2_copy(p, nxt).start())
        pl.when((nxt >= nc) & (p + 1 < n_pass))(lambda: start_w1(p + 1, nxt - nc))

        @pl.when(c >= 2)
        def _():
            for_subs(nsub, lambda i: y_copy(i, row0, c - 2, slot).wait())

        @pl.when((c < 2) & (p > 0))
        def _():
            for_subs(pn_ref[p - 1], lambda i: y_copy(i, pr_ref[p - 1], nc - 2 + c, slot).wait())

        def group(start, size):
            wd = wd_buf[wslot].astype(BF16)
            for r in range(0, size, DOT_ROWS):
                rows = pl.ds(start + r, DOT_ROWS)
                hx = jnp.concatenate([h_ref[j, rows, :] for j in range(nj)], axis=1)
                ybuf[slot, rows, :] = _dot(hx, wd) + bdn_ref[c]

        for_groups(group)
        for_subs(nsub, lambda i: y_copy(i, row0, c, slot).start())
        return carry

    lax.fori_loop(0, nc, second_step, 0)

    @pl.when(p == n_pass - 1)
    def _():
        for c in (nc - 2, nc - 1):
            for_subs(nsub, lambda i, c=c: y_copy(i, row0, c, c % 2).wait())
        for_tail(lambda b, cc: z_copy(b, cc).wait())


def _experts(meta, pass_e, pass_row0, pass_nsub, xs, w_gu, b_gu, w_dn, b_dn):
    n_slots, d2 = xs.shape
    d = 2 * d2
    n_e, de = w_dn.shape[0], w_dn.shape[1]
    tn = EXPERT_TN
    nj = de // tn
    nc = d // tn
    assert W_AHEAD < W_SLOTS and W_AHEAD <= min(nj, nc) and nc % 2 == 0
    rmax = PASS_SUBS * MOE_BLOCK
    return pl.pallas_call(
        _expert_body,
        grid_spec=pltpu.PrefetchScalarGridSpec(
            num_scalar_prefetch=4,
            grid=(meta[0],),
            in_specs=[
                pl.BlockSpec(memory_space=pl.ANY),
                pl.BlockSpec(memory_space=pl.ANY),
                pl.BlockSpec(memory_space=pl.ANY),
                pl.BlockSpec((None, 2 * nj, 1, tn), lambda p, m, e, r, n: (e[p], 0, 0, 0)),
                pl.BlockSpec((None, nc, 1, tn), lambda p, m, e, r, n: (e[p], 0, 0, 0)),
            ],
            out_specs=pl.BlockSpec(memory_space=pl.ANY),
            scratch_shapes=[pltpu.VMEM((rmax, d2), U32),
                            pltpu.VMEM((rmax, d), BF16),
                            pltpu.VMEM((nj, rmax, tn), BF16),
                            pltpu.VMEM((W_SLOTS, d, tn), F32),
                            pltpu.VMEM((W_SLOTS, d, tn), F32),
                            pltpu.VMEM((W_SLOTS, de, tn), F32),
                            pltpu.VMEM((2, rmax, tn), F32),
                            pltpu.VMEM((MOE_BLOCK, tn), F32),
                            pltpu.SemaphoreType.DMA(()),
                            pltpu.SemaphoreType.DMA((2,)),
                            pltpu.SemaphoreType.DMA(()),
                            pltpu.SemaphoreType.DMA((W_SLOTS,)),
                            pltpu.SemaphoreType.DMA((W_SLOTS,))]),
        out_shape=jax.ShapeDtypeStruct((n_slots, d), F32),
        compiler_params=_cparams(1),
        name="experts",
    )(meta, pass_e, pass_row0, pass_nsub, xs, w_gu, w_dn,
      b_gu.reshape(n_e, 2 * nj, 1, tn), b_dn.reshape(n_e, nc, 1, tn))


ROW_DMA_UNROLL = 2
COMBINE_CHUNK = 64


def _combine_body(dcur_ref, dnxt_ref, wt_ref, h1_ref, g2_ref, l2g_ref, l2b_ref, y_ref, out_ref, buf_ref, sem):
    i = pl.program_id(0)
    n_tiles = pl.num_programs(0)
    t = h1_ref.shape[0]
    slot = i % 2

    def gather(dest_ref, sl):
        def body(it, carry):
            for u in range(ROW_DMA_UNROLL):
                tok = it * ROW_DMA_UNROLL + u
                for k in range(TOP_K):
                    d = dest_ref[0, tok * TOP_K + k]
                    pltpu.make_async_copy(y_ref.at[pl.ds(d, 1)], buf_ref.at[sl, k, pl.ds(tok, 1)],
                                          sem.at[sl]).start()
            return carry
        lax.fori_loop(0, t // ROW_DMA_UNROLL, body, 0)

    pl.when(i == 0)(lambda: gather(dcur_ref, 0))
    pl.when(i + 1 < n_tiles)(lambda: gather(dnxt_ref, 1 - slot))
    for k in range(TOP_K):
        pltpu.make_async_copy(y_ref.at[pl.ds(0, t)], buf_ref.at[slot, k], sem.at[slot]).wait()
    for r in range(0, t, COMBINE_CHUNK):
        rows = slice(r, r + COMBINE_CHUNK)
        wt = wt_ref[rows, :]
        f = buf_ref[slot, 0, rows, :] * wt[:, 0:1]
        for k in range(1, TOP_K):
            f = f + buf_ref[slot, k, rows, :] * wt[:, k:k + 1]
        out_ref[rows, :] = _layer_norm(DEEPNORM_ALPHA * h1_ref[rows, :] + g2_ref[...] * f,
                                       l2g_ref[...], l2b_ref[...])


def _combine(dest, wt, h1, mod3, rows_per_batch, l2g, l2b, y, tm):
    n, d = h1.shape
    n_tiles = n // tm
    dest3 = dest.reshape(n_tiles, 1, tm * TOP_K)
    row = lambda i: (i * tm) // rows_per_batch
    return pl.pallas_call(
        _combine_body,
        grid=(n_tiles,),
        in_specs=[pl.BlockSpec((None, 1, tm * TOP_K), lambda i: (i, 0, 0), memory_space=pltpu.SMEM),
                  pl.BlockSpec((None, 1, tm * TOP_K), lambda i: (jnp.minimum(i + 1, n_tiles - 1), 0, 0),
                               memory_space=pltpu.SMEM),
                  pl.BlockSpec((tm, LANES), lambda i: (i, 0)),
                  pl.BlockSpec((tm, d), lambda i: (i, 0)),
                  pl.BlockSpec((None, 1, d), lambda i: (row(i), 0, 5)),
                  pl.BlockSpec((1, d), lambda i: (0, 0)),
                  pl.BlockSpec((1, d), lambda i: (0, 0)),
                  pl.BlockSpec(memory_space=pl.ANY)],
        out_specs=pl.BlockSpec((tm, d), lambda i: (i, 0)),
        out_shape=jax.ShapeDtypeStruct((n, d), F32),
        scratch_shapes=[pltpu.VMEM((2, TOP_K, tm, d), F32), pltpu.SemaphoreType.DMA((2,))],
        compiler_params=_cparams(1),
        name="combine",
    )(dest3, dest3, wt, h1, mod3, l2g, l2b, y)


def _pick_tile(n, pref):
    t = pref
    while n % t:
        t //= 2
    return t


def kernel(x, c, ctx, c_ctx, ln_in_g, ln_in_b, w_ada, b_ada, w_in, cm_norm_g, cm_norm_b, cm_w_s, cm_b_s,
           gla_w_gk_f, gla_b_gk_f, gla_w_gk_b, gla_b_gk_b, gla_norm_g, w_out, ln1_g, ln1_b,
           w_router, b_router, w_gate_up, b_gate_up, w_down, b_down, ln2_g, ln2_b):
    bsz, l, d = x.shape
    lc = ctx.shape[1]
    n, nc = bsz * l, bsz * lc
    assert w_ada.shape[0] == 1, "single-layer configuration"
    assert bsz + 1 <= 8 and l % (2 * GLA_TILE) == 0 and lc % GLA_TILE == 0
    row = lambda v: v.reshape(1, -1)

    cc = jnp.concatenate([c, c_ctx[None, :], jnp.zeros((8 - bsz - 1, d), F32)], axis=0)
    mod3 = _ada(cc, w_ada[0], row(b_ada[0])).reshape(8, 1, N_MOD * d)

    w_in_b = w_in[0].astype(BF16)
    n_uv = 2 * CM_HEADS * CM_CHUNK
    n_main = n_uv + 2 * GLA_HEADS * GLA_DK + 2 * GLA_HEADS * GLA_DV
    w_uv = w_in_b[:, :n_uv]
    w_rest = w_in_b[:, n_uv:n_main]
    w_lr = jnp.pad(w_in_b[:, n_main:], ((0, 0), (0, LANES - 2 * GLA_RANK)))
    x2 = x.reshape(n, d)
    tm_x = _pick_tile(l, 512)
    bs_tile = jnp.repeat(cm_b_s[0].T, CM_CHUNK, axis=1)
    cm, p, lr = _inproj_cm(x2, mod3, l, row(ln_in_g), row(ln_in_b), w_uv, w_rest, w_lr,
                           row(cm_norm_g[0]), row(cm_norm_b[0]), cm_w_s[0].astype(BF16), bs_tile, tm_x)
    tm_c = _pick_tile(nc, 512)
    pc, lrc = _inproj(ctx.reshape(nc, d), mod3, lambda i: bsz, row(ln_in_g), row(ln_in_b),
                      w_rest, w_lr, 0, 2, tm_c, 1024)

    kw = GLA_HEADS * GLA_DK
    wgf = jnp.zeros((LANES, kw), BF16).at[:GLA_RANK].set(gla_w_gk_f[0].astype(BF16))
    wgb = jnp.zeros((LANES, kw), BF16).at[GLA_RANK:2 * GLA_RANK].set(gla_w_gk_b[0].astype(BF16))
    gla = _gla(p.reshape(bsz, l, -1), lr.reshape(bsz, l, LANES), pc.reshape(bsz, lc, -1),
               lrc.reshape(bsz, lc, LANES), wgf, row(gla_b_gk_f[0]), wgb, row(gla_b_gk_b[0]),
               row(gla_norm_g[0])).reshape(n, -1)

    w_r = jnp.pad(w_router[0], ((0, 0), (0, LANES - N_EXPERTS))).astype(BF16)
    b_r = jnp.pad(b_router[0], (0, LANES - N_EXPERTS)).reshape(1, LANES)
    h1, hp, logits = _post_attn(cm, gla, x2, mod3, l, row(ln_in_g), row(ln_in_b), w_out[0].astype(BF16),
                                row(ln1_g[0]), row(ln1_b[0]), w_r, b_r, tm_x)

    idx, wt, rank, cnt = _route(logits, _pick_tile(n, 1024))
    counts = cnt[0, :N_EXPERTS].astype(I32)
    n_blocks = (n * TOP_K + N_EXPERTS * (MOE_BLOCK - 1)) // MOE_BLOCK
    blocks_e = (counts + MOE_BLOCK - 1) // MOE_BLOCK
    blk_end = jnp.cumsum(blocks_e)
    blk_start = blk_end - blocks_e
    dest = ((blk_start * MOE_BLOCK)[idx[:, :TOP_K]] + rank[:, :TOP_K]).reshape(-1)
    n_pass_max = n_blocks // PASS_SUBS + N_EXPERTS
    pass_cnt = (blocks_e + PASS_SUBS - 1) // PASS_SUBS
    pass_end = jnp.cumsum(pass_cnt)
    pass_start = pass_end - pass_cnt
    pid = jnp.arange(n_pass_max, dtype=I32)
    pass_e = jnp.minimum(jnp.searchsorted(pass_end, pid, side="right"), N_EXPERTS - 1).astype(I32)
    local = pid - pass_start[pass_e]
    pass_row0 = (blk_start[pass_e] + local * PASS_SUBS).astype(I32)
    pass_nsub = jnp.clip(blocks_e[pass_e] - local * PASS_SUBS, 0, PASS_SUBS).astype(I32)
    meta = jnp.stack([pass_end[-1], blk_end[-1]]).astype(I32)

    ar = jnp.arange(N_EXPERTS, dtype=I32)
    cand = jnp.concatenate([jnp.where(counts % MOE_BLOCK != 0, blk_end - 1, -1),
                            jnp.where(blk_end[-1] + ar < n_blocks, blk_end[-1] + ar, -1)])
    zero_blocks = cand[jnp.argsort(cand < 0, stable=True)].astype(I32)
    n_zero = jnp.sum(cand >= 0).astype(I32).reshape(1)

    xs = _dispatch(n_zero, zero_blocks, dest, hp, n_blocks * MOE_BLOCK, _pick_tile(n, 512))
    ys = _experts(meta, pass_e, pass_row0, pass_nsub, xs, w_gate_up[0], b_gate_up[0], w_down[0], b_down[0])
    out = _combine(dest, wt, h1, mod3, l, row(ln2_g[0]), row(ln2_b[0]), ys, _pick_tile(n, 512))
    return out.reshape(bsz, l, d)
```

```python
import math

import jax
import jax.numpy as jnp
from jax import lax
from jax.experimental import pallas as pl
from jax.experimental.pallas import tpu as pltpu

F32 = jnp.float32
BF16 = jnp.bfloat16
U32 = jnp.uint32
I32 = jnp.int32

CM_CHUNK = 128
CM_HEADS = 8
GLA_HEADS = 4
GLA_DK = 128
GLA_DV = 256
GLA_CHUNK = 64
GLA_RANK = 16
GLA_GATE_NORMALIZER = 16.0
N_EXPERTS = 32
TOP_K = 4
MOE_BLOCK = 256
SWIGLU_LIMIT = 7.0
SWIGLU_ALPHA = 1.702
N_MOD = 6
DEEPNORM_ALPHA = 2.0 ** 0.25
LN_EPS = 1e-5
RMS_EPS = 1e-6

LANES = 128
VMEM_LIMIT = 56 * 1024 * 1024
ROW_CHUNK = 256


def _cparams(n_axes, vmem=VMEM_LIMIT):
    return pltpu.CompilerParams(dimension_semantics=("arbitrary",) * n_axes,
                                vmem_limit_bytes=vmem)


def _layer_norm(t, g, b):
    mu = jnp.mean(t, axis=-1, keepdims=True)
    d = t - mu
    var = jnp.mean(d * d, axis=-1, keepdims=True)
    return d * lax.rsqrt(var + LN_EPS) * g + b


def _gelu(t):
    return 0.5 * t * (1.0 + lax.erf(t * (1.0 / math.sqrt(2.0))))


def _silu(t):
    return t * jax.nn.sigmoid(t)


def _ones_where(mask, dtype):
    return jnp.where(mask, 1.0, 0.0).astype(dtype)


def _dot(a, b):
    return jnp.dot(a, b, preferred_element_type=F32)


def _dot_nt(a, b):
    return lax.dot_general(a, b, (((1,), (1,)), ((), ())), preferred_element_type=F32)


def _dot_tn(a, b):
    return lax.dot_general(a, b, (((0,), (0,)), ((), ())), preferred_element_type=F32)


def _pack_bf16_pair(lo, hi):
    lo_b = lax.bitcast_convert_type(lo.astype(BF16).astype(F32), U32)
    hi_b = lax.bitcast_convert_type(hi.astype(BF16).astype(F32), U32)
    return hi_b | (lo_b >> 16)


def _unpack_pair_f32(p):
    lo = lax.bitcast_convert_type(p << 16, F32)
    hi = lax.bitcast_convert_type(p & jnp.uint32(0xFFFF0000), F32)
    return lo, hi


def _ada_body(c_ref, w_ref, b_ref, o_ref):
    a = _silu(c_ref[...]).astype(BF16)
    o_ref[...] = _dot(a, w_ref[...].astype(BF16)) + b_ref[...]


def _ada(cc, w, b):
    rows, d = cc.shape
    n = w.shape[1]
    tn = 1024
    return pl.pallas_call(
        _ada_body,
        grid=(n // tn,),
        in_specs=[pl.BlockSpec((rows, d), lambda j: (0, 0)),
                  pl.BlockSpec((d, tn), lambda j: (0, j)),
                  pl.BlockSpec((1, tn), lambda j: (0, j))],
        out_specs=pl.BlockSpec((rows, tn), lambda j: (0, j)),
        out_shape=jax.ShapeDtypeStruct((rows, n), F32),
        compiler_params=_cparams(1),
        name="ada",
    )(cc, w, b)


def _inproj_body(x_ref, g_ref, b_ref, sh_ref, sc_ref, w_ref, wlr_ref, o_ref, olr_ref, hm_ref):
    @pl.when(pl.program_id(1) == 0)
    def _():
        h = _layer_norm(x_ref[...], g_ref[...], b_ref[...])
        hm = (h * (1.0 + sc_ref[...]) + sh_ref[...]).astype(BF16)
        hm_ref[...] = hm
        olr_ref[...] = _dot(hm, wlr_ref[...])

    o_ref[...] = _dot(hm_ref[...], w_ref[...])


def _inproj(x2, mod3, mod_row, ln_g, ln_b, w_main, w_lr, col0, ncols, tm, tn):
    r, d = x2.shape
    return pl.pallas_call(
        _inproj_body,
        grid=(r // tm, ncols),
        in_specs=[pl.BlockSpec((tm, d), lambda i, j: (i, 0)),
                  pl.BlockSpec((1, d), lambda i, j: (0, 0)),
                  pl.BlockSpec((1, d), lambda i, j: (0, 0)),
                  pl.BlockSpec((None, 1, d), lambda i, j: (mod_row(i), 0, 0)),
                  pl.BlockSpec((None, 1, d), lambda i, j: (mod_row(i), 0, 1)),
                  pl.BlockSpec((d, tn), lambda i, j: (0, col0 + j)),
                  pl.BlockSpec((d, LANES), lambda i, j: (0, 0))],
        out_specs=[pl.BlockSpec((tm, tn), lambda i, j: (i, j)),
                   pl.BlockSpec((tm, LANES), lambda i, j: (i, 0))],
        out_shape=[jax.ShapeDtypeStruct((r, ncols * tn), F32),
                   jax.ShapeDtypeStruct((r, LANES), F32)],
        scratch_shapes=[pltpu.VMEM((tm, d), BF16)],
        compiler_params=_cparams(2),
        name="inproj",
    )(x2, ln_g, ln_b, mod3, mod3, w_main, w_lr)


def _inproj_cm_body(x_ref, g_ref, b_ref, sh_ref, sc_ref, wuv_ref, wr_ref, wlr_ref, ng_ref, nb_ref, ws_ref, bs_ref,
                    cm_ref, p_ref, lr_ref):
    tm = x_ref.shape[0]
    half = wuv_ref.shape[1] // 2
    hd = CM_CHUNK
    for r in range(0, tm, ROW_CHUNK):
        rows = slice(r, r + ROW_CHUNK)
        h = _layer_norm(x_ref[rows, :], g_ref[...], b_ref[...])
        hm = (h * (1.0 + sc_ref[...]) + sh_ref[...]).astype(BF16)
        uv = _dot(hm, wuv_ref[...])
        u = _gelu(uv[:, :half])
        vb = _layer_norm(_gelu(uv[:, half:]), ng_ref[...], nb_ref[...]).astype(BF16)
        for c in range(ROW_CHUNK // CM_CHUNK):
            crow = slice(c * CM_CHUNK, (c + 1) * CM_CHUNK)
            orow = slice(r + c * CM_CHUNK, r + (c + 1) * CM_CHUNK)
            for hh in range(CM_HEADS):
                cols = slice(hh * hd, (hh + 1) * hd)
                s = _dot(ws_ref[hh], vb[crow, cols]) + bs_ref[:, cols]
                cm_ref[orow, cols] = (u[crow, cols] * s).astype(BF16)
        p_ref[rows, :] = _dot(hm, wr_ref[...])
        lr_ref[rows, :] = _dot(hm, wlr_ref[...])


def _inproj_cm(x2, mod3, rows_per_batch, ln_g, ln_b, w_uv, w_rest, w_lr, ng, nb, ws, bs, tm):
    r, d = x2.shape
    n_uv, n_rest = w_uv.shape[1], w_rest.shape[1]
    row = lambda i: (i * tm) // rows_per_batch

    def const(shape):
        return pl.BlockSpec(shape, lambda i: (0,) * len(shape), pipeline_mode=pl.Buffered(1))

    return pl.pallas_call(
        _inproj_cm_body,
        grid=(r // tm,),
        in_specs=[pl.BlockSpec((tm, d), lambda i: (i, 0)),
                  const((1, d)), const((1, d)),
                  pl.BlockSpec((None, 1, d), lambda i: (row(i), 0, 0)),
                  pl.BlockSpec((None, 1, d), lambda i: (row(i), 0, 1)),
                  const((d, n_uv)), const((d, n_rest)), const((d, LANES)),
                  const((1, n_uv // 2)), const((1, n_uv // 2)),
                  const((CM_HEADS, CM_CHUNK, CM_CHUNK)), const((CM_CHUNK, n_uv // 2))],
        out_specs=[pl.BlockSpec((tm, n_uv // 2), lambda i: (i, 0)),
                   pl.BlockSpec((tm, n_rest), lambda i: (i, 0)),
                   pl.BlockSpec((tm, LANES), lambda i: (i, 0))],
        out_shape=[jax.ShapeDtypeStruct((r, n_uv // 2), BF16),
                   jax.ShapeDtypeStruct((r, n_rest), F32),
                   jax.ShapeDtypeStruct((r, LANES), F32)],
        compiler_params=_cparams(1),
        name="inproj_cm",
    )(x2, ln_g, ln_b, mod3, mod3, w_uv, w_rest, w_lr, ng, nb, ws, bs)


GLA_TILE = 256
GLA_TILES_PER_ITER = 2


def _gla_tile(q, k, v, lr, wg, bg, st_ref, forward, need_o):
    t = k.shape[0]
    n_chunks = t // GLA_CHUNK
    z = _dot(lr.astype(BF16), wg) + bg
    g = jax.nn.log_sigmoid(z) * (1.0 / GLA_GATE_NORMALIZER)
    r_id = lax.broadcasted_iota(I32, (t, t), 0)
    c_id = lax.broadcasted_iota(I32, (t, t), 1)
    shift = GLA_CHUNK.bit_length() - 1
    same = (r_id >> shift) == (c_id >> shift)
    lower = same & (c_id <= r_id)
    tri = _ones_where(lower, BF16)
    g_hi = g.astype(BF16)
    g_lo = (g - g_hi.astype(F32)).astype(BF16)
    csum = _dot(tri, g_hi) + _dot(tri, g_lo)
    g3 = g.reshape(n_chunks, GLA_CHUNK, GLA_DK)
    tot = jnp.broadcast_to(jnp.sum(g3, axis=1, keepdims=True), g3.shape).reshape(t, GLA_DK)
    bcum = csum if forward else tot - csum + g
    kd = (k * jnp.exp(tot - bcum)).astype(BF16)
    decay = jnp.exp(tot)
    vb = v.astype(BF16)
    o = None
    if need_o:
        qe = ((q * (GLA_DK ** -0.5)) * jnp.exp(bcum)).astype(BF16)
        ke = (k * jnp.exp(-bcum)).astype(BF16)
        att = _dot_nt(qe, ke)
        mask = lower if forward else same & (c_id >= r_id)
        att = jnp.where(mask, att, 0.0).astype(BF16)
        o = _dot(att, vb)
    outs = [None] * n_chunks
    order = range(n_chunks) if forward else range(n_chunks - 1, -1, -1)
    for c in order:
        rows = slice(c * GLA_CHUNK, (c + 1) * GLA_CHUNK)
        s_t = st_ref[...]
        if need_o:
            outs[c] = o[rows] + _dot_nt(qe[rows], s_t.astype(BF16))
        u_t = _dot_tn(vb[rows], kd[rows])
        st_ref[...] = s_t * decay[c * GLA_CHUNK:c * GLA_CHUNK + 1, :] + u_t
    if need_o:
        return jnp.concatenate(outs, axis=0)
    return None


def _gla_body(q_ref, k_ref, v_ref, go_ref, lr_ref, kc_ref, vc_ref, lrc_ref,
              wgf_ref, bgf_ref, wgb_ref, bgb_ref, ng_ref, out_ref, o_scr, sf_ref, sb_ref):
    t = GLA_TILE
    n_x = q_ref.shape[0] // t
    n_c = kc_ref.shape[0] // t
    half = n_x // 2
    sf_ref[...] = jnp.zeros_like(sf_ref)
    sb_ref[...] = jnp.zeros_like(sb_ref)
    wgf, bgf, wgb, bgb = wgf_ref[...], bgf_ref[...], wgb_ref[...], bgb_ref[...]

    for i in range(n_c):
        rf = slice(i * t, (i + 1) * t)
        rb = slice((n_c - 1 - i) * t, (n_c - i) * t)
        _gla_tile(None, kc_ref[rf], vc_ref[rf], lrc_ref[rf], wgf, bgf, sf_ref, True, False)
        _gla_tile(None, kc_ref[rb], vc_ref[rb], lrc_ref[rb], wgb, bgb, sb_ref, False, False)

    def tile_out(i, forward):
        rows = pl.ds(pl.multiple_of(i * t, t), t)
        if forward:
            return rows, _gla_tile(q_ref[rows], k_ref[rows], v_ref[rows], lr_ref[rows],
                                   wgf, bgf, sf_ref, True, True)
        return rows, _gla_tile(q_ref[rows], k_ref[rows], v_ref[rows], lr_ref[rows],
                               wgb, bgb, sb_ref, False, True)

    def finish(rows, o):
        o = o + o_scr[rows]
        ms = jnp.mean(o * o, axis=-1, keepdims=True)
        on = o * lax.rsqrt(ms + RMS_EPS) * ng_ref[...]
        out_ref[rows] = (on * _silu(go_ref[rows])).astype(BF16)

    def keep(rows, o):
        o_scr[rows] = o

    u = math.gcd(GLA_TILES_PER_ITER, half)

    def make_step(sink):
        def step(it, carry):
            for w in range(u):
                i = it * u + w
                sink(*tile_out(i, True))
                sink(*tile_out(n_x - 1 - i, False))
            return carry
        return step

    lax.fori_loop(0, half // u, make_step(keep), 0)
    lax.fori_loop(half // u, n_x // u, make_step(finish), 0)


def _gla(p3, lr3, pc3, lrc3, wgf, bgf, wgb, bgb, ng):
    bsz, l, _ = p3.shape
    lc = pc3.shape[1]
    dk, dv = GLA_DK, GLA_DV
    kw = GLA_HEADS * dk
    q0, k0 = 0, kw // dk
    v0, go0 = 2 * kw // dv, (2 * kw + GLA_HEADS * dv) // dv
    kc0, vc0 = k0, v0
    return pl.pallas_call(
        _gla_body,
        grid=(bsz, GLA_HEADS),
        in_specs=[pl.BlockSpec((None, l, dk), lambda b, h: (b, 0, q0 + h)),
                  pl.BlockSpec((None, l, dk), lambda b, h: (b, 0, k0 + h)),
                  pl.BlockSpec((None, l, dv), lambda b, h: (b, 0, v0 + h)),
                  pl.BlockSpec((None, l, dv), lambda b, h: (b, 0, go0 + h)),
                  pl.BlockSpec((None, l, LANES), lambda b, h: (b, 0, 0)),
                  pl.BlockSpec((None, lc, dk), lambda b, h: (b, 0, kc0 + h)),
                  pl.BlockSpec((None, lc, dv), lambda b, h: (b, 0, vc0 + h)),
                  pl.BlockSpec((None, lc, LANES), lambda b, h: (b, 0, 0)),
                  pl.BlockSpec((LANES, dk), lambda b, h: (0, h)),
                  pl.BlockSpec((1, dk), lambda b, h: (0, h)),
                  pl.BlockSpec((LANES, dk), lambda b, h: (0, h)),
                  pl.BlockSpec((1, dk), lambda b, h: (0, h)),
                  pl.BlockSpec((1, dv), lambda b, h: (0, 0))],
        out_specs=pl.BlockSpec((None, l, dv), lambda b, h: (b, 0, h)),
        out_shape=jax.ShapeDtypeStruct((bsz, l, GLA_HEADS * dv), BF16),
        scratch_shapes=[pltpu.VMEM((l, dv), F32),
                        pltpu.VMEM((dv, dk), F32),
                        pltpu.VMEM((dv, dk), F32)],
        compiler_params=_cparams(2),
        name="gla",
    )(p3, p3, p3, p3, lr3, pc3, pc3, lrc3, wgf, bgf, wgb, bgb, ng)


def _post_body(cm_ref, gla_ref, x_ref, lng_ref, lnb_ref, g1_ref, sh2_ref, sc2_ref,
               wo_ref, l1g_ref, l1b_ref, wr_ref, br_ref, h1_ref, hp_ref, lg_ref):
    half = cm_ref.shape[1]
    d2 = x_ref.shape[1] // 2
    for r in range(0, x_ref.shape[0], ROW_CHUNK):
        rows = slice(r, r + ROW_CHUNK)
        y = _dot(cm_ref[rows, :], wo_ref[:half, :]) + _dot(gla_ref[rows, :], wo_ref[half:, :])
        hx = _layer_norm(x_ref[rows, :], lng_ref[...], lnb_ref[...])
        h1 = _layer_norm(DEEPNORM_ALPHA * hx + g1_ref[...] * y, l1g_ref[...], l1b_ref[...])
        h1_ref[rows, :] = h1
        hm = h1 * (1.0 + sc2_ref[...]) + sh2_ref[...]
        hp_ref[rows, :] = _pack_bf16_pair(hm[:, :d2], hm[:, d2:])
        lg_ref[rows, :] = _dot(hm.astype(BF16), wr_ref[...]) + br_ref[...]


def _post_attn(cm, gla, x2, mod3, rows_per_batch, ln_g, ln_b, w_out, l1g, l1b, w_r, b_r, tm):
    r, d = x2.shape
    half = d // 2
    row = lambda i: (i * tm) // rows_per_batch
    full = lambda shape: pl.BlockSpec(shape, lambda i: (0,) * len(shape))
    return pl.pallas_call(
        _post_body,
        grid=(r // tm,),
        in_specs=[pl.BlockSpec((tm, half), lambda i: (i, 0)),
                  pl.BlockSpec((tm, half), lambda i: (i, 0)),
                  pl.BlockSpec((tm, d), lambda i: (i, 0)),
                  full((1, d)), full((1, d)),
                  pl.BlockSpec((None, 1, d), lambda i: (row(i), 0, 2)),
                  pl.BlockSpec((None, 1, d), lambda i: (row(i), 0, 3)),
                  pl.BlockSpec((None, 1, d), lambda i: (row(i), 0, 4)),
                  full((d, d)), full((1, d)), full((1, d)),
                  full((d, LANES)), full((1, LANES))],
        out_specs=[pl.BlockSpec((tm, d), lambda i: (i, 0)),
                   pl.BlockSpec((tm, half), lambda i: (i, 0)),
                   pl.BlockSpec((tm, LANES), lambda i: (i, 0))],
        out_shape=[jax.ShapeDtypeStruct((r, d), F32),
                   jax.ShapeDtypeStruct((r, half), U32),
                   jax.ShapeDtypeStruct((r, LANES), F32)],
        compiler_params=_cparams(1),
        name="post_attn",
    )(cm, gla, x2, ln_g, ln_b, mod3, mod3, mod3, w_out, l1g, l1b, w_r, b_r)


def _route_body(lg_ref, idx_ref, wt_ref, rank_ref, cnt_ref, carry_ref):
    i = pl.program_id(0)
    t = lg_ref.shape[0]

    @pl.when(i == 0)
    def _():
        carry_ref[...] = jnp.zeros_like(carry_ref)

    lane = lax.broadcasted_iota(I32, (t, LANES), 1)
    lane_f = lane.astype(F32)
    neg = jnp.float32(-jnp.inf)
    l = jnp.where(lane < N_EXPERTS, lg_ref[...], neg)
    tops, onehots, idxs = [], [], []
    for _ in range(TOP_K):
        m = jnp.max(l, axis=-1, keepdims=True)
        idx = jnp.min(jnp.where(l == m, lane_f, float(LANES)), axis=-1, keepdims=True).astype(I32)
        oh = lane == idx
        l = jnp.where(oh, neg, l)
        tops.append(m)
        idxs.append(idx)
        onehots.append(oh)
    exps = [jnp.exp(m - tops[0]) for m in tops]
    denom = exps[0] + exps[1] + exps[2] + exps[3]
    sel = _ones_where(onehots[0] | onehots[1] | onehots[2] | onehots[3], F32)
    r_id = lax.broadcasted_iota(I32, (t, t), 0)
    c_id = lax.broadcasted_iota(I32, (t, t), 1)
    strict = _ones_where(c_id < r_id, BF16)
    before = _dot(strict, sel.astype(BF16)) + carry_ref[0:1, :]
    idx_out = jnp.zeros((t, LANES), I32)
    wt_out = jnp.zeros((t, LANES), F32)
    rank_out = jnp.zeros((t, LANES), I32)
    for k in range(TOP_K):
        rk = jnp.sum(jnp.where(onehots[k], before, 0.0), axis=-1, keepdims=True).astype(I32)
        idx_out = jnp.where(lane == k, idxs[k], idx_out)
        wt_out = jnp.where(lane == k, exps[k] / denom, wt_out)
        rank_out = jnp.where(lane == k, rk, rank_out)
    idx_ref[...] = idx_out[:, :TOP_K]
    wt_ref[...] = wt_out
    rank_ref[...] = rank_out[:, :TOP_K]
    total = carry_ref[0:1, :] + jnp.sum(sel, axis=0, keepdims=True)
    carry_ref[...] = jnp.broadcast_to(total, carry_ref.shape)
    cnt_ref[...] = jnp.broadcast_to(total, cnt_ref.shape)


def _route(logits, tm):
    n = logits.shape[0]
    blk = pl.BlockSpec((tm, LANES), lambda i: (i, 0))
    small = pl.BlockSpec((tm, TOP_K), lambda i: (i, 0))
    return pl.pallas_call(
        _route_body,
        grid=(n // tm,),
        in_specs=[blk],
        out_specs=[small, blk, small, pl.BlockSpec((8, LANES), lambda i: (0, 0))],
        out_shape=[jax.ShapeDtypeStruct((n, TOP_K), I32),
                   jax.ShapeDtypeStruct((n, LANES), F32),
                   jax.ShapeDtypeStruct((n, TOP_K), I32),
                   jax.ShapeDtypeStruct((8, LANES), F32)],
        scratch_shapes=[pltpu.VMEM((8, LANES), F32)],
        compiler_params=_cparams(1),
        name="route",
    )(logits)


def _dispatch_body(nz_ref, zl_ref, dest_ref, hp_ref, xs_ref, zero_ref, sem, zsem):
    t = hp_ref.shape[0]
    unroll = 4

    @pl.when(pl.program_id(0) == 0)
    def _():
        zero_ref[...] = jnp.zeros_like(zero_ref)

        def z_copy(b):
            r = pl.multiple_of(zl_ref[b] * MOE_BLOCK, MOE_BLOCK)
            return pltpu.make_async_copy(zero_ref, xs_ref.at[pl.ds(r, MOE_BLOCK)], zsem)

        def start(b, carry):
            z_copy(b).start()
            return carry

        def wait(b, carry):
            z_copy(b).wait()
            return carry

        lax.fori_loop(0, nz_ref[0], start, 0)
        lax.fori_loop(0, nz_ref[0], wait, 0)

    def issue(it, carry):
        for u in range(unroll):
            tok = it * unroll + u
            for k in range(TOP_K):
                d = dest_ref[0, tok * TOP_K + k]
                pltpu.make_async_copy(hp_ref.at[pl.ds(tok, 1)], xs_ref.at[pl.ds(d, 1)], sem).start()
        return carry

    lax.fori_loop(0, t // unroll, issue, 0)
    for _ in range(TOP_K):
        pltpu.make_async_copy(hp_ref, xs_ref.at[pl.ds(0, t)], sem).wait()


def _dispatch(n_zero, zero_blocks, dest, hp, n_slots, tm):
    n, w = hp.shape
    dest3 = dest.reshape(n // tm, 1, tm * TOP_K)
    return pl.pallas_call(
        _dispatch_body,
        grid_spec=pltpu.PrefetchScalarGridSpec(
            num_scalar_prefetch=2,
            grid=(n // tm,),
            in_specs=[pl.BlockSpec((None, 1, tm * TOP_K), lambda i, nz, zl: (i, 0, 0), memory_space=pltpu.SMEM),
                      pl.BlockSpec((tm, w), lambda i, nz, zl: (i, 0))],
            out_specs=pl.BlockSpec(memory_space=pl.ANY),
            scratch_shapes=[pltpu.VMEM((MOE_BLOCK, w), U32),
                            pltpu.SemaphoreType.DMA(()),
                            pltpu.SemaphoreType.DMA(())]),
        out_shape=jax.ShapeDtypeStruct((n_slots, w), U32),
        compiler_params=_cparams(1),
        name="dispatch",
    )(n_zero, zero_blocks, dest3, hp)


PASS_SUBS = 9
EXPERT_TN = 256
DOT_ROWS = 1024
W_SLOTS = 3
W_AHEAD = 2


def _expert_body(meta_ref, pe_ref, pr_ref, pn_ref, xs_hbm, wgu_hbm, wdn_hbm, bgu_ref, bdn_ref,
                 y_hbm, xraw, xb, h_ref, wg_buf, wu_buf, wd_buf, ybuf, zbuf,
                 sem_x, sem_y, sem_z, sem_w, sem_d):
    sub = MOE_BLOCK
    tn = EXPERT_TN
    tp = tn // 2
    nj = h_ref.shape[0]
    nc = y_hbm.shape[1] // tp
    de = nj * tn
    d2 = xraw.shape[1]
    p = pl.program_id(0)
    n_pass = meta_ref[0]
    nsub = pn_ref[p]
    row0 = pr_ref[p]

    def x_copy(i, pp):
        r = pl.multiple_of((pr_ref[pp] + i) * sub, sub)
        return pltpu.make_async_copy(xs_hbm.at[pl.ds(r, sub)], xraw.at[pl.ds(i * sub, sub)], sem_x)

    def y_copy(i, r0, c, slot):
        r = pl.multiple_of((r0 + i) * sub, sub)
        col = pl.multiple_of(c * tp, tp)
        return pltpu.make_async_copy(ybuf.at[slot, pl.ds(i * sub, sub), :],
                                     y_hbm.at[pl.ds(r, sub), pl.ds(col, tp)], sem_y.at[slot])

    def z_copy(b, c):
        r = pl.multiple_of(b * sub, sub)
        return pltpu.make_async_copy(zbuf, y_hbm.at[pl.ds(r, sub), pl.ds(c * tp, tp)], sem_z)

    def w1_copies(pp, j):
        e = pe_ref[pp]
        slot = j % W_SLOTS
        col = pl.multiple_of(j * tn, tn)
        return (pltpu.make_async_copy(wgu_hbm.at[e, :, pl.ds(col, tn)], wg_buf.at[slot], sem_w.at[slot]),
                pltpu.make_async_copy(wgu_hbm.at[e, :, pl.ds(de + col, tn)], wu_buf.at[slot], sem_w.at[slot]))

    def w2_copy(pp, c):
        slot = c % W_SLOTS
        col = pl.multiple_of(c * tn, tn)
        return pltpu.make_async_copy(wdn_hbm.at[pe_ref[pp], :, pl.ds(col, tn)], wd_buf.at[slot], sem_d.at[slot])

    def start_w1(pp, j):
        for cp in w1_copies(pp, j):
            cp.start()

    def for_subs(count, fn):
        for i in range(PASS_SUBS):
            pl.when(i < count)(lambda i=i: fn(i))

    def for_tail(fn):
        def body(b, carry):
            for c in range(nc):
                fn(b, c)
            return carry
        lax.fori_loop(meta_ref[1], y_hbm.shape[0] // sub, body, 0)

    def for_groups(fn):
        pl.when(nsub == PASS_SUBS)(lambda: fn(0, PASS_SUBS * sub))
        k = 1 << (PASS_SUBS.bit_length() - 1)
        while k:
            start = pl.multiple_of((nsub & (-2 * k)) * sub, sub)
            pl.when((nsub != PASS_SUBS) & ((nsub & k) != 0))(lambda start=start, k=k: fn(start, k * sub))
            k //= 2

    @pl.when(p == 0)
    def _():
        for_subs(nsub, lambda i: x_copy(i, p).start())
        for j in range(W_AHEAD):
            start_w1(p, j)
        zbuf[...] = jnp.zeros_like(zbuf)
        for_tail(lambda b, c: z_copy(b, c).start())

    for_subs(nsub, lambda i: x_copy(i, p).wait())

    def unpack(i):
        rows = slice(i * sub, (i + 1) * sub)
        lo, hi = _unpack_pair_f32(xraw[rows, :])
        xb[rows, :d2] = lo.astype(BF16)
        xb[rows, d2:] = hi.astype(BF16)

    for_subs(nsub, unpack)

    @pl.when(p + 1 < n_pass)
    def _():
        for_subs(pn_ref[p + 1], lambda i: x_copy(i, p + 1).start())

    def first_step(j, carry):
        slot = j % W_SLOTS
        for cp in w1_copies(p, j):
            cp.wait()
        nxt = j + W_AHEAD
        pl.when(nxt < nj)(lambda: start_w1(p, nxt))
        pl.when(nxt >= nj)(lambda: w2_copy(p, nxt - nj).start())

        def group(start, size):
            wg = wg_buf[slot].astype(BF16)
            wu = wu_buf[slot].astype(BF16)
            dr = DOT_ROWS if size % DOT_ROWS == 0 else size
            for r in range(0, size, dr):
                rows = pl.ds(start + r, dr)
                x = xb[rows, :]
                gate = jnp.minimum(_dot(x, wg) + bgu_ref[j], SWIGLU_LIMIT)
                up = jnp.clip(_dot(x, wu) + bgu_ref[nj + j], -SWIGLU_LIMIT, SWIGLU_LIMIT)
                h_ref[j, rows, :] = ((up + 1.0) * gate * jax.nn.sigmoid(SWIGLU_ALPHA * gate)).astype(BF16)

        for_groups(group)
        return carry

    lax.fori_loop(0, nj, first_step, 0)

    def second_step(c, carry):
        wslot = c % W_SLOTS
        slot = c % 2
        w2_copy(p, c).wait()
        nxt = c + W_AHEAD
        pl.when(nxt < nc)(lambda: w2_copy(p, nxt).start())
        pl.when((nxt >= nc) & (p + 1 < n_pass))(lambda: start_w1(p + 1, nxt - nc))

        @pl.when(c >= 2)
        def _():
            for_subs(nsub, lambda i: y_copy(i, row0, c - 2, slot).wait())

        @pl.when((c < 2) & (p > 0))
        def _():
            for_subs(pn_ref[p - 1], lambda i: y_copy(i, pr_ref[p - 1], nc - 2 + c, slot).wait())

        def group(start, size):
            wd = wd_buf[wslot].astype(BF16)
            dr = DOT_ROWS if size % DOT_ROWS == 0 else size
            for r in range(0, size, dr):
                rows = pl.ds(start + r, dr)
                hx = jnp.concatenate([h_ref[j, rows, :] for j in range(nj)], axis=1)
                y = _dot(hx, wd) + bdn_ref[c]
                ybuf[slot, rows, :] = _pack_bf16_pair(y[:, :tp], y[:, tp:])

        for_groups(group)
        for_subs(nsub, lambda i: y_copy(i, row0, c, slot).start())
        return carry

    lax.fori_loop(0, nc, second_step, 0)

    @pl.when(p == n_pass - 1)
    def _():
        for c in (nc - 2, nc - 1):
            for_subs(nsub, lambda i, c=c: y_copy(i, row0, c, c % 2).wait())
        for_tail(lambda b, cc: z_copy(b, cc).wait())


def _experts(meta, pass_e, pass_row0, pass_nsub, xs, w_gu, b_gu, w_dn, b_dn):
    n_slots, d2 = xs.shape
    d = 2 * d2
    n_e, de = w_dn.shape[0], w_dn.shape[1]
    tn = EXPERT_TN
    nj = de // tn
    nc = d // tn
    assert W_AHEAD < W_SLOTS and W_AHEAD <= min(nj, nc) and nc % 2 == 0
    rmax = PASS_SUBS * MOE_BLOCK
    return pl.pallas_call(
        _expert_body,
        grid_spec=pltpu.PrefetchScalarGridSpec(
            num_scalar_prefetch=4,
            grid=(meta[0],),
            in_specs=[
                pl.BlockSpec(memory_space=pl.ANY),
                pl.BlockSpec(memory_space=pl.ANY),
                pl.BlockSpec(memory_space=pl.ANY),
                pl.BlockSpec((None, 2 * nj, 1, tn), lambda p, m, e, r, n: (e[p], 0, 0, 0)),
                pl.BlockSpec((None, nc, 1, tn), lambda p, m, e, r, n: (e[p], 0, 0, 0)),
            ],
            out_specs=pl.BlockSpec(memory_space=pl.ANY),
            scratch_shapes=[pltpu.VMEM((rmax, d2), U32),
                            pltpu.VMEM((rmax, d), BF16),
                            pltpu.VMEM((nj, rmax, tn), BF16),
                            pltpu.VMEM((W_SLOTS, d, tn), F32),
                            pltpu.VMEM((W_SLOTS, d, tn), F32),
                            pltpu.VMEM((W_SLOTS, de, tn), F32),
                            pltpu.VMEM((2, rmax, tn // 2), U32),
                            pltpu.VMEM((MOE_BLOCK, tn // 2), U32),
                            pltpu.SemaphoreType.DMA(()),
                            pltpu.SemaphoreType.DMA((2,)),
                            pltpu.SemaphoreType.DMA(()),
                            pltpu.SemaphoreType.DMA((W_SLOTS,)),
                            pltpu.SemaphoreType.DMA((W_SLOTS,))]),
        out_shape=jax.ShapeDtypeStruct((n_slots, d2), U32),
        compiler_params=_cparams(1),
        name="experts",
    )(meta, pass_e, pass_row0, pass_nsub, xs, w_gu, w_dn,
      b_gu.reshape(n_e, 2 * nj, 1, tn), b_dn.reshape(n_e, nc, 1, tn))


ROW_DMA_UNROLL = 2
COMBINE_CHUNK = 64


def _combine_body(dcur_ref, dnxt_ref, wt_ref, h1_ref, g2_ref, l2g_ref, l2b_ref, y_ref, out_ref, buf_ref, sem):
    i = pl.program_id(0)
    n_tiles = pl.num_programs(0)
    t = h1_ref.shape[0]
    tp = EXPERT_TN // 2
    slot = i % 2

    def gather(dest_ref, sl):
        def body(it, carry):
            for u in range(ROW_DMA_UNROLL):
                tok = it * ROW_DMA_UNROLL + u
                for k in range(TOP_K):
                    d = dest_ref[0, tok * TOP_K + k]
                    pltpu.make_async_copy(y_ref.at[pl.ds(d, 1)], buf_ref.at[sl, k, pl.ds(tok, 1)],
                                          sem.at[sl]).start()
            return carry
        lax.fori_loop(0, t // ROW_DMA_UNROLL, body, 0)

    pl.when(i == 0)(lambda: gather(dcur_ref, 0))
    pl.when(i + 1 < n_tiles)(lambda: gather(dnxt_ref, 1 - slot))
    for k in range(TOP_K):
        pltpu.make_async_copy(y_ref.at[pl.ds(0, t)], buf_ref.at[slot, k], sem.at[slot]).wait()
    for r in range(0, t, COMBINE_CHUNK):
        rows = slice(r, r + COMBINE_CHUNK)
        wt = wt_ref[rows, :]
        f_lo, f_hi = None, None
        for k in range(TOP_K):
            lo, hi = _unpack_pair_f32(buf_ref[slot, k, rows, :])
            w = wt[:, k:k + 1]
            f_lo = lo * w if f_lo is None else f_lo + lo * w
            f_hi = hi * w if f_hi is None else f_hi + hi * w
        parts = []
        for c in range(f_lo.shape[1] // tp):
            parts += [f_lo[:, c * tp:(c + 1) * tp], f_hi[:, c * tp:(c + 1) * tp]]
        f = jnp.concatenate(parts, axis=1)
        out_ref[rows, :] = _layer_norm(DEEPNORM_ALPHA * h1_ref[rows, :] + g2_ref[...] * f,
                                       l2g_ref[...], l2b_ref[...])


def _combine(dest, wt, h1, mod3, rows_per_batch, l2g, l2b, y, tm):
    n, d = h1.shape
    n_tiles = n // tm
    dest3 = dest.reshape(n_tiles, 1, tm * TOP_K)
    row = lambda i: (i * tm) // rows_per_batch
    return pl.pallas_call(
        _combine_body,
        grid=(n_tiles,),
        in_specs=[pl.BlockSpec((None, 1, tm * TOP_K), lambda i: (i, 0, 0), memory_space=pltpu.SMEM),
                  pl.BlockSpec((None, 1, tm * TOP_K), lambda i: (jnp.minimum(i + 1, n_tiles - 1), 0, 0),
                               memory_space=pltpu.SMEM),
                  pl.BlockSpec((tm, LANES), lambda i: (i, 0)),
                  pl.BlockSpec((tm, d), lambda i: (i, 0)),
                  pl.BlockSpec((None, 1, d), lambda i: (row(i), 0, 5)),
                  pl.BlockSpec((1, d), lambda i: (0, 0)),
                  pl.BlockSpec((1, d), lambda i: (0, 0)),
                  pl.BlockSpec(memory_space=pl.ANY)],
        out_specs=pl.BlockSpec((tm, d), lambda i: (i, 0)),
        out_shape=jax.ShapeDtypeStruct((n, d), F32),
        scratch_shapes=[pltpu.VMEM((2, TOP_K, tm, y.shape[1]), U32), pltpu.SemaphoreType.DMA((2,))],
        compiler_params=_cparams(1),
        name="combine",
    )(dest3, dest3, wt, h1, mod3, l2g, l2b, y)


def _pick_tile(n, pref):
    t = pref
    while n % t:
        t //= 2
    return t


def kernel(x, c, ctx, c_ctx, ln_in_g, ln_in_b, w_ada, b_ada, w_in, cm_norm_g, cm_norm_b, cm_w_s, cm_b_s,
           gla_w_gk_f, gla_b_gk_f, gla_w_gk_b, gla_b_gk_b, gla_norm_g, w_out, ln1_g, ln1_b,
           w_router, b_router, w_gate_up, b_gate_up, w_down, b_down, ln2_g, ln2_b):
    bsz, l, d = x.shape
    lc = ctx.shape[1]
    n, nc = bsz * l, bsz * lc
    assert w_ada.shape[0] == 1, "single-layer configuration"
    assert bsz + 1 <= 8 and l % (2 * GLA_TILE) == 0 and lc % GLA_TILE == 0
    row = lambda v: v.reshape(1, -1)

    cc = jnp.concatenate([c, c_ctx[None, :], jnp.zeros((8 - bsz - 1, d), F32)], axis=0)
    mod3 = _ada(cc, w_ada[0], row(b_ada[0])).reshape(8, 1, N_MOD * d)

    n_uv = 2 * CM_HEADS * CM_CHUNK
    n_main = n_uv + 2 * GLA_HEADS * GLA_DK + 2 * GLA_HEADS * GLA_DV
    w_uv = w_in[0][:, :n_uv].astype(BF16)
    w_rest = w_in[0][:, n_uv:n_main].astype(BF16)
    w_lr = jnp.pad(w_in[0][:, n_main:].astype(BF16), ((0, 0), (0, LANES - 2 * GLA_RANK)))
    x2 = x.reshape(n, d)
    tm_x = _pick_tile(l, 512)
    bs_tile = jnp.repeat(cm_b_s[0].T, CM_CHUNK, axis=1)
    cm, p, lr = _inproj_cm(x2, mod3, l, row(ln_in_g), row(ln_in_b), w_uv, w_rest, w_lr,
                           row(cm_norm_g[0]), row(cm_norm_b[0]), cm_w_s[0].astype(BF16), bs_tile, tm_x)
    tm_c = _pick_tile(nc, 512)
    pc, lrc = _inproj(ctx.reshape(nc, d), mod3, lambda i: bsz, row(ln_in_g), row(ln_in_b),
                      w_rest, w_lr, 0, 2, tm_c, 1024)

    kw = GLA_HEADS * GLA_DK
    wgf = jnp.zeros((LANES, kw), BF16).at[:GLA_RANK].set(gla_w_gk_f[0].astype(BF16))
    wgb = jnp.zeros((LANES, kw), BF16).at[GLA_RANK:2 * GLA_RANK].set(gla_w_gk_b[0].astype(BF16))
    gla = _gla(p.reshape(bsz, l, -1), lr.reshape(bsz, l, LANES), pc.reshape(bsz, lc, -1),
               lrc.reshape(bsz, lc, LANES), wgf, row(gla_b_gk_f[0]), wgb, row(gla_b_gk_b[0]),
               row(gla_norm_g[0])).reshape(n, -1)

    w_r = jnp.pad(w_router[0], ((0, 0), (0, LANES - N_EXPERTS))).astype(BF16)
    b_r = jnp.pad(b_router[0], (0, LANES - N_EXPERTS)).reshape(1, LANES)
    h1, hp, logits = _post_attn(cm, gla, x2, mod3, l, row(ln_in_g), row(ln_in_b), w_out[0].astype(BF16),
                                row(ln1_g[0]), row(ln1_b[0]), w_r, b_r, tm_x)

    idx, wt, rank, cnt = _route(logits, _pick_tile(n, 1024))
    counts = cnt[0, :N_EXPERTS].astype(I32)
    n_blocks = (n * TOP_K + N_EXPERTS * (MOE_BLOCK - 1)) // MOE_BLOCK
    blocks_e = (counts + MOE_BLOCK - 1) // MOE_BLOCK
    blk_end = jnp.cumsum(blocks_e)
    blk_start = blk_end - blocks_e
    dest = ((blk_start * MOE_BLOCK)[idx] + rank).reshape(-1)
    n_pass_max = n_blocks // PASS_SUBS + N_EXPERTS
    pass_cnt = (blocks_e + PASS_SUBS - 1) // PASS_SUBS
    pass_end = jnp.cumsum(pass_cnt)
    pass_start = pass_end - pass_cnt
    pid = jnp.arange(n_pass_max, dtype=I32)
    pass_e = jnp.minimum(jnp.searchsorted(pass_end, pid, side="right"), N_EXPERTS - 1).astype(I32)
    local = pid - pass_start[pass_e]
    pass_row0 = (blk_start[pass_e] + local * PASS_SUBS).astype(I32)
    pass_nsub = jnp.clip(blocks_e[pass_e] - local * PASS_SUBS, 0, PASS_SUBS).astype(I32)
    meta = jnp.stack([pass_end[-1], blk_end[-1]]).astype(I32)

    ar = jnp.arange(N_EXPERTS, dtype=I32)
    cand = jnp.concatenate([jnp.where(counts % MOE_BLOCK != 0, blk_end - 1, -1),
                            jnp.where(blk_end[-1] + ar < n_blocks, blk_end[-1] + ar, -1)])
    zero_blocks = cand[jnp.argsort(cand < 0, stable=True)].astype(I32)
    n_zero = jnp.sum(cand >= 0).astype(I32).reshape(1)

    xs = _dispatch(n_zero, zero_blocks, dest, hp, n_blocks * MOE_BLOCK, _pick_tile(n, 512))
    ys = _experts(meta, pass_e, pass_row0, pass_nsub, xs, w_gate_up[0], b_gate_up[0], w_down[0], b_down[0])
    out = _combine(dest, wt, h1, mod3, l, row(ln2_g[0]), row(ln2_b[0]), ys, _pick_tile(n, 512))
    return out.reshape(bsz, l, d)
```

```python
import math

import jax
import jax.numpy as jnp
from jax import lax
from jax.experimental import pallas as pl
from jax.experimental.pallas import tpu as pltpu

F32 = jnp.float32
BF16 = jnp.bfloat16
U32 = jnp.uint32
I32 = jnp.int32

CM_CHUNK = 128
CM_HEADS = 8
GLA_HEADS = 4
GLA_DK = 128
GLA_DV = 256
GLA_CHUNK = 64
GLA_RANK = 16
GLA_GATE_NORMALIZER = 16.0
N_EXPERTS = 32
TOP_K = 4
MOE_BLOCK = 256
SWIGLU_LIMIT = 7.0
SWIGLU_ALPHA = 1.702
N_MOD = 6
DEEPNORM_ALPHA = 2.0 ** 0.25
LN_EPS = 1e-5
RMS_EPS = 1e-6

LANES = 128
SUBLANES = 8
VMEM_LIMIT = 56 * 1024 * 1024
ROW_CHUNK = 256


def _cparams(n_axes, vmem=VMEM_LIMIT):
    return pltpu.CompilerParams(dimension_semantics=("arbitrary",) * n_axes,
                                vmem_limit_bytes=vmem)


def _layer_norm(t, g, b):
    mu = jnp.mean(t, axis=-1, keepdims=True)
    d = t - mu
    var = jnp.mean(d * d, axis=-1, keepdims=True)
    return d * lax.rsqrt(var + LN_EPS) * g + b


def _gelu(t):
    return 0.5 * t * (1.0 + lax.erf(t * (1.0 / math.sqrt(2.0))))


def _silu(t):
    return t * jax.nn.sigmoid(t)


def _ones_where(mask, dtype):
    return jnp.where(mask, 1.0, 0.0).astype(dtype)


def _dot(a, b):
    return jnp.dot(a, b, preferred_element_type=F32)


def _dot_nt(a, b):
    return lax.dot_general(a, b, (((1,), (1,)), ((), ())), preferred_element_type=F32)


def _dot_tn(a, b):
    return lax.dot_general(a, b, (((0,), (0,)), ((), ())), preferred_element_type=F32)


def _pack_bf16_pair(lo, hi):
    lo_b = lax.bitcast_convert_type(lo.astype(BF16).astype(F32), U32)
    hi_b = lax.bitcast_convert_type(hi.astype(BF16).astype(F32), U32)
    return hi_b | (lo_b >> 16)


def _unpack_pair_f32(p):
    lo = lax.bitcast_convert_type(p << 16, F32)
    hi = lax.bitcast_convert_type(p & jnp.uint32(0xFFFF0000), F32)
    return lo, hi


def _ada_body(c_ref, w_ref, b_ref, o_ref):
    a = _silu(c_ref[...]).astype(BF16)
    o_ref[...] = _dot(a, w_ref[...].astype(BF16)) + b_ref[...]


def _ada(cc, w, b):
    rows, d = cc.shape
    n = w.shape[1]
    tn = 1024
    return pl.pallas_call(
        _ada_body,
        grid=(n // tn,),
        in_specs=[pl.BlockSpec((rows, d), lambda j: (0, 0)),
                  pl.BlockSpec((d, tn), lambda j: (0, j)),
                  pl.BlockSpec((1, tn), lambda j: (0, j))],
        out_specs=pl.BlockSpec((rows, tn), lambda j: (0, j)),
        out_shape=jax.ShapeDtypeStruct((rows, n), F32),
        compiler_params=_cparams(1),
        name="ada",
    )(cc, w, b)


def _inproj_body(x_ref, g_ref, b_ref, sh_ref, sc_ref, w_ref, wlr_ref, o_ref, olr_ref, hm_ref):
    @pl.when(pl.program_id(1) == 0)
    def _():
        h = _layer_norm(x_ref[...], g_ref[...], b_ref[...])
        hm = (h * (1.0 + sc_ref[...]) + sh_ref[...]).astype(BF16)
        hm_ref[...] = hm
        olr_ref[...] = _dot(hm, wlr_ref[...])

    o_ref[...] = _dot(hm_ref[...], w_ref[...])


def _inproj(x2, mod3, mod_row, ln_g, ln_b, w_main, w_lr, col0, ncols, tm, tn):
    r, d = x2.shape
    return pl.pallas_call(
        _inproj_body,
        grid=(r // tm, ncols),
        in_specs=[pl.BlockSpec((tm, d), lambda i, j: (i, 0)),
                  pl.BlockSpec((1, d), lambda i, j: (0, 0)),
                  pl.BlockSpec((1, d), lambda i, j: (0, 0)),
                  pl.BlockSpec((None, 1, d), lambda i, j: (mod_row(i), 0, 0)),
                  pl.BlockSpec((None, 1, d), lambda i, j: (mod_row(i), 0, 1)),
                  pl.BlockSpec((d, tn), lambda i, j: (0, col0 + j)),
                  pl.BlockSpec((d, LANES), lambda i, j: (0, 0))],
        out_specs=[pl.BlockSpec((tm, tn), lambda i, j: (i, j)),
                   pl.BlockSpec((tm, LANES), lambda i, j: (i, 0))],
        out_shape=[jax.ShapeDtypeStruct((r, ncols * tn), F32),
                   jax.ShapeDtypeStruct((r, LANES), F32)],
        scratch_shapes=[pltpu.VMEM((tm, d), BF16)],
        compiler_params=_cparams(2),
        name="inproj",
    )(x2, ln_g, ln_b, mod3, mod3, w_main, w_lr)


def _inproj_cm_body(x_ref, g_ref, b_ref, sh_ref, sc_ref, wuv_ref, wr_ref, wlr_ref, ng_ref, nb_ref, ws_ref, bs_ref,
                    cm_ref, p_ref, lr_ref):
    tm = x_ref.shape[0]
    half = wuv_ref.shape[1] // 2
    hd = CM_CHUNK
    for r in range(0, tm, ROW_CHUNK):
        rows = slice(r, r + ROW_CHUNK)
        h = _layer_norm(x_ref[rows, :], g_ref[...], b_ref[...])
        hm = (h * (1.0 + sc_ref[...]) + sh_ref[...]).astype(BF16)
        uv = _dot(hm, wuv_ref[...])
        u = _gelu(uv[:, :half])
        vb = _layer_norm(_gelu(uv[:, half:]), ng_ref[...], nb_ref[...]).astype(BF16)
        for c in range(ROW_CHUNK // CM_CHUNK):
            crow = slice(c * CM_CHUNK, (c + 1) * CM_CHUNK)
            orow = slice(r + c * CM_CHUNK, r + (c + 1) * CM_CHUNK)
            for hh in range(CM_HEADS):
                cols = slice(hh * hd, (hh + 1) * hd)
                s = _dot(ws_ref[hh], vb[crow, cols]) + bs_ref[:, cols]
                cm_ref[orow, cols] = (u[crow, cols] * s).astype(BF16)
        p_ref[rows, :] = _dot(hm, wr_ref[...])
        lr_ref[rows, :] = _dot(hm, wlr_ref[...])


def _inproj_cm(x2, mod3, rows_per_batch, ln_g, ln_b, w_uv, w_rest, w_lr, ng, nb, ws, bs, tm):
    r, d = x2.shape
    n_uv, n_rest = w_uv.shape[1], w_rest.shape[1]
    row = lambda i: (i * tm) // rows_per_batch

    def const(shape):
        return pl.BlockSpec(shape, lambda i: (0,) * len(shape), pipeline_mode=pl.Buffered(1))

    return pl.pallas_call(
        _inproj_cm_body,
        grid=(r // tm,),
        in_specs=[pl.BlockSpec((tm, d), lambda i: (i, 0)),
                  const((1, d)), const((1, d)),
                  pl.BlockSpec((None, 1, d), lambda i: (row(i), 0, 0)),
                  pl.BlockSpec((None, 1, d), lambda i: (row(i), 0, 1)),
                  const((d, n_uv)), const((d, n_rest)), const((d, LANES)),
                  const((1, n_uv // 2)), const((1, n_uv // 2)),
                  const((CM_HEADS, CM_CHUNK, CM_CHUNK)), const((CM_CHUNK, n_uv // 2))],
        out_specs=[pl.BlockSpec((tm, n_uv // 2), lambda i: (i, 0)),
                   pl.BlockSpec((tm, n_rest), lambda i: (i, 0)),
                   pl.BlockSpec((tm, LANES), lambda i: (i, 0))],
        out_shape=[jax.ShapeDtypeStruct((r, n_uv // 2), BF16),
                   jax.ShapeDtypeStruct((r, n_rest), F32),
                   jax.ShapeDtypeStruct((r, LANES), F32)],
        compiler_params=_cparams(1),
        name="inproj_cm",
    )(x2, ln_g, ln_b, mod3, mod3, w_uv, w_rest, w_lr, ng, nb, ws, bs)


GLA_TILE = 256
GLA_TILES_PER_ITER = 2


def _gla_tile(q, k, v, lr, wg, bg, st_ref, forward, need_o):
    t = k.shape[0]
    n_chunks = t // GLA_CHUNK
    z = _dot(lr.astype(BF16), wg) + bg
    g = jax.nn.log_sigmoid(z) * (1.0 / GLA_GATE_NORMALIZER)
    r_id = lax.broadcasted_iota(I32, (t, t), 0)
    c_id = lax.broadcasted_iota(I32, (t, t), 1)
    shift = GLA_CHUNK.bit_length() - 1
    same = (r_id >> shift) == (c_id >> shift)
    lower = same & (c_id <= r_id)
    tri = _ones_where(lower, BF16)
    g_hi = g.astype(BF16)
    g_lo = (g - g_hi.astype(F32)).astype(BF16)
    csum = _dot(tri, g_hi) + _dot(tri, g_lo)
    g3 = g.reshape(n_chunks, GLA_CHUNK, GLA_DK)
    tot = jnp.broadcast_to(jnp.sum(g3, axis=1, keepdims=True), g3.shape).reshape(t, GLA_DK)
    bcum = csum if forward else tot - csum + g
    kd = (k * jnp.exp(tot - bcum)).astype(BF16)
    decay = jnp.exp(tot)
    vb = v.astype(BF16)
    o = None
    if need_o:
        qe = ((q * (GLA_DK ** -0.5)) * jnp.exp(bcum)).astype(BF16)
        ke = (k * jnp.exp(-bcum)).astype(BF16)
        att = _dot_nt(qe, ke)
        mask = lower if forward else same & (c_id >= r_id)
        att = jnp.where(mask, att, 0.0).astype(BF16)
        o = _dot(att, vb)
    outs = [None] * n_chunks
    order = range(n_chunks) if forward else range(n_chunks - 1, -1, -1)
    for c in order:
        rows = slice(c * GLA_CHUNK, (c + 1) * GLA_CHUNK)
        s_t = st_ref[...]
        if need_o:
            outs[c] = o[rows] + _dot_nt(qe[rows], s_t.astype(BF16))
        u_t = _dot_tn(vb[rows], kd[rows])
        st_ref[...] = s_t * decay[c * GLA_CHUNK:c * GLA_CHUNK + 1, :] + u_t
    if need_o:
        return jnp.concatenate(outs, axis=0)
    return None


def _gla_body(q_ref, k_ref, v_ref, go_ref, lr_ref, kc_ref, vc_ref, lrc_ref,
              wgf_ref, bgf_ref, wgb_ref, bgb_ref, ng_ref, out_ref, o_scr, sf_ref, sb_ref):
    t = GLA_TILE
    n_x = q_ref.shape[0] // t
    n_c = kc_ref.shape[0] // t
    half = n_x // 2
    sf_ref[...] = jnp.zeros_like(sf_ref)
    sb_ref[...] = jnp.zeros_like(sb_ref)
    wgf, bgf, wgb, bgb = wgf_ref[...], bgf_ref[...], wgb_ref[...], bgb_ref[...]

    for i in range(n_c):
        rf = slice(i * t, (i + 1) * t)
        rb = slice((n_c - 1 - i) * t, (n_c - i) * t)
        _gla_tile(None, kc_ref[rf], vc_ref[rf], lrc_ref[rf], wgf, bgf, sf_ref, True, False)
        _gla_tile(None, kc_ref[rb], vc_ref[rb], lrc_ref[rb], wgb, bgb, sb_ref, False, False)

    def tile_out(i, forward):
        rows = pl.ds(pl.multiple_of(i * t, t), t)
        if forward:
            return rows, _gla_tile(q_ref[rows], k_ref[rows], v_ref[rows], lr_ref[rows],
                                   wgf, bgf, sf_ref, True, True)
        return rows, _gla_tile(q_ref[rows], k_ref[rows], v_ref[rows], lr_ref[rows],
                               wgb, bgb, sb_ref, False, True)

    def finish(rows, o):
        o = o + o_scr[rows]
        ms = jnp.mean(o * o, axis=-1, keepdims=True)
        on = o * lax.rsqrt(ms + RMS_EPS) * ng_ref[...]
        out_ref[rows] = (on * _silu(go_ref[rows])).astype(BF16)

    def keep(rows, o):
        o_scr[rows] = o

    u = math.gcd(GLA_TILES_PER_ITER, half)

    def make_step(sink):
        def step(it, carry):
            for w in range(u):
                i = it * u + w
                sink(*tile_out(i, True))
                sink(*tile_out(n_x - 1 - i, False))
            return carry
        return step

    lax.fori_loop(0, half // u, make_step(keep), 0)
    lax.fori_loop(half // u, n_x // u, make_step(finish), 0)


def _gla(p3, lr3, pc3, lrc3, wgf, bgf, wgb, bgb, ng):
    bsz, l, _ = p3.shape
    lc = pc3.shape[1]
    dk, dv = GLA_DK, GLA_DV
    kw = GLA_HEADS * dk
    q0, k0 = 0, kw // dk
    v0, go0 = 2 * kw // dv, (2 * kw + GLA_HEADS * dv) // dv
    kc0, vc0 = k0, v0
    return pl.pallas_call(
        _gla_body,
        grid=(bsz, GLA_HEADS),
        in_specs=[pl.BlockSpec((None, l, dk), lambda b, h: (b, 0, q0 + h)),
                  pl.BlockSpec((None, l, dk), lambda b, h: (b, 0, k0 + h)),
                  pl.BlockSpec((None, l, dv), lambda b, h: (b, 0, v0 + h)),
                  pl.BlockSpec((None, l, dv), lambda b, h: (b, 0, go0 + h)),
                  pl.BlockSpec((None, l, LANES), lambda b, h: (b, 0, 0)),
                  pl.BlockSpec((None, lc, dk), lambda b, h: (b, 0, kc0 + h)),
                  pl.BlockSpec((None, lc, dv), lambda b, h: (b, 0, vc0 + h)),
                  pl.BlockSpec((None, lc, LANES), lambda b, h: (b, 0, 0)),
                  pl.BlockSpec((LANES, dk), lambda b, h: (0, h)),
                  pl.BlockSpec((1, dk), lambda b, h: (0, h)),
                  pl.BlockSpec((LANES, dk), lambda b, h: (0, h)),
                  pl.BlockSpec((1, dk), lambda b, h: (0, h)),
                  pl.BlockSpec((1, dv), lambda b, h: (0, 0))],
        out_specs=pl.BlockSpec((None, l, dv), lambda b, h: (b, 0, h)),
        out_shape=jax.ShapeDtypeStruct((bsz, l, GLA_HEADS * dv), BF16),
        scratch_shapes=[pltpu.VMEM((l, dv), F32),
                        pltpu.VMEM((dv, dk), F32),
                        pltpu.VMEM((dv, dk), F32)],
        compiler_params=_cparams(2),
        name="gla",
    )(p3, p3, p3, p3, lr3, pc3, pc3, lrc3, wgf, bgf, wgb, bgb, ng)


def _post_body(cm_ref, gla_ref, x_ref, lng_ref, lnb_ref, g1_ref, sh2_ref, sc2_ref,
               wo_ref, l1g_ref, l1b_ref, wr_ref, br_ref, h1_ref, hp_ref, lg_ref):
    half = cm_ref.shape[1]
    d2 = x_ref.shape[1] // 2
    for r in range(0, x_ref.shape[0], ROW_CHUNK):
        rows = slice(r, r + ROW_CHUNK)
        y = _dot(cm_ref[rows, :], wo_ref[:half, :]) + _dot(gla_ref[rows, :], wo_ref[half:, :])
        hx = _layer_norm(x_ref[rows, :], lng_ref[...], lnb_ref[...])
        h1 = _layer_norm(DEEPNORM_ALPHA * hx + g1_ref[...] * y, l1g_ref[...], l1b_ref[...])
        h1_ref[rows, :] = h1
        hm = h1 * (1.0 + sc2_ref[...]) + sh2_ref[...]
        hp_ref[rows, :] = _pack_bf16_pair(hm[:, :d2], hm[:, d2:])
        lg_ref[rows, :] = _dot(hm.astype(BF16), wr_ref[...]) + br_ref[...]


def _post_attn(cm, gla, x2, mod3, rows_per_batch, ln_g, ln_b, w_out, l1g, l1b, w_r, b_r, tm):
    r, d = x2.shape
    half = d // 2
    row = lambda i: (i * tm) // rows_per_batch
    full = lambda shape: pl.BlockSpec(shape, lambda i: (0,) * len(shape))
    return pl.pallas_call(
        _post_body,
        grid=(r // tm,),
        in_specs=[pl.BlockSpec((tm, half), lambda i: (i, 0)),
                  pl.BlockSpec((tm, half), lambda i: (i, 0)),
                  pl.BlockSpec((tm, d), lambda i: (i, 0)),
                  full((1, d)), full((1, d)),
                  pl.BlockSpec((None, 1, d), lambda i: (row(i), 0, 2)),
                  pl.BlockSpec((None, 1, d), lambda i: (row(i), 0, 3)),
                  pl.BlockSpec((None, 1, d), lambda i: (row(i), 0, 4)),
                  full((d, d)), full((1, d)), full((1, d)),
                  full((d, LANES)), full((1, LANES))],
        out_specs=[pl.BlockSpec((tm, d), lambda i: (i, 0)),
                   pl.BlockSpec((tm, half), lambda i: (i, 0)),
                   pl.BlockSpec((tm, LANES), lambda i: (i, 0))],
        out_shape=[jax.ShapeDtypeStruct((r, d), F32),
                   jax.ShapeDtypeStruct((r, half), U32),
                   jax.ShapeDtypeStruct((r, LANES), F32)],
        compiler_params=_cparams(1),
        name="post_attn",
    )(cm, gla, x2, ln_g, ln_b, mod3, mod3, mod3, w_out, l1g, l1b, w_r, b_r)


def _route_body(lg_ref, idx_ref, wt_ref, rank_ref, cnt_ref, carry_ref):
    i = pl.program_id(0)
    t = lg_ref.shape[0]

    @pl.when(i == 0)
    def _():
        carry_ref[...] = jnp.zeros_like(carry_ref)

    lane = lax.broadcasted_iota(I32, (t, LANES), 1)
    lane_f = lane.astype(F32)
    neg = jnp.float32(-jnp.inf)
    l = jnp.where(lane < N_EXPERTS, lg_ref[...], neg)
    tops, onehots, idxs = [], [], []
    for _ in range(TOP_K):
        m = jnp.max(l, axis=-1, keepdims=True)
        idx = jnp.min(jnp.where(l == m, lane_f, float(LANES)), axis=-1, keepdims=True).astype(I32)
        oh = lane == idx
        l = jnp.where(oh, neg, l)
        tops.append(m)
        idxs.append(idx)
        onehots.append(oh)
    exps = [jnp.exp(m - tops[0]) for m in tops]
    denom = exps[0] + exps[1] + exps[2] + exps[3]
    sel = _ones_where(onehots[0] | onehots[1] | onehots[2] | onehots[3], F32)
    r_id = lax.broadcasted_iota(I32, (t, t), 0)
    c_id = lax.broadcasted_iota(I32, (t, t), 1)
    strict = _ones_where(c_id < r_id, BF16)
    before = _dot(strict, sel.astype(BF16)) + carry_ref[0:1, :]
    idx_out = jnp.zeros((t, LANES), I32)
    wt_out = jnp.zeros((t, LANES), F32)
    rank_out = jnp.zeros((t, LANES), I32)
    for k in range(TOP_K):
        rk = jnp.sum(jnp.where(onehots[k], before, 0.0), axis=-1, keepdims=True).astype(I32)
        idx_out = jnp.where(lane == k, idxs[k], idx_out)
        wt_out = jnp.where(lane == k, exps[k] / denom, wt_out)
        rank_out = jnp.where(lane == k, rk, rank_out)
    idx_ref[...] = idx_out[:, :TOP_K]
    wt_ref[...] = wt_out
    rank_ref[...] = rank_out[:, :TOP_K]
    total = carry_ref[0:1, :] + jnp.sum(sel, axis=0, keepdims=True)
    carry_ref[...] = jnp.broadcast_to(total, carry_ref.shape)
    cnt_ref[...] = jnp.broadcast_to(total, cnt_ref.shape)


def _route(logits, tm):
    n = logits.shape[0]
    blk = pl.BlockSpec((tm, LANES), lambda i: (i, 0))
    small = pl.BlockSpec((tm, TOP_K), lambda i: (i, 0))
    return pl.pallas_call(
        _route_body,
        grid=(n // tm,),
        in_specs=[blk],
        out_specs=[small, blk, small, pl.BlockSpec((8, LANES), lambda i: (0, 0))],
        out_shape=[jax.ShapeDtypeStruct((n, TOP_K), I32),
                   jax.ShapeDtypeStruct((n, LANES), F32),
                   jax.ShapeDtypeStruct((n, TOP_K), I32),
                   jax.ShapeDtypeStruct((8, LANES), F32)],
        scratch_shapes=[pltpu.VMEM((8, LANES), F32)],
        compiler_params=_cparams(1),
        name="route",
    )(logits)


def _dispatch_body(nz_ref, zl_ref, dest_ref, hp_ref, xs_ref, zero_ref, sem, zsem):
    t = hp_ref.shape[0] * hp_ref.shape[1]

    @pl.when(pl.program_id(0) == 0)
    def _():
        zero_ref[...] = jnp.zeros_like(zero_ref)

        def z_copy(b):
            r = pl.multiple_of(zl_ref[b] * MOE_BLOCK, MOE_BLOCK)
            return pltpu.make_async_copy(zero_ref, xs_ref.at[pl.ds(r, MOE_BLOCK)], zsem)

        def start(b, carry):
            z_copy(b).start()
            return carry

        def wait(b, carry):
            z_copy(b).wait()
            return carry

        lax.fori_loop(0, nz_ref[0], start, 0)
        lax.fori_loop(0, nz_ref[0], wait, 0)

    def issue(g, carry):
        for u in range(SUBLANES):
            for k in range(TOP_K):
                d = dest_ref[0, (g * SUBLANES + u) * TOP_K + k]
                pltpu.make_async_copy(hp_ref.at[g, pl.ds(u, 1)], xs_ref.at[pl.ds(d, 1)], sem).start(priority=k % 2)
        return carry

    lax.fori_loop(0, t // SUBLANES, issue, 0)
    for _ in range(TOP_K):
        pltpu.make_async_copy(xs_ref.at[pl.ds(0, t)], xs_ref.at[pl.ds(0, t)], sem).wait()


def _dispatch(n_zero, zero_blocks, dest, hp, n_slots, tm):
    n, w = hp.shape
    dest3 = dest.reshape(n // tm, 1, tm * TOP_K)
    return pl.pallas_call(
        _dispatch_body,
        grid_spec=pltpu.PrefetchScalarGridSpec(
            num_scalar_prefetch=2,
            grid=(n // tm,),
            in_specs=[pl.BlockSpec((None, 1, tm * TOP_K), lambda i, nz, zl: (i, 0, 0), memory_space=pltpu.SMEM),
                      pl.BlockSpec((tm // SUBLANES, SUBLANES, w), lambda i, nz, zl: (i, 0, 0))],
            out_specs=pl.BlockSpec(memory_space=pl.ANY),
            scratch_shapes=[pltpu.VMEM((MOE_BLOCK, w), U32),
                            pltpu.SemaphoreType.DMA(()),
                            pltpu.SemaphoreType.DMA(())]),
        out_shape=jax.ShapeDtypeStruct((n_slots, w), U32),
        compiler_params=_cparams(1),
        name="dispatch",
    )(n_zero, zero_blocks, dest3, hp.reshape(n // SUBLANES, SUBLANES, w))


PASS_SUBS = 9
EXPERT_TN = 256
DOT_ROWS = 1024
W_SLOTS = 3
W_AHEAD = 2


def _expert_body(meta_ref, pe_ref, pr_ref, pn_ref, xs_hbm, wgu_hbm, wdn_hbm, bgu_ref, bdn_ref,
                 y_hbm, xraw, xb, h_ref, wg_buf, wu_buf, wd_buf, ybuf, zbuf,
                 sem_x, sem_y, sem_z, sem_w, sem_d):
    sub = MOE_BLOCK
    tn = EXPERT_TN
    tp = tn // 2
    nj = h_ref.shape[0]
    nc = y_hbm.shape[1] // tp
    de = nj * tn
    d2 = xraw.shape[1]
    p = pl.program_id(0)
    n_pass = meta_ref[0]
    nsub = pn_ref[p]
    row0 = pr_ref[p]

    def x_copy(i, pp):
        r = pl.multiple_of((pr_ref[pp] + i) * sub, sub)
        return pltpu.make_async_copy(xs_hbm.at[pl.ds(r, sub)], xraw.at[pl.ds(i * sub, sub)], sem_x)

    def y_copy(i, r0, c, slot):
        r = pl.multiple_of((r0 + i) * sub, sub)
        col = pl.multiple_of(c * tp, tp)
        return pltpu.make_async_copy(ybuf.at[slot, pl.ds(i * sub, sub), :],
                                     y_hbm.at[pl.ds(r, sub), pl.ds(col, tp)], sem_y.at[slot])

    def z_copy(b, c):
        r = pl.multiple_of(b * sub, sub)
        return pltpu.make_async_copy(zbuf, y_hbm.at[pl.ds(r, sub), pl.ds(c * tp, tp)], sem_z)

    def w1_copies(pp, j):
        e = pe_ref[pp]
        slot = j % W_SLOTS
        col = pl.multiple_of(j * tn, tn)
        return (pltpu.make_async_copy(wgu_hbm.at[e, :, pl.ds(col, tn)], wg_buf.at[slot], sem_w.at[slot]),
                pltpu.make_async_copy(wgu_hbm.at[e, :, pl.ds(de + col, tn)], wu_buf.at[slot], sem_w.at[slot]))

    def w2_copy(pp, c):
        slot = c % W_SLOTS
        col = pl.multiple_of(c * tn, tn)
        return pltpu.make_async_copy(wdn_hbm.at[pe_ref[pp], :, pl.ds(col, tn)], wd_buf.at[slot], sem_d.at[slot])

    def start_w1(pp, j):
        for cp in w1_copies(pp, j):
            cp.start()

    def for_subs(count, fn):
        for i in range(PASS_SUBS):
            pl.when(i < count)(lambda i=i: fn(i))

    def for_tail(fn):
        def body(b, carry):
            for c in range(nc):
                fn(b, c)
            return carry
        lax.fori_loop(meta_ref[1], y_hbm.shape[0] // sub, body, 0)

    def for_groups(fn):
        pl.when(nsub == PASS_SUBS)(lambda: fn(0, PASS_SUBS * sub))
        k = 1 << (PASS_SUBS.bit_length() - 1)
        while k:
            start = pl.multiple_of((nsub & (-2 * k)) * sub, sub)
            pl.when((nsub != PASS_SUBS) & ((nsub & k) != 0))(lambda start=start, k=k: fn(start, k * sub))
            k //= 2

    @pl.when(p == 0)
    def _():
        for_subs(nsub, lambda i: x_copy(i, p).start())
        for j in range(W_AHEAD):
            start_w1(p, j)
        zbuf[...] = jnp.zeros_like(zbuf)
        for_tail(lambda b, c: z_copy(b, c).start())

    for_subs(nsub, lambda i: x_copy(i, p).wait())

    def unpack(i):
        rows = slice(i * sub, (i + 1) * sub)
        lo, hi = _unpack_pair_f32(xraw[rows, :])
        xb[rows, :d2] = lo.astype(BF16)
        xb[rows, d2:] = hi.astype(BF16)

    for_subs(nsub, unpack)

    @pl.when(p + 1 < n_pass)
    def _():
        for_subs(pn_ref[p + 1], lambda i: x_copy(i, p + 1).start())

    def first_step(j, carry):
        slot = j % W_SLOTS
        for cp in w1_copies(p, j):
            cp.wait()
        nxt = j + W_AHEAD
        pl.when(nxt < nj)(lambda: start_w1(p, nxt))
        pl.when(nxt >= nj)(lambda: w2_copy(p, nxt - nj).start())

        def group(start, size):
            wg = wg_buf[slot].astype(BF16)
            wu = wu_buf[slot].astype(BF16)
            dr = DOT_ROWS if size % DOT_ROWS == 0 else size
            for r in range(0, size, dr):
                rows = pl.ds(start + r, dr)
                x = xb[rows, :]
                gate = jnp.minimum(_dot(x, wg) + bgu_ref[j], SWIGLU_LIMIT)
                up = jnp.clip(_dot(x, wu) + bgu_ref[nj + j], -SWIGLU_LIMIT, SWIGLU_LIMIT)
                h_ref[j, rows, :] = ((up + 1.0) * gate * jax.nn.sigmoid(SWIGLU_ALPHA * gate)).astype(BF16)

        for_groups(group)
        return carry

    lax.fori_loop(0, nj, first_step, 0)

    def second_step(c, carry):
        wslot = c % W_SLOTS
        slot = c % 2
        w2_copy(p, c).wait()
        nxt = c + W_AHEAD
        pl.when(nxt < nc)(lambda: w2_copy(p, nxt).start())
        pl.when((nxt >= nc) & (p + 1 < n_pass))(lambda: start_w1(p + 1, nxt - nc))

        @pl.when(c >= 2)
        def _():
            for_subs(nsub, lambda i: y_copy(i, row0, c - 2, slot).wait())

        @pl.when((c < 2) & (p > 0))
        def _():
            for_subs(pn_ref[p - 1], lambda i: y_copy(i, pr_ref[p - 1], nc - 2 + c, slot).wait())

        def group(start, size):
            wd = wd_buf[wslot].astype(BF16)
            dr = DOT_ROWS if size % DOT_ROWS == 0 else size
            for r in range(0, size, dr):
                rows = pl.ds(start + r, dr)
                hx = jnp.concatenate([h_ref[j, rows, :] for j in range(nj)], axis=1)
                y = _dot(hx, wd) + bdn_ref[c]
                ybuf[slot, rows, :] = _pack_bf16_pair(y[:, :tp], y[:, tp:])

        for_groups(group)
        for_subs(nsub, lambda i: y_copy(i, row0, c, slot).start())
        return carry

    lax.fori_loop(0, nc, second_step, 0)

    @pl.when(p == n_pass - 1)
    def _():
        for c in (nc - 2, nc - 1):
            for_subs(nsub, lambda i, c=c: y_copy(i, row0, c, c % 2).wait())
        for_tail(lambda b, cc: z_copy(b, cc).wait())


def _experts(meta, pass_e, pass_row0, pass_nsub, xs, w_gu, b_gu, w_dn, b_dn):
    n_slots, d2 = xs.shape
    d = 2 * d2
    n_e, de = w_dn.shape[0], w_dn.shape[1]
    tn = EXPERT_TN
    nj = de // tn
    nc = d // tn
    assert W_AHEAD < W_SLOTS and W_AHEAD <= min(nj, nc) and nc % 2 == 0
    rmax = PASS_SUBS * MOE_BLOCK
    return pl.pallas_call(
        _expert_body,
        grid_spec=pltpu.PrefetchScalarGridSpec(
            num_scalar_prefetch=4,
            grid=(meta[0],),
            in_specs=[
                pl.BlockSpec(memory_space=pl.ANY),
                pl.BlockSpec(memory_space=pl.ANY),
                pl.BlockSpec(memory_space=pl.ANY),
                pl.BlockSpec((None, 2 * nj, 1, tn), lambda p, m, e, r, n: (e[p], 0, 0, 0)),
                pl.BlockSpec((None, nc, 1, tn), lambda p, m, e, r, n: (e[p], 0, 0, 0)),
            ],
            out_specs=pl.BlockSpec(memory_space=pl.ANY),
            scratch_shapes=[pltpu.VMEM((rmax, d2), U32),
                            pltpu.VMEM((rmax, d), BF16),
                            pltpu.VMEM((nj, rmax, tn), BF16),
                            pltpu.VMEM((W_SLOTS, d, tn), w_gu.dtype),
                            pltpu.VMEM((W_SLOTS, d, tn), w_gu.dtype),
                            pltpu.VMEM((W_SLOTS, de, tn), w_dn.dtype),
                            pltpu.VMEM((2, rmax, tn // 2), U32),
                            pltpu.VMEM((MOE_BLOCK, tn // 2), U32),
                            pltpu.SemaphoreType.DMA(()),
                            pltpu.SemaphoreType.DMA((2,)),
                            pltpu.SemaphoreType.DMA(()),
                            pltpu.SemaphoreType.DMA((W_SLOTS,)),
                            pltpu.SemaphoreType.DMA((W_SLOTS,))]),
        out_shape=jax.ShapeDtypeStruct((n_slots, d2), U32),
        compiler_params=_cparams(1),
        name="experts",
    )(meta, pass_e, pass_row0, pass_nsub, xs, w_gu, w_dn,
      b_gu.reshape(n_e, 2 * nj, 1, tn), b_dn.reshape(n_e, nc, 1, tn))


COMBINE_CHUNK = 64


def _combine_body(dcur_ref, dnxt_ref, wt_ref, h1_ref, g2_ref, l2g_ref, l2b_ref, y_ref, out_ref, buf_ref, sem):
    i = pl.program_id(0)
    n_tiles = pl.num_programs(0)
    t = h1_ref.shape[0]
    tp = EXPERT_TN // 2
    slot = i % 2

    def gather(dest_ref, sl):
        def body(g, carry):
            for u in range(SUBLANES):
                for k in range(TOP_K):
                    d = dest_ref[0, (g * SUBLANES + u) * TOP_K + k]
                    pltpu.make_async_copy(y_ref.at[pl.ds(d, 1)], buf_ref.at[sl, k, g, pl.ds(u, 1)],
                                          sem.at[sl]).start(priority=k % 2)
            return carry
        lax.fori_loop(0, t // SUBLANES, body, 0)

    pl.when(i == 0)(lambda: gather(dcur_ref, 0))
    pl.when(i + 1 < n_tiles)(lambda: gather(dnxt_ref, 1 - slot))
    for k in range(TOP_K):
        pltpu.make_async_copy(buf_ref.at[1 - slot, k], buf_ref.at[slot, k], sem.at[slot]).wait()
    for r in range(0, t, COMBINE_CHUNK):
        rows = slice(r, r + COMBINE_CHUNK)
        groups = slice(r // SUBLANES, (r + COMBINE_CHUNK) // SUBLANES)
        wt = wt_ref[rows, :]
        f_lo, f_hi = None, None
        for k in range(TOP_K):
            lo, hi = _unpack_pair_f32(buf_ref[slot, k, groups].reshape(COMBINE_CHUNK, buf_ref.shape[-1]))
            w = wt[:, k:k + 1]
            f_lo = lo * w if f_lo is None else f_lo + lo * w
            f_hi = hi * w if f_hi is None else f_hi + hi * w
        parts = []
        for c in range(f_lo.shape[1] // tp):
            parts += [f_lo[:, c * tp:(c + 1) * tp], f_hi[:, c * tp:(c + 1) * tp]]
        f = jnp.concatenate(parts, axis=1)
        out_ref[rows, :] = _layer_norm(DEEPNORM_ALPHA * h1_ref[rows, :] + g2_ref[...] * f,
                                       l2g_ref[...], l2b_ref[...])


def _combine(dest, wt, h1, mod3, rows_per_batch, l2g, l2b, y, tm):
    n, d = h1.shape
    n_tiles = n // tm
    dest3 = dest.reshape(n_tiles, 1, tm * TOP_K)
    row = lambda i: (i * tm) // rows_per_batch
    return pl.pallas_call(
        _combine_body,
        grid=(n_tiles,),
        in_specs=[pl.BlockSpec((None, 1, tm * TOP_K), lambda i: (i, 0, 0), memory_space=pltpu.SMEM),
                  pl.BlockSpec((None, 1, tm * TOP_K), lambda i: (jnp.minimum(i + 1, n_tiles - 1), 0, 0),
                               memory_space=pltpu.SMEM),
                  pl.BlockSpec((tm, LANES), lambda i: (i, 0)),
                  pl.BlockSpec((tm, d), lambda i: (i, 0)),
                  pl.BlockSpec((None, 1, d), lambda i: (row(i), 0, 5)),
                  pl.BlockSpec((1, d), lambda i: (0, 0)),
                  pl.BlockSpec((1, d), lambda i: (0, 0)),
                  pl.BlockSpec(memory_space=pl.ANY)],
        out_specs=pl.BlockSpec((tm, d), lambda i: (i, 0)),
        out_shape=jax.ShapeDtypeStruct((n, d), F32),
        scratch_shapes=[pltpu.VMEM((2, TOP_K, tm // SUBLANES, SUBLANES, y.shape[1]), U32),
                        pltpu.SemaphoreType.DMA((2,))],
        compiler_params=_cparams(1),
        name="combine",
    )(dest3, dest3, wt, h1, mod3, l2g, l2b, y)


def _pick_tile(n, pref):
    t = pref
    while n % t:
        t //= 2
    return t


def kernel(x, c, ctx, c_ctx, ln_in_g, ln_in_b, w_ada, b_ada, w_in, cm_norm_g, cm_norm_b, cm_w_s, cm_b_s,
           gla_w_gk_f, gla_b_gk_f, gla_w_gk_b, gla_b_gk_b, gla_norm_g, w_out, ln1_g, ln1_b,
           w_router, b_router, w_gate_up, b_gate_up, w_down, b_down, ln2_g, ln2_b):
    bsz, l, d = x.shape
    lc = ctx.shape[1]
    n, nc = bsz * l, bsz * lc
    assert w_ada.shape[0] == 1, "single-layer configuration"
    assert bsz + 1 <= 8 and l % (2 * GLA_TILE) == 0 and lc % GLA_TILE == 0
    row = lambda v: v.reshape(1, -1)

    cc = jnp.concatenate([c, c_ctx[None, :], jnp.zeros((8 - bsz - 1, d), F32)], axis=0)
    mod3 = _ada(cc, w_ada[0], row(b_ada[0])).reshape(8, 1, N_MOD * d)

    n_uv = 2 * CM_HEADS * CM_CHUNK
    n_main = n_uv + 2 * GLA_HEADS * GLA_DK + 2 * GLA_HEADS * GLA_DV
    w_uv = w_in[0][:, :n_uv].astype(BF16)
    w_rest = w_in[0][:, n_uv:n_main].astype(BF16)
    w_lr = jnp.pad(w_in[0][:, n_main:].astype(BF16), ((0, 0), (0, LANES - 2 * GLA_RANK)))
    x2 = x.reshape(n, d)
    tm_x = _pick_tile(l, 512)
    bs_tile = jnp.repeat(cm_b_s[0].T, CM_CHUNK, axis=1)
    cm, p, lr = _inproj_cm(x2, mod3, l, row(ln_in_g), row(ln_in_b), w_uv, w_rest, w_lr,
                           row(cm_norm_g[0]), row(cm_norm_b[0]), cm_w_s[0].astype(BF16), bs_tile, tm_x)
    tm_c = _pick_tile(nc, 512)
    pc, lrc = _inproj(ctx.reshape(nc, d), mod3, lambda i: bsz, row(ln_in_g), row(ln_in_b),
                      w_rest, w_lr, 0, 2, tm_c, 1024)

    kw = GLA_HEADS * GLA_DK
    wgf = jnp.zeros((LANES, kw), BF16).at[:GLA_RANK].set(gla_w_gk_f[0].astype(BF16))
    wgb = jnp.zeros((LANES, kw), BF16).at[GLA_RANK:2 * GLA_RANK].set(gla_w_gk_b[0].astype(BF16))
    gla = _gla(p.reshape(bsz, l, -1), lr.reshape(bsz, l, LANES), pc.reshape(bsz, lc, -1),
               lrc.reshape(bsz, lc, LANES), wgf, row(gla_b_gk_f[0]), wgb, row(gla_b_gk_b[0]),
               row(gla_norm_g[0])).reshape(n, -1)

    w_r = jnp.pad(w_router[0], ((0, 0), (0, LANES - N_EXPERTS))).astype(BF16)
    b_r = jnp.pad(b_router[0], (0, LANES - N_EXPERTS)).reshape(1, LANES)
    h1, hp, logits = _post_attn(cm, gla, x2, mod3, l, row(ln_in_g), row(ln_in_b), w_out[0].astype(BF16),
                                row(ln1_g[0]), row(ln1_b[0]), w_r, b_r, tm_x)

    idx, wt, rank, cnt = _route(logits, _pick_tile(n, 1024))
    counts = cnt[0, :N_EXPERTS].astype(I32)
    n_blocks = (n * TOP_K + N_EXPERTS * (MOE_BLOCK - 1)) // MOE_BLOCK
    blocks_e = (counts + MOE_BLOCK - 1) // MOE_BLOCK
    blk_end = jnp.cumsum(blocks_e)
    blk_start = blk_end - blocks_e
    dest = ((blk_start * MOE_BLOCK)[idx] + rank).reshape(-1)
    n_pass_max = n_blocks // PASS_SUBS + N_EXPERTS
    pass_cnt = (blocks_e + PASS_SUBS - 1) // PASS_SUBS
    pass_end = jnp.cumsum(pass_cnt)
    pass_start = pass_end - pass_cnt
    pid = jnp.arange(n_pass_max, dtype=I32)
    pass_e = jnp.minimum(jnp.searchsorted(pass_end, pid, side="right"), N_EXPERTS - 1).astype(I32)
    local = pid - pass_start[pass_e]
    pass_row0 = (blk_start[pass_e] + local * PASS_SUBS).astype(I32)
    pass_nsub = jnp.clip(blocks_e[pass_e] - local * PASS_SUBS, 0, PASS_SUBS).astype(I32)
    meta = jnp.stack([pass_end[-1], blk_end[-1]]).astype(I32)

    ar = jnp.arange(N_EXPERTS, dtype=I32)
    cand = jnp.concatenate([jnp.where(counts % MOE_BLOCK != 0, blk_end - 1, -1),
                            jnp.where(blk_end[-1] + ar < n_blocks, blk_end[-1] + ar, -1)])
    zero_blocks = cand[jnp.argsort(cand < 0, stable=True)].astype(I32)
    n_zero = jnp.sum(cand >= 0).astype(I32).reshape(1)

    xs = _dispatch(n_zero, zero_blocks, dest, hp, n_blocks * MOE_BLOCK, _pick_tile(n, 512))
    ys = _experts(meta, pass_e, pass_row0, pass_nsub, xs, w_gate_up[0], b_gate_up[0], w_down[0], b_down[0])
    out = _combine(dest, wt, h1, mod3, l, row(ln2_g[0]), row(ln2_b[0]), ys, _pick_tile(n, 512))
    return out.reshape(bsz, l, d)
```

```python
import math

import jax
import jax.numpy as jnp
from jax import lax
from jax.experimental import pallas as pl
from jax.experimental.pallas import tpu as pltpu

F32 = jnp.float32
BF16 = jnp.bfloat16
U32 = jnp.uint32
I32 = jnp.int32

CM_CHUNK = 128
CM_HEADS = 8
GLA_HEADS = 4
GLA_DK = 128
GLA_DV = 256
GLA_CHUNK = 64
GLA_RANK = 16
GLA_GATE_NORMALIZER = 16.0
N_EXPERTS = 32
TOP_K = 4
MOE_BLOCK = 256
SWIGLU_LIMIT = 7.0
SWIGLU_ALPHA = 1.702
N_MOD = 6
DEEPNORM_ALPHA = 2.0 ** 0.25
LN_EPS = 1e-5
RMS_EPS = 1e-6

LANES = 128
SUBLANES = 8
VMEM_LIMIT = 56 * 1024 * 1024
ROW_CHUNK = 256


def _cparams(n_axes, vmem=VMEM_LIMIT):
    return pltpu.CompilerParams(dimension_semantics=("arbitrary",) * n_axes,
                                vmem_limit_bytes=vmem)


def _layer_norm(t, g, b):
    mu = jnp.mean(t, axis=-1, keepdims=True)
    d = t - mu
    var = jnp.mean(d * d, axis=-1, keepdims=True)
    return d * lax.rsqrt(var + LN_EPS) * g + b


def _gelu(t):
    return 0.5 * t * (1.0 + lax.erf(t * (1.0 / math.sqrt(2.0))))


def _silu(t):
    return t * jax.nn.sigmoid(t)


def _ones_where(mask, dtype):
    return jnp.where(mask, 1.0, 0.0).astype(dtype)


def _dot(a, b):
    return jnp.dot(a, b, preferred_element_type=F32)


def _dot_nt(a, b):
    return lax.dot_general(a, b, (((1,), (1,)), ((), ())), preferred_element_type=F32)


def _dot_tn(a, b):
    return lax.dot_general(a, b, (((0,), (0,)), ((), ())), preferred_element_type=F32)


def _pack_bf16_pair(lo, hi):
    lo_b = lax.bitcast_convert_type(lo.astype(BF16).astype(F32), U32)
    hi_b = lax.bitcast_convert_type(hi.astype(BF16).astype(F32), U32)
    return hi_b | (lo_b >> 16)


def _unpack_pair_f32(p):
    lo = lax.bitcast_convert_type(p << 16, F32)
    hi = lax.bitcast_convert_type(p & jnp.uint32(0xFFFF0000), F32)
    return lo, hi


def _ada_body(c_ref, w_ref, b_ref, o_ref):
    a = _silu(c_ref[...]).astype(BF16)
    o_ref[...] = _dot(a, w_ref[...].astype(BF16)) + b_ref[...]


def _ada(cc, w, b):
    rows, d = cc.shape
    n = w.shape[1]
    tn = 1024
    return pl.pallas_call(
        _ada_body,
        grid=(n // tn,),
        in_specs=[pl.BlockSpec((rows, d), lambda j: (0, 0)),
                  pl.BlockSpec((d, tn), lambda j: (0, j)),
                  pl.BlockSpec((1, tn), lambda j: (0, j))],
        out_specs=pl.BlockSpec((rows, tn), lambda j: (0, j)),
        out_shape=jax.ShapeDtypeStruct((rows, n), F32),
        compiler_params=_cparams(1),
        name="ada",
    )(cc, w, b)


def _inproj_body(x_ref, g_ref, b_ref, sh_ref, sc_ref, w_ref, wlr_ref, o_ref, olr_ref, hm_ref):
    @pl.when(pl.program_id(1) == 0)
    def _():
        h = _layer_norm(x_ref[...], g_ref[...], b_ref[...])
        hm = (h * (1.0 + sc_ref[...]) + sh_ref[...]).astype(BF16)
        hm_ref[...] = hm
        olr_ref[...] = _dot(hm, wlr_ref[...])

    o_ref[...] = _dot(hm_ref[...], w_ref[...])


def _inproj(x2, mod3, mod_row, ln_g, ln_b, w_main, w_lr, col0, ncols, tm, tn):
    r, d = x2.shape
    return pl.pallas_call(
        _inproj_body,
        grid=(r // tm, ncols),
        in_specs=[pl.BlockSpec((tm, d), lambda i, j: (i, 0)),
                  pl.BlockSpec((1, d), lambda i, j: (0, 0)),
                  pl.BlockSpec((1, d), lambda i, j: (0, 0)),
                  pl.BlockSpec((None, 1, d), lambda i, j: (mod_row(i), 0, 0)),
                  pl.BlockSpec((None, 1, d), lambda i, j: (mod_row(i), 0, 1)),
                  pl.BlockSpec((d, tn), lambda i, j: (0, col0 + j)),
                  pl.BlockSpec((d, LANES), lambda i, j: (0, 0))],
        out_specs=[pl.BlockSpec((tm, tn), lambda i, j: (i, j)),
                   pl.BlockSpec((tm, LANES), lambda i, j: (i, 0))],
        out_shape=[jax.ShapeDtypeStruct((r, ncols * tn), F32),
                   jax.ShapeDtypeStruct((r, LANES), F32)],
        scratch_shapes=[pltpu.VMEM((tm, d), BF16)],
        compiler_params=_cparams(2),
        name="inproj",
    )(x2, ln_g, ln_b, mod3, mod3, w_main, w_lr)


def _inproj_cm_body(x_ref, g_ref, b_ref, sh_ref, sc_ref, wuv_ref, wr_ref, wlr_ref, ng_ref, nb_ref, ws_ref, bs_ref,
                    cm_ref, p_ref, lr_ref):
    tm = x_ref.shape[0]
    half = wuv_ref.shape[1] // 2
    hd = CM_CHUNK
    for r in range(0, tm, ROW_CHUNK):
        rows = slice(r, r + ROW_CHUNK)
        h = _layer_norm(x_ref[rows, :], g_ref[...], b_ref[...])
        hm = (h * (1.0 + sc_ref[...]) + sh_ref[...]).astype(BF16)
        uv = _dot(hm, wuv_ref[...])
        u = _gelu(uv[:, :half])
        vb = _layer_norm(_gelu(uv[:, half:]), ng_ref[...], nb_ref[...]).astype(BF16)
        for c in range(ROW_CHUNK // CM_CHUNK):
            crow = slice(c * CM_CHUNK, (c + 1) * CM_CHUNK)
            orow = slice(r + c * CM_CHUNK, r + (c + 1) * CM_CHUNK)
            for hh in range(CM_HEADS):
                cols = slice(hh * hd, (hh + 1) * hd)
                s = _dot(ws_ref[hh], vb[crow, cols]) + bs_ref[:, cols]
                cm_ref[orow, cols] = (u[crow, cols] * s).astype(BF16)
        p_ref[rows, :] = _dot(hm, wr_ref[...])
        lr_ref[rows, :] = _dot(hm, wlr_ref[...])


def _inproj_cm(x2, mod3, rows_per_batch, ln_g, ln_b, w_uv, w_rest, w_lr, ng, nb, ws, bs, tm):
    r, d = x2.shape
    n_uv, n_rest = w_uv.shape[1], w_rest.shape[1]
    row = lambda i: (i * tm) // rows_per_batch

    def const(shape):
        return pl.BlockSpec(shape, lambda i: (0,) * len(shape), pipeline_mode=pl.Buffered(1))

    return pl.pallas_call(
        _inproj_cm_body,
        grid=(r // tm,),
        in_specs=[pl.BlockSpec((tm, d), lambda i: (i, 0)),
                  const((1, d)), const((1, d)),
                  pl.BlockSpec((None, 1, d), lambda i: (row(i), 0, 0)),
                  pl.BlockSpec((None, 1, d), lambda i: (row(i), 0, 1)),
                  const((d, n_uv)), const((d, n_rest)), const((d, LANES)),
                  const((1, n_uv // 2)), const((1, n_uv // 2)),
                  const((CM_HEADS, CM_CHUNK, CM_CHUNK)), const((CM_CHUNK, n_uv // 2))],
        out_specs=[pl.BlockSpec((tm, n_uv // 2), lambda i: (i, 0)),
                   pl.BlockSpec((tm, n_rest), lambda i: (i, 0)),
                   pl.BlockSpec((tm, LANES), lambda i: (i, 0))],
        out_shape=[jax.ShapeDtypeStruct((r, n_uv // 2), BF16),
                   jax.ShapeDtypeStruct((r, n_rest), F32),
                   jax.ShapeDtypeStruct((r, LANES), F32)],
        compiler_params=_cparams(1),
        name="inproj_cm",
    )(x2, ln_g, ln_b, mod3, mod3, w_uv, w_rest, w_lr, ng, nb, ws, bs)


GLA_TILE = 256
GLA_TILES_PER_ITER = 2


def _gla_tile(q, k, v, lr, wg, bg, st_ref, forward, need_o):
    t = k.shape[0]
    n_chunks = t // GLA_CHUNK
    z = _dot(lr.astype(BF16), wg) + bg
    g = jax.nn.log_sigmoid(z) * (1.0 / GLA_GATE_NORMALIZER)
    r_id = lax.broadcasted_iota(I32, (t, t), 0)
    c_id = lax.broadcasted_iota(I32, (t, t), 1)
    shift = GLA_CHUNK.bit_length() - 1
    same = (r_id >> shift) == (c_id >> shift)
    lower = same & (c_id <= r_id)
    tri = _ones_where(lower, BF16)
    g_hi = g.astype(BF16)
    g_lo = (g - g_hi.astype(F32)).astype(BF16)
    csum2 = _dot(tri, jnp.concatenate([g_hi, g_lo], axis=1))
    csum = csum2[:, :GLA_DK] + csum2[:, GLA_DK:]
    g3 = g.reshape(n_chunks, GLA_CHUNK, GLA_DK)
    tot = jnp.broadcast_to(jnp.sum(g3, axis=1, keepdims=True), g3.shape).reshape(t, GLA_DK)
    bcum = csum if forward else tot - csum + g
    kd = (k * jnp.exp(tot - bcum)).astype(BF16)
    decay = jnp.exp(tot)
    vb = v.astype(BF16)
    row_chunk = lax.broadcasted_iota(I32, (t, GLA_DK), 0) >> shift
    kd_blocks = jnp.concatenate([jnp.where(row_chunk == c, kd, jnp.zeros_like(kd)) for c in range(n_chunks)], axis=1)
    u_all = _dot_tn(vb, kd_blocks)
    o = None
    if need_o:
        qe = ((q * (GLA_DK ** -0.5)) * jnp.exp(bcum)).astype(BF16)
        ke = (k * jnp.exp(-bcum)).astype(BF16)
        att = _dot_nt(qe, ke)
        mask = lower if forward else same & (c_id >= r_id)
        att = jnp.where(mask, att, 0.0).astype(BF16)
        o = _dot(att, vb)
    outs = [None] * n_chunks
    order = range(n_chunks) if forward else range(n_chunks - 1, -1, -1)
    for c in order:
        rows = slice(c * GLA_CHUNK, (c + 1) * GLA_CHUNK)
        s_t = st_ref[...]
        if need_o:
            outs[c] = o[rows] + _dot_nt(qe[rows], s_t.astype(BF16))
        u_t = u_all[:, c * GLA_DK:(c + 1) * GLA_DK]
        st_ref[...] = s_t * decay[c * GLA_CHUNK:c * GLA_CHUNK + 1, :] + u_t
    if need_o:
        return jnp.concatenate(outs, axis=0)
    return None


def _gla_body(q_ref, k_ref, v_ref, go_ref, lr_ref, kc_ref, vc_ref, lrc_ref,
              wgf_ref, bgf_ref, wgb_ref, bgb_ref, ng_ref, out_ref, o_scr, sf_ref, sb_ref):
    t = GLA_TILE
    n_x = q_ref.shape[0] // t
    n_c = kc_ref.shape[0] // t
    half = n_x // 2
    sf_ref[...] = jnp.zeros_like(sf_ref)
    sb_ref[...] = jnp.zeros_like(sb_ref)
    wgf, bgf, wgb, bgb = wgf_ref[...], bgf_ref[...], wgb_ref[...], bgb_ref[...]

    for i in range(n_c):
        rf = slice(i * t, (i + 1) * t)
        rb = slice((n_c - 1 - i) * t, (n_c - i) * t)
        _gla_tile(None, kc_ref[rf], vc_ref[rf], lrc_ref[rf], wgf, bgf, sf_ref, True, False)
        _gla_tile(None, kc_ref[rb], vc_ref[rb], lrc_ref[rb], wgb, bgb, sb_ref, False, False)

    def tile_out(i, forward):
        rows = pl.ds(pl.multiple_of(i * t, t), t)
        if forward:
            return rows, _gla_tile(q_ref[rows], k_ref[rows], v_ref[rows], lr_ref[rows],
                                   wgf, bgf, sf_ref, True, True)
        return rows, _gla_tile(q_ref[rows], k_ref[rows], v_ref[rows], lr_ref[rows],
                               wgb, bgb, sb_ref, False, True)

    def finish(rows, o):
        o = o + o_scr[rows]
        ms = jnp.mean(o * o, axis=-1, keepdims=True)
        on = o * lax.rsqrt(ms + RMS_EPS) * ng_ref[...]
        out_ref[rows] = (on * _silu(go_ref[rows])).astype(BF16)

    def keep(rows, o):
        o_scr[rows] = o

    u = math.gcd(GLA_TILES_PER_ITER, half)

    def make_step(sink):
        def step(it, carry):
            for w in range(u):
                i = it * u + w
                sink(*tile_out(i, True))
                sink(*tile_out(n_x - 1 - i, False))
            return carry
        return step

    lax.fori_loop(0, half // u, make_step(keep), 0)
    lax.fori_loop(half // u, n_x // u, make_step(finish), 0)


def _gla(p3, lr3, pc3, lrc3, wgf, bgf, wgb, bgb, ng):
    bsz, l, _ = p3.shape
    lc = pc3.shape[1]
    dk, dv = GLA_DK, GLA_DV
    kw = GLA_HEADS * dk
    q0, k0 = 0, kw // dk
    v0, go0 = 2 * kw // dv, (2 * kw + GLA_HEADS * dv) // dv
    kc0, vc0 = k0, v0
    return pl.pallas_call(
        _gla_body,
        grid=(bsz, GLA_HEADS),
        in_specs=[pl.BlockSpec((None, l, dk), lambda b, h: (b, 0, q0 + h)),
                  pl.BlockSpec((None, l, dk), lambda b, h: (b, 0, k0 + h)),
                  pl.BlockSpec((None, l, dv), lambda b, h: (b, 0, v0 + h)),
                  pl.BlockSpec((None, l, dv), lambda b, h: (b, 0, go0 + h)),
                  pl.BlockSpec((None, l, LANES), lambda b, h: (b, 0, 0)),
                  pl.BlockSpec((None, lc, dk), lambda b, h: (b, 0, kc0 + h)),
                  pl.BlockSpec((None, lc, dv), lambda b, h: (b, 0, vc0 + h)),
                  pl.BlockSpec((None, lc, LANES), lambda b, h: (b, 0, 0)),
                  pl.BlockSpec((LANES, dk), lambda b, h: (0, h)),
                  pl.BlockSpec((1, dk), lambda b, h: (0, h)),
                  pl.BlockSpec((LANES, dk), lambda b, h: (0, h)),
                  pl.BlockSpec((1, dk), lambda b, h: (0, h)),
                  pl.BlockSpec((1, dv), lambda b, h: (0, 0))],
        out_specs=pl.BlockSpec((None, l, dv), lambda b, h: (b, 0, h)),
        out_shape=jax.ShapeDtypeStruct((bsz, l, GLA_HEADS * dv), BF16),
        scratch_shapes=[pltpu.VMEM((l, dv), F32),
                        pltpu.VMEM((dv, dk), F32),
                        pltpu.VMEM((dv, dk), F32)],
        compiler_params=_cparams(2),
        name="gla",
    )(p3, p3, p3, p3, lr3, pc3, pc3, lrc3, wgf, bgf, wgb, bgb, ng)


def _post_body(cm_ref, gla_ref, x_ref, lng_ref, lnb_ref, g1_ref, sh2_ref, sc2_ref,
               wo_ref, l1g_ref, l1b_ref, wr_ref, br_ref, h1_ref, hp_ref, lg_ref):
    half = cm_ref.shape[1]
    d2 = x_ref.shape[1] // 2
    for r in range(0, x_ref.shape[0], ROW_CHUNK):
        rows = slice(r, r + ROW_CHUNK)
        y = _dot(cm_ref[rows, :], wo_ref[:half, :]) + _dot(gla_ref[rows, :], wo_ref[half:, :])
        hx = _layer_norm(x_ref[rows, :], lng_ref[...], lnb_ref[...])
        h1 = _layer_norm(DEEPNORM_ALPHA * hx + g1_ref[...] * y, l1g_ref[...], l1b_ref[...])
        h1_ref[rows, :] = h1
        hm = h1 * (1.0 + sc2_ref[...]) + sh2_ref[...]
        hp_ref[rows, :] = _pack_bf16_pair(hm[:, :d2], hm[:, d2:])
        lg_ref[rows, :] = _dot(hm.astype(BF16), wr_ref[...]) + br_ref[...]


def _post_attn(cm, gla, x2, mod3, rows_per_batch, ln_g, ln_b, w_out, l1g, l1b, w_r, b_r, tm):
    r, d = x2.shape
    half = d // 2
    row = lambda i: (i * tm) // rows_per_batch
    full = lambda shape: pl.BlockSpec(shape, lambda i: (0,) * len(shape))
    return pl.pallas_call(
        _post_body,
        grid=(r // tm,),
        in_specs=[pl.BlockSpec((tm, half), lambda i: (i, 0)),
                  pl.BlockSpec((tm, half), lambda i: (i, 0)),
                  pl.BlockSpec((tm, d), lambda i: (i, 0)),
                  full((1, d)), full((1, d)),
                  pl.BlockSpec((None, 1, d), lambda i: (row(i), 0, 2)),
                  pl.BlockSpec((None, 1, d), lambda i: (row(i), 0, 3)),
                  pl.BlockSpec((None, 1, d), lambda i: (row(i), 0, 4)),
                  full((d, d)), full((1, d)), full((1, d)),
                  full((d, LANES)), full((1, LANES))],
        out_specs=[pl.BlockSpec((tm, d), lambda i: (i, 0)),
                   pl.BlockSpec((tm, half), lambda i: (i, 0)),
                   pl.BlockSpec((tm, LANES), lambda i: (i, 0))],
        out_shape=[jax.ShapeDtypeStruct((r, d), F32),
                   jax.ShapeDtypeStruct((r, half), U32),
                   jax.ShapeDtypeStruct((r, LANES), F32)],
        compiler_params=_cparams(1),
        name="post_attn",
    )(cm, gla, x2, ln_g, ln_b, mod3, mod3, mod3, w_out, l1g, l1b, w_r, b_r)


def _route_body(lg_ref, idx_ref, wt_ref, rank_ref, cnt_ref, carry_ref):
    i = pl.program_id(0)
    t = lg_ref.shape[0]

    @pl.when(i == 0)
    def _():
        carry_ref[...] = jnp.zeros_like(carry_ref)

    lane = lax.broadcasted_iota(I32, (t, LANES), 1)
    lane_f = lane.astype(F32)
    neg = jnp.float32(-jnp.inf)
    l = jnp.where(lane < N_EXPERTS, lg_ref[...], neg)
    tops, onehots, idxs = [], [], []
    for _ in range(TOP_K):
        m = jnp.max(l, axis=-1, keepdims=True)
        idx = jnp.min(jnp.where(l == m, lane_f, float(LANES)), axis=-1, keepdims=True).astype(I32)
        oh = lane == idx
        l = jnp.where(oh, neg, l)
        tops.append(m)
        idxs.append(idx)
        onehots.append(oh)
    exps = [jnp.exp(m - tops[0]) for m in tops]
    denom = exps[0] + exps[1] + exps[2] + exps[3]
    sel = _ones_where(onehots[0] | onehots[1] | onehots[2] | onehots[3], F32)
    r_id = lax.broadcasted_iota(I32, (t, t), 0)
    c_id = lax.broadcasted_iota(I32, (t, t), 1)
    strict = _ones_where(c_id < r_id, BF16)
    before = _dot(strict, sel.astype(BF16)) + carry_ref[0:1, :]
    idx_out = jnp.zeros((t, LANES), I32)
    wt_out = jnp.zeros((t, LANES), F32)
    rank_out = jnp.zeros((t, LANES), I32)
    for k in range(TOP_K):
        rk = jnp.sum(jnp.where(onehots[k], before, 0.0), axis=-1, keepdims=True).astype(I32)
        idx_out = jnp.where(lane == k, idxs[k], idx_out)
        wt_out = jnp.where(lane == k, exps[k] / denom, wt_out)
        rank_out = jnp.where(lane == k, rk, rank_out)
    idx_ref[...] = idx_out[:, :TOP_K]
    wt_ref[...] = wt_out
    rank_ref[...] = rank_out[:, :TOP_K]
    total = carry_ref[0:1, :] + jnp.sum(sel, axis=0, keepdims=True)
    carry_ref[...] = jnp.broadcast_to(total, carry_ref.shape)
    cnt_ref[...] = jnp.broadcast_to(total, cnt_ref.shape)


def _route(logits, tm):
    n = logits.shape[0]
    blk = pl.BlockSpec((tm, LANES), lambda i: (i, 0))
    small = pl.BlockSpec((tm, TOP_K), lambda i: (i, 0))
    return pl.pallas_call(
        _route_body,
        grid=(n // tm,),
        in_specs=[blk],
        out_specs=[small, blk, small, pl.BlockSpec((8, LANES), lambda i: (0, 0))],
        out_shape=[jax.ShapeDtypeStruct((n, TOP_K), I32),
                   jax.ShapeDtypeStruct((n, LANES), F32),
                   jax.ShapeDtypeStruct((n, TOP_K), I32),
                   jax.ShapeDtypeStruct((8, LANES), F32)],
        scratch_shapes=[pltpu.VMEM((8, LANES), F32)],
        compiler_params=_cparams(1),
        name="route",
    )(logits)


def _dispatch_body(nz_ref, zl_ref, dest_ref, hp_ref, xs_ref, zero_ref, sem, zsem):
    t = hp_ref.shape[0] * hp_ref.shape[1]

    @pl.when(pl.program_id(0) == 0)
    def _():
        zero_ref[...] = jnp.zeros_like(zero_ref)

        def z_copy(b):
            r = pl.multiple_of(zl_ref[b] * MOE_BLOCK, MOE_BLOCK)
            return pltpu.make_async_copy(zero_ref, xs_ref.at[pl.ds(r, MOE_BLOCK)], zsem)

        def start(b, carry):
            z_copy(b).start()
            return carry

        def wait(b, carry):
            z_copy(b).wait()
            return carry

        lax.fori_loop(0, nz_ref[0], start, 0)
        lax.fori_loop(0, nz_ref[0], wait, 0)

    def issue(g, carry):
        for u in range(SUBLANES):
            for k in range(TOP_K):
                d = dest_ref[0, (g * SUBLANES + u) * TOP_K + k]
                pltpu.make_async_copy(hp_ref.at[g, pl.ds(u, 1)], xs_ref.at[pl.ds(d, 1)], sem).start(priority=k % 2)
        return carry

    lax.fori_loop(0, t // SUBLANES, issue, 0)
    for _ in range(TOP_K):
        pltpu.make_async_copy(xs_ref.at[pl.ds(0, t)], xs_ref.at[pl.ds(0, t)], sem).wait()


def _dispatch(n_zero, zero_blocks, dest, hp, n_slots, tm):
    n, w = hp.shape
    dest3 = dest.reshape(n // tm, 1, tm * TOP_K)
    return pl.pallas_call(
        _dispatch_body,
        grid_spec=pltpu.PrefetchScalarGridSpec(
            num_scalar_prefetch=2,
            grid=(n // tm,),
            in_specs=[pl.BlockSpec((None, 1, tm * TOP_K), lambda i, nz, zl: (i, 0, 0), memory_space=pltpu.SMEM),
                      pl.BlockSpec((tm // SUBLANES, SUBLANES, w), lambda i, nz, zl: (i, 0, 0))],
            out_specs=pl.BlockSpec(memory_space=pl.ANY),
            scratch_shapes=[pltpu.VMEM((MOE_BLOCK, w), U32),
                            pltpu.SemaphoreType.DMA(()),
                            pltpu.SemaphoreType.DMA(())]),
        out_shape=jax.ShapeDtypeStruct((n_slots, w), U32),
        compiler_params=_cparams(1),
        name="dispatch",
    )(n_zero, zero_blocks, dest3, hp.reshape(n // SUBLANES, SUBLANES, w))


PASS_SUBS = 9
EXPERT_TN = 256
DOT_ROWS = 1024
W_SLOTS = 3
W_AHEAD = 2


def _expert_body(meta_ref, pe_ref, pr_ref, pn_ref, xs_hbm, wgu_hbm, wdn_hbm, bgu_ref, bdn_ref,
                 y_hbm, xraw, xb, h_ref, wg_buf, wu_buf, wd_buf, ybuf, zbuf,
                 sem_x, sem_y, sem_z, sem_w, sem_d):
    sub = MOE_BLOCK
    tn = EXPERT_TN
    tp = tn // 2
    nj = h_ref.shape[0]
    nc = y_hbm.shape[1] // tp
    de = nj * tn
    d2 = xraw.shape[1]
    p = pl.program_id(0)
    n_pass = meta_ref[0]
    nsub = pn_ref[p]
    row0 = pr_ref[p]

    def x_copy(i, pp):
        r = pl.multiple_of((pr_ref[pp] + i) * sub, sub)
        return pltpu.make_async_copy(xs_hbm.at[pl.ds(r, sub)], xraw.at[pl.ds(i * sub, sub)], sem_x)

    def y_copy(start, size, r0, c, slot):
        r = pl.multiple_of(r0 * sub + start, sub)
        col = pl.multiple_of(c * tp, tp)
        return pltpu.make_async_copy(ybuf.at[slot, pl.ds(start, size), :],
                                     y_hbm.at[pl.ds(r, size), pl.ds(col, tp)], sem_y.at[slot])

    def z_copy(b, c):
        r = pl.multiple_of(b * sub, sub)
        return pltpu.make_async_copy(zbuf, y_hbm.at[pl.ds(r, sub), pl.ds(c * tp, tp)], sem_z)

    def w1_copies(pp, j):
        e = pe_ref[pp]
        slot = j % W_SLOTS
        col = pl.multiple_of(j * tn, tn)
        return (pltpu.make_async_copy(wgu_hbm.at[e, :, pl.ds(col, tn)], wg_buf.at[slot], sem_w.at[slot]),
                pltpu.make_async_copy(wgu_hbm.at[e, :, pl.ds(de + col, tn)], wu_buf.at[slot], sem_w.at[slot]))

    def w2_copy(pp, c):
        slot = c % W_SLOTS
        col = pl.multiple_of(c * tn, tn)
        return pltpu.make_async_copy(wdn_hbm.at[pe_ref[pp], :, pl.ds(col, tn)], wd_buf.at[slot], sem_d.at[slot])

    def start_w1(pp, j):
        for cp in w1_copies(pp, j):
            cp.start()

    def for_subs(count, fn):
        for i in range(PASS_SUBS):
            pl.when(i < count)(lambda i=i: fn(i))

    def for_tail(fn):
        def body(b, carry):
            for c in range(nc):
                fn(b, c)
            return carry
        lax.fori_loop(meta_ref[1], y_hbm.shape[0] // sub, body, 0)

    def for_groups(count, fn):
        pl.when(count == PASS_SUBS)(lambda: fn(0, PASS_SUBS * sub))
        k = 1 << (PASS_SUBS.bit_length() - 1)
        while k:
            start = pl.multiple_of((count & (-2 * k)) * sub, sub)
            pl.when((count != PASS_SUBS) & ((count & k) != 0))(lambda start=start, k=k: fn(start, k * sub))
            k //= 2

    @pl.when(p == 0)
    def _():
        for_subs(nsub, lambda i: x_copy(i, p).start())
        for j in range(W_AHEAD):
            start_w1(p, j)
        zbuf[...] = jnp.zeros_like(zbuf)
        for_tail(lambda b, c: z_copy(b, c).start())

    for_subs(nsub, lambda i: x_copy(i, p).wait())

    def unpack(i):
        rows = slice(i * sub, (i + 1) * sub)
        lo, hi = _unpack_pair_f32(xraw[rows, :])
        xb[rows, :d2] = lo.astype(BF16)
        xb[rows, d2:] = hi.astype(BF16)

    for_subs(nsub, unpack)

    @pl.when(p + 1 < n_pass)
    def _():
        for_subs(pn_ref[p + 1], lambda i: x_copy(i, p + 1).start())

    def first_step(j, carry):
        slot = j % W_SLOTS
        for cp in w1_copies(p, j):
            cp.wait()
        nxt = j + W_AHEAD
        pl.when(nxt < nj)(lambda: start_w1(p, nxt))
        pl.when(nxt >= nj)(lambda: w2_copy(p, nxt - nj).start())

        def group(start, size):
            wg = wg_buf[slot].astype(BF16)
            wu = wu_buf[slot].astype(BF16)
            dr = DOT_ROWS if size % DOT_ROWS == 0 else size
            for r in range(0, size, dr):
                rows = pl.ds(start + r, dr)
                x = xb[rows, :]
                gate = jnp.minimum(_dot(x, wg) + bgu_ref[j], SWIGLU_LIMIT)
                up = jnp.clip(_dot(x, wu) + bgu_ref[nj + j], -SWIGLU_LIMIT, SWIGLU_LIMIT)
                h_ref[j, rows, :] = ((up + 1.0) * gate * jax.nn.sigmoid(SWIGLU_ALPHA * gate)).astype(BF16)

        for_groups(nsub, group)
        return carry

    lax.fori_loop(0, nj, first_step, 0)

    def second_step(c, carry):
        wslot = c % W_SLOTS
        slot = c % 2
        w2_copy(p, c).wait()
        nxt = c + W_AHEAD
        pl.when(nxt < nc)(lambda: w2_copy(p, nxt).start())
        pl.when((nxt >= nc) & (p + 1 < n_pass))(lambda: start_w1(p + 1, nxt - nc))

        @pl.when(c >= 2)
        def _():
            for_groups(nsub, lambda start, size: y_copy(start, size, row0, c - 2, slot).wait())

        @pl.when((c < 2) & (p > 0))
        def _():
            for_groups(pn_ref[p - 1],
                       lambda start, size: y_copy(start, size, pr_ref[p - 1], nc - 2 + c, slot).wait())

        def group(start, size):
            wd = wd_buf[wslot].astype(BF16)
            dr = DOT_ROWS if size % DOT_ROWS == 0 else size
            for r in range(0, size, dr):
                rows = pl.ds(start + r, dr)
                hx = jnp.concatenate([h_ref[j, rows, :] for j in range(nj)], axis=1)
                y = _dot(hx, wd) + bdn_ref[c]
                ybuf[slot, rows, :] = _pack_bf16_pair(y[:, :tp], y[:, tp:])
            y_copy(start, size, row0, c, slot).start()

        for_groups(nsub, group)
        return carry

    lax.fori_loop(0, nc, second_step, 0)

    @pl.when(p == n_pass - 1)
    def _():
        for c in (nc - 2, nc - 1):
            for_groups(nsub, lambda start, size, c=c: y_copy(start, size, row0, c, c % 2).wait())
        for_tail(lambda b, cc: z_copy(b, cc).wait())


def _experts(meta, pass_e, pass_row0, pass_nsub, xs, w_gu, b_gu, w_dn, b_dn):
    n_slots, d2 = xs.shape
    d = 2 * d2
    n_e, de = w_dn.shape[0], w_dn.shape[1]
    tn = EXPERT_TN
    nj = de // tn
    nc = d // tn
    assert W_AHEAD < W_SLOTS and W_AHEAD <= min(nj, nc) and nc % 2 == 0
    rmax = PASS_SUBS * MOE_BLOCK
    return pl.pallas_call(
        _expert_body,
        grid_spec=pltpu.PrefetchScalarGridSpec(
            num_scalar_prefetch=4,
            grid=(meta[0],),
            in_specs=[
                pl.BlockSpec(memory_space=pl.ANY),
                pl.BlockSpec(memory_space=pl.ANY),
                pl.BlockSpec(memory_space=pl.ANY),
                pl.BlockSpec((None, 2 * nj, 1, tn), lambda p, m, e, r, n: (e[p], 0, 0, 0)),
                pl.BlockSpec((None, nc, 1, tn), lambda p, m, e, r, n: (e[p], 0, 0, 0)),
            ],
            out_specs=pl.BlockSpec(memory_space=pl.ANY),
            scratch_shapes=[pltpu.VMEM((rmax, d2), U32),
                            pltpu.VMEM((rmax, d), BF16),
                            pltpu.VMEM((nj, rmax, tn), BF16),
                            pltpu.VMEM((W_SLOTS, d, tn), w_gu.dtype),
                            pltpu.VMEM((W_SLOTS, d, tn), w_gu.dtype),
                            pltpu.VMEM((W_SLOTS, de, tn), w_dn.dtype),
                            pltpu.VMEM((2, rmax, tn // 2), U32),
                            pltpu.VMEM((MOE_BLOCK, tn // 2), U32),
                            pltpu.SemaphoreType.DMA(()),
                            pltpu.SemaphoreType.DMA((2,)),
                            pltpu.SemaphoreType.DMA(()),
                            pltpu.SemaphoreType.DMA((W_SLOTS,)),
                            pltpu.SemaphoreType.DMA((W_SLOTS,))]),
        out_shape=jax.ShapeDtypeStruct((n_slots, d2), U32),
        compiler_params=_cparams(1),
        name="experts",
    )(meta, pass_e, pass_row0, pass_nsub, xs, w_gu, w_dn,
      b_gu.reshape(n_e, 2 * nj, 1, tn), b_dn.reshape(n_e, nc, 1, tn))


COMBINE_CHUNK = 64


def _combine_body(dcur_ref, dnxt_ref, wt_ref, h1_ref, g2_ref, l2g_ref, l2b_ref, y_ref, out_ref, buf_ref, sem):
    i = pl.program_id(0)
    n_tiles = pl.num_programs(0)
    t = h1_ref.shape[0]
    tp = EXPERT_TN // 2
    slot = i % 2

    def gather(dest_ref, sl):
        def body(g, carry):
            for u in range(SUBLANES):
                for k in range(TOP_K):
                    d = dest_ref[0, (g * SUBLANES + u) * TOP_K + k]
                    pltpu.make_async_copy(y_ref.at[pl.ds(d, 1)], buf_ref.at[sl, k, g, pl.ds(u, 1)],
                                          sem.at[sl]).start(priority=k % 2)
            return carry
        lax.fori_loop(0, t // SUBLANES, body, 0)

    pl.when(i == 0)(lambda: gather(dcur_ref, 0))
    pl.when(i + 1 < n_tiles)(lambda: gather(dnxt_ref, 1 - slot))
    for k in range(TOP_K):
        pltpu.make_async_copy(buf_ref.at[1 - slot, k], buf_ref.at[slot, k], sem.at[slot]).wait()
    for r in range(0, t, COMBINE_CHUNK):
        rows = slice(r, r + COMBINE_CHUNK)
        groups = slice(r // SUBLANES, (r + COMBINE_CHUNK) // SUBLANES)
        wt = wt_ref[rows, :]
        f_lo, f_hi = None, None
        for k in range(TOP_K):
            lo, hi = _unpack_pair_f32(buf_ref[slot, k, groups].reshape(COMBINE_CHUNK, buf_ref.shape[-1]))
            w = wt[:, k:k + 1]
            f_lo = lo * w if f_lo is None else f_lo + lo * w
            f_hi = hi * w if f_hi is None else f_hi + hi * w
        parts = []
        for c in range(f_lo.shape[1] // tp):
            parts += [f_lo[:, c * tp:(c + 1) * tp], f_hi[:, c * tp:(c + 1) * tp]]
        f = jnp.concatenate(parts, axis=1)
        out_ref[rows, :] = _layer_norm(DEEPNORM_ALPHA * h1_ref[rows, :] + g2_ref[...] * f,
                                       l2g_ref[...], l2b_ref[...])


def _combine(dest, wt, h1, mod3, rows_per_batch, l2g, l2b, y, tm):
    n, d = h1.shape
    n_tiles = n // tm
    dest3 = dest.reshape(n_tiles, 1, tm * TOP_K)
    row = lambda i: (i * tm) // rows_per_batch
    return pl.pallas_call(
        _combine_body,
        grid=(n_tiles,),
        in_specs=[pl.BlockSpec((None, 1, tm * TOP_K), lambda i: (i, 0, 0), memory_space=pltpu.SMEM),
                  pl.BlockSpec((None, 1, tm * TOP_K), lambda i: (jnp.minimum(i + 1, n_tiles - 1), 0, 0),
                               memory_space=pltpu.SMEM),
                  pl.BlockSpec((tm, LANES), lambda i: (i, 0)),
                  pl.BlockSpec((tm, d), lambda i: (i, 0)),
                  pl.BlockSpec((None, 1, d), lambda i: (row(i), 0, 5)),
                  pl.BlockSpec((1, d), lambda i: (0, 0)),
                  pl.BlockSpec((1, d), lambda i: (0, 0)),
                  pl.BlockSpec(memory_space=pl.ANY)],
        out_specs=pl.BlockSpec((tm, d), lambda i: (i, 0)),
        out_shape=jax.ShapeDtypeStruct((n, d), F32),
        scratch_shapes=[pltpu.VMEM((2, TOP_K, tm // SUBLANES, SUBLANES, y.shape[1]), U32),
                        pltpu.SemaphoreType.DMA((2,))],
        compiler_params=_cparams(1),
        name="combine",
    )(dest3, dest3, wt, h1, mod3, l2g, l2b, y)


def _pick_tile(n, pref):
    t = pref
    while n % t:
        t //= 2
    return t


def kernel(x, c, ctx, c_ctx, ln_in_g, ln_in_b, w_ada, b_ada, w_in, cm_norm_g, cm_norm_b, cm_w_s, cm_b_s,
           gla_w_gk_f, gla_b_gk_f, gla_w_gk_b, gla_b_gk_b, gla_norm_g, w_out, ln1_g, ln1_b,
           w_router, b_router, w_gate_up, b_gate_up, w_down, b_down, ln2_g, ln2_b):
    bsz, l, d = x.shape
    lc = ctx.shape[1]
    n, nc = bsz * l, bsz * lc
    assert w_ada.shape[0] == 1, "single-layer configuration"
    assert bsz + 1 <= 8 and l % (2 * GLA_TILE) == 0 and lc % GLA_TILE == 0
    row = lambda v: v.reshape(1, -1)

    cc = jnp.concatenate([c, c_ctx[None, :], jnp.zeros((8 - bsz - 1, d), F32)], axis=0)
    mod3 = _ada(cc, w_ada[0], row(b_ada[0])).reshape(8, 1, N_MOD * d)

    n_uv = 2 * CM_HEADS * CM_CHUNK
    n_main = n_uv + 2 * GLA_HEADS * GLA_DK + 2 * GLA_HEADS * GLA_DV
    w_uv = w_in[0][:, :n_uv].astype(BF16)
    w_rest = w_in[0][:, n_uv:n_main].astype(BF16)
    w_lr = jnp.pad(w_in[0][:, n_main:].astype(BF16), ((0, 0), (0, LANES - 2 * GLA_RANK)))
    x2 = x.reshape(n, d)
    tm_x = _pick_tile(l, 512)
    bs_tile = jnp.repeat(cm_b_s[0].T, CM_CHUNK, axis=1)
    cm, p, lr = _inproj_cm(x2, mod3, l, row(ln_in_g), row(ln_in_b), w_uv, w_rest, w_lr,
                           row(cm_norm_g[0]), row(cm_norm_b[0]), cm_w_s[0].astype(BF16), bs_tile, tm_x)
    tm_c = _pick_tile(nc, 512)
    pc, lrc = _inproj(ctx.reshape(nc, d), mod3, lambda i: bsz, row(ln_in_g), row(ln_in_b),
                      w_rest, w_lr, 0, 2, tm_c, 1024)

    kw = GLA_HEADS * GLA_DK
    wgf = jnp.zeros((LANES, kw), BF16).at[:GLA_RANK].set(gla_w_gk_f[0].astype(BF16))
    wgb = jnp.zeros((LANES, kw), BF16).at[GLA_RANK:2 * GLA_RANK].set(gla_w_gk_b[0].astype(BF16))
    gla = _gla(p.reshape(bsz, l, -1), lr.reshape(bsz, l, LANES), pc.reshape(bsz, lc, -1),
               lrc.reshape(bsz, lc, LANES), wgf, row(gla_b_gk_f[0]), wgb, row(gla_b_gk_b[0]),
               row(gla_norm_g[0])).reshape(n, -1)

    w_r = jnp.pad(w_router[0], ((0, 0), (0, LANES - N_EXPERTS))).astype(BF16)
    b_r = jnp.pad(b_router[0], (0, LANES - N_EXPERTS)).reshape(1, LANES)
    h1, hp, logits = _post_attn(cm, gla, x2, mod3, l, row(ln_in_g), row(ln_in_b), w_out[0].astype(BF16),
                                row(ln1_g[0]), row(ln1_b[0]), w_r, b_r, tm_x)

    idx, wt, rank, cnt = _route(logits, _pick_tile(n, 1024))
    counts = cnt[0, :N_EXPERTS].astype(I32)
    n_blocks = (n * TOP_K + N_EXPERTS * (MOE_BLOCK - 1)) // MOE_BLOCK
    blocks_e = (counts + MOE_BLOCK - 1) // MOE_BLOCK
    blk_end = jnp.cumsum(blocks_e)
    blk_start = blk_end - blocks_e
    dest = ((blk_start * MOE_BLOCK)[idx] + rank).reshape(-1)
    n_pass_max = n_blocks // PASS_SUBS + N_EXPERTS
    pass_cnt = (blocks_e + PASS_SUBS - 1) // PASS_SUBS
    pass_end = jnp.cumsum(pass_cnt)
    pass_start = pass_end - pass_cnt
    pid = jnp.arange(n_pass_max, dtype=I32)
    pass_e = jnp.minimum(jnp.searchsorted(pass_end, pid, side="right"), N_EXPERTS - 1).astype(I32)
    local = pid - pass_start[pass_e]
    pass_row0 = (blk_start[pass_e] + local * PASS_SUBS).astype(I32)
    pass_nsub = jnp.clip(blocks_e[pass_e] - local * PASS_SUBS, 0, PASS_SUBS).astype(I32)
    meta = jnp.stack([pass_end[-1], blk_end[-1]]).astype(I32)

    ar = jnp.arange(N_EXPERTS, dtype=I32)
    cand = jnp.concatenate([jnp.where(counts % MOE_BLOCK != 0, blk_end - 1, -1),
                            jnp.where(blk_end[-1] + ar < n_blocks, blk_end[-1] + ar, -1)])
    zero_blocks = cand[jnp.argsort(cand < 0, stable=True)].astype(I32)
    n_zero = jnp.sum(cand >= 0).astype(I32).reshape(1)

    xs = _dispatch(n_zero, zero_blocks, dest, hp, n_blocks * MOE_BLOCK, _pick_tile(n, 512))
    ys = _experts(meta, pass_e, pass_row0, pass_nsub, xs, w_gate_up[0], b_gate_up[0], w_down[0], b_down[0])
    out = _combine(dest, wt, h1, mod3, l, row(ln2_g[0]), row(ln2_b[0]), ys, _pick_tile(n, 512))
    return out.reshape(bsz, l, d)
```

```python
import math

import jax
import jax.numpy as jnp
from jax import lax
from jax.experimental import pallas as pl
from jax.experimental.pallas import tpu as pltpu

F32 = jnp.float32
BF16 = jnp.bfloat16
U32 = jnp.uint32
I32 = jnp.int32

CM_CHUNK = 128
CM_HEADS = 8
GLA_HEADS = 4
GLA_DK = 128
GLA_DV = 256
GLA_CHUNK = 64
GLA_RANK = 16
GLA_GATE_NORMALIZER = 16.0
N_EXPERTS = 32
TOP_K = 4
MOE_BLOCK = 256
SWIGLU_LIMIT = 7.0
SWIGLU_ALPHA = 1.702
N_MOD = 6
DEEPNORM_ALPHA = 2.0 ** 0.25
LN_EPS = 1e-5
RMS_EPS = 1e-6

LANES = 128
SUBLANES = 8
VMEM_LIMIT = 56 * 1024 * 1024
ROW_CHUNK = 256


def _cparams(n_axes, vmem=VMEM_LIMIT):
    return pltpu.CompilerParams(dimension_semantics=("arbitrary",) * n_axes,
                                vmem_limit_bytes=vmem)


def _layer_norm(t, g, b):
    mu = jnp.mean(t, axis=-1, keepdims=True)
    d = t - mu
    var = jnp.mean(d * d, axis=-1, keepdims=True)
    return d * lax.rsqrt(var + LN_EPS) * g + b


def _gelu(t):
    return 0.5 * t * (1.0 + lax.erf(t * (1.0 / math.sqrt(2.0))))


def _silu(t):
    return t * jax.nn.sigmoid(t)


def _ones_where(mask, dtype):
    return jnp.where(mask, 1.0, 0.0).astype(dtype)


def _dot(a, b):
    return jnp.dot(a, b, preferred_element_type=F32)


def _dot_nt(a, b):
    return lax.dot_general(a, b, (((1,), (1,)), ((), ())), preferred_element_type=F32)


def _dot_tn(a, b):
    return lax.dot_general(a, b, (((0,), (0,)), ((), ())), preferred_element_type=F32)


def _pack_bf16_pair(lo, hi):
    lo_b = lax.bitcast_convert_type(lo.astype(BF16).astype(F32), U32)
    hi_b = lax.bitcast_convert_type(hi.astype(BF16).astype(F32), U32)
    return hi_b | (lo_b >> 16)


def _unpack_pair_f32(p):
    lo = lax.bitcast_convert_type(p << 16, F32)
    hi = lax.bitcast_convert_type(p & jnp.uint32(0xFFFF0000), F32)
    return lo, hi


def _ada_body(c_ref, w_ref, b_ref, o_ref):
    a = _silu(c_ref[...]).astype(BF16)
    o_ref[...] = _dot(a, w_ref[...].astype(BF16)) + b_ref[...]


def _ada(cc, w, b):
    rows, d = cc.shape
    n = w.shape[1]
    tn = 1024
    return pl.pallas_call(
        _ada_body,
        grid=(n // tn,),
        in_specs=[pl.BlockSpec((rows, d), lambda j: (0, 0)),
                  pl.BlockSpec((d, tn), lambda j: (0, j)),
                  pl.BlockSpec((1, tn), lambda j: (0, j))],
        out_specs=pl.BlockSpec((rows, tn), lambda j: (0, j)),
        out_shape=jax.ShapeDtypeStruct((rows, n), F32),
        compiler_params=_cparams(1),
        name="ada",
    )(cc, w, b)


def _inproj_body(x_ref, g_ref, b_ref, sh_ref, sc_ref, w_ref, wlr_ref, o_ref, olr_ref, hm_ref):
    @pl.when(pl.program_id(1) == 0)
    def _():
        h = _layer_norm(x_ref[...], g_ref[...], b_ref[...])
        hm = (h * (1.0 + sc_ref[...]) + sh_ref[...]).astype(BF16)
        hm_ref[...] = hm
        olr_ref[...] = _dot(hm, wlr_ref[...])

    o_ref[...] = _dot(hm_ref[...], w_ref[...])


def _inproj(x2, mod3, mod_row, ln_g, ln_b, w_main, w_lr, col0, ncols, tm, tn):
    r, d = x2.shape
    return pl.pallas_call(
        _inproj_body,
        grid=(r // tm, ncols),
        in_specs=[pl.BlockSpec((tm, d), lambda i, j: (i, 0)),
                  pl.BlockSpec((1, d), lambda i, j: (0, 0)),
                  pl.BlockSpec((1, d), lambda i, j: (0, 0)),
                  pl.BlockSpec((None, 1, d), lambda i, j: (mod_row(i), 0, 0)),
                  pl.BlockSpec((None, 1, d), lambda i, j: (mod_row(i), 0, 1)),
                  pl.BlockSpec((d, tn), lambda i, j: (0, col0 + j)),
                  pl.BlockSpec((d, LANES), lambda i, j: (0, 0))],
        out_specs=[pl.BlockSpec((tm, tn), lambda i, j: (i, j)),
                   pl.BlockSpec((tm, LANES), lambda i, j: (i, 0))],
        out_shape=[jax.ShapeDtypeStruct((r, ncols * tn), F32),
                   jax.ShapeDtypeStruct((r, LANES), F32)],
        scratch_shapes=[pltpu.VMEM((tm, d), BF16)],
        compiler_params=_cparams(2),
        name="inproj",
    )(x2, ln_g, ln_b, mod3, mod3, w_main, w_lr)


def _inproj_cm_body(x_ref, g_ref, b_ref, sh_ref, sc_ref, wuv_ref, wr_ref, wlr_ref, ng_ref, nb_ref, ws_ref, bs_ref,
                    cm_ref, p_ref, lr_ref):
    tm = x_ref.shape[0]
    half = wuv_ref.shape[1] // 2
    hd = CM_CHUNK
    for r in range(0, tm, ROW_CHUNK):
        rows = slice(r, r + ROW_CHUNK)
        h = _layer_norm(x_ref[rows, :], g_ref[...], b_ref[...])
        hm = (h * (1.0 + sc_ref[...]) + sh_ref[...]).astype(BF16)
        uv = _dot(hm, wuv_ref[...])
        u = _gelu(uv[:, :half])
        vb = _layer_norm(_gelu(uv[:, half:]), ng_ref[...], nb_ref[...]).astype(BF16)
        for c in range(ROW_CHUNK // CM_CHUNK):
            crow = slice(c * CM_CHUNK, (c + 1) * CM_CHUNK)
            orow = slice(r + c * CM_CHUNK, r + (c + 1) * CM_CHUNK)
            for hh in range(CM_HEADS):
                cols = slice(hh * hd, (hh + 1) * hd)
                s = _dot(ws_ref[hh], vb[crow, cols]) + bs_ref[:, cols]
                cm_ref[orow, cols] = (u[crow, cols] * s).astype(BF16)
        p_ref[rows, :] = _dot(hm, wr_ref[...])
        lr_ref[rows, :] = _dot(hm, wlr_ref[...])


def _inproj_cm(x2, mod3, rows_per_batch, ln_g, ln_b, w_uv, w_rest, w_lr, ng, nb, ws, bs, tm):
    r, d = x2.shape
    n_uv, n_rest = w_uv.shape[1], w_rest.shape[1]
    row = lambda i: (i * tm) // rows_per_batch

    def const(shape):
        return pl.BlockSpec(shape, lambda i: (0,) * len(shape), pipeline_mode=pl.Buffered(1))

    return pl.pallas_call(
        _inproj_cm_body,
        grid=(r // tm,),
        in_specs=[pl.BlockSpec((tm, d), lambda i: (i, 0)),
                  const((1, d)), const((1, d)),
                  pl.BlockSpec((None, 1, d), lambda i: (row(i), 0, 0)),
                  pl.BlockSpec((None, 1, d), lambda i: (row(i), 0, 1)),
                  const((d, n_uv)), const((d, n_rest)), const((d, LANES)),
                  const((1, n_uv // 2)), const((1, n_uv // 2)),
                  const((CM_HEADS, CM_CHUNK, CM_CHUNK)), const((CM_CHUNK, n_uv // 2))],
        out_specs=[pl.BlockSpec((tm, n_uv // 2), lambda i: (i, 0)),
                   pl.BlockSpec((tm, n_rest), lambda i: (i, 0)),
                   pl.BlockSpec((tm, LANES), lambda i: (i, 0))],
        out_shape=[jax.ShapeDtypeStruct((r, n_uv // 2), BF16),
                   jax.ShapeDtypeStruct((r, n_rest), F32),
                   jax.ShapeDtypeStruct((r, LANES), F32)],
        compiler_params=_cparams(1),
        name="inproj_cm",
    )(x2, ln_g, ln_b, mod3, mod3, w_uv, w_rest, w_lr, ng, nb, ws, bs)


GLA_TILE = 256
GLA_TILES_PER_ITER = 2


def _gla_tile(q, k, v, lr, wg, bg, st_ref, forward, need_o):
    t = k.shape[0]
    n_chunks = t // GLA_CHUNK
    z = _dot(lr.astype(BF16), wg) + bg
    g = jax.nn.log_sigmoid(z) * (1.0 / GLA_GATE_NORMALIZER)
    r_id = lax.broadcasted_iota(I32, (t, t), 0)
    c_id = lax.broadcasted_iota(I32, (t, t), 1)
    shift = GLA_CHUNK.bit_length() - 1
    same = (r_id >> shift) == (c_id >> shift)
    lower = same & (c_id <= r_id)
    tri = _ones_where(lower, BF16)
    g_hi = g.astype(BF16)
    g_lo = (g - g_hi.astype(F32)).astype(BF16)
    csum2 = _dot(tri, jnp.concatenate([g_hi, g_lo], axis=1))
    csum = csum2[:, :GLA_DK] + csum2[:, GLA_DK:]
    g3 = g.reshape(n_chunks, GLA_CHUNK, GLA_DK)
    tot = jnp.broadcast_to(jnp.sum(g3, axis=1, keepdims=True), g3.shape).reshape(t, GLA_DK)
    bcum = csum if forward else tot - csum + g
    kd = (k * jnp.exp(tot - bcum)).astype(BF16)
    decay = jnp.exp(tot)
    vb = v.astype(BF16)
    row_chunk = lax.broadcasted_iota(I32, (t, GLA_DK), 0) >> shift
    kd_blocks = jnp.concatenate([jnp.where(row_chunk == c, kd, jnp.zeros_like(kd)) for c in range(n_chunks)], axis=1)
    u_all = _dot_tn(vb, kd_blocks)
    o = None
    if need_o:
        qe = ((q * (GLA_DK ** -0.5)) * jnp.exp(bcum)).astype(BF16)
        ke = (k * jnp.exp(-bcum)).astype(BF16)
        att = _dot_nt(qe, ke)
        mask = lower if forward else same & (c_id >= r_id)
        att = jnp.where(mask, att, 0.0).astype(BF16)
        o = _dot(att, vb)
    outs = [None] * n_chunks
    order = range(n_chunks) if forward else range(n_chunks - 1, -1, -1)
    for c in order:
        rows = slice(c * GLA_CHUNK, (c + 1) * GLA_CHUNK)
        s_t = st_ref[...]
        if need_o:
            outs[c] = o[rows] + _dot_nt(qe[rows], s_t.astype(BF16))
        u_t = u_all[:, c * GLA_DK:(c + 1) * GLA_DK]
        st_ref[...] = s_t * decay[c * GLA_CHUNK:c * GLA_CHUNK + 1, :] + u_t
    if need_o:
        return jnp.concatenate(outs, axis=0)
    return None


def _gla_body(q_ref, k_ref, v_ref, go_ref, lr_ref, kc_ref, vc_ref, lrc_ref,
              wgf_ref, bgf_ref, wgb_ref, bgb_ref, ng_ref, out_ref, o_scr, sf_ref, sb_ref):
    t = GLA_TILE
    n_x = q_ref.shape[0] // t
    n_c = kc_ref.shape[0] // t
    half = n_x // 2
    sf_ref[...] = jnp.zeros_like(sf_ref)
    sb_ref[...] = jnp.zeros_like(sb_ref)
    wgf, bgf, wgb, bgb = wgf_ref[...], bgf_ref[...], wgb_ref[...], bgb_ref[...]

    for i in range(n_c):
        rf = slice(i * t, (i + 1) * t)
        rb = slice((n_c - 1 - i) * t, (n_c - i) * t)
        _gla_tile(None, kc_ref[rf], vc_ref[rf], lrc_ref[rf], wgf, bgf, sf_ref, True, False)
        _gla_tile(None, kc_ref[rb], vc_ref[rb], lrc_ref[rb], wgb, bgb, sb_ref, False, False)

    def tile_out(i, forward):
        rows = pl.ds(pl.multiple_of(i * t, t), t)
        if forward:
            return rows, _gla_tile(q_ref[rows], k_ref[rows], v_ref[rows], lr_ref[rows],
                                   wgf, bgf, sf_ref, True, True)
        return rows, _gla_tile(q_ref[rows], k_ref[rows], v_ref[rows], lr_ref[rows],
                               wgb, bgb, sb_ref, False, True)

    def finish(rows, o):
        o = o + o_scr[rows]
        ms = jnp.mean(o * o, axis=-1, keepdims=True)
        on = o * lax.rsqrt(ms + RMS_EPS) * ng_ref[...]
        out_ref[rows] = (on * _silu(go_ref[rows])).astype(BF16)

    def keep(rows, o):
        o_scr[rows] = o

    u = math.gcd(GLA_TILES_PER_ITER, half)

    def make_step(sink):
        def step(it, carry):
            for w in range(u):
                i = it * u + w
                sink(*tile_out(i, True))
                sink(*tile_out(n_x - 1 - i, False))
            return carry
        return step

    lax.fori_loop(0, half // u, make_step(keep), 0)
    lax.fori_loop(half // u, n_x // u, make_step(finish), 0)


def _gla(p3, lr3, pc3, lrc3, wgf, bgf, wgb, bgb, ng):
    bsz, l, _ = p3.shape
    lc = pc3.shape[1]
    dk, dv = GLA_DK, GLA_DV
    kw = GLA_HEADS * dk
    q0, k0 = 0, kw // dk
    v0, go0 = 2 * kw // dv, (2 * kw + GLA_HEADS * dv) // dv
    kc0, vc0 = k0, v0
    return pl.pallas_call(
        _gla_body,
        grid=(bsz, GLA_HEADS),
        in_specs=[pl.BlockSpec((None, l, dk), lambda b, h: (b, 0, q0 + h)),
                  pl.BlockSpec((None, l, dk), lambda b, h: (b, 0, k0 + h)),
                  pl.BlockSpec((None, l, dv), lambda b, h: (b, 0, v0 + h)),
                  pl.BlockSpec((None, l, dv), lambda b, h: (b, 0, go0 + h)),
                  pl.BlockSpec((None, l, LANES), lambda b, h: (b, 0, 0)),
                  pl.BlockSpec((None, lc, dk), lambda b, h: (b, 0, kc0 + h)),
                  pl.BlockSpec((None, lc, dv), lambda b, h: (b, 0, vc0 + h)),
                  pl.BlockSpec((None, lc, LANES), lambda b, h: (b, 0, 0)),
                  pl.BlockSpec((LANES, dk), lambda b, h: (0, h)),
                  pl.BlockSpec((1, dk), lambda b, h: (0, h)),
                  pl.BlockSpec((LANES, dk), lambda b, h: (0, h)),
                  pl.BlockSpec((1, dk), lambda b, h: (0, h)),
                  pl.BlockSpec((1, dv), lambda b, h: (0, 0))],
        out_specs=pl.BlockSpec((None, l, dv), lambda b, h: (b, 0, h)),
        out_shape=jax.ShapeDtypeStruct((bsz, l, GLA_HEADS * dv), BF16),
        scratch_shapes=[pltpu.VMEM((l, dv), F32),
                        pltpu.VMEM((dv, dk), F32),
                        pltpu.VMEM((dv, dk), F32)],
        compiler_params=_cparams(2),
        name="gla",
    )(p3, p3, p3, p3, lr3, pc3, pc3, lrc3, wgf, bgf, wgb, bgb, ng)


def _post_body(cm_ref, gla_ref, x_ref, lng_ref, lnb_ref, g1_ref, sh2_ref, sc2_ref,
               wo_ref, l1g_ref, l1b_ref, wr_ref, br_ref, h1_ref, hp_ref, lg_ref):
    half = cm_ref.shape[1]
    d2 = x_ref.shape[1] // 2
    for r in range(0, x_ref.shape[0], ROW_CHUNK):
        rows = slice(r, r + ROW_CHUNK)
        y = _dot(cm_ref[rows, :], wo_ref[:half, :]) + _dot(gla_ref[rows, :], wo_ref[half:, :])
        hx = _layer_norm(x_ref[rows, :], lng_ref[...], lnb_ref[...])
        h1 = _layer_norm(DEEPNORM_ALPHA * hx + g1_ref[...] * y, l1g_ref[...], l1b_ref[...])
        h1_ref[rows, :] = h1
        hm = h1 * (1.0 + sc2_ref[...]) + sh2_ref[...]
        hp_ref[rows, :] = _pack_bf16_pair(hm[:, :d2], hm[:, d2:])
        lg_ref[rows, :] = _dot(hm.astype(BF16), wr_ref[...]) + br_ref[...]


def _post_attn(cm, gla, x2, mod3, rows_per_batch, ln_g, ln_b, w_out, l1g, l1b, w_r, b_r, tm):
    r, d = x2.shape
    half = d // 2
    row = lambda i: (i * tm) // rows_per_batch
    full = lambda shape: pl.BlockSpec(shape, lambda i: (0,) * len(shape))
    return pl.pallas_call(
        _post_body,
        grid=(r // tm,),
        in_specs=[pl.BlockSpec((tm, half), lambda i: (i, 0)),
                  pl.BlockSpec((tm, half), lambda i: (i, 0)),
                  pl.BlockSpec((tm, d), lambda i: (i, 0)),
                  full((1, d)), full((1, d)),
                  pl.BlockSpec((None, 1, d), lambda i: (row(i), 0, 2)),
                  pl.BlockSpec((None, 1, d), lambda i: (row(i), 0, 3)),
                  pl.BlockSpec((None, 1, d), lambda i: (row(i), 0, 4)),
                  full((d, d)), full((1, d)), full((1, d)),
                  full((d, LANES)), full((1, LANES))],
        out_specs=[pl.BlockSpec((tm, d), lambda i: (i, 0)),
                   pl.BlockSpec((tm, half), lambda i: (i, 0)),
                   pl.BlockSpec((tm, LANES), lambda i: (i, 0))],
        out_shape=[jax.ShapeDtypeStruct((r, d), F32),
                   jax.ShapeDtypeStruct((r, half), U32),
                   jax.ShapeDtypeStruct((r, LANES), F32)],
        compiler_params=_cparams(1),
        name="post_attn",
    )(cm, gla, x2, ln_g, ln_b, mod3, mod3, mod3, w_out, l1g, l1b, w_r, b_r)


def _route_body(lg_ref, idx_ref, wt_ref, rank_ref, cnt_ref, carry_ref):
    i = pl.program_id(0)
    t = lg_ref.shape[0]

    @pl.when(i == 0)
    def _():
        carry_ref[...] = jnp.zeros_like(carry_ref)

    lane = lax.broadcasted_iota(I32, (t, LANES), 1)
    lane_f = lane.astype(F32)
    neg = jnp.float32(-jnp.inf)
    l = jnp.where(lane < N_EXPERTS, lg_ref[...], neg)
    tops, onehots, idxs = [], [], []
    for _ in range(TOP_K):
        m = jnp.max(l, axis=-1, keepdims=True)
        idx = jnp.min(jnp.where(l == m, lane_f, float(LANES)), axis=-1, keepdims=True).astype(I32)
        oh = lane == idx
        l = jnp.where(oh, neg, l)
        tops.append(m)
        idxs.append(idx)
        onehots.append(oh)
    exps = [jnp.exp(m - tops[0]) for m in tops]
    denom = exps[0] + exps[1] + exps[2] + exps[3]
    sel = _ones_where(onehots[0] | onehots[1] | onehots[2] | onehots[3], F32)
    r_id = lax.broadcasted_iota(I32, (t, t), 0)
    c_id = lax.broadcasted_iota(I32, (t, t), 1)
    strict = _ones_where(c_id < r_id, BF16)
    before = _dot(strict, sel.astype(BF16)) + carry_ref[0:1, :]
    idx_out = jnp.zeros((t, LANES), I32)
    wt_out = jnp.zeros((t, LANES), F32)
    rank_out = jnp.zeros((t, LANES), I32)
    for k in range(TOP_K):
        rk = jnp.sum(jnp.where(onehots[k], before, 0.0), axis=-1, keepdims=True).astype(I32)
        idx_out = jnp.where(lane == k, idxs[k], idx_out)
        wt_out = jnp.where(lane == k, exps[k] / denom, wt_out)
        rank_out = jnp.where(lane == k, rk, rank_out)
    idx_ref[...] = jnp.transpose(idx_out)[:SUBLANES, :]
    wt_ref[...] = wt_out
    rank_ref[...] = jnp.transpose(rank_out)[:SUBLANES, :]
    total = carry_ref[0:1, :] + jnp.sum(sel, axis=0, keepdims=True)
    carry_ref[...] = jnp.broadcast_to(total, carry_ref.shape)
    cnt_ref[...] = jnp.broadcast_to(total, cnt_ref.shape)


def _route(logits, tm):
    n = logits.shape[0]
    blk = pl.BlockSpec((tm, LANES), lambda i: (i, 0))
    small = pl.BlockSpec((SUBLANES, tm), lambda i: (0, i))
    return pl.pallas_call(
        _route_body,
        grid=(n // tm,),
        in_specs=[blk],
        out_specs=[small, blk, small, pl.BlockSpec((8, LANES), lambda i: (0, 0))],
        out_shape=[jax.ShapeDtypeStruct((SUBLANES, n), I32),
                   jax.ShapeDtypeStruct((n, LANES), F32),
                   jax.ShapeDtypeStruct((SUBLANES, n), I32),
                   jax.ShapeDtypeStruct((8, LANES), F32)],
        scratch_shapes=[pltpu.VMEM((8, LANES), F32)],
        compiler_params=_cparams(1),
        name="route",
    )(logits)


def _dispatch_body(nz_ref, zl_ref, dest_ref, hp_ref, xs_ref, zero_ref, sem, zsem):
    t = hp_ref.shape[0] * hp_ref.shape[1]

    @pl.when(pl.program_id(0) == 0)
    def _():
        zero_ref[...] = jnp.zeros_like(zero_ref)

        def z_copy(b):
            r = pl.multiple_of(zl_ref[b] * MOE_BLOCK, MOE_BLOCK)
            return pltpu.make_async_copy(zero_ref, xs_ref.at[pl.ds(r, MOE_BLOCK)], zsem)

        def start(b, carry):
            z_copy(b).start()
            return carry

        def wait(b, carry):
            z_copy(b).wait()
            return carry

        lax.fori_loop(0, nz_ref[0], start, 0)
        lax.fori_loop(0, nz_ref[0], wait, 0)

    def issue(g, carry):
        for u in range(SUBLANES):
            for k in range(TOP_K):
                d = dest_ref[0, k * t + g * SUBLANES + u]
                pltpu.make_async_copy(hp_ref.at[g, pl.ds(u, 1)], xs_ref.at[pl.ds(d, 1)], sem).start(priority=k % 2)
        return carry

    lax.fori_loop(0, t // SUBLANES, issue, 0)
    for _ in range(TOP_K):
        pltpu.make_async_copy(xs_ref.at[pl.ds(0, t)], xs_ref.at[pl.ds(0, t)], sem).wait()


def _tile_major(slot, tm):
    n = slot.shape[1]
    return slot.reshape(TOP_K, n // tm, tm).transpose(1, 0, 2).reshape(n // tm, 1, TOP_K * tm)


def _dispatch(n_zero, zero_blocks, dest, hp, n_slots, tm):
    n, w = hp.shape
    dest3 = _tile_major(dest, tm)
    return pl.pallas_call(
        _dispatch_body,
        grid_spec=pltpu.PrefetchScalarGridSpec(
            num_scalar_prefetch=2,
            grid=(n // tm,),
            in_specs=[pl.BlockSpec((None, 1, tm * TOP_K), lambda i, nz, zl: (i, 0, 0), memory_space=pltpu.SMEM),
                      pl.BlockSpec((tm // SUBLANES, SUBLANES, w), lambda i, nz, zl: (i, 0, 0))],
            out_specs=pl.BlockSpec(memory_space=pl.ANY),
            scratch_shapes=[pltpu.VMEM((MOE_BLOCK, w), U32),
                            pltpu.SemaphoreType.DMA(()),
                            pltpu.SemaphoreType.DMA(())]),
        out_shape=jax.ShapeDtypeStruct((n_slots, w), U32),
        compiler_params=_cparams(1),
        name="dispatch",
    )(n_zero, zero_blocks, dest3, hp.reshape(n // SUBLANES, SUBLANES, w))


PASS_SUBS = 9
EXPERT_TN = 256
DOT_ROWS = 1024
W_SLOTS = 3
W_AHEAD = 2


def _expert_body(meta_ref, pe_ref, pr_ref, pn_ref, xs_hbm, wgu_hbm, wdn_hbm, bgu_ref, bdn_ref,
                 y_hbm, xraw, xb, h_ref, wg_buf, wu_buf, wd_buf, ybuf, zbuf,
                 sem_x, sem_y, sem_z, sem_w, sem_d):
    sub = MOE_BLOCK
    tn = EXPERT_TN
    tp = tn // 2
    nj = h_ref.shape[0]
    nc = y_hbm.shape[1] // tp
    de = nj * tn
    d2 = xraw.shape[1]
    p = pl.program_id(0)
    n_pass = meta_ref[0]
    nsub = pn_ref[p]
    row0 = pr_ref[p]

    def x_copy(i, pp):
        r = pl.multiple_of((pr_ref[pp] + i) * sub, sub)
        return pltpu.make_async_copy(xs_hbm.at[pl.ds(r, sub)], xraw.at[pl.ds(i * sub, sub)], sem_x)

    def y_copy(start, size, r0, c, slot):
        r = pl.multiple_of(r0 * sub + start, sub)
        col = pl.multiple_of(c * tp, tp)
        return pltpu.make_async_copy(ybuf.at[slot, pl.ds(start, size), :],
                                     y_hbm.at[pl.ds(r, size), pl.ds(col, tp)], sem_y.at[slot])

    def z_copy(b, c):
        r = pl.multiple_of(b * sub, sub)
        return pltpu.make_async_copy(zbuf, y_hbm.at[pl.ds(r, sub), pl.ds(c * tp, tp)], sem_z)

    def w1_copies(pp, j):
        e = pe_ref[pp]
        slot = j % W_SLOTS
        col = pl.multiple_of(j * tn, tn)
        return (pltpu.make_async_copy(wgu_hbm.at[e, :, pl.ds(col, tn)], wg_buf.at[slot], sem_w.at[slot]),
                pltpu.make_async_copy(wgu_hbm.at[e, :, pl.ds(de + col, tn)], wu_buf.at[slot], sem_w.at[slot]))

    def w2_copy(pp, c):
        slot = c % W_SLOTS
        col = pl.multiple_of(c * tn, tn)
        return pltpu.make_async_copy(wdn_hbm.at[pe_ref[pp], :, pl.ds(col, tn)], wd_buf.at[slot], sem_d.at[slot])

    def start_w1(pp, j):
        for cp in w1_copies(pp, j):
            cp.start()

    def for_subs(count, fn):
        for i in range(PASS_SUBS):
            pl.when(i < count)(lambda i=i: fn(i))

    def for_tail(fn):
        def body(b, carry):
            for c in range(nc):
                fn(b, c)
            return carry
        lax.fori_loop(meta_ref[1], y_hbm.shape[0] // sub, body, 0)

    def for_groups(count, fn):
        pl.when(count == PASS_SUBS)(lambda: fn(0, PASS_SUBS * sub))
        k = 1 << (PASS_SUBS.bit_length() - 1)
        while k:
            start = pl.multiple_of((count & (-2 * k)) * sub, sub)
            pl.when((count != PASS_SUBS) & ((count & k) != 0))(lambda start=start, k=k: fn(start, k * sub))
            k //= 2

    @pl.when(p == 0)
    def _():
        for_subs(nsub, lambda i: x_copy(i, p).start())
        for j in range(W_AHEAD):
            start_w1(p, j)
        zbuf[...] = jnp.zeros_like(zbuf)
        for_tail(lambda b, c: z_copy(b, c).start())

    for_subs(nsub, lambda i: x_copy(i, p).wait())

    def unpack(i):
        rows = slice(i * sub, (i + 1) * sub)
        lo, hi = _unpack_pair_f32(xraw[rows, :])
        xb[rows, :d2] = lo.astype(BF16)
        xb[rows, d2:] = hi.astype(BF16)

    for_subs(nsub, unpack)

    @pl.when(p + 1 < n_pass)
    def _():
        for_subs(pn_ref[p + 1], lambda i: x_copy(i, p + 1).start())

    def first_step(j, carry):
        slot = j % W_SLOTS
        for cp in w1_copies(p, j):
            cp.wait()
        nxt = j + W_AHEAD
        pl.when(nxt < nj)(lambda: start_w1(p, nxt))
        pl.when(nxt >= nj)(lambda: w2_copy(p, nxt - nj).start())

        def group(start, size):
            wg = wg_buf[slot].astype(BF16)
            wu = wu_buf[slot].astype(BF16)
            dr = DOT_ROWS if size % DOT_ROWS == 0 else size
            for r in range(0, size, dr):
                rows = pl.ds(start + r, dr)
                x = xb[rows, :]
                gate = jnp.minimum(_dot(x, wg) + bgu_ref[j], SWIGLU_LIMIT)
                up = jnp.clip(_dot(x, wu) + bgu_ref[nj + j], -SWIGLU_LIMIT, SWIGLU_LIMIT)
                h_ref[j, rows, :] = ((up + 1.0) * gate * jax.nn.sigmoid(SWIGLU_ALPHA * gate)).astype(BF16)

        for_groups(nsub, group)
        return carry

    lax.fori_loop(0, nj, first_step, 0)

    def second_step(c, carry):
        wslot = c % W_SLOTS
        slot = c % 2
        w2_copy(p, c).wait()
        nxt = c + W_AHEAD
        pl.when(nxt < nc)(lambda: w2_copy(p, nxt).start())
        pl.when((nxt >= nc) & (p + 1 < n_pass))(lambda: start_w1(p + 1, nxt - nc))

        @pl.when(c >= 2)
        def _():
            for_groups(nsub, lambda start, size: y_copy(start, size, row0, c - 2, slot).wait())

        @pl.when((c < 2) & (p > 0))
        def _():
            for_groups(pn_ref[p - 1],
                       lambda start, size: y_copy(start, size, pr_ref[p - 1], nc - 2 + c, slot).wait())

        def group(start, size):
            wd = wd_buf[wslot].astype(BF16)
            dr = DOT_ROWS if size % DOT_ROWS == 0 else size
            for r in range(0, size, dr):
                rows = pl.ds(start + r, dr)
                hx = jnp.concatenate([h_ref[j, rows, :] for j in range(nj)], axis=1)
                y = _dot(hx, wd) + bdn_ref[c]
                ybuf[slot, rows, :] = _pack_bf16_pair(y[:, :tp], y[:, tp:])
            y_copy(start, size, row0, c, slot).start()

        for_groups(nsub, group)
        return carry

    lax.fori_loop(0, nc, second_step, 0)

    @pl.when(p == n_pass - 1)
    def _():
        for c in (nc - 2, nc - 1):
            for_groups(nsub, lambda start, size, c=c: y_copy(start, size, row0, c, c % 2).wait())
        for_tail(lambda b, cc: z_copy(b, cc).wait())


def _experts(meta, pass_e, pass_row0, pass_nsub, xs, w_gu, b_gu, w_dn, b_dn):
    n_slots, d2 = xs.shape
    d = 2 * d2
    n_e, de = w_dn.shape[0], w_dn.shape[1]
    tn = EXPERT_TN
    nj = de // tn
    nc = d // tn
    assert W_AHEAD < W_SLOTS and W_AHEAD <= min(nj, nc) and nc % 2 == 0
    rmax = PASS_SUBS * MOE_BLOCK
    return pl.pallas_call(
        _expert_body,
        grid_spec=pltpu.PrefetchScalarGridSpec(
            num_scalar_prefetch=4,
            grid=(meta[0],),
            in_specs=[
                pl.BlockSpec(memory_space=pl.ANY),
                pl.BlockSpec(memory_space=pl.ANY),
                pl.BlockSpec(memory_space=pl.ANY),
                pl.BlockSpec((None, 2 * nj, 1, tn), lambda p, m, e, r, n: (e[p], 0, 0, 0)),
                pl.BlockSpec((None, nc, 1, tn), lambda p, m, e, r, n: (e[p], 0, 0, 0)),
            ],
            out_specs=pl.BlockSpec(memory_space=pl.ANY),
            scratch_shapes=[pltpu.VMEM((rmax, d2), U32),
                            pltpu.VMEM((rmax, d), BF16),
                            pltpu.VMEM((nj, rmax, tn), BF16),
                            pltpu.VMEM((W_SLOTS, d, tn), w_gu.dtype),
                            pltpu.VMEM((W_SLOTS, d, tn), w_gu.dtype),
                            pltpu.VMEM((W_SLOTS, de, tn), w_dn.dtype),
                            pltpu.VMEM((2, rmax, tn // 2), U32),
                            pltpu.VMEM((MOE_BLOCK, tn // 2), U32),
                            pltpu.SemaphoreType.DMA(()),
                            pltpu.SemaphoreType.DMA((2,)),
                            pltpu.SemaphoreType.DMA(()),
                            pltpu.SemaphoreType.DMA((W_SLOTS,)),
                            pltpu.SemaphoreType.DMA((W_SLOTS,))]),
        out_shape=jax.ShapeDtypeStruct((n_slots, d2), U32),
        compiler_params=_cparams(1),
        name="experts",
    )(meta, pass_e, pass_row0, pass_nsub, xs, w_gu, w_dn,
      b_gu.reshape(n_e, 2 * nj, 1, tn), b_dn.reshape(n_e, nc, 1, tn))


COMBINE_CHUNK = 64


def _combine_body(dcur_ref, dnxt_ref, wt_ref, h1_ref, g2_ref, l2g_ref, l2b_ref, y_ref, out_ref, buf_ref, sem):
    i = pl.program_id(0)
    n_tiles = pl.num_programs(0)
    t = h1_ref.shape[0]
    tp = EXPERT_TN // 2
    slot = i % 2

    def gather(dest_ref, sl):
        def body(g, carry):
            for u in range(SUBLANES):
                for k in range(TOP_K):
                    d = dest_ref[0, k * t + g * SUBLANES + u]
                    pltpu.make_async_copy(y_ref.at[pl.ds(d, 1)], buf_ref.at[sl, k, g, pl.ds(u, 1)],
                                          sem.at[sl]).start(priority=k % 2)
            return carry
        lax.fori_loop(0, t // SUBLANES, body, 0)

    pl.when(i == 0)(lambda: gather(dcur_ref, 0))
    pl.when(i + 1 < n_tiles)(lambda: gather(dnxt_ref, 1 - slot))
    for k in range(TOP_K):
        pltpu.make_async_copy(buf_ref.at[1 - slot, k], buf_ref.at[slot, k], sem.at[slot]).wait()
    for r in range(0, t, COMBINE_CHUNK):
        rows = slice(r, r + COMBINE_CHUNK)
        groups = slice(r // SUBLANES, (r + COMBINE_CHUNK) // SUBLANES)
        wt = wt_ref[rows, :]
        f_lo, f_hi = None, None
        for k in range(TOP_K):
            lo, hi = _unpack_pair_f32(buf_ref[slot, k, groups].reshape(COMBINE_CHUNK, buf_ref.shape[-1]))
            w = wt[:, k:k + 1]
            f_lo = lo * w if f_lo is None else f_lo + lo * w
            f_hi = hi * w if f_hi is None else f_hi + hi * w
        parts = []
        for c in range(f_lo.shape[1] // tp):
            parts += [f_lo[:, c * tp:(c + 1) * tp], f_hi[:, c * tp:(c + 1) * tp]]
        f = jnp.concatenate(parts, axis=1)
        out_ref[rows, :] = _layer_norm(DEEPNORM_ALPHA * h1_ref[rows, :] + g2_ref[...] * f,
                                       l2g_ref[...], l2b_ref[...])


def _combine(dest, wt, h1, mod3, rows_per_batch, l2g, l2b, y, tm):
    n, d = h1.shape
    n_tiles = n // tm
    dest3 = _tile_major(dest, tm)
    row = lambda i: (i * tm) // rows_per_batch
    return pl.pallas_call(
        _combine_body,
        grid=(n_tiles,),
        in_specs=[pl.BlockSpec((None, 1, tm * TOP_K), lambda i: (i, 0, 0), memory_space=pltpu.SMEM),
                  pl.BlockSpec((None, 1, tm * TOP_K), lambda i: (jnp.minimum(i + 1, n_tiles - 1), 0, 0),
                               memory_space=pltpu.SMEM),
                  pl.BlockSpec((tm, LANES), lambda i: (i, 0)),
                  pl.BlockSpec((tm, d), lambda i: (i, 0)),
                  pl.BlockSpec((None, 1, d), lambda i: (row(i), 0, 5)),
                  pl.BlockSpec((1, d), lambda i: (0, 0)),
                  pl.BlockSpec((1, d), lambda i: (0, 0)),
                  pl.BlockSpec(memory_space=pl.ANY)],
        out_specs=pl.BlockSpec((tm, d), lambda i: (i, 0)),
        out_shape=jax.ShapeDtypeStruct((n, d), F32),
        scratch_shapes=[pltpu.VMEM((2, TOP_K, tm // SUBLANES, SUBLANES, y.shape[1]), U32),
                        pltpu.SemaphoreType.DMA((2,))],
        compiler_params=_cparams(1),
        name="combine",
    )(dest3, dest3, wt, h1, mod3, l2g, l2b, y)


def _pick_tile(n, pref):
    t = pref
    while n % t:
        t //= 2
    return t


def kernel(x, c, ctx, c_ctx, ln_in_g, ln_in_b, w_ada, b_ada, w_in, cm_norm_g, cm_norm_b, cm_w_s, cm_b_s,
           gla_w_gk_f, gla_b_gk_f, gla_w_gk_b, gla_b_gk_b, gla_norm_g, w_out, ln1_g, ln1_b,
           w_router, b_router, w_gate_up, b_gate_up, w_down, b_down, ln2_g, ln2_b):
    bsz, l, d = x.shape
    lc = ctx.shape[1]
    n, nc = bsz * l, bsz * lc
    assert w_ada.shape[0] == 1, "single-layer configuration"
    assert bsz + 1 <= 8 and l % (2 * GLA_TILE) == 0 and lc % GLA_TILE == 0
    row = lambda v: v.reshape(1, -1)

    cc = jnp.concatenate([c, c_ctx[None, :], jnp.zeros((8 - bsz - 1, d), F32)], axis=0)
    mod3 = _ada(cc, w_ada[0], row(b_ada[0])).reshape(8, 1, N_MOD * d)

    n_uv = 2 * CM_HEADS * CM_CHUNK
    n_main = n_uv + 2 * GLA_HEADS * GLA_DK + 2 * GLA_HEADS * GLA_DV
    w_uv = w_in[0][:, :n_uv].astype(BF16)
    w_rest = w_in[0][:, n_uv:n_main].astype(BF16)
    w_lr = jnp.pad(w_in[0][:, n_main:].astype(BF16), ((0, 0), (0, LANES - 2 * GLA_RANK)))
    x2 = x.reshape(n, d)
    tm_x = _pick_tile(l, 512)
    bs_tile = jnp.repeat(cm_b_s[0].T, CM_CHUNK, axis=1)
    cm, p, lr = _inproj_cm(x2, mod3, l, row(ln_in_g), row(ln_in_b), w_uv, w_rest, w_lr,
                           row(cm_norm_g[0]), row(cm_norm_b[0]), cm_w_s[0].astype(BF16), bs_tile, tm_x)
    tm_c = _pick_tile(nc, 512)
    pc, lrc = _inproj(ctx.reshape(nc, d), mod3, lambda i: bsz, row(ln_in_g), row(ln_in_b),
                      w_rest, w_lr, 0, 2, tm_c, 1024)

    kw = GLA_HEADS * GLA_DK
    wgf = jnp.zeros((LANES, kw), BF16).at[:GLA_RANK].set(gla_w_gk_f[0].astype(BF16))
    wgb = jnp.zeros((LANES, kw), BF16).at[GLA_RANK:2 * GLA_RANK].set(gla_w_gk_b[0].astype(BF16))
    gla = _gla(p.reshape(bsz, l, -1), lr.reshape(bsz, l, LANES), pc.reshape(bsz, lc, -1),
               lrc.reshape(bsz, lc, LANES), wgf, row(gla_b_gk_f[0]), wgb, row(gla_b_gk_b[0]),
               row(gla_norm_g[0])).reshape(n, -1)

    w_r = jnp.pad(w_router[0], ((0, 0), (0, LANES - N_EXPERTS))).astype(BF16)
    b_r = jnp.pad(b_router[0], (0, LANES - N_EXPERTS)).reshape(1, LANES)
    h1, hp, logits = _post_attn(cm, gla, x2, mod3, l, row(ln_in_g), row(ln_in_b), w_out[0].astype(BF16),
                                row(ln1_g[0]), row(ln1_b[0]), w_r, b_r, tm_x)

    idx, wt, rank, cnt = _route(logits, _pick_tile(n, 1024))
    counts = cnt[0, :N_EXPERTS].astype(I32)
    n_blocks = (n * TOP_K + N_EXPERTS * (MOE_BLOCK - 1)) // MOE_BLOCK
    blocks_e = (counts + MOE_BLOCK - 1) // MOE_BLOCK
    blk_end = jnp.cumsum(blocks_e)
    blk_start = blk_end - blocks_e
    dest = (blk_start * MOE_BLOCK)[idx[:TOP_K]] + rank[:TOP_K]
    n_pass_max = n_blocks // PASS_SUBS + N_EXPERTS
    pass_cnt = (blocks_e + PASS_SUBS - 1) // PASS_SUBS
    pass_end = jnp.cumsum(pass_cnt)
    pass_start = pass_end - pass_cnt
    pid = jnp.arange(n_pass_max, dtype=I32)
    pass_e = jnp.minimum(jnp.searchsorted(pass_end, pid, side="right"), N_EXPERTS - 1).astype(I32)
    local = pid - pass_start[pass_e]
    pass_row0 = (blk_start[pass_e] + local * PASS_SUBS).astype(I32)
    pass_nsub = jnp.clip(blocks_e[pass_e] - local * PASS_SUBS, 0, PASS_SUBS).astype(I32)
    meta = jnp.stack([pass_end[-1], blk_end[-1]]).astype(I32)

    ar = jnp.arange(N_EXPERTS, dtype=I32)
    cand = jnp.concatenate([jnp.where(counts % MOE_BLOCK != 0, blk_end - 1, -1),
                            jnp.where(blk_end[-1] + ar < n_blocks, blk_end[-1] + ar, -1)])
    zero_blocks = cand[jnp.argsort(cand < 0, stable=True)].astype(I32)
    n_zero = jnp.sum(cand >= 0).astype(I32).reshape(1)

    xs = _dispatch(n_zero, zero_blocks, dest, hp, n_blocks * MOE_BLOCK, _pick_tile(n, 512))
    ys = _experts(meta, pass_e, pass_row0, pass_nsub, xs, w_gate_up[0], b_gate_up[0], w_down[0], b_down[0])
    out = _combine(dest, wt, h1, mod3, l, row(ln2_g[0]), row(ln2_b[0]), ys, _pick_tile(n, 512))
    return out.reshape(bsz, l, d)
```

```python
import math

import jax
import jax.numpy as jnp
from jax import lax
from jax.experimental import pallas as pl
from jax.experimental.pallas import tpu as pltpu

F32 = jnp.float32
BF16 = jnp.bfloat16
U32 = jnp.uint32
I32 = jnp.int32

CM_CHUNK = 128
CM_HEADS = 8
GLA_HEADS = 4
GLA_DK = 128
GLA_DV = 256
GLA_CHUNK = 64
GLA_RANK = 16
GLA_GATE_NORMALIZER = 16.0
N_EXPERTS = 32
TOP_K = 4
MOE_BLOCK = 256
SWIGLU_LIMIT = 7.0
SWIGLU_ALPHA = 1.702
N_MOD = 6
DEEPNORM_ALPHA = 2.0 ** 0.25
LN_EPS = 1e-5
RMS_EPS = 1e-6

LANES = 128
SUBLANES = 8
VMEM_LIMIT = 56 * 1024 * 1024
ROW_CHUNK = 256


def _cparams(n_axes, vmem=VMEM_LIMIT):
    return pltpu.CompilerParams(dimension_semantics=("arbitrary",) * n_axes,
                                vmem_limit_bytes=vmem)


def _layer_norm(t, g, b):
    mu = jnp.mean(t, axis=-1, keepdims=True)
    d = t - mu
    var = jnp.mean(d * d, axis=-1, keepdims=True)
    return d * lax.rsqrt(var + LN_EPS) * g + b


def _gelu(t):
    return 0.5 * t * (1.0 + lax.erf(t * (1.0 / math.sqrt(2.0))))


def _silu(t):
    return t * jax.nn.sigmoid(t)


def _ones_where(mask, dtype):
    return jnp.where(mask, 1.0, 0.0).astype(dtype)


def _dot(a, b):
    return jnp.dot(a, b, preferred_element_type=F32)


def _dot_nt(a, b):
    return lax.dot_general(a, b, (((1,), (1,)), ((), ())), preferred_element_type=F32)


def _dot_tn(a, b):
    return lax.dot_general(a, b, (((0,), (0,)), ((), ())), preferred_element_type=F32)


def _pack_bf16_pair(lo, hi):
    lo_b = lax.bitcast_convert_type(lo.astype(BF16).astype(F32), U32)
    hi_b = lax.bitcast_convert_type(hi.astype(BF16).astype(F32), U32)
    return hi_b | (lo_b >> 16)


def _unpack_pair_f32(p):
    lo = lax.bitcast_convert_type(p << 16, F32)
    hi = lax.bitcast_convert_type(p & jnp.uint32(0xFFFF0000), F32)
    return lo, hi


def _ada_body(c_ref, w_ref, b_ref, o_ref):
    a = _silu(c_ref[...]).astype(BF16)
    o_ref[...] = _dot(a, w_ref[...].astype(BF16)) + b_ref[...]


def _ada(cc, w, b):
    rows, d = cc.shape
    n = w.shape[1]
    tn = 1024
    return pl.pallas_call(
        _ada_body,
        grid=(n // tn,),
        in_specs=[pl.BlockSpec((rows, d), lambda j: (0, 0)),
                  pl.BlockSpec((d, tn), lambda j: (0, j)),
                  pl.BlockSpec((1, tn), lambda j: (0, j))],
        out_specs=pl.BlockSpec((rows, tn), lambda j: (0, j)),
        out_shape=jax.ShapeDtypeStruct((rows, n), F32),
        compiler_params=_cparams(1),
        name="ada",
    )(cc, w, b)


def _inproj_body(x_ref, g_ref, b_ref, sh_ref, sc_ref, w_ref, wlr_ref, o_ref, olr_ref, hm_ref):
    @pl.when(pl.program_id(1) == 0)
    def _():
        h = _layer_norm(x_ref[...], g_ref[...], b_ref[...])
        hm = (h * (1.0 + sc_ref[...]) + sh_ref[...]).astype(BF16)
        hm_ref[...] = hm
        olr_ref[...] = _dot(hm, wlr_ref[...])

    o_ref[...] = _dot(hm_ref[...], w_ref[...])


def _inproj(x2, mod3, mod_row, ln_g, ln_b, w_main, w_lr, col0, ncols, tm, tn):
    r, d = x2.shape
    return pl.pallas_call(
        _inproj_body,
        grid=(r // tm, ncols),
        in_specs=[pl.BlockSpec((tm, d), lambda i, j: (i, 0)),
                  pl.BlockSpec((1, d), lambda i, j: (0, 0)),
                  pl.BlockSpec((1, d), lambda i, j: (0, 0)),
                  pl.BlockSpec((None, 1, d), lambda i, j: (mod_row(i), 0, 0)),
                  pl.BlockSpec((None, 1, d), lambda i, j: (mod_row(i), 0, 1)),
                  pl.BlockSpec((d, tn), lambda i, j: (0, col0 + j)),
                  pl.BlockSpec((d, LANES), lambda i, j: (0, 0))],
        out_specs=[pl.BlockSpec((tm, tn), lambda i, j: (i, j)),
                   pl.BlockSpec((tm, LANES), lambda i, j: (i, 0))],
        out_shape=[jax.ShapeDtypeStruct((r, ncols * tn), F32),
                   jax.ShapeDtypeStruct((r, LANES), F32)],
        scratch_shapes=[pltpu.VMEM((tm, d), BF16)],
        compiler_params=_cparams(2),
        name="inproj",
    )(x2, ln_g, ln_b, mod3, mod3, w_main, w_lr)


def _inproj_cm_body(x_ref, g_ref, b_ref, sh_ref, sc_ref, wuv_ref, wr_ref, wlr_ref, ng_ref, nb_ref, ws_ref, bs_ref,
                    cm_ref, p_ref, lr_ref):
    tm = x_ref.shape[0]
    half = wuv_ref.shape[1] // 2
    hd = CM_CHUNK
    for r in range(0, tm, ROW_CHUNK):
        rows = slice(r, r + ROW_CHUNK)
        h = _layer_norm(x_ref[rows, :], g_ref[...], b_ref[...])
        hm = (h * (1.0 + sc_ref[...]) + sh_ref[...]).astype(BF16)
        uv = _dot(hm, wuv_ref[...])
        u = _gelu(uv[:, :half])
        vb = _layer_norm(_gelu(uv[:, half:]), ng_ref[...], nb_ref[...]).astype(BF16)
        for c in range(ROW_CHUNK // CM_CHUNK):
            crow = slice(c * CM_CHUNK, (c + 1) * CM_CHUNK)
            orow = slice(r + c * CM_CHUNK, r + (c + 1) * CM_CHUNK)
            for hh in range(CM_HEADS):
                cols = slice(hh * hd, (hh + 1) * hd)
                s = _dot(ws_ref[hh], vb[crow, cols]) + bs_ref[:, cols]
                cm_ref[orow, cols] = (u[crow, cols] * s).astype(BF16)
        p_ref[rows, :] = _dot(hm, wr_ref[...])
        lr_ref[rows, :] = _dot(hm, wlr_ref[...])


def _inproj_cm(x2, mod3, rows_per_batch, ln_g, ln_b, w_uv, w_rest, w_lr, ng, nb, ws, bs, tm):
    r, d = x2.shape
    n_uv, n_rest = w_uv.shape[1], w_rest.shape[1]
    row = lambda i: (i * tm) // rows_per_batch

    def const(shape):
        return pl.BlockSpec(shape, lambda i: (0,) * len(shape), pipeline_mode=pl.Buffered(1))

    return pl.pallas_call(
        _inproj_cm_body,
        grid=(r // tm,),
        in_specs=[pl.BlockSpec((tm, d), lambda i: (i, 0)),
                  const((1, d)), const((1, d)),
                  pl.BlockSpec((None, 1, d), lambda i: (row(i), 0, 0)),
                  pl.BlockSpec((None, 1, d), lambda i: (row(i), 0, 1)),
                  const((d, n_uv)), const((d, n_rest)), const((d, LANES)),
                  const((1, n_uv // 2)), const((1, n_uv // 2)),
                  const((CM_HEADS, CM_CHUNK, CM_CHUNK)), const((CM_CHUNK, n_uv // 2))],
        out_specs=[pl.BlockSpec((tm, n_uv // 2), lambda i: (i, 0)),
                   pl.BlockSpec((tm, n_rest), lambda i: (i, 0)),
                   pl.BlockSpec((tm, LANES), lambda i: (i, 0))],
        out_shape=[jax.ShapeDtypeStruct((r, n_uv // 2), BF16),
                   jax.ShapeDtypeStruct((r, n_rest), F32),
                   jax.ShapeDtypeStruct((r, LANES), F32)],
        compiler_params=_cparams(1),
        name="inproj_cm",
    )(x2, ln_g, ln_b, mod3, mod3, w_uv, w_rest, w_lr, ng, nb, ws, bs)


GLA_TILE = 256
GLA_TILES_PER_ITER = 2


def _gla_tile(q, k, v, lr, wg, bg, st_ref, forward, need_o):
    t = k.shape[0]
    n_chunks = t // GLA_CHUNK
    z = _dot(lr.astype(BF16), wg) + bg
    g = jax.nn.log_sigmoid(z) * (1.0 / GLA_GATE_NORMALIZER)
    r_id = lax.broadcasted_iota(I32, (t, t), 0)
    c_id = lax.broadcasted_iota(I32, (t, t), 1)
    shift = GLA_CHUNK.bit_length() - 1
    same = (r_id >> shift) == (c_id >> shift)
    lower = same & (c_id <= r_id)
    tri = _ones_where(lower, BF16)
    g_hi = g.astype(BF16)
    g_lo = (g - g_hi.astype(F32)).astype(BF16)
    csum2 = _dot(tri, jnp.concatenate([g_hi, g_lo], axis=1))
    csum = csum2[:, :GLA_DK] + csum2[:, GLA_DK:]
    g3 = g.reshape(n_chunks, GLA_CHUNK, GLA_DK)
    tot = jnp.broadcast_to(jnp.sum(g3, axis=1, keepdims=True), g3.shape).reshape(t, GLA_DK)
    bcum = csum if forward else tot - csum + g
    kd = (k * jnp.exp(tot - bcum)).astype(BF16)
    decay = jnp.exp(tot)
    vb = v.astype(BF16)
    row_chunk = lax.broadcasted_iota(I32, (t, GLA_DK), 0) >> shift
    kd_blocks = jnp.concatenate([jnp.where(row_chunk == c, kd, jnp.zeros_like(kd)) for c in range(n_chunks)], axis=1)
    u_all = _dot_tn(vb, kd_blocks)
    o = None
    if need_o:
        qe = ((q * (GLA_DK ** -0.5)) * jnp.exp(bcum)).astype(BF16)
        ke = (k * jnp.exp(-bcum)).astype(BF16)
        att = _dot_nt(qe, ke)
        mask = lower if forward else same & (c_id >= r_id)
        att = jnp.where(mask, att, 0.0).astype(BF16)
        o = _dot(att, vb)
    outs = [None] * n_chunks
    order = range(n_chunks) if forward else range(n_chunks - 1, -1, -1)
    for c in order:
        rows = slice(c * GLA_CHUNK, (c + 1) * GLA_CHUNK)
        s_t = st_ref[...]
        if need_o:
            outs[c] = o[rows] + _dot_nt(qe[rows], s_t.astype(BF16))
        u_t = u_all[:, c * GLA_DK:(c + 1) * GLA_DK]
        st_ref[...] = s_t * decay[c * GLA_CHUNK:c * GLA_CHUNK + 1, :] + u_t
    if need_o:
        return jnp.concatenate(outs, axis=0)
    return None


def _gla_body(q_ref, k_ref, v_ref, go_ref, lr_ref, kc_ref, vc_ref, lrc_ref,
              wgf_ref, bgf_ref, wgb_ref, bgb_ref, ng_ref, out_ref, o_scr, sf_ref, sb_ref):
    t = GLA_TILE
    n_x = q_ref.shape[0] // t
    n_c = kc_ref.shape[0] // t
    half = n_x // 2
    sf_ref[...] = jnp.zeros_like(sf_ref)
    sb_ref[...] = jnp.zeros_like(sb_ref)
    wgf, bgf, wgb, bgb = wgf_ref[...], bgf_ref[...], wgb_ref[...], bgb_ref[...]

    for i in range(n_c):
        rf = slice(i * t, (i + 1) * t)
        rb = slice((n_c - 1 - i) * t, (n_c - i) * t)
        _gla_tile(None, kc_ref[rf], vc_ref[rf], lrc_ref[rf], wgf, bgf, sf_ref, True, False)
        _gla_tile(None, kc_ref[rb], vc_ref[rb], lrc_ref[rb], wgb, bgb, sb_ref, False, False)

    def tile_out(i, forward):
        rows = pl.ds(pl.multiple_of(i * t, t), t)
        if forward:
            return rows, _gla_tile(q_ref[rows], k_ref[rows], v_ref[rows], lr_ref[rows],
                                   wgf, bgf, sf_ref, True, True)
        return rows, _gla_tile(q_ref[rows], k_ref[rows], v_ref[rows], lr_ref[rows],
                               wgb, bgb, sb_ref, False, True)

    def finish(rows, o):
        o = o + o_scr[rows]
        ms = jnp.mean(o * o, axis=-1, keepdims=True)
        on = o * lax.rsqrt(ms + RMS_EPS) * ng_ref[...]
        out_ref[rows] = (on * _silu(go_ref[rows])).astype(BF16)

    def keep(rows, o):
        o_scr[rows] = o

    u = math.gcd(GLA_TILES_PER_ITER, half)

    def make_step(sink):
        def step(it, carry):
            for w in range(u):
                i = it * u + w
                sink(*tile_out(i, True))
                sink(*tile_out(n_x - 1 - i, False))
            return carry
        return step

    lax.fori_loop(0, half // u, make_step(keep), 0)
    lax.fori_loop(half // u, n_x // u, make_step(finish), 0)


def _gla(p3, lr3, pc3, lrc3, wgf, bgf, wgb, bgb, ng):
    bsz, l, _ = p3.shape
    lc = pc3.shape[1]
    dk, dv = GLA_DK, GLA_DV
    kw = GLA_HEADS * dk
    q0, k0 = 0, kw // dk
    v0, go0 = 2 * kw // dv, (2 * kw + GLA_HEADS * dv) // dv
    kc0, vc0 = k0, v0
    return pl.pallas_call(
        _gla_body,
        grid=(bsz, GLA_HEADS),
        in_specs=[pl.BlockSpec((None, l, dk), lambda b, h: (b, 0, q0 + h)),
                  pl.BlockSpec((None, l, dk), lambda b, h: (b, 0, k0 + h)),
                  pl.BlockSpec((None, l, dv), lambda b, h: (b, 0, v0 + h)),
                  pl.BlockSpec((None, l, dv), lambda b, h: (b, 0, go0 + h)),
                  pl.BlockSpec((None, l, LANES), lambda b, h: (b, 0, 0)),
                  pl.BlockSpec((None, lc, dk), lambda b, h: (b, 0, kc0 + h)),
                  pl.BlockSpec((None, lc, dv), lambda b, h: (b, 0, vc0 + h)),
                  pl.BlockSpec((None, lc, LANES), lambda b, h: (b, 0, 0)),
                  pl.BlockSpec((LANES, dk), lambda b, h: (0, h)),
                  pl.BlockSpec((1, dk), lambda b, h: (0, h)),
                  pl.BlockSpec((LANES, dk), lambda b, h: (0, h)),
                  pl.BlockSpec((1, dk), lambda b, h: (0, h)),
                  pl.BlockSpec((1, dv), lambda b, h: (0, 0))],
        out_specs=pl.BlockSpec((None, l, dv), lambda b, h: (b, 0, h)),
        out_shape=jax.ShapeDtypeStruct((bsz, l, GLA_HEADS * dv), BF16),
        scratch_shapes=[pltpu.VMEM((l, dv), F32),
                        pltpu.VMEM((dv, dk), F32),
                        pltpu.VMEM((dv, dk), F32)],
        compiler_params=_cparams(2),
        name="gla",
    )(p3, p3, p3, p3, lr3, pc3, pc3, lrc3, wgf, bgf, wgb, bgb, ng)


def _post_body(cm_ref, gla_ref, x_ref, lng_ref, lnb_ref, g1_ref, sh2_ref, sc2_ref,
               wo_ref, l1g_ref, l1b_ref, wr_ref, br_ref, h1_ref, hp_ref, lg_ref):
    half = cm_ref.shape[1]
    d2 = x_ref.shape[1] // 2
    for r in range(0, x_ref.shape[0], ROW_CHUNK):
        rows = slice(r, r + ROW_CHUNK)
        y = _dot(cm_ref[rows, :], wo_ref[:half, :]) + _dot(gla_ref[rows, :], wo_ref[half:, :])
        hx = _layer_norm(x_ref[rows, :], lng_ref[...], lnb_ref[...])
        h1 = _layer_norm(DEEPNORM_ALPHA * hx + g1_ref[...] * y, l1g_ref[...], l1b_ref[...])
        h1_ref[rows, :] = h1
        hm = h1 * (1.0 + sc2_ref[...]) + sh2_ref[...]
        hp_ref[rows, :] = _pack_bf16_pair(hm[:, :d2], hm[:, d2:])
        lg_ref[rows, :] = _dot(hm.astype(BF16), wr_ref[...]) + br_ref[...]


def _post_attn(cm, gla, x2, mod3, rows_per_batch, ln_g, ln_b, w_out, l1g, l1b, w_r, b_r, tm):
    r, d = x2.shape
    half = d // 2
    row = lambda i: (i * tm) // rows_per_batch
    full = lambda shape: pl.BlockSpec(shape, lambda i: (0,) * len(shape))
    return pl.pallas_call(
        _post_body,
        grid=(r // tm,),
        in_specs=[pl.BlockSpec((tm, half), lambda i: (i, 0)),
                  pl.BlockSpec((tm, half), lambda i: (i, 0)),
                  pl.BlockSpec((tm, d), lambda i: (i, 0)),
                  full((1, d)), full((1, d)),
                  pl.BlockSpec((None, 1, d), lambda i: (row(i), 0, 2)),
                  pl.BlockSpec((None, 1, d), lambda i: (row(i), 0, 3)),
                  pl.BlockSpec((None, 1, d), lambda i: (row(i), 0, 4)),
                  full((d, d)), full((1, d)), full((1, d)),
                  full((d, LANES)), full((1, LANES))],
        out_specs=[pl.BlockSpec((tm, d), lambda i: (i, 0)),
                   pl.BlockSpec((tm, half), lambda i: (i, 0)),
                   pl.BlockSpec((tm, LANES), lambda i: (i, 0))],
        out_shape=[jax.ShapeDtypeStruct((r, d), F32),
                   jax.ShapeDtypeStruct((r, half), U32),
                   jax.ShapeDtypeStruct((r, LANES), F32)],
        compiler_params=_cparams(1),
        name="post_attn",
    )(cm, gla, x2, ln_g, ln_b, mod3, mod3, mod3, w_out, l1g, l1b, w_r, b_r)


def _route_body(lg_ref, idx_ref, wt_ref, rank_ref, cnt_ref, carry_ref):
    i = pl.program_id(0)
    t = lg_ref.shape[0]

    @pl.when(i == 0)
    def _():
        carry_ref[...] = jnp.zeros_like(carry_ref)

    lane = lax.broadcasted_iota(I32, (t, LANES), 1)
    lane_f = lane.astype(F32)
    neg = jnp.float32(-jnp.inf)
    l = jnp.where(lane < N_EXPERTS, lg_ref[...], neg)
    tops, onehots, idxs = [], [], []
    for _ in range(TOP_K):
        m = jnp.max(l, axis=-1, keepdims=True)
        idx = jnp.min(jnp.where(l == m, lane_f, float(LANES)), axis=-1, keepdims=True).astype(I32)
        oh = lane == idx
        l = jnp.where(oh, neg, l)
        tops.append(m)
        idxs.append(idx)
        onehots.append(oh)
    exps = [jnp.exp(m - tops[0]) for m in tops]
    denom = exps[0] + exps[1] + exps[2] + exps[3]
    sel = _ones_where(onehots[0] | onehots[1] | onehots[2] | onehots[3], F32)
    r_id = lax.broadcasted_iota(I32, (t, t), 0)
    c_id = lax.broadcasted_iota(I32, (t, t), 1)
    strict = _ones_where(c_id < r_id, BF16)
    before = _dot(strict, sel.astype(BF16)) + carry_ref[0:1, :]
    idx_out = jnp.zeros((t, LANES), I32)
    wt_out = jnp.zeros((t, LANES), F32)
    rank_out = jnp.zeros((t, LANES), I32)
    for k in range(TOP_K):
        rk = jnp.sum(jnp.where(onehots[k], before, 0.0), axis=-1, keepdims=True).astype(I32)
        idx_out = jnp.where(lane == k, idxs[k], idx_out)
        wt_out = jnp.where(lane == k, exps[k] / denom, wt_out)
        rank_out = jnp.where(lane == k, rk, rank_out)
    idx_ref[...] = jnp.transpose(idx_out)[:SUBLANES, :]
    wt_ref[...] = wt_out
    rank_ref[...] = jnp.transpose(rank_out)[:SUBLANES, :]
    total = carry_ref[0:1, :] + jnp.sum(sel, axis=0, keepdims=True)
    carry_ref[...] = jnp.broadcast_to(total, carry_ref.shape)
    cnt_ref[...] = jnp.broadcast_to(total, cnt_ref.shape)


def _route(logits, tm):
    n = logits.shape[0]
    blk = pl.BlockSpec((tm, LANES), lambda i: (i, 0))
    small = pl.BlockSpec((SUBLANES, tm), lambda i: (0, i))
    return pl.pallas_call(
        _route_body,
        grid=(n // tm,),
        in_specs=[blk],
        out_specs=[small, blk, small, pl.BlockSpec((8, LANES), lambda i: (0, 0))],
        out_shape=[jax.ShapeDtypeStruct((SUBLANES, n), I32),
                   jax.ShapeDtypeStruct((n, LANES), F32),
                   jax.ShapeDtypeStruct((SUBLANES, n), I32),
                   jax.ShapeDtypeStruct((8, LANES), F32)],
        scratch_shapes=[pltpu.VMEM((8, LANES), F32)],
        compiler_params=_cparams(1),
        name="route",
    )(logits)


def _dispatch_body(nz_ref, zl_ref, dest_ref, hp_ref, xs_ref, zero_ref, sem, zsem):
    t = hp_ref.shape[0] * hp_ref.shape[1]

    @pl.when(pl.program_id(0) == 0)
    def _():
        zero_ref[...] = jnp.zeros_like(zero_ref)

        def z_copy(b):
            r = pl.multiple_of(zl_ref[b] * MOE_BLOCK, MOE_BLOCK)
            return pltpu.make_async_copy(zero_ref, xs_ref.at[pl.ds(r, MOE_BLOCK)], zsem)

        def start(b, carry):
            z_copy(b).start()
            return carry

        def wait(b, carry):
            z_copy(b).wait()
            return carry

        lax.fori_loop(0, nz_ref[0], start, 0)
        lax.fori_loop(0, nz_ref[0], wait, 0)

    def issue(g, carry):
        for u in range(SUBLANES):
            for k in range(TOP_K):
                d = dest_ref[0, k * t + g * SUBLANES + u]
                pltpu.make_async_copy(hp_ref.at[g, pl.ds(u, 1)], xs_ref.at[pl.ds(d, 1)], sem).start(priority=k % 2)
        return carry

    lax.fori_loop(0, t // SUBLANES, issue, 0)
    for _ in range(TOP_K):
        pltpu.make_async_copy(xs_ref.at[pl.ds(0, t)], xs_ref.at[pl.ds(0, t)], sem).wait()


def _tile_major(slot, tm):
    n = slot.shape[1]
    return slot.reshape(TOP_K, n // tm, tm).transpose(1, 0, 2).reshape(n // tm, 1, TOP_K * tm)


def _dispatch(n_zero, zero_blocks, dest, hp, n_slots, tm):
    n, w = hp.shape
    dest3 = _tile_major(dest, tm)
    return pl.pallas_call(
        _dispatch_body,
        grid_spec=pltpu.PrefetchScalarGridSpec(
            num_scalar_prefetch=2,
            grid=(n // tm,),
            in_specs=[pl.BlockSpec((None, 1, tm * TOP_K), lambda i, nz, zl: (i, 0, 0), memory_space=pltpu.SMEM),
                      pl.BlockSpec((tm // SUBLANES, SUBLANES, w), lambda i, nz, zl: (i, 0, 0))],
            out_specs=pl.BlockSpec(memory_space=pl.ANY),
            scratch_shapes=[pltpu.VMEM((MOE_BLOCK, w), U32),
                            pltpu.SemaphoreType.DMA(()),
                            pltpu.SemaphoreType.DMA(())]),
        out_shape=jax.ShapeDtypeStruct((n_slots, w), U32),
        compiler_params=_cparams(1),
        name="dispatch",
    )(n_zero, zero_blocks, dest3, hp.reshape(n // SUBLANES, SUBLANES, w))


PASS_SUBS = 9
EXPERT_TN = 256
DOT_ROWS = 1024
W_SLOTS = 3
W_AHEAD = 2


def _expert_body(meta_ref, pe_ref, pr_ref, pn_ref, xs_hbm, wgu_hbm, wdn_hbm, bgu_ref, bdn_ref,
                 y_hbm, xraw, xb, h_ref, wg_buf, wu_buf, wd_buf, ybuf, zbuf,
                 sem_x, sem_y, sem_z, sem_w, sem_d):
    sub = MOE_BLOCK
    tn = EXPERT_TN
    tp = tn // 2
    nj = h_ref.shape[0]
    nc = y_hbm.shape[1] // tp
    de = nj * tn
    d2 = xraw.shape[1]
    p = pl.program_id(0)
    n_pass = meta_ref[0]
    nsub = pn_ref[p]
    row0 = pr_ref[p]

    def x_copy(i, pp):
        r = pl.multiple_of((pr_ref[pp] + i) * sub, sub)
        return pltpu.make_async_copy(xs_hbm.at[pl.ds(r, sub)], xraw.at[pl.ds(i * sub, sub)], sem_x)

    def y_copy(start, size, r0, c, slot):
        r = pl.multiple_of(r0 * sub + start, sub)
        col = pl.multiple_of(c * tp, tp)
        return pltpu.make_async_copy(ybuf.at[slot, pl.ds(start, size), :],
                                     y_hbm.at[pl.ds(r, size), pl.ds(col, tp)], sem_y.at[slot])

    def z_copy(b, c):
        r = pl.multiple_of(b * sub, sub)
        return pltpu.make_async_copy(zbuf, y_hbm.at[pl.ds(r, sub), pl.ds(c * tp, tp)], sem_z)

    def w1_copies(pp, j):
        e = pe_ref[pp]
        slot = j % W_SLOTS
        col = pl.multiple_of(j * tn, tn)
        return (pltpu.make_async_copy(wgu_hbm.at[e, :, pl.ds(col, tn)], wg_buf.at[slot], sem_w.at[slot]),
                pltpu.make_async_copy(wgu_hbm.at[e, :, pl.ds(de + col, tn)], wu_buf.at[slot], sem_w.at[slot]))

    def w2_copy(pp, c):
        slot = c % W_SLOTS
        col = pl.multiple_of(c * tn, tn)
        return pltpu.make_async_copy(wdn_hbm.at[pe_ref[pp], :, pl.ds(col, tn)], wd_buf.at[slot], sem_d.at[slot])

    def start_w1(pp, j):
        for cp in w1_copies(pp, j):
            cp.start()

    def for_subs(count, fn):
        for i in range(PASS_SUBS):
            pl.when(i < count)(lambda i=i: fn(i))

    def for_tail(fn):
        def body(b, carry):
            for c in range(nc):
                fn(b, c)
            return carry
        lax.fori_loop(meta_ref[1], y_hbm.shape[0] // sub, body, 0)

    def for_groups(count, fn):
        pl.when(count == PASS_SUBS)(lambda: fn(0, PASS_SUBS * sub))
        k = 1 << (PASS_SUBS.bit_length() - 1)
        while k:
            start = pl.multiple_of((count & (-2 * k)) * sub, sub)
            pl.when((count != PASS_SUBS) & ((count & k) != 0))(lambda start=start, k=k: fn(start, k * sub))
            k //= 2

    @pl.when(p == 0)
    def _():
        for_subs(nsub, lambda i: x_copy(i, p).start())
        for j in range(W_AHEAD):
            start_w1(p, j)
        zbuf[...] = jnp.zeros_like(zbuf)
        for_tail(lambda b, c: z_copy(b, c).start())

    for_subs(nsub, lambda i: x_copy(i, p).wait())

    def unpack(i):
        rows = slice(i * sub, (i + 1) * sub)
        lo, hi = _unpack_pair_f32(xraw[rows, :])
        xb[rows, :d2] = lo.astype(BF16)
        xb[rows, d2:] = hi.astype(BF16)

    for_subs(nsub, unpack)

    @pl.when(p + 1 < n_pass)
    def _():
        for_subs(pn_ref[p + 1], lambda i: x_copy(i, p + 1).start())

    def first_step(j, carry):
        slot = j % W_SLOTS
        for cp in w1_copies(p, j):
            cp.wait()
        nxt = j + W_AHEAD
        pl.when(nxt < nj)(lambda: start_w1(p, nxt))
        pl.when(nxt >= nj)(lambda: w2_copy(p, nxt - nj).start())

        def group(start, size):
            wg = wg_buf[slot].astype(BF16)
            wu = wu_buf[slot].astype(BF16)
            dr = DOT_ROWS if size % DOT_ROWS == 0 else size
            for r in range(0, size, dr):
                rows = pl.ds(start + r, dr)
                x = xb[rows, :]
                gate = jnp.minimum(_dot(x, wg) + bgu_ref[j], SWIGLU_LIMIT)
                up = jnp.clip(_dot(x, wu) + bgu_ref[nj + j], -SWIGLU_LIMIT, SWIGLU_LIMIT)
                h_ref[j, rows, :] = ((up + 1.0) * gate * jax.nn.sigmoid(SWIGLU_ALPHA * gate)).astype(BF16)

        for_groups(nsub, group)
        return carry

    lax.fori_loop(0, nj, first_step, 0)

    def second_step(c, carry):
        wslot = c % W_SLOTS
        slot = c % 2
        w2_copy(p, c).wait()
        nxt = c + W_AHEAD
        pl.when(nxt < nc)(lambda: w2_copy(p, nxt).start())
        pl.when((nxt >= nc) & (p + 1 < n_pass))(lambda: start_w1(p + 1, nxt - nc))

        @pl.when(c >= 2)
        def _():
            for_groups(nsub, lambda start, size: y_copy(start, size, row0, c - 2, slot).wait())

        @pl.when((c < 2) & (p > 0))
        def _():
            for_groups(pn_ref[p - 1],
                       lambda start, size: y_copy(start, size, pr_ref[p - 1], nc - 2 + c, slot).wait())

        def group(start, size):
            wd = wd_buf[wslot].astype(BF16)
            dr = DOT_ROWS if size % DOT_ROWS == 0 else size
            for r in range(0, size, dr):
                rows = pl.ds(start + r, dr)
                hx = jnp.concatenate([h_ref[j, rows, :] for j in range(nj)], axis=1)
                y = _dot(hx, wd) + bdn_ref[c]
                ybuf[slot, rows, :] = _pack_bf16_pair(y[:, :tp], y[:, tp:])
            y_copy(start, size, row0, c, slot).start()

        for_groups(nsub, group)
        return carry

    lax.fori_loop(0, nc, second_step, 0)

    @pl.when(p == n_pass - 1)
    def _():
        for c in (nc - 2, nc - 1):
            for_groups(nsub, lambda start, size, c=c: y_copy(start, size, row0, c, c % 2).wait())
        for_tail(lambda b, cc: z_copy(b, cc).wait())


def _experts(meta, pass_e, pass_row0, pass_nsub, xs, w_gu, b_gu, w_dn, b_dn):
    n_slots, d2 = xs.shape
    d = 2 * d2
    n_e, de = w_dn.shape[0], w_dn.shape[1]
    tn = EXPERT_TN
    nj = de // tn
    nc = d // tn
    assert W_AHEAD < W_SLOTS and W_AHEAD <= min(nj, nc) and nc % 2 == 0
    rmax = PASS_SUBS * MOE_BLOCK
    return pl.pallas_call(
        _expert_body,
        grid_spec=pltpu.PrefetchScalarGridSpec(
            num_scalar_prefetch=4,
            grid=(meta[0],),
            in_specs=[
                pl.BlockSpec(memory_space=pl.ANY),
                pl.BlockSpec(memory_space=pl.ANY),
                pl.BlockSpec(memory_space=pl.ANY),
                pl.BlockSpec((None, 2 * nj, 1, tn), lambda p, m, e, r, n: (e[p], 0, 0, 0)),
                pl.BlockSpec((None, nc, 1, tn), lambda p, m, e, r, n: (e[p], 0, 0, 0)),
            ],
            out_specs=pl.BlockSpec(memory_space=pl.ANY),
            scratch_shapes=[pltpu.VMEM((rmax, d2), U32),
                            pltpu.VMEM((rmax, d), BF16),
                            pltpu.VMEM((nj, rmax, tn), BF16),
                            pltpu.VMEM((W_SLOTS, d, tn), w_gu.dtype),
                            pltpu.VMEM((W_SLOTS, d, tn), w_gu.dtype),
                            pltpu.VMEM((W_SLOTS, de, tn), w_dn.dtype),
                            pltpu.VMEM((2, rmax, tn // 2), U32),
                            pltpu.VMEM((MOE_BLOCK, tn // 2), U32),
                            pltpu.SemaphoreType.DMA(()),
                            pltpu.SemaphoreType.DMA((2,)),
                            pltpu.SemaphoreType.DMA(()),
                            pltpu.SemaphoreType.DMA((W_SLOTS,)),
                            pltpu.SemaphoreType.DMA((W_SLOTS,))]),
        out_shape=jax.ShapeDtypeStruct((n_slots, d2), U32),
        compiler_params=_cparams(1),
        name="experts",
    )(meta, pass_e, pass_row0, pass_nsub, xs, w_gu, w_dn,
      b_gu.reshape(n_e, 2 * nj, 1, tn), b_dn.reshape(n_e, nc, 1, tn))


COMBINE_CHUNK = 64


def _combine_body(dcur_ref, dnxt_ref, wt_ref, h1_ref, g2_ref, l2g_ref, l2b_ref, y_ref, out_ref, buf_ref, sem):
    i = pl.program_id(0)
    n_tiles = pl.num_programs(0)
    t = h1_ref.shape[0]
    tp = EXPERT_TN // 2
    slot = i % 2

    def gather(dest_ref, sl):
        def body(g, carry):
            for u in range(SUBLANES):
                for k in range(TOP_K):
                    d = dest_ref[0, k * t + g * SUBLANES + u]
                    pltpu.make_async_copy(y_ref.at[pl.ds(d, 1)], buf_ref.at[sl, k, g, pl.ds(u, 1)],
                                          sem.at[sl]).start(priority=k % 2)
            return carry
        lax.fori_loop(0, t // SUBLANES, body, 0)

    pl.when(i == 0)(lambda: gather(dcur_ref, 0))
    pl.when(i + 1 < n_tiles)(lambda: gather(dnxt_ref, 1 - slot))
    for k in range(TOP_K):
        pltpu.make_async_copy(buf_ref.at[1 - slot, k], buf_ref.at[slot, k], sem.at[slot]).wait()
    for r in range(0, t, COMBINE_CHUNK):
        rows = slice(r, r + COMBINE_CHUNK)
        groups = slice(r // SUBLANES, (r + COMBINE_CHUNK) // SUBLANES)
        wt = wt_ref[rows, :]
        f_lo, f_hi = None, None
        for k in range(TOP_K):
            lo, hi = _unpack_pair_f32(buf_ref[slot, k, groups].reshape(COMBINE_CHUNK, buf_ref.shape[-1]))
            w = wt[:, k:k + 1]
            f_lo = lo * w if f_lo is None else f_lo + lo * w
            f_hi = hi * w if f_hi is None else f_hi + hi * w
        parts = []
        for c in range(f_lo.shape[1] // tp):
            parts += [f_lo[:, c * tp:(c + 1) * tp], f_hi[:, c * tp:(c + 1) * tp]]
        f = jnp.concatenate(parts, axis=1)
        out_ref[rows, :] = _layer_norm(DEEPNORM_ALPHA * h1_ref[rows, :] + g2_ref[...] * f,
                                       l2g_ref[...], l2b_ref[...])


def _combine(dest, wt, h1, mod3, rows_per_batch, l2g, l2b, y, tm):
    n, d = h1.shape
    n_tiles = n // tm
    dest3 = _tile_major(dest, tm)
    row = lambda i: (i * tm) // rows_per_batch
    return pl.pallas_call(
        _combine_body,
        grid=(n_tiles,),
        in_specs=[pl.BlockSpec((None, 1, tm * TOP_K), lambda i: (i, 0, 0), memory_space=pltpu.SMEM),
                  pl.BlockSpec((None, 1, tm * TOP_K), lambda i: (jnp.minimum(i + 1, n_tiles - 1), 0, 0),
                               memory_space=pltpu.SMEM),
                  pl.BlockSpec((tm, LANES), lambda i: (i, 0)),
                  pl.BlockSpec((tm, d), lambda i: (i, 0)),
                  pl.BlockSpec((None, 1, d), lambda i: (row(i), 0, 5)),
                  pl.BlockSpec((1, d), lambda i: (0, 0)),
                  pl.BlockSpec((1, d), lambda i: (0, 0)),
                  pl.BlockSpec(memory_space=pl.ANY)],
        out_specs=pl.BlockSpec((tm, d), lambda i: (i, 0)),
        out_shape=jax.ShapeDtypeStruct((n, d), F32),
        scratch_shapes=[pltpu.VMEM((2, TOP_K, tm // SUBLANES, SUBLANES, y.shape[1]), U32),
                        pltpu.SemaphoreType.DMA((2,))],
        compiler_params=_cparams(1),
        name="combine",
    )(dest3, dest3, wt, h1, mod3, l2g, l2b, y)


def _pick_tile(n, pref):
    t = pref
    while n % t:
        t //= 2
    return t


def kernel(x, c, ctx, c_ctx, ln_in_g, ln_in_b, w_ada, b_ada, w_in, cm_norm_g, cm_norm_b, cm_w_s, cm_b_s,
           gla_w_gk_f, gla_b_gk_f, gla_w_gk_b, gla_b_gk_b, gla_norm_g, w_out, ln1_g, ln1_b,
           w_router, b_router, w_gate_up, b_gate_up, w_down, b_down, ln2_g, ln2_b):
    bsz, l, d = x.shape
    lc = ctx.shape[1]
    n, nc = bsz * l, bsz * lc
    assert w_ada.shape[0] == 1, "single-layer configuration"
    assert bsz + 1 <= 8 and l % (2 * GLA_TILE) == 0 and lc % GLA_TILE == 0
    row = lambda v: v.reshape(1, -1)

    cc = jnp.concatenate([c, c_ctx[None, :], jnp.zeros((8 - bsz - 1, d), F32)], axis=0)
    mod3 = _ada(cc, w_ada[0], row(b_ada[0])).reshape(8, 1, N_MOD * d)

    n_uv = 2 * CM_HEADS * CM_CHUNK
    n_main = n_uv + 2 * GLA_HEADS * GLA_DK + 2 * GLA_HEADS * GLA_DV
    w_uv = w_in[0][:, :n_uv].astype(BF16)
    w_rest = w_in[0][:, n_uv:n_main].astype(BF16)
    w_lr = jnp.pad(w_in[0][:, n_main:].astype(BF16), ((0, 0), (0, LANES - 2 * GLA_RANK)))
    x2 = x.reshape(n, d)
    tm_x = _pick_tile(l, 512)
    bs_tile = jnp.repeat(cm_b_s[0].T, CM_CHUNK, axis=1)
    cm, p, lr = _inproj_cm(x2, mod3, l, row(ln_in_g), row(ln_in_b), w_uv, w_rest, w_lr,
                           row(cm_norm_g[0]), row(cm_norm_b[0]), cm_w_s[0].astype(BF16), bs_tile, tm_x)
    tm_c = _pick_tile(nc, 512)
    pc, lrc = _inproj(ctx.reshape(nc, d), mod3, lambda i: bsz, row(ln_in_g), row(ln_in_b),
                      w_rest, w_lr, 0, 2, tm_c, 1024)

    kw = GLA_HEADS * GLA_DK
    wgf = jnp.zeros((LANES, kw), BF16).at[:GLA_RANK].set(gla_w_gk_f[0].astype(BF16))
    wgb = jnp.zeros((LANES, kw), BF16).at[GLA_RANK:2 * GLA_RANK].set(gla_w_gk_b[0].astype(BF16))
    gla = _gla(p.reshape(bsz, l, -1), lr.reshape(bsz, l, LANES), pc.reshape(bsz, lc, -1),
               lrc.reshape(bsz, lc, LANES), wgf, row(gla_b_gk_f[0]), wgb, row(gla_b_gk_b[0]),
               row(gla_norm_g[0])).reshape(n, -1)

    w_r = jnp.pad(w_router[0], ((0, 0), (0, LANES - N_EXPERTS))).astype(BF16)
    b_r = jnp.pad(b_router[0], (0, LANES - N_EXPERTS)).reshape(1, LANES)
    h1, hp, logits = _post_attn(cm, gla, x2, mod3, l, row(ln_in_g), row(ln_in_b), w_out[0].astype(BF16),
                                row(ln1_g[0]), row(ln1_b[0]), w_r, b_r, tm_x)

    idx, wt, rank, cnt = _route(logits, _pick_tile(n, 1024))
    counts = cnt[0, :N_EXPERTS].astype(I32)
    n_blocks = (n * TOP_K + N_EXPERTS * (MOE_BLOCK - 1)) // MOE_BLOCK
    blocks_e = (counts + MOE_BLOCK - 1) // MOE_BLOCK
    blk_end = jnp.cumsum(blocks_e)
    blk_start = blk_end - blocks_e
    dest = rank[:TOP_K]
    for e in range(N_EXPERTS):
        dest = dest + jnp.where(idx[:TOP_K] == e, blk_start[e] * MOE_BLOCK, 0)
    n_pass_max = n_blocks // PASS_SUBS + N_EXPERTS
    pass_cnt = (blocks_e + PASS_SUBS - 1) // PASS_SUBS
    pass_end = jnp.cumsum(pass_cnt)
    pass_start = pass_end - pass_cnt
    pid = jnp.arange(n_pass_max, dtype=I32)
    pass_e = jnp.minimum(jnp.searchsorted(pass_end, pid, side="right"), N_EXPERTS - 1).astype(I32)
    local = pid - pass_start[pass_e]
    pass_row0 = (blk_start[pass_e] + local * PASS_SUBS).astype(I32)
    pass_nsub = jnp.clip(blocks_e[pass_e] - local * PASS_SUBS, 0, PASS_SUBS).astype(I32)
    meta = jnp.stack([pass_end[-1], blk_end[-1]]).astype(I32)

    ar = jnp.arange(N_EXPERTS, dtype=I32)
    cand = jnp.concatenate([jnp.where(counts % MOE_BLOCK != 0, blk_end - 1, -1),
                            jnp.where(blk_end[-1] + ar < n_blocks, blk_end[-1] + ar, -1)])
    zero_blocks = cand[jnp.argsort(cand < 0, stable=True)].astype(I32)
    n_zero = jnp.sum(cand >= 0).astype(I32).reshape(1)

    xs = _dispatch(n_zero, zero_blocks, dest, hp, n_blocks * MOE_BLOCK, _pick_tile(n, 512))
    ys = _experts(meta, pass_e, pass_row0, pass_nsub, xs, w_gate_up[0], b_gate_up[0], w_down[0], b_down[0])
    out = _combine(dest, wt, h1, mod3, l, row(ln2_g[0]), row(ln2_b[0]), ys, _pick_tile(n, 512))
    return out.reshape(bsz, l, d)
```

```python
import math

import jax
import jax.numpy as jnp
from jax import lax
from jax.experimental import pallas as pl
from jax.experimental.pallas import tpu as pltpu

F32 = jnp.float32
BF16 = jnp.bfloat16
U32 = jnp.uint32
I32 = jnp.int32

CM_CHUNK = 128
CM_HEADS = 8
GLA_HEADS = 4
GLA_DK = 128
GLA_DV = 256
GLA_CHUNK = 64
GLA_RANK = 16
GLA_GATE_NORMALIZER = 16.0
N_EXPERTS = 32
TOP_K = 4
MOE_BLOCK = 256
SWIGLU_LIMIT = 7.0
SWIGLU_ALPHA = 1.702
N_MOD = 6
DEEPNORM_ALPHA = 2.0 ** 0.25
LN_EPS = 1e-5
RMS_EPS = 1e-6

LANES = 128
SUBLANES = 8
VMEM_LIMIT = 56 * 1024 * 1024
ROW_CHUNK = 256


def _cparams(n_axes, vmem=VMEM_LIMIT):
    return pltpu.CompilerParams(dimension_semantics=("arbitrary",) * n_axes,
                                vmem_limit_bytes=vmem)


def _layer_norm(t, g, b):
    mu = jnp.mean(t, axis=-1, keepdims=True)
    d = t - mu
    var = jnp.mean(d * d, axis=-1, keepdims=True)
    return d * lax.rsqrt(var + LN_EPS) * g + b


def _gelu(t):
    return 0.5 * t * (1.0 + lax.erf(t * (1.0 / math.sqrt(2.0))))


def _silu(t):
    return t * jax.nn.sigmoid(t)


def _ones_where(mask, dtype):
    return jnp.where(mask, 1.0, 0.0).astype(dtype)


def _dot(a, b):
    return jnp.dot(a, b, preferred_element_type=F32)


def _dot_nt(a, b):
    return lax.dot_general(a, b, (((1,), (1,)), ((), ())), preferred_element_type=F32)


def _dot_tn(a, b):
    return lax.dot_general(a, b, (((0,), (0,)), ((), ())), preferred_element_type=F32)


def _pack_bf16_pair(lo, hi):
    lo_b = lax.bitcast_convert_type(lo.astype(BF16).astype(F32), U32)
    hi_b = lax.bitcast_convert_type(hi.astype(BF16).astype(F32), U32)
    return hi_b | (lo_b >> 16)


def _unpack_pair_f32(p):
    lo = lax.bitcast_convert_type(p << 16, F32)
    hi = lax.bitcast_convert_type(p & jnp.uint32(0xFFFF0000), F32)
    return lo, hi


def _ada_body(c_ref, w_ref, b_ref, o_ref):
    a = _silu(c_ref[...]).astype(BF16)
    o_ref[...] = _dot(a, w_ref[...].astype(BF16)) + b_ref[...]


def _ada(cc, w, b):
    rows, d = cc.shape
    n = w.shape[1]
    tn = 1024
    return pl.pallas_call(
        _ada_body,
        grid=(n // tn,),
        in_specs=[pl.BlockSpec((rows, d), lambda j: (0, 0)),
                  pl.BlockSpec((d, tn), lambda j: (0, j)),
                  pl.BlockSpec((1, tn), lambda j: (0, j))],
        out_specs=pl.BlockSpec((rows, tn), lambda j: (0, j)),
        out_shape=jax.ShapeDtypeStruct((rows, n), F32),
        compiler_params=_cparams(1),
        name="ada",
    )(cc, w, b)


def _inproj_body(x_ref, g_ref, b_ref, sh_ref, sc_ref, w_ref, wlr_ref, o_ref, olr_ref, hm_ref):
    @pl.when(pl.program_id(1) == 0)
    def _():
        h = _layer_norm(x_ref[...], g_ref[...], b_ref[...])
        hm = (h * (1.0 + sc_ref[...]) + sh_ref[...]).astype(BF16)
        hm_ref[...] = hm
        olr_ref[...] = _dot(hm, wlr_ref[...])

    o_ref[...] = _dot(hm_ref[...], w_ref[...])


def _inproj(x2, mod3, mod_row, ln_g, ln_b, w_main, w_lr, col0, ncols, tm, tn):
    r, d = x2.shape
    return pl.pallas_call(
        _inproj_body,
        grid=(r // tm, ncols),
        in_specs=[pl.BlockSpec((tm, d), lambda i, j: (i, 0)),
                  pl.BlockSpec((1, d), lambda i, j: (0, 0)),
                  pl.BlockSpec((1, d), lambda i, j: (0, 0)),
                  pl.BlockSpec((None, 1, d), lambda i, j: (mod_row(i), 0, 0)),
                  pl.BlockSpec((None, 1, d), lambda i, j: (mod_row(i), 0, 1)),
                  pl.BlockSpec((d, tn), lambda i, j: (0, col0 + j)),
                  pl.BlockSpec((d, LANES), lambda i, j: (0, 0))],
        out_specs=[pl.BlockSpec((tm, tn), lambda i, j: (i, j)),
                   pl.BlockSpec((tm, LANES), lambda i, j: (i, 0))],
        out_shape=[jax.ShapeDtypeStruct((r, ncols * tn), F32),
                   jax.ShapeDtypeStruct((r, LANES), F32)],
        scratch_shapes=[pltpu.VMEM((tm, d), BF16)],
        compiler_params=_cparams(2),
        name="inproj",
    )(x2, ln_g, ln_b, mod3, mod3, w_main, w_lr)


def _inproj_cm_body(x_ref, g_ref, b_ref, sh_ref, sc_ref, wuv_ref, wr_ref, wlr_ref, ng_ref, nb_ref, ws_ref, bs_ref,
                    cm_ref, p_ref, lr_ref):
    tm = x_ref.shape[0]
    half = wuv_ref.shape[1] // 2
    hd = CM_CHUNK
    for r in range(0, tm, ROW_CHUNK):
        rows = slice(r, r + ROW_CHUNK)
        h = _layer_norm(x_ref[rows, :], g_ref[...], b_ref[...])
        hm = (h * (1.0 + sc_ref[...]) + sh_ref[...]).astype(BF16)
        uv = _dot(hm, wuv_ref[...])
        u = _gelu(uv[:, :half])
        vb = _layer_norm(_gelu(uv[:, half:]), ng_ref[...], nb_ref[...]).astype(BF16)
        for c in range(ROW_CHUNK // CM_CHUNK):
            crow = slice(c * CM_CHUNK, (c + 1) * CM_CHUNK)
            orow = slice(r + c * CM_CHUNK, r + (c + 1) * CM_CHUNK)
            for hh in range(CM_HEADS):
                cols = slice(hh * hd, (hh + 1) * hd)
                s = _dot(ws_ref[hh], vb[crow, cols]) + bs_ref[:, cols]
                cm_ref[orow, cols] = (u[crow, cols] * s).astype(BF16)
        p_ref[rows, :] = _dot(hm, wr_ref[...])
        lr_ref[rows, :] = _dot(hm, wlr_ref[...])


def _inproj_cm(x2, mod3, rows_per_batch, ln_g, ln_b, w_uv, w_rest, w_lr, ng, nb, ws, bs, tm):
    r, d = x2.shape
    n_uv, n_rest = w_uv.shape[1], w_rest.shape[1]
    row = lambda i: (i * tm) // rows_per_batch

    def const(shape):
        return pl.BlockSpec(shape, lambda i: (0,) * len(shape), pipeline_mode=pl.Buffered(1))

    return pl.pallas_call(
        _inproj_cm_body,
        grid=(r // tm,),
        in_specs=[pl.BlockSpec((tm, d), lambda i: (i, 0)),
                  const((1, d)), const((1, d)),
                  pl.BlockSpec((None, 1, d), lambda i: (row(i), 0, 0)),
                  pl.BlockSpec((None, 1, d), lambda i: (row(i), 0, 1)),
                  const((d, n_uv)), const((d, n_rest)), const((d, LANES)),
                  const((1, n_uv // 2)), const((1, n_uv // 2)),
                  const((CM_HEADS, CM_CHUNK, CM_CHUNK)), const((CM_CHUNK, n_uv // 2))],
        out_specs=[pl.BlockSpec((tm, n_uv // 2), lambda i: (i, 0)),
                   pl.BlockSpec((tm, n_rest), lambda i: (i, 0)),
                   pl.BlockSpec((tm, LANES), lambda i: (i, 0))],
        out_shape=[jax.ShapeDtypeStruct((r, n_uv // 2), BF16),
                   jax.ShapeDtypeStruct((r, n_rest), F32),
                   jax.ShapeDtypeStruct((r, LANES), F32)],
        compiler_params=_cparams(1),
        name="inproj_cm",
    )(x2, ln_g, ln_b, mod3, mod3, w_uv, w_rest, w_lr, ng, nb, ws, bs)


GLA_TILE = 256
GLA_TILES_PER_ITER = 2


def _gla_tile(q, k, v, lr, wg, bg, st_ref, forward, need_o):
    t = k.shape[0]
    n_chunks = t // GLA_CHUNK
    z = _dot(lr.astype(BF16), wg) + bg
    g = jax.nn.log_sigmoid(z) * (1.0 / GLA_GATE_NORMALIZER)
    r_id = lax.broadcasted_iota(I32, (t, t), 0)
    c_id = lax.broadcasted_iota(I32, (t, t), 1)
    shift = GLA_CHUNK.bit_length() - 1
    same = (r_id >> shift) == (c_id >> shift)
    lower = same & (c_id <= r_id)
    tri = _ones_where(lower, BF16)
    g_hi = g.astype(BF16)
    g_lo = (g - g_hi.astype(F32)).astype(BF16)
    csum2 = _dot(tri, jnp.concatenate([g_hi, g_lo], axis=1))
    csum = csum2[:, :GLA_DK] + csum2[:, GLA_DK:]
    g3 = g.reshape(n_chunks, GLA_CHUNK, GLA_DK)
    tot = jnp.broadcast_to(jnp.sum(g3, axis=1, keepdims=True), g3.shape).reshape(t, GLA_DK)
    bcum = csum if forward else tot - csum + g
    kd = (k * jnp.exp(tot - bcum)).astype(BF16)
    decay = jnp.exp(tot)
    vb = v.astype(BF16)
    row_chunk = lax.broadcasted_iota(I32, (t, GLA_DK), 0) >> shift
    kd_blocks = jnp.concatenate([jnp.where(row_chunk == c, kd, jnp.zeros_like(kd)) for c in range(n_chunks)], axis=1)
    u_all = _dot_tn(vb, kd_blocks)
    o = None
    if need_o:
        qe = ((q * (GLA_DK ** -0.5)) * jnp.exp(bcum)).astype(BF16)
        ke = (k * jnp.exp(-bcum)).astype(BF16)
        att = _dot_nt(qe, ke)
        mask = lower if forward else same & (c_id >= r_id)
        att = jnp.where(mask, att, 0.0).astype(BF16)
        o = _dot(att, vb)
    outs = [None] * n_chunks
    order = range(n_chunks) if forward else range(n_chunks - 1, -1, -1)
    for c in order:
        rows = slice(c * GLA_CHUNK, (c + 1) * GLA_CHUNK)
        s_t = st_ref[...]
        if need_o:
            outs[c] = o[rows] + _dot_nt(qe[rows], s_t.astype(BF16))
        u_t = u_all[:, c * GLA_DK:(c + 1) * GLA_DK]
        st_ref[...] = s_t * decay[c * GLA_CHUNK:c * GLA_CHUNK + 1, :] + u_t
    if need_o:
        return jnp.concatenate(outs, axis=0)
    return None


def _gla_body(q_ref, k_ref, v_ref, go_ref, lr_ref, kc_ref, vc_ref, lrc_ref,
              wgf_ref, bgf_ref, wgb_ref, bgb_ref, ng_ref, out_ref, o_scr, sf_ref, sb_ref):
    t = GLA_TILE
    n_x = q_ref.shape[0] // t
    n_c = kc_ref.shape[0] // t
    half = n_x // 2
    sf_ref[...] = jnp.zeros_like(sf_ref)
    sb_ref[...] = jnp.zeros_like(sb_ref)
    wgf, bgf, wgb, bgb = wgf_ref[...], bgf_ref[...], wgb_ref[...], bgb_ref[...]

    for i in range(n_c):
        rf = slice(i * t, (i + 1) * t)
        rb = slice((n_c - 1 - i) * t, (n_c - i) * t)
        _gla_tile(None, kc_ref[rf], vc_ref[rf], lrc_ref[rf], wgf, bgf, sf_ref, True, False)
        _gla_tile(None, kc_ref[rb], vc_ref[rb], lrc_ref[rb], wgb, bgb, sb_ref, False, False)

    def tile_out(i, forward):
        rows = pl.ds(pl.multiple_of(i * t, t), t)
        if forward:
            return rows, _gla_tile(q_ref[rows], k_ref[rows], v_ref[rows], lr_ref[rows],
                                   wgf, bgf, sf_ref, True, True)
        return rows, _gla_tile(q_ref[rows], k_ref[rows], v_ref[rows], lr_ref[rows],
                               wgb, bgb, sb_ref, False, True)

    def finish(rows, o):
        o = o + o_scr[rows]
        ms = jnp.mean(o * o, axis=-1, keepdims=True)
        on = o * lax.rsqrt(ms + RMS_EPS) * ng_ref[...]
        out_ref[rows] = (on * _silu(go_ref[rows])).astype(BF16)

    def keep(rows, o):
        o_scr[rows] = o

    u = math.gcd(GLA_TILES_PER_ITER, half)

    def make_step(sink):
        def step(it, carry):
            for w in range(u):
                i = it * u + w
                sink(*tile_out(i, True))
                sink(*tile_out(n_x - 1 - i, False))
            return carry
        return step

    lax.fori_loop(0, half // u, make_step(keep), 0)
    lax.fori_loop(half // u, n_x // u, make_step(finish), 0)


def _gla(p3, lr3, pc3, lrc3, wgf, bgf, wgb, bgb, ng):
    bsz, l, _ = p3.shape
    lc = pc3.shape[1]
    dk, dv = GLA_DK, GLA_DV
    kw = GLA_HEADS * dk
    q0, k0 = 0, kw // dk
    v0, go0 = 2 * kw // dv, (2 * kw + GLA_HEADS * dv) // dv
    kc0, vc0 = k0, v0
    return pl.pallas_call(
        _gla_body,
        grid=(bsz, GLA_HEADS),
        in_specs=[pl.BlockSpec((None, l, dk), lambda b, h: (b, 0, q0 + h)),
                  pl.BlockSpec((None, l, dk), lambda b, h: (b, 0, k0 + h)),
                  pl.BlockSpec((None, l, dv), lambda b, h: (b, 0, v0 + h)),
                  pl.BlockSpec((None, l, dv), lambda b, h: (b, 0, go0 + h)),
                  pl.BlockSpec((None, l, LANES), lambda b, h: (b, 0, 0)),
                  pl.BlockSpec((None, lc, dk), lambda b, h: (b, 0, kc0 + h)),
                  pl.BlockSpec((None, lc, dv), lambda b, h: (b, 0, vc0 + h)),
                  pl.BlockSpec((None, lc, LANES), lambda b, h: (b, 0, 0)),
                  pl.BlockSpec((LANES, dk), lambda b, h: (0, h)),
                  pl.BlockSpec((1, dk), lambda b, h: (0, h)),
                  pl.BlockSpec((LANES, dk), lambda b, h: (0, h)),
                  pl.BlockSpec((1, dk), lambda b, h: (0, h)),
                  pl.BlockSpec((1, dv), lambda b, h: (0, 0))],
        out_specs=pl.BlockSpec((None, l, dv), lambda b, h: (b, 0, h)),
        out_shape=jax.ShapeDtypeStruct((bsz, l, GLA_HEADS * dv), BF16),
        scratch_shapes=[pltpu.VMEM((l, dv), F32),
                        pltpu.VMEM((dv, dk), F32),
                        pltpu.VMEM((dv, dk), F32)],
        compiler_params=_cparams(2),
        name="gla",
    )(p3, p3, p3, p3, lr3, pc3, pc3, lrc3, wgf, bgf, wgb, bgb, ng)


def _post_body(cm_ref, gla_ref, x_ref, lng_ref, lnb_ref, g1_ref, sh2_ref, sc2_ref,
               wo_ref, l1g_ref, l1b_ref, wr_ref, br_ref, h1_ref, hp_ref, lg_ref):
    half = cm_ref.shape[1]
    d2 = x_ref.shape[1] // 2
    for r in range(0, x_ref.shape[0], ROW_CHUNK):
        rows = slice(r, r + ROW_CHUNK)
        y = _dot(cm_ref[rows, :], wo_ref[:half, :]) + _dot(gla_ref[rows, :], wo_ref[half:, :])
        hx = _layer_norm(x_ref[rows, :], lng_ref[...], lnb_ref[...])
        h1 = _layer_norm(DEEPNORM_ALPHA * hx + g1_ref[...] * y, l1g_ref[...], l1b_ref[...])
        h1_ref[rows, :] = h1
        hm = h1 * (1.0 + sc2_ref[...]) + sh2_ref[...]
        hp_ref[rows, :] = _pack_bf16_pair(hm[:, :d2], hm[:, d2:])
        lg_ref[rows, :] = _dot(hm.astype(BF16), wr_ref[...]) + br_ref[...]


def _post_attn(cm, gla, x2, mod3, rows_per_batch, ln_g, ln_b, w_out, l1g, l1b, w_r, b_r, tm):
    r, d = x2.shape
    half = d // 2
    row = lambda i: (i * tm) // rows_per_batch
    full = lambda shape: pl.BlockSpec(shape, lambda i: (0,) * len(shape))
    return pl.pallas_call(
        _post_body,
        grid=(r // tm,),
        in_specs=[pl.BlockSpec((tm, half), lambda i: (i, 0)),
                  pl.BlockSpec((tm, half), lambda i: (i, 0)),
                  pl.BlockSpec((tm, d), lambda i: (i, 0)),
                  full((1, d)), full((1, d)),
                  pl.BlockSpec((None, 1, d), lambda i: (row(i), 0, 2)),
                  pl.BlockSpec((None, 1, d), lambda i: (row(i), 0, 3)),
                  pl.BlockSpec((None, 1, d), lambda i: (row(i), 0, 4)),
                  full((d, d)), full((1, d)), full((1, d)),
                  full((d, LANES)), full((1, LANES))],
        out_specs=[pl.BlockSpec((tm, d), lambda i: (i, 0)),
                   pl.BlockSpec((tm, half), lambda i: (i, 0)),
                   pl.BlockSpec((tm, LANES), lambda i: (i, 0))],
        out_shape=[jax.ShapeDtypeStruct((r, d), F32),
                   jax.ShapeDtypeStruct((r, half), U32),
                   jax.ShapeDtypeStruct((r, LANES), F32)],
        compiler_params=_cparams(1),
        name="post_attn",
    )(cm, gla, x2, ln_g, ln_b, mod3, mod3, mod3, w_out, l1g, l1b, w_r, b_r)


def _route_body(lg_ref, idx_ref, wt_ref, rank_ref, cnt_ref, carry_ref):
    i = pl.program_id(0)
    t = lg_ref.shape[0]

    @pl.when(i == 0)
    def _():
        carry_ref[...] = jnp.zeros_like(carry_ref)

    lane = lax.broadcasted_iota(I32, (t, LANES), 1)
    lane_f = lane.astype(F32)
    neg = jnp.float32(-jnp.inf)
    l = jnp.where(lane < N_EXPERTS, lg_ref[...], neg)
    tops, onehots, idxs = [], [], []
    for _ in range(TOP_K):
        m = jnp.max(l, axis=-1, keepdims=True)
        idx = jnp.min(jnp.where(l == m, lane_f, float(LANES)), axis=-1, keepdims=True).astype(I32)
        oh = lane == idx
        l = jnp.where(oh, neg, l)
        tops.append(m)
        idxs.append(idx)
        onehots.append(oh)
    exps = [jnp.exp(m - tops[0]) for m in tops]
    denom = exps[0] + exps[1] + exps[2] + exps[3]
    sel = _ones_where(onehots[0] | onehots[1] | onehots[2] | onehots[3], F32)
    r_id = lax.broadcasted_iota(I32, (t, t), 0)
    c_id = lax.broadcasted_iota(I32, (t, t), 1)
    strict = _ones_where(c_id < r_id, BF16)
    before = _dot(strict, sel.astype(BF16)) + carry_ref[0:1, :]
    idx_out = jnp.zeros((t, LANES), I32)
    wt_out = jnp.zeros((t, LANES), F32)
    rank_out = jnp.zeros((t, LANES), I32)
    for k in range(TOP_K):
        rk = jnp.sum(jnp.where(onehots[k], before, 0.0), axis=-1, keepdims=True).astype(I32)
        idx_out = jnp.where(lane == k, idxs[k], idx_out)
        wt_out = jnp.where(lane == k, exps[k] / denom, wt_out)
        rank_out = jnp.where(lane == k, rk, rank_out)
    idx_ref[...] = jnp.transpose(idx_out)[:SUBLANES, :]
    wt_ref[...] = wt_out
    rank_ref[...] = jnp.transpose(rank_out)[:SUBLANES, :]
    total = carry_ref[0:1, :] + jnp.sum(sel, axis=0, keepdims=True)
    carry_ref[...] = jnp.broadcast_to(total, carry_ref.shape)
    cnt_ref[...] = jnp.broadcast_to(total, cnt_ref.shape)


def _route(logits, tm):
    n = logits.shape[0]
    blk = pl.BlockSpec((tm, LANES), lambda i: (i, 0))
    small = pl.BlockSpec((SUBLANES, tm), lambda i: (0, i))
    return pl.pallas_call(
        _route_body,
        grid=(n // tm,),
        in_specs=[blk],
        out_specs=[small, blk, small, pl.BlockSpec((8, LANES), lambda i: (0, 0))],
        out_shape=[jax.ShapeDtypeStruct((SUBLANES, n), I32),
                   jax.ShapeDtypeStruct((n, LANES), F32),
                   jax.ShapeDtypeStruct((SUBLANES, n), I32),
                   jax.ShapeDtypeStruct((8, LANES), F32)],
        scratch_shapes=[pltpu.VMEM((8, LANES), F32)],
        compiler_params=_cparams(1),
        name="route",
    )(logits)


def _dispatch_body(nz_ref, zl_ref, dest_ref, hp_ref, xs_ref, zero_ref, sem, zsem):
    t = hp_ref.shape[0] * hp_ref.shape[1]

    @pl.when(pl.program_id(0) == 0)
    def _():
        zero_ref[...] = jnp.zeros_like(zero_ref)

        def z_copy(b):
            r = pl.multiple_of(zl_ref[b] * MOE_BLOCK, MOE_BLOCK)
            return pltpu.make_async_copy(zero_ref, xs_ref.at[pl.ds(r, MOE_BLOCK)], zsem)

        def start(b, carry):
            z_copy(b).start()
            return carry

        def wait(b, carry):
            z_copy(b).wait()
            return carry

        lax.fori_loop(0, nz_ref[0], start, 0)
        lax.fori_loop(0, nz_ref[0], wait, 0)

    def issue(g, carry):
        for u in range(SUBLANES):
            for k in range(TOP_K):
                d = dest_ref[0, k * t + g * SUBLANES + u]
                pltpu.make_async_copy(hp_ref.at[g, pl.ds(u, 1)], xs_ref.at[pl.ds(d, 1)], sem).start(priority=k % 2)
        return carry

    lax.fori_loop(0, t // SUBLANES, issue, 0)
    for _ in range(TOP_K):
        pltpu.make_async_copy(xs_ref.at[pl.ds(0, t)], xs_ref.at[pl.ds(0, t)], sem).wait()


def _tile_major(slot, tm):
    n = slot.shape[1]
    return slot.reshape(TOP_K, n // tm, tm).transpose(1, 0, 2).reshape(n // tm, 1, TOP_K * tm)


def _dispatch(n_zero, zero_blocks, dest, hp, n_slots, tm):
    n, w = hp.shape
    dest3 = _tile_major(dest, tm)
    return pl.pallas_call(
        _dispatch_body,
        grid_spec=pltpu.PrefetchScalarGridSpec(
            num_scalar_prefetch=2,
            grid=(n // tm,),
            in_specs=[pl.BlockSpec((None, 1, tm * TOP_K), lambda i, nz, zl: (i, 0, 0), memory_space=pltpu.SMEM),
                      pl.BlockSpec((tm // SUBLANES, SUBLANES, w), lambda i, nz, zl: (i, 0, 0))],
            out_specs=pl.BlockSpec(memory_space=pl.ANY),
            scratch_shapes=[pltpu.VMEM((MOE_BLOCK, w), U32),
                            pltpu.SemaphoreType.DMA(()),
                            pltpu.SemaphoreType.DMA(())]),
        out_shape=jax.ShapeDtypeStruct((n_slots, w), U32),
        compiler_params=_cparams(1),
        name="dispatch",
    )(n_zero, zero_blocks, dest3, hp.reshape(n // SUBLANES, SUBLANES, w))


PASS_SUBS = 9
EXPERT_TN = 256
DOT_ROWS = 1024
W_SLOTS = 3
W_AHEAD = 2


def _expert_body(meta_ref, pe_ref, pr_ref, pn_ref, xs_hbm, wgu_hbm, wdn_hbm, bgu_ref, bdn_ref,
                 y_hbm, xraw, xb, h_ref, wg_buf, wu_buf, wd_buf, ybuf, zbuf,
                 sem_x, sem_y, sem_z, sem_w, sem_d):
    sub = MOE_BLOCK
    tn = EXPERT_TN
    tp = tn // 2
    nj = h_ref.shape[0]
    nc = y_hbm.shape[1] // tp
    de = nj * tn
    d2 = xraw.shape[1]
    p = pl.program_id(0)
    n_pass = meta_ref[0]
    nsub = pn_ref[p]
    row0 = pr_ref[p]

    def x_copy(i, pp):
        r = pl.multiple_of((pr_ref[pp] + i) * sub, sub)
        return pltpu.make_async_copy(xs_hbm.at[pl.ds(r, sub)], xraw.at[pl.ds(i * sub, sub)], sem_x)

    def y_copy(start, size, r0, c, slot):
        r = pl.multiple_of(r0 * sub + start, sub)
        col = pl.multiple_of(c * tp, tp)
        return pltpu.make_async_copy(ybuf.at[slot, pl.ds(start, size), :],
                                     y_hbm.at[pl.ds(r, size), pl.ds(col, tp)], sem_y.at[slot])

    def z_copy(b, c):
        r = pl.multiple_of(b * sub, sub)
        return pltpu.make_async_copy(zbuf, y_hbm.at[pl.ds(r, sub), pl.ds(c * tp, tp)], sem_z)

    def w1_copies(pp, j):
        e = pe_ref[pp]
        slot = j % W_SLOTS
        col = pl.multiple_of(j * tn, tn)
        return (pltpu.make_async_copy(wgu_hbm.at[e, :, pl.ds(col, tn)], wg_buf.at[slot], sem_w.at[slot]),
                pltpu.make_async_copy(wgu_hbm.at[e, :, pl.ds(de + col, tn)], wu_buf.at[slot], sem_w.at[slot]))

    def w2_copy(pp, c):
        slot = c % W_SLOTS
        col = pl.multiple_of(c * tn, tn)
        return pltpu.make_async_copy(wdn_hbm.at[pe_ref[pp], :, pl.ds(col, tn)], wd_buf.at[slot], sem_d.at[slot])

    def start_w1(pp, j):
        for cp in w1_copies(pp, j):
            cp.start()

    def for_subs(count, fn):
        for i in range(PASS_SUBS):
            pl.when(i < count)(lambda i=i: fn(i))

    def for_tail(fn):
        def body(b, carry):
            for c in range(nc):
                fn(b, c)
            return carry
        lax.fori_loop(meta_ref[1], y_hbm.shape[0] // sub, body, 0)

    def for_groups(count, fn):
        pl.when(count == PASS_SUBS)(lambda: fn(0, PASS_SUBS * sub))
        k = 1 << (PASS_SUBS.bit_length() - 1)
        while k:
            start = pl.multiple_of((count & (-2 * k)) * sub, sub)
            pl.when((count != PASS_SUBS) & ((count & k) != 0))(lambda start=start, k=k: fn(start, k * sub))
            k //= 2

    @pl.when(p == 0)
    def _():
        for_subs(nsub, lambda i: x_copy(i, p).start())
        for j in range(W_AHEAD):
            start_w1(p, j)
        zbuf[...] = jnp.zeros_like(zbuf)
        for_tail(lambda b, c: z_copy(b, c).start())

    for_subs(nsub, lambda i: x_copy(i, p).wait())

    def unpack(i):
        rows = slice(i * sub, (i + 1) * sub)
        lo, hi = _unpack_pair_f32(xraw[rows, :])
        xb[rows, :d2] = lo.astype(BF16)
        xb[rows, d2:] = hi.astype(BF16)

    for_subs(nsub, unpack)

    @pl.when(p + 1 < n_pass)
    def _():
        for_subs(pn_ref[p + 1], lambda i: x_copy(i, p + 1).start())

    def first_step(j, carry):
        slot = j % W_SLOTS
        for cp in w1_copies(p, j):
            cp.wait()
        nxt = j + W_AHEAD
        pl.when(nxt < nj)(lambda: start_w1(p, nxt))
        pl.when(nxt >= nj)(lambda: w2_copy(p, nxt - nj).start())

        def group(start, size):
            wg = wg_buf[slot].astype(BF16)
            wu = wu_buf[slot].astype(BF16)
            dr = DOT_ROWS if size % DOT_ROWS == 0 else size
            for r in range(0, size, dr):
                rows = pl.ds(start + r, dr)
                x = xb[rows, :]
                gate = jnp.minimum(_dot(x, wg) + bgu_ref[j], SWIGLU_LIMIT)
                up = jnp.clip(_dot(x, wu) + bgu_ref[nj + j], -SWIGLU_LIMIT, SWIGLU_LIMIT)
                h_ref[j, rows, :] = ((up + 1.0) * gate * jax.nn.sigmoid(SWIGLU_ALPHA * gate)).astype(BF16)

        for_groups(nsub, group)
        return carry

    lax.fori_loop(0, nj, first_step, 0)

    def second_step(c, carry):
        wslot = c % W_SLOTS
        slot = c % 2
        w2_copy(p, c).wait()
        nxt = c + W_AHEAD
        pl.when(nxt < nc)(lambda: w2_copy(p, nxt).start())
        pl.when((nxt >= nc) & (p + 1 < n_pass))(lambda: start_w1(p + 1, nxt - nc))

        @pl.when(c >= 2)
        def _():
            for_groups(nsub, lambda start, size: y_copy(start, size, row0, c - 2, slot).wait())

        @pl.when((c < 2) & (p > 0))
        def _():
            for_groups(pn_ref[p - 1],
                       lambda start, size: y_copy(start, size, pr_ref[p - 1], nc - 2 + c, slot).wait())

        def group(start, size):
            wd = wd_buf[wslot].astype(BF16)
            dr = DOT_ROWS if size % DOT_ROWS == 0 else size
            for r in range(0, size, dr):
                rows = pl.ds(start + r, dr)
                hx = jnp.concatenate([h_ref[j, rows, :] for j in range(nj)], axis=1)
                y = _dot(hx, wd) + bdn_ref[c]
                ybuf[slot, rows, :] = _pack_bf16_pair(y[:, :tp], y[:, tp:])
            y_copy(start, size, row0, c, slot).start()

        for_groups(nsub, group)
        return carry

    lax.fori_loop(0, nc, second_step, 0)

    @pl.when(p == n_pass - 1)
    def _():
        for c in (nc - 2, nc - 1):
            for_groups(nsub, lambda start, size, c=c: y_copy(start, size, row0, c, c % 2).wait())
        for_tail(lambda b, cc: z_copy(b, cc).wait())


def _experts(meta, pass_e, pass_row0, pass_nsub, xs, w_gu, b_gu, w_dn, b_dn):
    n_slots, d2 = xs.shape
    d = 2 * d2
    n_e, de = w_dn.shape[0], w_dn.shape[1]
    tn = EXPERT_TN
    nj = de // tn
    nc = d // tn
    assert W_AHEAD < W_SLOTS and W_AHEAD <= min(nj, nc) and nc % 2 == 0
    rmax = PASS_SUBS * MOE_BLOCK
    return pl.pallas_call(
        _expert_body,
        grid_spec=pltpu.PrefetchScalarGridSpec(
            num_scalar_prefetch=4,
            grid=(meta[0],),
            in_specs=[
                pl.BlockSpec(memory_space=pl.ANY),
                pl.BlockSpec(memory_space=pl.ANY),
                pl.BlockSpec(memory_space=pl.ANY),
                pl.BlockSpec((None, 2 * nj, 1, tn), lambda p, m, e, r, n: (e[p], 0, 0, 0)),
                pl.BlockSpec((None, nc, 1, tn), lambda p, m, e, r, n: (e[p], 0, 0, 0)),
            ],
            out_specs=pl.BlockSpec(memory_space=pl.ANY),
            scratch_shapes=[pltpu.VMEM((rmax, d2), U32),
                            pltpu.VMEM((rmax, d), BF16),
                            pltpu.VMEM((nj, rmax, tn), BF16),
                            pltpu.VMEM((W_SLOTS, d, tn), w_gu.dtype),
                            pltpu.VMEM((W_SLOTS, d, tn), w_gu.dtype),
                            pltpu.VMEM((W_SLOTS, de, tn), w_dn.dtype),
                            pltpu.VMEM((2, rmax, tn // 2), U32),
                            pltpu.VMEM((MOE_BLOCK, tn // 2), U32),
                            pltpu.SemaphoreType.DMA(()),
                            pltpu.SemaphoreType.DMA((2,)),
                            pltpu.SemaphoreType.DMA(()),
                            pltpu.SemaphoreType.DMA((W_SLOTS,)),
                            pltpu.SemaphoreType.DMA((W_SLOTS,))]),
        out_shape=jax.ShapeDtypeStruct((n_slots, d2), U32),
        compiler_params=_cparams(1),
        name="experts",
    )(meta, pass_e, pass_row0, pass_nsub, xs, w_gu, w_dn,
      b_gu.reshape(n_e, 2 * nj, 1, tn), b_dn.reshape(n_e, nc, 1, tn))


COMBINE_CHUNK = 64


def _combine_body(dcur_ref, dnxt_ref, wt_ref, h1_ref, g2_ref, l2g_ref, l2b_ref, y_ref, out_ref, buf_ref, sem):
    i = pl.program_id(0)
    n_tiles = pl.num_programs(0)
    t = h1_ref.shape[0]
    tp = EXPERT_TN // 2
    slot = i % 2

    def issue(dest_ref, sl, g):
        for u in range(SUBLANES):
            for k in range(TOP_K):
                d = dest_ref[0, k * t + g * SUBLANES + u]
                pltpu.make_async_copy(y_ref.at[pl.ds(d, 1)], buf_ref.at[sl, k, g, pl.ds(u, 1)],
                                      sem.at[sl]).start(priority=k % 2)

    gpc = COMBINE_CHUNK // SUBLANES

    def reduce_rows(j):
        rows = pl.ds(pl.multiple_of(j * COMBINE_CHUNK, COMBINE_CHUNK), COMBINE_CHUNK)
        groups = pl.ds(pl.multiple_of(j * gpc, gpc), gpc)
        wt = wt_ref[rows, :]
        f_lo, f_hi = None, None
        for k in range(TOP_K):
            lo, hi = _unpack_pair_f32(buf_ref[slot, k, groups].reshape(COMBINE_CHUNK, buf_ref.shape[-1]))
            w = wt[:, k:k + 1]
            f_lo = lo * w if f_lo is None else f_lo + lo * w
            f_hi = hi * w if f_hi is None else f_hi + hi * w
        parts = []
        for c in range(f_lo.shape[1] // tp):
            parts += [f_lo[:, c * tp:(c + 1) * tp], f_hi[:, c * tp:(c + 1) * tp]]
        f = jnp.concatenate(parts, axis=1)
        out_ref[rows, :] = _layer_norm(DEEPNORM_ALPHA * h1_ref[rows, :] + g2_ref[...] * f,
                                       l2g_ref[...], l2b_ref[...])

    def first_gather(g, carry):
        issue(dcur_ref, 0, g)
        return carry

    @pl.when(i == 0)
    def _():
        lax.fori_loop(0, t // SUBLANES, first_gather, 0)

    for k in range(TOP_K):
        pltpu.make_async_copy(buf_ref.at[1 - slot, k], buf_ref.at[slot, k], sem.at[slot]).wait()

    def both(j, carry):
        reduce_rows(j)
        for w in range(gpc):
            issue(dnxt_ref, 1 - slot, j * gpc + w)
        return carry

    def only_reduce(j, carry):
        reduce_rows(j)
        return carry

    @pl.when(i + 1 < n_tiles)
    def _():
        lax.fori_loop(0, t // COMBINE_CHUNK, both, 0)

    @pl.when(i + 1 == n_tiles)
    def _():
        lax.fori_loop(0, t // COMBINE_CHUNK, only_reduce, 0)


def _combine(dest, wt, h1, mod3, rows_per_batch, l2g, l2b, y, tm):
    n, d = h1.shape
    n_tiles = n // tm
    dest3 = _tile_major(dest, tm)
    row = lambda i: (i * tm) // rows_per_batch
    return pl.pallas_call(
        _combine_body,
        grid=(n_tiles,),
        in_specs=[pl.BlockSpec((None, 1, tm * TOP_K), lambda i: (i, 0, 0), memory_space=pltpu.SMEM),
                  pl.BlockSpec((None, 1, tm * TOP_K), lambda i: (jnp.minimum(i + 1, n_tiles - 1), 0, 0),
                               memory_space=pltpu.SMEM),
                  pl.BlockSpec((tm, LANES), lambda i: (i, 0)),
                  pl.BlockSpec((tm, d), lambda i: (i, 0)),
                  pl.BlockSpec((None, 1, d), lambda i: (row(i), 0, 5)),
                  pl.BlockSpec((1, d), lambda i: (0, 0)),
                  pl.BlockSpec((1, d), lambda i: (0, 0)),
                  pl.BlockSpec(memory_space=pl.ANY)],
        out_specs=pl.BlockSpec((tm, d), lambda i: (i, 0)),
        out_shape=jax.ShapeDtypeStruct((n, d), F32),
        scratch_shapes=[pltpu.VMEM((2, TOP_K, tm // SUBLANES, SUBLANES, y.shape[1]), U32),
                        pltpu.SemaphoreType.DMA((2,))],
        compiler_params=_cparams(1),
        name="combine",
    )(dest3, dest3, wt, h1, mod3, l2g, l2b, y)


def _pick_tile(n, pref):
    t = pref
    while n % t:
        t //= 2
    return t


def kernel(x, c, ctx, c_ctx, ln_in_g, ln_in_b, w_ada, b_ada, w_in, cm_norm_g, cm_norm_b, cm_w_s, cm_b_s,
           gla_w_gk_f, gla_b_gk_f, gla_w_gk_b, gla_b_gk_b, gla_norm_g, w_out, ln1_g, ln1_b,
           w_router, b_router, w_gate_up, b_gate_up, w_down, b_down, ln2_g, ln2_b):
    bsz, l, d = x.shape
    lc = ctx.shape[1]
    n, nc = bsz * l, bsz * lc
    assert w_ada.shape[0] == 1, "single-layer configuration"
    assert bsz + 1 <= 8 and l % (2 * GLA_TILE) == 0 and lc % GLA_TILE == 0
    row = lambda v: v.reshape(1, -1)

    cc = jnp.concatenate([c, c_ctx[None, :], jnp.zeros((8 - bsz - 1, d), F32)], axis=0)
    mod3 = _ada(cc, w_ada[0], row(b_ada[0])).reshape(8, 1, N_MOD * d)

    n_uv = 2 * CM_HEADS * CM_CHUNK
    n_main = n_uv + 2 * GLA_HEADS * GLA_DK + 2 * GLA_HEADS * GLA_DV
    w_uv = w_in[0][:, :n_uv].astype(BF16)
    w_rest = w_in[0][:, n_uv:n_main].astype(BF16)
    w_lr = jnp.pad(w_in[0][:, n_main:].astype(BF16), ((0, 0), (0, LANES - 2 * GLA_RANK)))
    x2 = x.reshape(n, d)
    tm_x = _pick_tile(l, 512)
    bs_tile = jnp.repeat(cm_b_s[0].T, CM_CHUNK, axis=1)
    cm, p, lr = _inproj_cm(x2, mod3, l, row(ln_in_g), row(ln_in_b), w_uv, w_rest, w_lr,
                           row(cm_norm_g[0]), row(cm_norm_b[0]), cm_w_s[0].astype(BF16), bs_tile, tm_x)
    tm_c = _pick_tile(nc, 512)
    pc, lrc = _inproj(ctx.reshape(nc, d), mod3, lambda i: bsz, row(ln_in_g), row(ln_in_b),
                      w_rest, w_lr, 0, 2, tm_c, 1024)

    kw = GLA_HEADS * GLA_DK
    wgf = jnp.zeros((LANES, kw), BF16).at[:GLA_RANK].set(gla_w_gk_f[0].astype(BF16))
    wgb = jnp.zeros((LANES, kw), BF16).at[GLA_RANK:2 * GLA_RANK].set(gla_w_gk_b[0].astype(BF16))
    gla = _gla(p.reshape(bsz, l, -1), lr.reshape(bsz, l, LANES), pc.reshape(bsz, lc, -1),
               lrc.reshape(bsz, lc, LANES), wgf, row(gla_b_gk_f[0]), wgb, row(gla_b_gk_b[0]),
               row(gla_norm_g[0])).reshape(n, -1)

    w_r = jnp.pad(w_router[0], ((0, 0), (0, LANES - N_EXPERTS))).astype(BF16)
    b_r = jnp.pad(b_router[0], (0, LANES - N_EXPERTS)).reshape(1, LANES)
    h1, hp, logits = _post_attn(cm, gla, x2, mod3, l, row(ln_in_g), row(ln_in_b), w_out[0].astype(BF16),
                                row(ln1_g[0]), row(ln1_b[0]), w_r, b_r, tm_x)

    idx, wt, rank, cnt = _route(logits, _pick_tile(n, 1024))
    counts = cnt[0, :N_EXPERTS].astype(I32)
    n_blocks = (n * TOP_K + N_EXPERTS * (MOE_BLOCK - 1)) // MOE_BLOCK
    blocks_e = (counts + MOE_BLOCK - 1) // MOE_BLOCK
    blk_end = jnp.cumsum(blocks_e)
    blk_start = blk_end - blocks_e
    dest = rank[:TOP_K]
    for e in range(N_EXPERTS):
        dest = dest + jnp.where(idx[:TOP_K] == e, blk_start[e] * MOE_BLOCK, 0)
    n_pass_max = n_blocks // PASS_SUBS + N_EXPERTS
    pass_cnt = (blocks_e + PASS_SUBS - 1) // PASS_SUBS
    pass_end = jnp.cumsum(pass_cnt)
    pass_start = pass_end - pass_cnt
    pid = jnp.arange(n_pass_max, dtype=I32)
    pass_e = jnp.minimum(jnp.searchsorted(pass_end, pid, side="right"), N_EXPERTS - 1).astype(I32)
    local = pid - pass_start[pass_e]
    pass_row0 = (blk_start[pass_e] + local * PASS_SUBS).astype(I32)
    pass_nsub = jnp.clip(blocks_e[pass_e] - local * PASS_SUBS, 0, PASS_SUBS).astype(I32)
    meta = jnp.stack([pass_end[-1], blk_end[-1]]).astype(I32)

    ar = jnp.arange(N_EXPERTS, dtype=I32)
    cand = jnp.concatenate([jnp.where(counts % MOE_BLOCK != 0, blk_end - 1, -1),
                            jnp.where(blk_end[-1] + ar < n_blocks, blk_end[-1] + ar, -1)])
    zero_blocks = cand[jnp.argsort(cand < 0, stable=True)].astype(I32)
    n_zero = jnp.sum(cand >= 0).astype(I32).reshape(1)

    xs = _dispatch(n_zero, zero_blocks, dest, hp, n_blocks * MOE_BLOCK, _pick_tile(n, 512))
    ys = _experts(meta, pass_e, pass_row0, pass_nsub, xs, w_gate_up[0], b_gate_up[0], w_down[0], b_down[0])
    out = _combine(dest, wt, h1, mod3, l, row(ln2_g[0]), row(ln2_b[0]), ys, _pick_tile(n, 512))
    return out.reshape(bsz, l, d)
```

```python
import math

import jax
import jax.numpy as jnp
from jax import lax
from jax.experimental import pallas as pl
from jax.experimental.pallas import tpu as pltpu

F32 = jnp.float32
BF16 = jnp.bfloat16
U32 = jnp.uint32
I32 = jnp.int32

CM_CHUNK = 128
CM_HEADS = 8
GLA_HEADS = 4
GLA_DK = 128
GLA_DV = 256
GLA_CHUNK = 64
GLA_RANK = 16
GLA_GATE_NORMALIZER = 16.0
N_EXPERTS = 32
TOP_K = 4
MOE_BLOCK = 256
SWIGLU_LIMIT = 7.0
SWIGLU_ALPHA = 1.702
N_MOD = 6
DEEPNORM_ALPHA = 2.0 ** 0.25
LN_EPS = 1e-5
RMS_EPS = 1e-6

LANES = 128
SUBLANES = 8
VMEM_LIMIT = 56 * 1024 * 1024
ROW_CHUNK = 256


def _cparams(n_axes, vmem=VMEM_LIMIT):
    return pltpu.CompilerParams(dimension_semantics=("arbitrary",) * n_axes,
                                vmem_limit_bytes=vmem)


def _layer_norm(t, g, b):
    mu = jnp.mean(t, axis=-1, keepdims=True)
    d = t - mu
    var = jnp.mean(d * d, axis=-1, keepdims=True)
    return d * lax.rsqrt(var + LN_EPS) * g + b


def _gelu(t):
    return 0.5 * t * (1.0 + lax.erf(t * (1.0 / math.sqrt(2.0))))


def _silu(t):
    return t * jax.nn.sigmoid(t)


def _ones_where(mask, dtype):
    return jnp.where(mask, 1.0, 0.0).astype(dtype)


def _dot(a, b):
    return jnp.dot(a, b, preferred_element_type=F32)


def _dot_nt(a, b):
    return lax.dot_general(a, b, (((1,), (1,)), ((), ())), preferred_element_type=F32)


def _dot_tn(a, b):
    return lax.dot_general(a, b, (((0,), (0,)), ((), ())), preferred_element_type=F32)


def _pack_bf16_pair(lo, hi):
    lo_b = lax.bitcast_convert_type(lo.astype(BF16).astype(F32), U32)
    hi_b = lax.bitcast_convert_type(hi.astype(BF16).astype(F32), U32)
    return hi_b | (lo_b >> 16)


def _unpack_pair_f32(p):
    lo = lax.bitcast_convert_type(p << 16, F32)
    hi = lax.bitcast_convert_type(p & jnp.uint32(0xFFFF0000), F32)
    return lo, hi


def _ada_body(c_ref, w_ref, b_ref, o_ref):
    a = _silu(c_ref[...]).astype(BF16)
    o_ref[...] = _dot(a, w_ref[...].astype(BF16)) + b_ref[...]


def _ada(cc, w, b):
    rows, d = cc.shape
    n = w.shape[1]
    tn = 1024
    return pl.pallas_call(
        _ada_body,
        grid=(n // tn,),
        in_specs=[pl.BlockSpec((rows, d), lambda j: (0, 0)),
                  pl.BlockSpec((d, tn), lambda j: (0, j)),
                  pl.BlockSpec((1, tn), lambda j: (0, j))],
        out_specs=pl.BlockSpec((rows, tn), lambda j: (0, j)),
        out_shape=jax.ShapeDtypeStruct((rows, n), F32),
        compiler_params=_cparams(1),
        name="ada",
    )(cc, w, b)


def _inproj_body(x_ref, g_ref, b_ref, sh_ref, sc_ref, w_ref, wlr_ref, o_ref, olr_ref, hm_ref):
    @pl.when(pl.program_id(1) == 0)
    def _():
        h = _layer_norm(x_ref[...], g_ref[...], b_ref[...])
        hm = (h * (1.0 + sc_ref[...]) + sh_ref[...]).astype(BF16)
        hm_ref[...] = hm
        olr_ref[...] = _dot(hm, wlr_ref[...])

    o_ref[...] = _dot(hm_ref[...], w_ref[...])


def _inproj(x2, mod3, mod_row, ln_g, ln_b, w_main, w_lr, col0, ncols, tm, tn):
    r, d = x2.shape
    return pl.pallas_call(
        _inproj_body,
        grid=(r // tm, ncols),
        in_specs=[pl.BlockSpec((tm, d), lambda i, j: (i, 0)),
                  pl.BlockSpec((1, d), lambda i, j: (0, 0)),
                  pl.BlockSpec((1, d), lambda i, j: (0, 0)),
                  pl.BlockSpec((None, 1, d), lambda i, j: (mod_row(i), 0, 0)),
                  pl.BlockSpec((None, 1, d), lambda i, j: (mod_row(i), 0, 1)),
                  pl.BlockSpec((d, tn), lambda i, j: (0, col0 + j)),
                  pl.BlockSpec((d, LANES), lambda i, j: (0, 0))],
        out_specs=[pl.BlockSpec((tm, tn), lambda i, j: (i, j)),
                   pl.BlockSpec((tm, LANES), lambda i, j: (i, 0))],
        out_shape=[jax.ShapeDtypeStruct((r, ncols * tn), F32),
                   jax.ShapeDtypeStruct((r, LANES), F32)],
        scratch_shapes=[pltpu.VMEM((tm, d), BF16)],
        compiler_params=_cparams(2),
        name="inproj",
    )(x2, ln_g, ln_b, mod3, mod3, w_main, w_lr)


def _inproj_cm_body(x_ref, g_ref, b_ref, sh_ref, sc_ref, wuv_ref, wr_ref, wlr_ref, ng_ref, nb_ref, ws_ref, bs_ref,
                    cm_ref, p_ref, lr_ref):
    tm = x_ref.shape[0]
    half = wuv_ref.shape[1] // 2
    hd = CM_CHUNK
    for r in range(0, tm, ROW_CHUNK):
        rows = slice(r, r + ROW_CHUNK)
        h = _layer_norm(x_ref[rows, :], g_ref[...], b_ref[...])
        hm = (h * (1.0 + sc_ref[...]) + sh_ref[...]).astype(BF16)
        uv = _dot(hm, wuv_ref[...])
        u = _gelu(uv[:, :half])
        vb = _layer_norm(_gelu(uv[:, half:]), ng_ref[...], nb_ref[...]).astype(BF16)
        for c in range(ROW_CHUNK // CM_CHUNK):
            crow = slice(c * CM_CHUNK, (c + 1) * CM_CHUNK)
            orow = slice(r + c * CM_CHUNK, r + (c + 1) * CM_CHUNK)
            for hh in range(CM_HEADS):
                cols = slice(hh * hd, (hh + 1) * hd)
                s = _dot(ws_ref[hh], vb[crow, cols]) + bs_ref[:, cols]
                cm_ref[orow, cols] = (u[crow, cols] * s).astype(BF16)
        p_ref[rows, :] = _dot(hm, wr_ref[...])
        lr_ref[rows, :] = _dot(hm, wlr_ref[...])


def _inproj_cm(x2, mod3, rows_per_batch, ln_g, ln_b, w_uv, w_rest, w_lr, ng, nb, ws, bs, tm):
    r, d = x2.shape
    n_uv, n_rest = w_uv.shape[1], w_rest.shape[1]
    row = lambda i: (i * tm) // rows_per_batch

    def const(shape):
        return pl.BlockSpec(shape, lambda i: (0,) * len(shape), pipeline_mode=pl.Buffered(1))

    return pl.pallas_call(
        _inproj_cm_body,
        grid=(r // tm,),
        in_specs=[pl.BlockSpec((tm, d), lambda i: (i, 0)),
                  const((1, d)), const((1, d)),
                  pl.BlockSpec((None, 1, d), lambda i: (row(i), 0, 0)),
                  pl.BlockSpec((None, 1, d), lambda i: (row(i), 0, 1)),
                  const((d, n_uv)), const((d, n_rest)), const((d, LANES)),
                  const((1, n_uv // 2)), const((1, n_uv // 2)),
                  const((CM_HEADS, CM_CHUNK, CM_CHUNK)), const((CM_CHUNK, n_uv // 2))],
        out_specs=[pl.BlockSpec((tm, n_uv // 2), lambda i: (i, 0)),
                   pl.BlockSpec((tm, n_rest), lambda i: (i, 0)),
                   pl.BlockSpec((tm, LANES), lambda i: (i, 0))],
        out_shape=[jax.ShapeDtypeStruct((r, n_uv // 2), BF16),
                   jax.ShapeDtypeStruct((r, n_rest), F32),
                   jax.ShapeDtypeStruct((r, LANES), F32)],
        compiler_params=_cparams(1),
        name="inproj_cm",
    )(x2, ln_g, ln_b, mod3, mod3, w_uv, w_rest, w_lr, ng, nb, ws, bs)


GLA_TILE = 256
GLA_TILES_PER_ITER = 4


def _gla_tile(q, k, v, lr, wg, bg, st_ref, forward, need_o):
    t = k.shape[0]
    n_chunks = t // GLA_CHUNK
    z = _dot(lr.astype(BF16), wg) + bg
    g = jax.nn.log_sigmoid(z) * (1.0 / GLA_GATE_NORMALIZER)
    r_id = lax.broadcasted_iota(I32, (t, t), 0)
    c_id = lax.broadcasted_iota(I32, (t, t), 1)
    shift = GLA_CHUNK.bit_length() - 1
    same = (r_id >> shift) == (c_id >> shift)
    lower = same & (c_id <= r_id)
    tri = _ones_where(lower, BF16)
    g_hi = g.astype(BF16)
    g_lo = (g - g_hi.astype(F32)).astype(BF16)
    csum2 = _dot(tri, jnp.concatenate([g_hi, g_lo], axis=1))
    csum = csum2[:, :GLA_DK] + csum2[:, GLA_DK:]
    g3 = g.reshape(n_chunks, GLA_CHUNK, GLA_DK)
    tot = jnp.broadcast_to(jnp.sum(g3, axis=1, keepdims=True), g3.shape).reshape(t, GLA_DK)
    bcum = csum if forward else tot - csum + g
    kd = (k * jnp.exp(tot - bcum)).astype(BF16)
    decay = jnp.exp(tot)
    vb = v.astype(BF16)
    row_chunk = lax.broadcasted_iota(I32, (t, GLA_DK), 0) >> shift
    kd_blocks = jnp.concatenate([jnp.where(row_chunk == c, kd, jnp.zeros_like(kd)) for c in range(n_chunks)], axis=1)
    u_all = _dot_tn(vb, kd_blocks)
    o = None
    if need_o:
        qe = ((q * (GLA_DK ** -0.5)) * jnp.exp(bcum)).astype(BF16)
        ke = (k * jnp.exp(-bcum)).astype(BF16)
        att = _dot_nt(qe, ke)
        mask = lower if forward else same & (c_id >= r_id)
        att = jnp.where(mask, att, 0.0).astype(BF16)
        o = _dot(att, vb)
    outs = [None] * n_chunks
    order = range(n_chunks) if forward else range(n_chunks - 1, -1, -1)
    for c in order:
        rows = slice(c * GLA_CHUNK, (c + 1) * GLA_CHUNK)
        s_t = st_ref[...]
        if need_o:
            outs[c] = o[rows] + _dot_nt(qe[rows], s_t.astype(BF16))
        u_t = u_all[:, c * GLA_DK:(c + 1) * GLA_DK]
        st_ref[...] = s_t * decay[c * GLA_CHUNK:c * GLA_CHUNK + 1, :] + u_t
    if need_o:
        return jnp.concatenate(outs, axis=0)
    return None


def _gla_body(q_ref, k_ref, v_ref, go_ref, lr_ref, kc_ref, vc_ref, lrc_ref,
              wgf_ref, bgf_ref, wgb_ref, bgb_ref, ng_ref, out_ref, o_scr, sf_ref, sb_ref):
    t = GLA_TILE
    n_x = q_ref.shape[0] // t
    n_c = kc_ref.shape[0] // t
    half = n_x // 2
    sf_ref[...] = jnp.zeros_like(sf_ref)
    sb_ref[...] = jnp.zeros_like(sb_ref)
    wgf, bgf, wgb, bgb = wgf_ref[...], bgf_ref[...], wgb_ref[...], bgb_ref[...]

    for i in range(n_c):
        rf = slice(i * t, (i + 1) * t)
        rb = slice((n_c - 1 - i) * t, (n_c - i) * t)
        _gla_tile(None, kc_ref[rf], vc_ref[rf], lrc_ref[rf], wgf, bgf, sf_ref, True, False)
        _gla_tile(None, kc_ref[rb], vc_ref[rb], lrc_ref[rb], wgb, bgb, sb_ref, False, False)

    def tile_out(i, forward):
        rows = pl.ds(pl.multiple_of(i * t, t), t)
        if forward:
            return rows, _gla_tile(q_ref[rows], k_ref[rows], v_ref[rows], lr_ref[rows],
                                   wgf, bgf, sf_ref, True, True)
        return rows, _gla_tile(q_ref[rows], k_ref[rows], v_ref[rows], lr_ref[rows],
                               wgb, bgb, sb_ref, False, True)

    def finish(rows, o):
        o = o + o_scr[rows]
        ms = jnp.mean(o * o, axis=-1, keepdims=True)
        on = o * lax.rsqrt(ms + RMS_EPS) * ng_ref[...]
        out_ref[rows] = (on * _silu(go_ref[rows])).astype(BF16)

    def keep(rows, o):
        o_scr[rows] = o

    u = math.gcd(GLA_TILES_PER_ITER, half)

    def make_step(sink):
        def step(it, carry):
            for w in range(u):
                i = it * u + w
                sink(*tile_out(i, True))
                sink(*tile_out(n_x - 1 - i, False))
            return carry
        return step

    lax.fori_loop(0, half // u, make_step(keep), 0)
    lax.fori_loop(half // u, n_x // u, make_step(finish), 0)


def _gla(p3, lr3, pc3, lrc3, wgf, bgf, wgb, bgb, ng):
    bsz, l, _ = p3.shape
    lc = pc3.shape[1]
    dk, dv = GLA_DK, GLA_DV
    kw = GLA_HEADS * dk
    q0, k0 = 0, kw // dk
    v0, go0 = 2 * kw // dv, (2 * kw + GLA_HEADS * dv) // dv
    kc0, vc0 = k0, v0
    return pl.pallas_call(
        _gla_body,
        grid=(bsz, GLA_HEADS),
        in_specs=[pl.BlockSpec((None, l, dk), lambda b, h: (b, 0, q0 + h)),
                  pl.BlockSpec((None, l, dk), lambda b, h: (b, 0, k0 + h)),
                  pl.BlockSpec((None, l, dv), lambda b, h: (b, 0, v0 + h)),
                  pl.BlockSpec((None, l, dv), lambda b, h: (b, 0, go0 + h)),
                  pl.BlockSpec((None, l, LANES), lambda b, h: (b, 0, 0)),
                  pl.BlockSpec((None, lc, dk), lambda b, h: (b, 0, kc0 + h)),
                  pl.BlockSpec((None, lc, dv), lambda b, h: (b, 0, vc0 + h)),
                  pl.BlockSpec((None, lc, LANES), lambda b, h: (b, 0, 0)),
                  pl.BlockSpec((LANES, dk), lambda b, h: (0, h)),
                  pl.BlockSpec((1, dk), lambda b, h: (0, h)),
                  pl.BlockSpec((LANES, dk), lambda b, h: (0, h)),
                  pl.BlockSpec((1, dk), lambda b, h: (0, h)),
                  pl.BlockSpec((1, dv), lambda b, h: (0, 0))],
        out_specs=pl.BlockSpec((None, l, dv), lambda b, h: (b, 0, h)),
        out_shape=jax.ShapeDtypeStruct((bsz, l, GLA_HEADS * dv), BF16),
        scratch_shapes=[pltpu.VMEM((l, dv), F32),
                        pltpu.VMEM((dv, dk), F32),
                        pltpu.VMEM((dv, dk), F32)],
        compiler_params=_cparams(2),
        name="gla",
    )(p3, p3, p3, p3, lr3, pc3, pc3, lrc3, wgf, bgf, wgb, bgb, ng)


def _post_body(cm_ref, gla_ref, x_ref, lng_ref, lnb_ref, g1_ref, sh2_ref, sc2_ref,
               wo_ref, l1g_ref, l1b_ref, wr_ref, br_ref, h1_ref, hp_ref, lg_ref):
    half = cm_ref.shape[1]
    d2 = x_ref.shape[1] // 2
    for r in range(0, x_ref.shape[0], ROW_CHUNK):
        rows = slice(r, r + ROW_CHUNK)
        y = _dot(cm_ref[rows, :], wo_ref[:half, :]) + _dot(gla_ref[rows, :], wo_ref[half:, :])
        hx = _layer_norm(x_ref[rows, :], lng_ref[...], lnb_ref[...])
        h1 = _layer_norm(DEEPNORM_ALPHA * hx + g1_ref[...] * y, l1g_ref[...], l1b_ref[...])
        h1_ref[rows, :] = h1
        hm = h1 * (1.0 + sc2_ref[...]) + sh2_ref[...]
        hp_ref[rows, :] = _pack_bf16_pair(hm[:, :d2], hm[:, d2:])
        lg_ref[rows, :] = _dot(hm.astype(BF16), wr_ref[...]) + br_ref[...]


def _post_attn(cm, gla, x2, mod3, rows_per_batch, ln_g, ln_b, w_out, l1g, l1b, w_r, b_r, tm):
    r, d = x2.shape
    half = d // 2
    row = lambda i: (i * tm) // rows_per_batch
    full = lambda shape: pl.BlockSpec(shape, lambda i: (0,) * len(shape))
    return pl.pallas_call(
        _post_body,
        grid=(r // tm,),
        in_specs=[pl.BlockSpec((tm, half), lambda i: (i, 0)),
                  pl.BlockSpec((tm, half), lambda i: (i, 0)),
                  pl.BlockSpec((tm, d), lambda i: (i, 0)),
                  full((1, d)), full((1, d)),
                  pl.BlockSpec((None, 1, d), lambda i: (row(i), 0, 2)),
                  pl.BlockSpec((None, 1, d), lambda i: (row(i), 0, 3)),
                  pl.BlockSpec((None, 1, d), lambda i: (row(i), 0, 4)),
                  full((d, d)), full((1, d)), full((1, d)),
                  full((d, LANES)), full((1, LANES))],
        out_specs=[pl.BlockSpec((tm, d), lambda i: (i, 0)),
                   pl.BlockSpec((tm, half), lambda i: (i, 0)),
                   pl.BlockSpec((tm, LANES), lambda i: (i, 0))],
        out_shape=[jax.ShapeDtypeStruct((r, d), F32),
                   jax.ShapeDtypeStruct((r, half), U32),
                   jax.ShapeDtypeStruct((r, LANES), F32)],
        compiler_params=_cparams(1),
        name="post_attn",
    )(cm, gla, x2, ln_g, ln_b, mod3, mod3, mod3, w_out, l1g, l1b, w_r, b_r)


def _route_body(lg_ref, idx_ref, wt_ref, rank_ref, cnt_ref, carry_ref):
    i = pl.program_id(0)
    t = lg_ref.shape[0]

    @pl.when(i == 0)
    def _():
        carry_ref[...] = jnp.zeros_like(carry_ref)

    lane = lax.broadcasted_iota(I32, (t, LANES), 1)
    lane_f = lane.astype(F32)
    neg = jnp.float32(-jnp.inf)
    l = jnp.where(lane < N_EXPERTS, lg_ref[...], neg)
    tops, onehots, idxs = [], [], []
    for _ in range(TOP_K):
        m = jnp.max(l, axis=-1, keepdims=True)
        idx = jnp.min(jnp.where(l == m, lane_f, float(LANES)), axis=-1, keepdims=True).astype(I32)
        oh = lane == idx
        l = jnp.where(oh, neg, l)
        tops.append(m)
        idxs.append(idx)
        onehots.append(oh)
    exps = [jnp.exp(m - tops[0]) for m in tops]
    denom = exps[0] + exps[1] + exps[2] + exps[3]
    sel = _ones_where(onehots[0] | onehots[1] | onehots[2] | onehots[3], F32)
    r_id = lax.broadcasted_iota(I32, (t, t), 0)
    c_id = lax.broadcasted_iota(I32, (t, t), 1)
    strict = _ones_where(c_id < r_id, BF16)
    before = _dot(strict, sel.astype(BF16)) + carry_ref[0:1, :]
    idx_out = jnp.zeros((t, LANES), I32)
    wt_out = jnp.zeros((t, LANES), F32)
    rank_out = jnp.zeros((t, LANES), I32)
    for k in range(TOP_K):
        rk = jnp.sum(jnp.where(onehots[k], before, 0.0), axis=-1, keepdims=True).astype(I32)
        idx_out = jnp.where(lane == k, idxs[k], idx_out)
        wt_out = jnp.where(lane == k, exps[k] / denom, wt_out)
        rank_out = jnp.where(lane == k, rk, rank_out)
    idx_ref[...] = jnp.transpose(idx_out)[:SUBLANES, :]
    wt_ref[...] = wt_out
    rank_ref[...] = jnp.transpose(rank_out)[:SUBLANES, :]
    total = carry_ref[0:1, :] + jnp.sum(sel, axis=0, keepdims=True)
    carry_ref[...] = jnp.broadcast_to(total, carry_ref.shape)
    cnt_ref[...] = jnp.broadcast_to(total, cnt_ref.shape)


def _route(logits, tm):
    n = logits.shape[0]
    blk = pl.BlockSpec((tm, LANES), lambda i: (i, 0))
    small = pl.BlockSpec((SUBLANES, tm), lambda i: (0, i))
    return pl.pallas_call(
        _route_body,
        grid=(n // tm,),
        in_specs=[blk],
        out_specs=[small, blk, small, pl.BlockSpec((8, LANES), lambda i: (0, 0))],
        out_shape=[jax.ShapeDtypeStruct((SUBLANES, n), I32),
                   jax.ShapeDtypeStruct((n, LANES), F32),
                   jax.ShapeDtypeStruct((SUBLANES, n), I32),
                   jax.ShapeDtypeStruct((8, LANES), F32)],
        scratch_shapes=[pltpu.VMEM((8, LANES), F32)],
        compiler_params=_cparams(1),
        name="route",
    )(logits)


def _dispatch_body(nz_ref, zl_ref, dest_ref, hp_ref, xs_ref, zero_ref, sem, zsem):
    t = hp_ref.shape[0] * hp_ref.shape[1]

    @pl.when(pl.program_id(0) == 0)
    def _():
        zero_ref[...] = jnp.zeros_like(zero_ref)

        def z_copy(b):
            r = pl.multiple_of(zl_ref[b] * MOE_BLOCK, MOE_BLOCK)
            return pltpu.make_async_copy(zero_ref, xs_ref.at[pl.ds(r, MOE_BLOCK)], zsem)

        def start(b, carry):
            z_copy(b).start()
            return carry

        def wait(b, carry):
            z_copy(b).wait()
            return carry

        lax.fori_loop(0, nz_ref[0], start, 0)
        lax.fori_loop(0, nz_ref[0], wait, 0)

    def issue(g, carry):
        for u in range(SUBLANES):
            for k in range(TOP_K):
                d = dest_ref[0, k * t + g * SUBLANES + u]
                pltpu.make_async_copy(hp_ref.at[g, pl.ds(u, 1)], xs_ref.at[pl.ds(d, 1)], sem).start(priority=k % 2)
        return carry

    lax.fori_loop(0, t // SUBLANES, issue, 0)
    for _ in range(TOP_K):
        pltpu.make_async_copy(xs_ref.at[pl.ds(0, t)], xs_ref.at[pl.ds(0, t)], sem).wait()


def _tile_major(slot, tm):
    n = slot.shape[1]
    return slot.reshape(TOP_K, n // tm, tm).transpose(1, 0, 2).reshape(n // tm, 1, TOP_K * tm)


def _dispatch(n_zero, zero_blocks, dest, hp, n_slots, tm):
    n, w = hp.shape
    dest3 = _tile_major(dest, tm)
    return pl.pallas_call(
        _dispatch_body,
        grid_spec=pltpu.PrefetchScalarGridSpec(
            num_scalar_prefetch=2,
            grid=(n // tm,),
            in_specs=[pl.BlockSpec((None, 1, tm * TOP_K), lambda i, nz, zl: (i, 0, 0), memory_space=pltpu.SMEM),
                      pl.BlockSpec((tm // SUBLANES, SUBLANES, w), lambda i, nz, zl: (i, 0, 0))],
            out_specs=pl.BlockSpec(memory_space=pl.ANY),
            scratch_shapes=[pltpu.VMEM((MOE_BLOCK, w), U32),
                            pltpu.SemaphoreType.DMA(()),
                            pltpu.SemaphoreType.DMA(())]),
        out_shape=jax.ShapeDtypeStruct((n_slots, w), U32),
        compiler_params=_cparams(1),
        name="dispatch",
    )(n_zero, zero_blocks, dest3, hp.reshape(n // SUBLANES, SUBLANES, w))


PASS_SUBS = 9
EXPERT_TN = 256
DOT_ROWS = 1024
W_SLOTS = 3
W_AHEAD = 2


def _expert_body(meta_ref, pe_ref, pr_ref, pn_ref, xs_hbm, wgu_hbm, wdn_hbm, bgu_ref, bdn_ref,
                 y_hbm, xraw, xb, h_ref, wg_buf, wu_buf, wd_buf, ybuf, zbuf,
                 sem_x, sem_y, sem_z, sem_w, sem_d):
    sub = MOE_BLOCK
    tn = EXPERT_TN
    tp = tn // 2
    nj = h_ref.shape[0]
    nc = y_hbm.shape[1] // tp
    de = nj * tn
    d2 = xraw.shape[1]
    p = pl.program_id(0)
    n_pass = meta_ref[0]
    nsub = pn_ref[p]
    row0 = pr_ref[p]

    def x_copy(i, pp):
        r = pl.multiple_of((pr_ref[pp] + i) * sub, sub)
        return pltpu.make_async_copy(xs_hbm.at[pl.ds(r, sub)], xraw.at[pl.ds(i * sub, sub)], sem_x)

    def y_copy(start, size, r0, c, slot):
        r = pl.multiple_of(r0 * sub + start, sub)
        col = pl.multiple_of(c * tp, tp)
        return pltpu.make_async_copy(ybuf.at[slot, pl.ds(start, size), :],
                                     y_hbm.at[pl.ds(r, size), pl.ds(col, tp)], sem_y.at[slot])

    def z_copy(b, c):
        r = pl.multiple_of(b * sub, sub)
        return pltpu.make_async_copy(zbuf, y_hbm.at[pl.ds(r, sub), pl.ds(c * tp, tp)], sem_z)

    def w1_copies(pp, j):
        e = pe_ref[pp]
        slot = j % W_SLOTS
        col = pl.multiple_of(j * tn, tn)
        return (pltpu.make_async_copy(wgu_hbm.at[e, :, pl.ds(col, tn)], wg_buf.at[slot], sem_w.at[slot]),
                pltpu.make_async_copy(wgu_hbm.at[e, :, pl.ds(de + col, tn)], wu_buf.at[slot], sem_w.at[slot]))

    def w2_copy(pp, c):
        slot = c % W_SLOTS
        col = pl.multiple_of(c * tn, tn)
        return pltpu.make_async_copy(wdn_hbm.at[pe_ref[pp], :, pl.ds(col, tn)], wd_buf.at[slot], sem_d.at[slot])

    def start_w1(pp, j):
        for cp in w1_copies(pp, j):
            cp.start()

    def for_subs(count, fn):
        for i in range(PASS_SUBS):
            pl.when(i < count)(lambda i=i: fn(i))

    def for_tail(fn):
        def body(b, carry):
            for c in range(nc):
                fn(b, c)
            return carry
        lax.fori_loop(meta_ref[1], y_hbm.shape[0] // sub, body, 0)

    def for_groups(count, fn):
        pl.when(count == PASS_SUBS)(lambda: fn(0, PASS_SUBS * sub))
        k = 1 << (PASS_SUBS.bit_length() - 1)
        while k:
            start = pl.multiple_of((count & (-2 * k)) * sub, sub)
            pl.when((count != PASS_SUBS) & ((count & k) != 0))(lambda start=start, k=k: fn(start, k * sub))
            k //= 2

    @pl.when(p == 0)
    def _():
        for_subs(nsub, lambda i: x_copy(i, p).start())
        for j in range(W_AHEAD):
            start_w1(p, j)
        zbuf[...] = jnp.zeros_like(zbuf)
        for_tail(lambda b, c: z_copy(b, c).start())

    for_subs(nsub, lambda i: x_copy(i, p).wait())

    def unpack(i):
        rows = slice(i * sub, (i + 1) * sub)
        lo, hi = _unpack_pair_f32(xraw[rows, :])
        xb[rows, :d2] = lo.astype(BF16)
        xb[rows, d2:] = hi.astype(BF16)

    for_subs(nsub, unpack)

    @pl.when(p + 1 < n_pass)
    def _():
        for_subs(pn_ref[p + 1], lambda i: x_copy(i, p + 1).start())

    def first_step(j, carry):
        slot = j % W_SLOTS
        for cp in w1_copies(p, j):
            cp.wait()
        nxt = j + W_AHEAD
        pl.when(nxt < nj)(lambda: start_w1(p, nxt))
        pl.when(nxt >= nj)(lambda: w2_copy(p, nxt - nj).start())

        def group(start, size):
            wg = wg_buf[slot].astype(BF16)
            wu = wu_buf[slot].astype(BF16)
            dr = DOT_ROWS if size % DOT_ROWS == 0 else size
            for r in range(0, size, dr):
                rows = pl.ds(start + r, dr)
                x = xb[rows, :]
                gate = jnp.minimum(_dot(x, wg) + bgu_ref[j], SWIGLU_LIMIT)
                up = jnp.clip(_dot(x, wu) + bgu_ref[nj + j], -SWIGLU_LIMIT, SWIGLU_LIMIT)
                h_ref[j, rows, :] = ((up + 1.0) * gate * jax.nn.sigmoid(SWIGLU_ALPHA * gate)).astype(BF16)

        for_groups(nsub, group)
        return carry

    lax.fori_loop(0, nj, first_step, 0)

    def second_step(c, carry):
        wslot = c % W_SLOTS
        slot = c % 2
        w2_copy(p, c).wait()
        nxt = c + W_AHEAD
        pl.when(nxt < nc)(lambda: w2_copy(p, nxt).start())
        pl.when((nxt >= nc) & (p + 1 < n_pass))(lambda: start_w1(p + 1, nxt - nc))

        @pl.when(c >= 2)
        def _():
            for_groups(nsub, lambda start, size: y_copy(start, size, row0, c - 2, slot).wait())

        @pl.when((c < 2) & (p > 0))
        def _():
            for_groups(pn_ref[p - 1],
                       lambda start, size: y_copy(start, size, pr_ref[p - 1], nc - 2 + c, slot).wait())

        def group(start, size):
            wd = wd_buf[wslot].astype(BF16)
            dr = DOT_ROWS if size % DOT_ROWS == 0 else size
            for r in range(0, size, dr):
                rows = pl.ds(start + r, dr)
                hx = jnp.concatenate([h_ref[j, rows, :] for j in range(nj)], axis=1)
                y = _dot(hx, wd) + bdn_ref[c]
                ybuf[slot, rows, :] = _pack_bf16_pair(y[:, :tp], y[:, tp:])
            y_copy(start, size, row0, c, slot).start()

        for_groups(nsub, group)
        return carry

    lax.fori_loop(0, nc, second_step, 0)

    @pl.when(p == n_pass - 1)
    def _():
        for c in (nc - 2, nc - 1):
            for_groups(nsub, lambda start, size, c=c: y_copy(start, size, row0, c, c % 2).wait())
        for_tail(lambda b, cc: z_copy(b, cc).wait())


def _experts(meta, pass_e, pass_row0, pass_nsub, xs, w_gu, b_gu, w_dn, b_dn):
    n_slots, d2 = xs.shape
    d = 2 * d2
    n_e, de = w_dn.shape[0], w_dn.shape[1]
    tn = EXPERT_TN
    nj = de // tn
    nc = d // tn
    assert W_AHEAD < W_SLOTS and W_AHEAD <= min(nj, nc) and nc % 2 == 0
    rmax = PASS_SUBS * MOE_BLOCK
    return pl.pallas_call(
        _expert_body,
        grid_spec=pltpu.PrefetchScalarGridSpec(
            num_scalar_prefetch=4,
            grid=(meta[0],),
            in_specs=[
                pl.BlockSpec(memory_space=pl.ANY),
                pl.BlockSpec(memory_space=pl.ANY),
                pl.BlockSpec(memory_space=pl.ANY),
                pl.BlockSpec((None, 2 * nj, 1, tn), lambda p, m, e, r, n: (e[p], 0, 0, 0)),
                pl.BlockSpec((None, nc, 1, tn), lambda p, m, e, r, n: (e[p], 0, 0, 0)),
            ],
            out_specs=pl.BlockSpec(memory_space=pl.ANY),
            scratch_shapes=[pltpu.VMEM((rmax, d2), U32),
                            pltpu.VMEM((rmax, d), BF16),
                            pltpu.VMEM((nj, rmax, tn), BF16),
                            pltpu.VMEM((W_SLOTS, d, tn), w_gu.dtype),
                            pltpu.VMEM((W_SLOTS, d, tn), w_gu.dtype),
                            pltpu.VMEM((W_SLOTS, de, tn), w_dn.dtype),
                            pltpu.VMEM((2, rmax, tn // 2), U32),
                            pltpu.VMEM((MOE_BLOCK, tn // 2), U32),
                            pltpu.SemaphoreType.DMA(()),
                            pltpu.SemaphoreType.DMA((2,)),
                            pltpu.SemaphoreType.DMA(()),
                            pltpu.SemaphoreType.DMA((W_SLOTS,)),
                            pltpu.SemaphoreType.DMA((W_SLOTS,))]),
        out_shape=jax.ShapeDtypeStruct((n_slots, d2), U32),
        compiler_params=_cparams(1),
        name="experts",
    )(meta, pass_e, pass_row0, pass_nsub, xs, w_gu, w_dn,
      b_gu.reshape(n_e, 2 * nj, 1, tn), b_dn.reshape(n_e, nc, 1, tn))


COMBINE_CHUNK = 128


def _combine_body(dcur_ref, dnxt_ref, wt_ref, h1_ref, g2_ref, l2g_ref, l2b_ref, y_ref, out_ref, buf_ref, sem):
    i = pl.program_id(0)
    n_tiles = pl.num_programs(0)
    t = h1_ref.shape[0]
    tp = EXPERT_TN // 2
    slot = i % 2

    def issue(dest_ref, sl, g):
        for u in range(SUBLANES):
            for k in range(TOP_K):
                d = dest_ref[0, k * t + g * SUBLANES + u]
                pltpu.make_async_copy(y_ref.at[pl.ds(d, 1)], buf_ref.at[sl, k, g, pl.ds(u, 1)],
                                      sem.at[sl]).start(priority=k % 2)

    gpc = COMBINE_CHUNK // SUBLANES

    def reduce_rows(j):
        rows = pl.ds(pl.multiple_of(j * COMBINE_CHUNK, COMBINE_CHUNK), COMBINE_CHUNK)
        groups = pl.ds(pl.multiple_of(j * gpc, gpc), gpc)
        wt = wt_ref[rows, :]
        f_lo, f_hi = None, None
        for k in range(TOP_K):
            lo, hi = _unpack_pair_f32(buf_ref[slot, k, groups].reshape(COMBINE_CHUNK, buf_ref.shape[-1]))
            w = wt[:, k:k + 1]
            f_lo = lo * w if f_lo is None else f_lo + lo * w
            f_hi = hi * w if f_hi is None else f_hi + hi * w
        parts = []
        for c in range(f_lo.shape[1] // tp):
            parts += [f_lo[:, c * tp:(c + 1) * tp], f_hi[:, c * tp:(c + 1) * tp]]
        f = jnp.concatenate(parts, axis=1)
        out_ref[rows, :] = _layer_norm(DEEPNORM_ALPHA * h1_ref[rows, :] + g2_ref[...] * f,
                                       l2g_ref[...], l2b_ref[...])

    def first_gather(g, carry):
        issue(dcur_ref, 0, g)
        return carry

    @pl.when(i == 0)
    def _():
        lax.fori_loop(0, t // SUBLANES, first_gather, 0)

    for k in range(TOP_K):
        pltpu.make_async_copy(buf_ref.at[1 - slot, k], buf_ref.at[slot, k], sem.at[slot]).wait()

    def both(j, carry):
        reduce_rows(j)
        for w in range(gpc):
            issue(dnxt_ref, 1 - slot, j * gpc + w)
        return carry

    def only_reduce(j, carry):
        reduce_rows(j)
        return carry

    @pl.when(i + 1 < n_tiles)
    def _():
        lax.fori_loop(0, t // COMBINE_CHUNK, both, 0)

    @pl.when(i + 1 == n_tiles)
    def _():
        lax.fori_loop(0, t // COMBINE_CHUNK, only_reduce, 0)


def _combine(dest, wt, h1, mod3, rows_per_batch, l2g, l2b, y, tm):
    n, d = h1.shape
    n_tiles = n // tm
    dest3 = _tile_major(dest, tm)
    row = lambda i: (i * tm) // rows_per_batch
    return pl.pallas_call(
        _combine_body,
        grid=(n_tiles,),
        in_specs=[pl.BlockSpec((None, 1, tm * TOP_K), lambda i: (i, 0, 0), memory_space=pltpu.SMEM),
                  pl.BlockSpec((None, 1, tm * TOP_K), lambda i: (jnp.minimum(i + 1, n_tiles - 1), 0, 0),
                               memory_space=pltpu.SMEM),
                  pl.BlockSpec((tm, LANES), lambda i: (i, 0)),
                  pl.BlockSpec((tm, d), lambda i: (i, 0)),
                  pl.BlockSpec((None, 1, d), lambda i: (row(i), 0, 5)),
                  pl.BlockSpec((1, d), lambda i: (0, 0)),
                  pl.BlockSpec((1, d), lambda i: (0, 0)),
                  pl.BlockSpec(memory_space=pl.ANY)],
        out_specs=pl.BlockSpec((tm, d), lambda i: (i, 0)),
        out_shape=jax.ShapeDtypeStruct((n, d), F32),
        scratch_shapes=[pltpu.VMEM((2, TOP_K, tm // SUBLANES, SUBLANES, y.shape[1]), U32),
                        pltpu.SemaphoreType.DMA((2,))],
        compiler_params=_cparams(1),
        name="combine",
    )(dest3, dest3, wt, h1, mod3, l2g, l2b, y)


def _pick_tile(n, pref):
    t = pref
    while n % t:
        t //= 2
    return t


def kernel(x, c, ctx, c_ctx, ln_in_g, ln_in_b, w_ada, b_ada, w_in, cm_norm_g, cm_norm_b, cm_w_s, cm_b_s,
           gla_w_gk_f, gla_b_gk_f, gla_w_gk_b, gla_b_gk_b, gla_norm_g, w_out, ln1_g, ln1_b,
           w_router, b_router, w_gate_up, b_gate_up, w_down, b_down, ln2_g, ln2_b):
    bsz, l, d = x.shape
    lc = ctx.shape[1]
    n, nc = bsz * l, bsz * lc
    assert w_ada.shape[0] == 1, "single-layer configuration"
    assert bsz + 1 <= 8 and l % (2 * GLA_TILE) == 0 and lc % GLA_TILE == 0
    row = lambda v: v.reshape(1, -1)

    cc = jnp.concatenate([c, c_ctx[None, :], jnp.zeros((8 - bsz - 1, d), F32)], axis=0)
    mod3 = _ada(cc, w_ada[0], row(b_ada[0])).reshape(8, 1, N_MOD * d)

    n_uv = 2 * CM_HEADS * CM_CHUNK
    n_main = n_uv + 2 * GLA_HEADS * GLA_DK + 2 * GLA_HEADS * GLA_DV
    w_uv = w_in[0][:, :n_uv].astype(BF16)
    w_rest = w_in[0][:, n_uv:n_main].astype(BF16)
    w_lr = jnp.pad(w_in[0][:, n_main:].astype(BF16), ((0, 0), (0, LANES - 2 * GLA_RANK)))
    x2 = x.reshape(n, d)
    tm_x = _pick_tile(l, 512)
    bs_tile = jnp.repeat(cm_b_s[0].T, CM_CHUNK, axis=1)
    cm, p, lr = _inproj_cm(x2, mod3, l, row(ln_in_g), row(ln_in_b), w_uv, w_rest, w_lr,
                           row(cm_norm_g[0]), row(cm_norm_b[0]), cm_w_s[0].astype(BF16), bs_tile, tm_x)
    tm_c = _pick_tile(nc, 512)
    pc, lrc = _inproj(ctx.reshape(nc, d), mod3, lambda i: bsz, row(ln_in_g), row(ln_in_b),
                      w_rest, w_lr, 0, 2, tm_c, 1024)

    kw = GLA_HEADS * GLA_DK
    wgf = jnp.zeros((LANES, kw), BF16).at[:GLA_RANK].set(gla_w_gk_f[0].astype(BF16))
    wgb = jnp.zeros((LANES, kw), BF16).at[GLA_RANK:2 * GLA_RANK].set(gla_w_gk_b[0].astype(BF16))
    gla = _gla(p.reshape(bsz, l, -1), lr.reshape(bsz, l, LANES), pc.reshape(bsz, lc, -1),
               lrc.reshape(bsz, lc, LANES), wgf, row(gla_b_gk_f[0]), wgb, row(gla_b_gk_b[0]),
               row(gla_norm_g[0])).reshape(n, -1)

    w_r = jnp.pad(w_router[0], ((0, 0), (0, LANES - N_EXPERTS))).astype(BF16)
    b_r = jnp.pad(b_router[0], (0, LANES - N_EXPERTS)).reshape(1, LANES)
    h1, hp, logits = _post_attn(cm, gla, x2, mod3, l, row(ln_in_g), row(ln_in_b), w_out[0].astype(BF16),
                                row(ln1_g[0]), row(ln1_b[0]), w_r, b_r, tm_x)

    idx, wt, rank, cnt = _route(logits, _pick_tile(n, 1024))
    counts = cnt[0, :N_EXPERTS].astype(I32)
    n_blocks = (n * TOP_K + N_EXPERTS * (MOE_BLOCK - 1)) // MOE_BLOCK
    blocks_e = (counts + MOE_BLOCK - 1) // MOE_BLOCK
    blk_end = jnp.cumsum(blocks_e)
    blk_start = blk_end - blocks_e
    dest = rank[:TOP_K]
    for e in range(N_EXPERTS):
        dest = dest + jnp.where(idx[:TOP_K] == e, blk_start[e] * MOE_BLOCK, 0)
    n_pass_max = n_blocks // PASS_SUBS + N_EXPERTS
    pass_cnt = (blocks_e + PASS_SUBS - 1) // PASS_SUBS
    pass_end = jnp.cumsum(pass_cnt)
    pass_start = pass_end - pass_cnt
    pid = jnp.arange(n_pass_max, dtype=I32)
    pass_e = jnp.minimum(jnp.searchsorted(pass_end, pid, side="right"), N_EXPERTS - 1).astype(I32)
    local = pid - pass_start[pass_e]
    pass_row0 = (blk_start[pass_e] + local * PASS_SUBS).astype(I32)
    pass_nsub = jnp.clip(blocks_e[pass_e] - local * PASS_SUBS, 0, PASS_SUBS).astype(I32)
    meta = jnp.stack([pass_end[-1], blk_end[-1]]).astype(I32)

    ar = jnp.arange(N_EXPERTS, dtype=I32)
    cand = jnp.concatenate([jnp.where(counts % MOE_BLOCK != 0, blk_end - 1, -1),
                            jnp.where(blk_end[-1] + ar < n_blocks, blk_end[-1] + ar, -1)])
    zero_blocks = cand[jnp.argsort(cand < 0, stable=True)].astype(I32)
    n_zero = jnp.sum(cand >= 0).astype(I32).reshape(1)

    xs = _dispatch(n_zero, zero_blocks, dest, hp, n_blocks * MOE_BLOCK, _pick_tile(n, 512))
    ys = _experts(meta, pass_e, pass_row0, pass_nsub, xs, w_gate_up[0], b_gate_up[0], w_down[0], b_down[0])
    out = _combine(dest, wt, h1, mod3, l, row(ln2_g[0]), row(ln2_b[0]), ys, _pick_tile(n, 512))
    return out.reshape(bsz, l, d)
```

```python
import math

import jax
import jax.numpy as jnp
from jax import lax
from jax.experimental import pallas as pl
from jax.experimental.pallas import tpu as pltpu

F32 = jnp.float32
BF16 = jnp.bfloat16
U32 = jnp.uint32
I32 = jnp.int32

CM_CHUNK = 128
CM_HEADS = 8
GLA_HEADS = 4
GLA_DK = 128
GLA_DV = 256
GLA_CHUNK = 64
GLA_RANK = 16
GLA_GATE_NORMALIZER = 16.0
N_EXPERTS = 32
TOP_K = 4
MOE_BLOCK = 256
SWIGLU_LIMIT = 7.0
SWIGLU_ALPHA = 1.702
N_MOD = 6
DEEPNORM_ALPHA = 2.0 ** 0.25
LN_EPS = 1e-5
RMS_EPS = 1e-6

LANES = 128
SUBLANES = 8
VMEM_LIMIT = 56 * 1024 * 1024
ROW_CHUNK = 256


def _cparams(n_axes, vmem=VMEM_LIMIT):
    return pltpu.CompilerParams(dimension_semantics=("arbitrary",) * n_axes,
                                vmem_limit_bytes=vmem)


def _layer_norm(t, g, b):
    mu = jnp.mean(t, axis=-1, keepdims=True)
    d = t - mu
    var = jnp.mean(d * d, axis=-1, keepdims=True)
    return d * lax.rsqrt(var + LN_EPS) * g + b


def _gelu(t):
    return 0.5 * t * (1.0 + lax.erf(t * (1.0 / math.sqrt(2.0))))


def _silu(t):
    return t * jax.nn.sigmoid(t)


def _ones_where(mask, dtype):
    return jnp.where(mask, 1.0, 0.0).astype(dtype)


def _dot(a, b):
    return jnp.dot(a, b, preferred_element_type=F32)


def _dot_nt(a, b):
    return lax.dot_general(a, b, (((1,), (1,)), ((), ())), preferred_element_type=F32)


def _dot_tn(a, b):
    return lax.dot_general(a, b, (((0,), (0,)), ((), ())), preferred_element_type=F32)


def _pack_bf16_pair(lo, hi):
    lo_b = lax.bitcast_convert_type(lo.astype(BF16).astype(F32), U32)
    hi_b = lax.bitcast_convert_type(hi.astype(BF16).astype(F32), U32)
    return hi_b | (lo_b >> 16)


def _unpack_pair_f32(p):
    lo = lax.bitcast_convert_type(p << 16, F32)
    hi = lax.bitcast_convert_type(p & jnp.uint32(0xFFFF0000), F32)
    return lo, hi


def _ada_body(c_ref, w_ref, b_ref, o_ref):
    a = _silu(c_ref[...]).astype(BF16)
    o_ref[...] = _dot(a, w_ref[...].astype(BF16)) + b_ref[...]


def _ada(cc, w, b):
    rows, d = cc.shape
    n = w.shape[1]
    tn = 1024
    return pl.pallas_call(
        _ada_body,
        grid=(n // tn,),
        in_specs=[pl.BlockSpec((rows, d), lambda j: (0, 0)),
                  pl.BlockSpec((d, tn), lambda j: (0, j)),
                  pl.BlockSpec((1, tn), lambda j: (0, j))],
        out_specs=pl.BlockSpec((rows, tn), lambda j: (0, j)),
        out_shape=jax.ShapeDtypeStruct((rows, n), F32),
        compiler_params=_cparams(1),
        name="ada",
    )(cc, w, b)


def _inproj_body(x_ref, g_ref, b_ref, sh_ref, sc_ref, w_ref, wlr_ref, o_ref, olr_ref, hm_ref):
    @pl.when(pl.program_id(1) == 0)
    def _():
        h = _layer_norm(x_ref[...], g_ref[...], b_ref[...])
        hm = (h * (1.0 + sc_ref[...]) + sh_ref[...]).astype(BF16)
        hm_ref[...] = hm
        olr_ref[...] = _dot(hm, wlr_ref[...])

    o_ref[...] = _dot(hm_ref[...], w_ref[...])


def _inproj(x2, mod3, mod_row, ln_g, ln_b, w_main, w_lr, col0, ncols, tm, tn):
    r, d = x2.shape
    return pl.pallas_call(
        _inproj_body,
        grid=(r // tm, ncols),
        in_specs=[pl.BlockSpec((tm, d), lambda i, j: (i, 0)),
                  pl.BlockSpec((1, d), lambda i, j: (0, 0)),
                  pl.BlockSpec((1, d), lambda i, j: (0, 0)),
                  pl.BlockSpec((None, 1, d), lambda i, j: (mod_row(i), 0, 0)),
                  pl.BlockSpec((None, 1, d), lambda i, j: (mod_row(i), 0, 1)),
                  pl.BlockSpec((d, tn), lambda i, j: (0, col0 + j)),
                  pl.BlockSpec((d, LANES), lambda i, j: (0, 0))],
        out_specs=[pl.BlockSpec((tm, tn), lambda i, j: (i, j)),
                   pl.BlockSpec((tm, LANES), lambda i, j: (i, 0))],
        out_shape=[jax.ShapeDtypeStruct((r, ncols * tn), F32),
                   jax.ShapeDtypeStruct((r, LANES), F32)],
        scratch_shapes=[pltpu.VMEM((tm, d), BF16)],
        compiler_params=_cparams(2),
        name="inproj",
    )(x2, ln_g, ln_b, mod3, mod3, w_main, w_lr)


def _inproj_cm_body(x_ref, g_ref, b_ref, sh_ref, sc_ref, wuv_ref, wr_ref, wlr_ref, ng_ref, nb_ref, ws_ref, bs_ref,
                    cm_ref, p_ref, lr_ref):
    tm = x_ref.shape[0]
    half = wuv_ref.shape[1] // 2
    hd = CM_CHUNK
    for r in range(0, tm, ROW_CHUNK):
        rows = slice(r, r + ROW_CHUNK)
        h = _layer_norm(x_ref[rows, :], g_ref[...], b_ref[...])
        hm = (h * (1.0 + sc_ref[...]) + sh_ref[...]).astype(BF16)
        uv = _dot(hm, wuv_ref[...])
        u = _gelu(uv[:, :half])
        vb = _layer_norm(_gelu(uv[:, half:]), ng_ref[...], nb_ref[...]).astype(BF16)
        for c in range(ROW_CHUNK // CM_CHUNK):
            crow = slice(c * CM_CHUNK, (c + 1) * CM_CHUNK)
            orow = slice(r + c * CM_CHUNK, r + (c + 1) * CM_CHUNK)
            for hh in range(CM_HEADS):
                cols = slice(hh * hd, (hh + 1) * hd)
                s = _dot(ws_ref[hh], vb[crow, cols]) + bs_ref[:, cols]
                cm_ref[orow, cols] = (u[crow, cols] * s).astype(BF16)
        p_ref[rows, :] = _dot(hm, wr_ref[...])
        lr_ref[rows, :] = _dot(hm, wlr_ref[...])


def _inproj_cm(x2, mod3, rows_per_batch, ln_g, ln_b, w_uv, w_rest, w_lr, ng, nb, ws, bs, tm):
    r, d = x2.shape
    n_uv, n_rest = w_uv.shape[1], w_rest.shape[1]
    row = lambda i: (i * tm) // rows_per_batch

    def const(shape):
        return pl.BlockSpec(shape, lambda i: (0,) * len(shape), pipeline_mode=pl.Buffered(1))

    return pl.pallas_call(
        _inproj_cm_body,
        grid=(r // tm,),
        in_specs=[pl.BlockSpec((tm, d), lambda i: (i, 0)),
                  const((1, d)), const((1, d)),
                  pl.BlockSpec((None, 1, d), lambda i: (row(i), 0, 0)),
                  pl.BlockSpec((None, 1, d), lambda i: (row(i), 0, 1)),
                  const((d, n_uv)), const((d, n_rest)), const((d, LANES)),
                  const((1, n_uv // 2)), const((1, n_uv // 2)),
                  const((CM_HEADS, CM_CHUNK, CM_CHUNK)), const((CM_CHUNK, n_uv // 2))],
        out_specs=[pl.BlockSpec((tm, n_uv // 2), lambda i: (i, 0)),
                   pl.BlockSpec((tm, n_rest), lambda i: (i, 0)),
                   pl.BlockSpec((tm, LANES), lambda i: (i, 0))],
        out_shape=[jax.ShapeDtypeStruct((r, n_uv // 2), BF16),
                   jax.ShapeDtypeStruct((r, n_rest), F32),
                   jax.ShapeDtypeStruct((r, LANES), F32)],
        compiler_params=_cparams(1),
        name="inproj_cm",
    )(x2, ln_g, ln_b, mod3, mod3, w_uv, w_rest, w_lr, ng, nb, ws, bs)


GLA_TILE = 256
GLA_TILES_PER_ITER = 8


def _gla_tile(q, k, v, lr, wg, bg, st_ref, forward, need_o):
    t = k.shape[0]
    n_chunks = t // GLA_CHUNK
    z = _dot(lr.astype(BF16), wg) + bg
    g = jax.nn.log_sigmoid(z) * (1.0 / GLA_GATE_NORMALIZER)
    r_id = lax.broadcasted_iota(I32, (t, t), 0)
    c_id = lax.broadcasted_iota(I32, (t, t), 1)
    shift = GLA_CHUNK.bit_length() - 1
    same = (r_id >> shift) == (c_id >> shift)
    lower = same & (c_id <= r_id)
    tri = _ones_where(lower, BF16)
    g_hi = g.astype(BF16)
    g_lo = (g - g_hi.astype(F32)).astype(BF16)
    csum2 = _dot(tri, jnp.concatenate([g_hi, g_lo], axis=1))
    csum = csum2[:, :GLA_DK] + csum2[:, GLA_DK:]
    g3 = g.reshape(n_chunks, GLA_CHUNK, GLA_DK)
    tot = jnp.broadcast_to(jnp.sum(g3, axis=1, keepdims=True), g3.shape).reshape(t, GLA_DK)
    bcum = csum if forward else tot - csum + g
    kd = (k * jnp.exp(tot - bcum)).astype(BF16)
    decay = jnp.exp(tot)
    vb = v.astype(BF16)
    row_chunk = lax.broadcasted_iota(I32, (t, GLA_DK), 0) >> shift
    kd_blocks = jnp.concatenate([jnp.where(row_chunk == c, kd, jnp.zeros_like(kd)) for c in range(n_chunks)], axis=1)
    u_all = _dot_tn(vb, kd_blocks)
    o = None
    if need_o:
        qe = ((q * (GLA_DK ** -0.5)) * jnp.exp(bcum)).astype(BF16)
        ke = (k * jnp.exp(-bcum)).astype(BF16)
        att = _dot_nt(qe, ke)
        mask = lower if forward else same & (c_id >= r_id)
        att = jnp.where(mask, att, 0.0).astype(BF16)
        o = _dot(att, vb)
    outs = [None] * n_chunks
    order = range(n_chunks) if forward else range(n_chunks - 1, -1, -1)
    for c in order:
        rows = slice(c * GLA_CHUNK, (c + 1) * GLA_CHUNK)
        s_t = st_ref[...]
        if need_o:
            outs[c] = o[rows] + _dot_nt(qe[rows], s_t.astype(BF16))
        u_t = u_all[:, c * GLA_DK:(c + 1) * GLA_DK]
        st_ref[...] = s_t * decay[c * GLA_CHUNK:c * GLA_CHUNK + 1, :] + u_t
    if need_o:
        return jnp.concatenate(outs, axis=0)
    return None


def _gla_body(q_ref, k_ref, v_ref, go_ref, lr_ref, kc_ref, vc_ref, lrc_ref,
              wgf_ref, bgf_ref, wgb_ref, bgb_ref, ng_ref, out_ref, o_scr, sf_ref, sb_ref):
    t = GLA_TILE
    n_x = q_ref.shape[0] // t
    n_c = kc_ref.shape[0] // t
    half = n_x // 2
    sf_ref[...] = jnp.zeros_like(sf_ref)
    sb_ref[...] = jnp.zeros_like(sb_ref)
    wgf, bgf, wgb, bgb = wgf_ref[...], bgf_ref[...], wgb_ref[...], bgb_ref[...]

    for i in range(n_c):
        rf = slice(i * t, (i + 1) * t)
        rb = slice((n_c - 1 - i) * t, (n_c - i) * t)
        _gla_tile(None, kc_ref[rf], vc_ref[rf], lrc_ref[rf], wgf, bgf, sf_ref, True, False)
        _gla_tile(None, kc_ref[rb], vc_ref[rb], lrc_ref[rb], wgb, bgb, sb_ref, False, False)

    def tile_out(i, forward):
        rows = pl.ds(pl.multiple_of(i * t, t), t)
        if forward:
            return rows, _gla_tile(q_ref[rows], k_ref[rows], v_ref[rows], lr_ref[rows],
                                   wgf, bgf, sf_ref, True, True)
        return rows, _gla_tile(q_ref[rows], k_ref[rows], v_ref[rows], lr_ref[rows],
                               wgb, bgb, sb_ref, False, True)

    def finish(rows, o):
        o = o + o_scr[rows]
        ms = jnp.mean(o * o, axis=-1, keepdims=True)
        on = o * lax.rsqrt(ms + RMS_EPS) * ng_ref[...]
        out_ref[rows] = (on * _silu(go_ref[rows])).astype(BF16)

    def keep(rows, o):
        o_scr[rows] = o

    u = math.gcd(GLA_TILES_PER_ITER, half)

    def make_step(sink):
        def step(it, carry):
            for w in range(u):
                i = it * u + w
                sink(*tile_out(i, True))
                sink(*tile_out(n_x - 1 - i, False))
            return carry
        return step

    lax.fori_loop(0, half // u, make_step(keep), 0)
    lax.fori_loop(half // u, n_x // u, make_step(finish), 0)


def _gla(p3, lr3, pc3, lrc3, wgf, bgf, wgb, bgb, ng):
    bsz, l, _ = p3.shape
    lc = pc3.shape[1]
    dk, dv = GLA_DK, GLA_DV
    kw = GLA_HEADS * dk
    q0, k0 = 0, kw // dk
    v0, go0 = 2 * kw // dv, (2 * kw + GLA_HEADS * dv) // dv
    kc0, vc0 = k0, v0
    return pl.pallas_call(
        _gla_body,
        grid=(bsz, GLA_HEADS),
        in_specs=[pl.BlockSpec((None, l, dk), lambda b, h: (b, 0, q0 + h)),
                  pl.BlockSpec((None, l, dk), lambda b, h: (b, 0, k0 + h)),
                  pl.BlockSpec((None, l, dv), lambda b, h: (b, 0, v0 + h)),
                  pl.BlockSpec((None, l, dv), lambda b, h: (b, 0, go0 + h)),
                  pl.BlockSpec((None, l, LANES), lambda b, h: (b, 0, 0)),
                  pl.BlockSpec((None, lc, dk), lambda b, h: (b, 0, kc0 + h)),
                  pl.BlockSpec((None, lc, dv), lambda b, h: (b, 0, vc0 + h)),
                  pl.BlockSpec((None, lc, LANES), lambda b, h: (b, 0, 0)),
                  pl.BlockSpec((LANES, dk), lambda b, h: (0, h)),
                  pl.BlockSpec((1, dk), lambda b, h: (0, h)),
                  pl.BlockSpec((LANES, dk), lambda b, h: (0, h)),
                  pl.BlockSpec((1, dk), lambda b, h: (0, h)),
                  pl.BlockSpec((1, dv), lambda b, h: (0, 0))],
        out_specs=pl.BlockSpec((None, l, dv), lambda b, h: (b, 0, h)),
        out_shape=jax.ShapeDtypeStruct((bsz, l, GLA_HEADS * dv), BF16),
        scratch_shapes=[pltpu.VMEM((l, dv), F32),
                        pltpu.VMEM((dv, dk), F32),
                        pltpu.VMEM((dv, dk), F32)],
        compiler_params=_cparams(2),
        name="gla",
    )(p3, p3, p3, p3, lr3, pc3, pc3, lrc3, wgf, bgf, wgb, bgb, ng)


def _post_body(cm_ref, gla_ref, x_ref, lng_ref, lnb_ref, g1_ref, sh2_ref, sc2_ref,
               wo_ref, l1g_ref, l1b_ref, wr_ref, br_ref, h1_ref, hp_ref, lg_ref):
    half = cm_ref.shape[1]
    d2 = x_ref.shape[1] // 2
    for r in range(0, x_ref.shape[0], ROW_CHUNK):
        rows = slice(r, r + ROW_CHUNK)
        y = _dot(cm_ref[rows, :], wo_ref[:half, :]) + _dot(gla_ref[rows, :], wo_ref[half:, :])
        hx = _layer_norm(x_ref[rows, :], lng_ref[...], lnb_ref[...])
        h1 = _layer_norm(DEEPNORM_ALPHA * hx + g1_ref[...] * y, l1g_ref[...], l1b_ref[...])
        h1_ref[rows, :] = h1
        hm = h1 * (1.0 + sc2_ref[...]) + sh2_ref[...]
        hp_ref[rows, :] = _pack_bf16_pair(hm[:, :d2], hm[:, d2:])
        lg_ref[rows, :] = _dot(hm.astype(BF16), wr_ref[...]) + br_ref[...]


def _post_attn(cm, gla, x2, mod3, rows_per_batch, ln_g, ln_b, w_out, l1g, l1b, w_r, b_r, tm):
    r, d = x2.shape
    half = d // 2
    row = lambda i: (i * tm) // rows_per_batch
    full = lambda shape: pl.BlockSpec(shape, lambda i: (0,) * len(shape))
    return pl.pallas_call(
        _post_body,
        grid=(r // tm,),
        in_specs=[pl.BlockSpec((tm, half), lambda i: (i, 0)),
                  pl.BlockSpec((tm, half), lambda i: (i, 0)),
                  pl.BlockSpec((tm, d), lambda i: (i, 0)),
                  full((1, d)), full((1, d)),
                  pl.BlockSpec((None, 1, d), lambda i: (row(i), 0, 2)),
                  pl.BlockSpec((None, 1, d), lambda i: (row(i), 0, 3)),
                  pl.BlockSpec((None, 1, d), lambda i: (row(i), 0, 4)),
                  full((d, d)), full((1, d)), full((1, d)),
                  full((d, LANES)), full((1, LANES))],
        out_specs=[pl.BlockSpec((tm, d), lambda i: (i, 0)),
                   pl.BlockSpec((tm, half), lambda i: (i, 0)),
                   pl.BlockSpec((tm, LANES), lambda i: (i, 0))],
        out_shape=[jax.ShapeDtypeStruct((r, d), F32),
                   jax.ShapeDtypeStruct((r, half), U32),
                   jax.ShapeDtypeStruct((r, LANES), F32)],
        compiler_params=_cparams(1),
        name="post_attn",
    )(cm, gla, x2, ln_g, ln_b, mod3, mod3, mod3, w_out, l1g, l1b, w_r, b_r)


def _route_body(lg_ref, idx_ref, wt_ref, rank_ref, cnt_ref, carry_ref):
    i = pl.program_id(0)
    t = lg_ref.shape[0]

    @pl.when(i == 0)
    def _():
        carry_ref[...] = jnp.zeros_like(carry_ref)

    lane = lax.broadcasted_iota(I32, (t, LANES), 1)
    lane_f = lane.astype(F32)
    neg = jnp.float32(-jnp.inf)
    l = jnp.where(lane < N_EXPERTS, lg_ref[...], neg)
    tops, onehots, idxs = [], [], []
    for _ in range(TOP_K):
        m = jnp.max(l, axis=-1, keepdims=True)
        idx = jnp.min(jnp.where(l == m, lane_f, float(LANES)), axis=-1, keepdims=True).astype(I32)
        oh = lane == idx
        l = jnp.where(oh, neg, l)
        tops.append(m)
        idxs.append(idx)
        onehots.append(oh)
    exps = [jnp.exp(m - tops[0]) for m in tops]
    denom = exps[0] + exps[1] + exps[2] + exps[3]
    sel = _ones_where(onehots[0] | onehots[1] | onehots[2] | onehots[3], F32)
    r_id = lax.broadcasted_iota(I32, (t, t), 0)
    c_id = lax.broadcasted_iota(I32, (t, t), 1)
    strict = _ones_where(c_id < r_id, BF16)
    before = _dot(strict, sel.astype(BF16)) + carry_ref[0:1, :]
    idx_out = jnp.zeros((t, LANES), I32)
    wt_out = jnp.zeros((t, LANES), F32)
    rank_out = jnp.zeros((t, LANES), I32)
    for k in range(TOP_K):
        rk = jnp.sum(jnp.where(onehots[k], before, 0.0), axis=-1, keepdims=True).astype(I32)
        idx_out = jnp.where(lane == k, idxs[k], idx_out)
        wt_out = jnp.where(lane == k, exps[k] / denom, wt_out)
        rank_out = jnp.where(lane == k, rk, rank_out)
    idx_ref[...] = jnp.transpose(idx_out)[:SUBLANES, :]
    wt_ref[...] = wt_out
    rank_ref[...] = jnp.transpose(rank_out)[:SUBLANES, :]
    total = carry_ref[0:1, :] + jnp.sum(sel, axis=0, keepdims=True)
    carry_ref[...] = jnp.broadcast_to(total, carry_ref.shape)
    cnt_ref[...] = jnp.broadcast_to(total, cnt_ref.shape)


def _route(logits, tm):
    n = logits.shape[0]
    blk = pl.BlockSpec((tm, LANES), lambda i: (i, 0))
    small = pl.BlockSpec((SUBLANES, tm), lambda i: (0, i))
    return pl.pallas_call(
        _route_body,
        grid=(n // tm,),
        in_specs=[blk],
        out_specs=[small, blk, small, pl.BlockSpec((8, LANES), lambda i: (0, 0))],
        out_shape=[jax.ShapeDtypeStruct((SUBLANES, n), I32),
                   jax.ShapeDtypeStruct((n, LANES), F32),
                   jax.ShapeDtypeStruct((SUBLANES, n), I32),
                   jax.ShapeDtypeStruct((8, LANES), F32)],
        scratch_shapes=[pltpu.VMEM((8, LANES), F32)],
        compiler_params=_cparams(1),
        name="route",
    )(logits)


def _dispatch_body(nz_ref, zl_ref, dest_ref, hp_ref, xs_ref, zero_ref, sem, zsem):
    t = hp_ref.shape[0] * hp_ref.shape[1]

    @pl.when(pl.program_id(0) == 0)
    def _():
        zero_ref[...] = jnp.zeros_like(zero_ref)

        def z_copy(b):
            r = pl.multiple_of(zl_ref[b] * MOE_BLOCK, MOE_BLOCK)
            return pltpu.make_async_copy(zero_ref, xs_ref.at[pl.ds(r, MOE_BLOCK)], zsem)

        def start(b, carry):
            z_copy(b).start()
            return carry

        def wait(b, carry):
            z_copy(b).wait()
            return carry

        lax.fori_loop(0, nz_ref[0], start, 0)
        lax.fori_loop(0, nz_ref[0], wait, 0)

    def issue(g, carry):
        for u in range(SUBLANES):
            for k in range(TOP_K):
                d = dest_ref[0, k * t + g * SUBLANES + u]
                pltpu.make_async_copy(hp_ref.at[g, pl.ds(u, 1)], xs_ref.at[pl.ds(d, 1)], sem).start(priority=k % 2)
        return carry

    lax.fori_loop(0, t // SUBLANES, issue, 0)
    for _ in range(TOP_K):
        pltpu.make_async_copy(xs_ref.at[pl.ds(0, t)], xs_ref.at[pl.ds(0, t)], sem).wait()


def _tile_major(slot, tm):
    n = slot.shape[1]
    return slot.reshape(TOP_K, n // tm, tm).transpose(1, 0, 2).reshape(n // tm, 1, TOP_K * tm)


def _dispatch(n_zero, zero_blocks, dest, hp, n_slots, tm):
    n, w = hp.shape
    dest3 = _tile_major(dest, tm)
    return pl.pallas_call(
        _dispatch_body,
        grid_spec=pltpu.PrefetchScalarGridSpec(
            num_scalar_prefetch=2,
            grid=(n // tm,),
            in_specs=[pl.BlockSpec((None, 1, tm * TOP_K), lambda i, nz, zl: (i, 0, 0), memory_space=pltpu.SMEM),
                      pl.BlockSpec((tm // SUBLANES, SUBLANES, w), lambda i, nz, zl: (i, 0, 0))],
            out_specs=pl.BlockSpec(memory_space=pl.ANY),
            scratch_shapes=[pltpu.VMEM((MOE_BLOCK, w), U32),
                            pltpu.SemaphoreType.DMA(()),
                            pltpu.SemaphoreType.DMA(())]),
        out_shape=jax.ShapeDtypeStruct((n_slots, w), U32),
        compiler_params=_cparams(1),
        name="dispatch",
    )(n_zero, zero_blocks, dest3, hp.reshape(n // SUBLANES, SUBLANES, w))


PASS_SUBS = 9
EXPERT_TN = 256
DOT_ROWS = 1024
W_SLOTS = 3
W_AHEAD = 2


def _expert_body(meta_ref, pe_ref, pr_ref, pn_ref, xs_hbm, wgu_hbm, wdn_hbm, bgu_ref, bdn_ref,
                 y_hbm, xraw, xb, h_ref, wg_buf, wu_buf, wd_buf, ybuf, zbuf,
                 sem_x, sem_y, sem_z, sem_w, sem_d):
    sub = MOE_BLOCK
    tn = EXPERT_TN
    tp = tn // 2
    nj = h_ref.shape[0]
    nc = y_hbm.shape[1] // tp
    de = nj * tn
    d2 = xraw.shape[1]
    p = pl.program_id(0)
    n_pass = meta_ref[0]
    nsub = pn_ref[p]
    row0 = pr_ref[p]

    def x_copy(i, pp):
        r = pl.multiple_of((pr_ref[pp] + i) * sub, sub)
        return pltpu.make_async_copy(xs_hbm.at[pl.ds(r, sub)], xraw.at[pl.ds(i * sub, sub)], sem_x)

    def y_copy(start, size, r0, c, slot):
        r = pl.multiple_of(r0 * sub + start, sub)
        col = pl.multiple_of(c * tp, tp)
        return pltpu.make_async_copy(ybuf.at[slot, pl.ds(start, size), :],
                                     y_hbm.at[pl.ds(r, size), pl.ds(col, tp)], sem_y.at[slot])

    def z_copy(b, c):
        r = pl.multiple_of(b * sub, sub)
        return pltpu.make_async_copy(zbuf, y_hbm.at[pl.ds(r, sub), pl.ds(c * tp, tp)], sem_z)

    def w1_copies(pp, j):
        e = pe_ref[pp]
        slot = j % W_SLOTS
        col = pl.multiple_of(j * tn, tn)
        return (pltpu.make_async_copy(wgu_hbm.at[e, :, pl.ds(col, tn)], wg_buf.at[slot], sem_w.at[slot]),
                pltpu.make_async_copy(wgu_hbm.at[e, :, pl.ds(de + col, tn)], wu_buf.at[slot], sem_w.at[slot]))

    def w2_copy(pp, c):
        slot = c % W_SLOTS
        col = pl.multiple_of(c * tn, tn)
        return pltpu.make_async_copy(wdn_hbm.at[pe_ref[pp], :, pl.ds(col, tn)], wd_buf.at[slot], sem_d.at[slot])

    def start_w1(pp, j):
        for cp in w1_copies(pp, j):
            cp.start()

    def for_subs(count, fn):
        for i in range(PASS_SUBS):
            pl.when(i < count)(lambda i=i: fn(i))

    def for_tail(fn):
        def body(b, carry):
            for c in range(nc):
                fn(b, c)
            return carry
        lax.fori_loop(meta_ref[1], y_hbm.shape[0] // sub, body, 0)

    def for_groups(count, fn):
        pl.when(count == PASS_SUBS)(lambda: fn(0, PASS_SUBS * sub))
        k = 1 << (PASS_SUBS.bit_length() - 1)
        while k:
            start = pl.multiple_of((count & (-2 * k)) * sub, sub)
            pl.when((count != PASS_SUBS) & ((count & k) != 0))(lambda start=start, k=k: fn(start, k * sub))
            k //= 2

    @pl.when(p == 0)
    def _():
        for_subs(nsub, lambda i: x_copy(i, p).start())
        for j in range(W_AHEAD):
            start_w1(p, j)
        zbuf[...] = jnp.zeros_like(zbuf)
        for_tail(lambda b, c: z_copy(b, c).start())

    for_subs(nsub, lambda i: x_copy(i, p).wait())

    def unpack(i):
        rows = slice(i * sub, (i + 1) * sub)
        lo, hi = _unpack_pair_f32(xraw[rows, :])
        xb[rows, :d2] = lo.astype(BF16)
        xb[rows, d2:] = hi.astype(BF16)

    for_subs(nsub, unpack)

    @pl.when(p + 1 < n_pass)
    def _():
        for_subs(pn_ref[p + 1], lambda i: x_copy(i, p + 1).start())

    def first_step(j, carry):
        slot = j % W_SLOTS
        for cp in w1_copies(p, j):
            cp.wait()
        nxt = j + W_AHEAD
        pl.when(nxt < nj)(lambda: start_w1(p, nxt))
        pl.when(nxt >= nj)(lambda: w2_copy(p, nxt - nj).start())

        def group(start, size):
            wg = wg_buf[slot].astype(BF16)
            wu = wu_buf[slot].astype(BF16)
            dr = DOT_ROWS if size % DOT_ROWS == 0 else size
            for r in range(0, size, dr):
                rows = pl.ds(start + r, dr)
                x = xb[rows, :]
                gate = jnp.minimum(_dot(x, wg) + bgu_ref[j], SWIGLU_LIMIT)
                up = jnp.clip(_dot(x, wu) + bgu_ref[nj + j], -SWIGLU_LIMIT, SWIGLU_LIMIT)
                h_ref[j, rows, :] = ((up + 1.0) * gate * jax.nn.sigmoid(SWIGLU_ALPHA * gate)).astype(BF16)

        for_groups(nsub, group)
        return carry

    lax.fori_loop(0, nj, first_step, 0)

    def second_step(c, carry):
        wslot = c % W_SLOTS
        slot = c % 2
        w2_copy(p, c).wait()
        nxt = c + W_AHEAD
        pl.when(nxt < nc)(lambda: w2_copy(p, nxt).start())
        pl.when((nxt >= nc) & (p + 1 < n_pass))(lambda: start_w1(p + 1, nxt - nc))

        @pl.when(c >= 2)
        def _():
            for_groups(nsub, lambda start, size: y_copy(start, size, row0, c - 2, slot).wait())

        @pl.when((c < 2) & (p > 0))
        def _():
            for_groups(pn_ref[p - 1],
                       lambda start, size: y_copy(start, size, pr_ref[p - 1], nc - 2 + c, slot).wait())

        def group(start, size):
            wd = wd_buf[wslot].astype(BF16)
            dr = DOT_ROWS if size % DOT_ROWS == 0 else size
            for r in range(0, size, dr):
                rows = pl.ds(start + r, dr)
                hx = jnp.concatenate([h_ref[j, rows, :] for j in range(nj)], axis=1)
                y = _dot(hx, wd) + bdn_ref[c]
                ybuf[slot, rows, :] = _pack_bf16_pair(y[:, :tp], y[:, tp:])
            y_copy(start, size, row0, c, slot).start()

        for_groups(nsub, group)
        return carry

    lax.fori_loop(0, nc, second_step, 0)

    @pl.when(p == n_pass - 1)
    def _():
        for c in (nc - 2, nc - 1):
            for_groups(nsub, lambda start, size, c=c: y_copy(start, size, row0, c, c % 2).wait())
        for_tail(lambda b, cc: z_copy(b, cc).wait())


def _experts(meta, pass_e, pass_row0, pass_nsub, xs, w_gu, b_gu, w_dn, b_dn):
    n_slots, d2 = xs.shape
    d = 2 * d2
    n_e, de = w_dn.shape[0], w_dn.shape[1]
    tn = EXPERT_TN
    nj = de // tn
    nc = d // tn
    assert W_AHEAD < W_SLOTS and W_AHEAD <= min(nj, nc) and nc % 2 == 0
    rmax = PASS_SUBS * MOE_BLOCK
    return pl.pallas_call(
        _expert_body,
        grid_spec=pltpu.PrefetchScalarGridSpec(
            num_scalar_prefetch=4,
            grid=(meta[0],),
            in_specs=[
                pl.BlockSpec(memory_space=pl.ANY),
                pl.BlockSpec(memory_space=pl.ANY),
                pl.BlockSpec(memory_space=pl.ANY),
                pl.BlockSpec((None, 2 * nj, 1, tn), lambda p, m, e, r, n: (e[p], 0, 0, 0)),
                pl.BlockSpec((None, nc, 1, tn), lambda p, m, e, r, n: (e[p], 0, 0, 0)),
            ],
            out_specs=pl.BlockSpec(memory_space=pl.ANY),
            scratch_shapes=[pltpu.VMEM((rmax, d2), U32),
                            pltpu.VMEM((rmax, d), BF16),
                            pltpu.VMEM((nj, rmax, tn), BF16),
                            pltpu.VMEM((W_SLOTS, d, tn), w_gu.dtype),
                            pltpu.VMEM((W_SLOTS, d, tn), w_gu.dtype),
                            pltpu.VMEM((W_SLOTS, de, tn), w_dn.dtype),
                            pltpu.VMEM((2, rmax, tn // 2), U32),
                            pltpu.VMEM((MOE_BLOCK, tn // 2), U32),
                            pltpu.SemaphoreType.DMA(()),
                            pltpu.SemaphoreType.DMA((2,)),
                            pltpu.SemaphoreType.DMA(()),
                            pltpu.SemaphoreType.DMA((W_SLOTS,)),
                            pltpu.SemaphoreType.DMA((W_SLOTS,))]),
        out_shape=jax.ShapeDtypeStruct((n_slots, d2), U32),
        compiler_params=_cparams(1),
        name="experts",
    )(meta, pass_e, pass_row0, pass_nsub, xs, w_gu, w_dn,
      b_gu.reshape(n_e, 2 * nj, 1, tn), b_dn.reshape(n_e, nc, 1, tn))


COMBINE_CHUNK = 128


def _combine_body(dcur_ref, dnxt_ref, wt_ref, h1_ref, g2_ref, l2g_ref, l2b_ref, y_ref, out_ref, buf_ref, sem):
    i = pl.program_id(0)
    n_tiles = pl.num_programs(0)
    t = h1_ref.shape[0]
    tp = EXPERT_TN // 2
    slot = i % 2

    def issue(dest_ref, sl, g):
        for u in range(SUBLANES):
            for k in range(TOP_K):
                d = dest_ref[0, k * t + g * SUBLANES + u]
                pltpu.make_async_copy(y_ref.at[pl.ds(d, 1)], buf_ref.at[sl, k, g, pl.ds(u, 1)],
                                      sem.at[sl]).start(priority=k % 2)

    gpc = COMBINE_CHUNK // SUBLANES

    def reduce_rows(j):
        rows = pl.ds(pl.multiple_of(j * COMBINE_CHUNK, COMBINE_CHUNK), COMBINE_CHUNK)
        groups = pl.ds(pl.multiple_of(j * gpc, gpc), gpc)
        wt = wt_ref[rows, :]
        f_lo, f_hi = None, None
        for k in range(TOP_K):
            lo, hi = _unpack_pair_f32(buf_ref[slot, k, groups].reshape(COMBINE_CHUNK, buf_ref.shape[-1]))
            w = wt[:, k:k + 1]
            f_lo = lo * w if f_lo is None else f_lo + lo * w
            f_hi = hi * w if f_hi is None else f_hi + hi * w
        parts = []
        for c in range(f_lo.shape[1] // tp):
            parts += [f_lo[:, c * tp:(c + 1) * tp], f_hi[:, c * tp:(c + 1) * tp]]
        f = jnp.concatenate(parts, axis=1)
        out_ref[rows, :] = _layer_norm(DEEPNORM_ALPHA * h1_ref[rows, :] + g2_ref[...] * f,
                                       l2g_ref[...], l2b_ref[...])

    def first_gather(g, carry):
        issue(dcur_ref, 0, g)
        return carry

    @pl.when(i == 0)
    def _():
        lax.fori_loop(0, t // SUBLANES, first_gather, 0)

    for k in range(TOP_K):
        pltpu.make_async_copy(buf_ref.at[1 - slot, k], buf_ref.at[slot, k], sem.at[slot]).wait()

    def both(j, carry):
        reduce_rows(j)
        for w in range(gpc):
            issue(dnxt_ref, 1 - slot, j * gpc + w)
        return carry

    def only_reduce(j, carry):
        reduce_rows(j)
        return carry

    @pl.when(i + 1 < n_tiles)
    def _():
        lax.fori_loop(0, t // COMBINE_CHUNK, both, 0)

    @pl.when(i + 1 == n_tiles)
    def _():
        lax.fori_loop(0, t // COMBINE_CHUNK, only_reduce, 0)


def _combine(dest, wt, h1, mod3, rows_per_batch, l2g, l2b, y, tm):
    n, d = h1.shape
    n_tiles = n // tm
    dest3 = _tile_major(dest, tm)
    row = lambda i: (i * tm) // rows_per_batch
    return pl.pallas_call(
        _combine_body,
        grid=(n_tiles,),
        in_specs=[pl.BlockSpec((None, 1, tm * TOP_K), lambda i: (i, 0, 0), memory_space=pltpu.SMEM),
                  pl.BlockSpec((None, 1, tm * TOP_K), lambda i: (jnp.minimum(i + 1, n_tiles - 1), 0, 0),
                               memory_space=pltpu.SMEM),
                  pl.BlockSpec((tm, LANES), lambda i: (i, 0)),
                  pl.BlockSpec((tm, d), lambda i: (i, 0)),
                  pl.BlockSpec((None, 1, d), lambda i: (row(i), 0, 5)),
                  pl.BlockSpec((1, d), lambda i: (0, 0)),
                  pl.BlockSpec((1, d), lambda i: (0, 0)),
                  pl.BlockSpec(memory_space=pl.ANY)],
        out_specs=pl.BlockSpec((tm, d), lambda i: (i, 0)),
        out_shape=jax.ShapeDtypeStruct((n, d), F32),
        scratch_shapes=[pltpu.VMEM((2, TOP_K, tm // SUBLANES, SUBLANES, y.shape[1]), U32),
                        pltpu.SemaphoreType.DMA((2,))],
        compiler_params=_cparams(1),
        name="combine",
    )(dest3, dest3, wt, h1, mod3, l2g, l2b, y)


def _pick_tile(n, pref):
    t = pref
    while n % t:
        t //= 2
    return t


def kernel(x, c, ctx, c_ctx, ln_in_g, ln_in_b, w_ada, b_ada, w_in, cm_norm_g, cm_norm_b, cm_w_s, cm_b_s,
           gla_w_gk_f, gla_b_gk_f, gla_w_gk_b, gla_b_gk_b, gla_norm_g, w_out, ln1_g, ln1_b,
           w_router, b_router, w_gate_up, b_gate_up, w_down, b_down, ln2_g, ln2_b):
    bsz, l, d = x.shape
    lc = ctx.shape[1]
    n, nc = bsz * l, bsz * lc
    assert w_ada.shape[0] == 1, "single-layer configuration"
    assert bsz + 1 <= 8 and l % (2 * GLA_TILE) == 0 and lc % GLA_TILE == 0
    row = lambda v: v.reshape(1, -1)

    cc = jnp.concatenate([c, c_ctx[None, :], jnp.zeros((8 - bsz - 1, d), F32)], axis=0)
    mod3 = _ada(cc, w_ada[0], row(b_ada[0])).reshape(8, 1, N_MOD * d)

    n_uv = 2 * CM_HEADS * CM_CHUNK
    n_main = n_uv + 2 * GLA_HEADS * GLA_DK + 2 * GLA_HEADS * GLA_DV
    w_uv = w_in[0][:, :n_uv].astype(BF16)
    w_rest = w_in[0][:, n_uv:n_main].astype(BF16)
    w_lr = jnp.pad(w_in[0][:, n_main:].astype(BF16), ((0, 0), (0, LANES - 2 * GLA_RANK)))
    x2 = x.reshape(n, d)
    tm_x = _pick_tile(l, 512)
    bs_tile = jnp.repeat(cm_b_s[0].T, CM_CHUNK, axis=1)
    cm, p, lr = _inproj_cm(x2, mod3, l, row(ln_in_g), row(ln_in_b), w_uv, w_rest, w_lr,
                           row(cm_norm_g[0]), row(cm_norm_b[0]), cm_w_s[0].astype(BF16), bs_tile, tm_x)
    tm_c = _pick_tile(nc, 512)
    pc, lrc = _inproj(ctx.reshape(nc, d), mod3, lambda i: bsz, row(ln_in_g), row(ln_in_b),
                      w_rest, w_lr, 0, 2, tm_c, 1024)

    kw = GLA_HEADS * GLA_DK
    wgf = jnp.zeros((LANES, kw), BF16).at[:GLA_RANK].set(gla_w_gk_f[0].astype(BF16))
    wgb = jnp.zeros((LANES, kw), BF16).at[GLA_RANK:2 * GLA_RANK].set(gla_w_gk_b[0].astype(BF16))
    gla = _gla(p.reshape(bsz, l, -1), lr.reshape(bsz, l, LANES), pc.reshape(bsz, lc, -1),
               lrc.reshape(bsz, lc, LANES), wgf, row(gla_b_gk_f[0]), wgb, row(gla_b_gk_b[0]),
               row(gla_norm_g[0])).reshape(n, -1)

    w_r = jnp.pad(w_router[0], ((0, 0), (0, LANES - N_EXPERTS))).astype(BF16)
    b_r = jnp.pad(b_router[0], (0, LANES - N_EXPERTS)).reshape(1, LANES)
    h1, hp, logits = _post_attn(cm, gla, x2, mod3, l, row(ln_in_g), row(ln_in_b), w_out[0].astype(BF16),
                                row(ln1_g[0]), row(ln1_b[0]), w_r, b_r, tm_x)

    idx, wt, rank, cnt = _route(logits, _pick_tile(n, 1024))
    counts = cnt[0, :N_EXPERTS].astype(I32)
    n_blocks = (n * TOP_K + N_EXPERTS * (MOE_BLOCK - 1)) // MOE_BLOCK
    blocks_e = (counts + MOE_BLOCK - 1) // MOE_BLOCK
    blk_end = jnp.cumsum(blocks_e)
    blk_start = blk_end - blocks_e
    dest = rank[:TOP_K]
    for e in range(N_EXPERTS):
        dest = dest + jnp.where(idx[:TOP_K] == e, blk_start[e] * MOE_BLOCK, 0)
    n_pass_max = n_blocks // PASS_SUBS + N_EXPERTS
    pass_cnt = (blocks_e + PASS_SUBS - 1) // PASS_SUBS
    pass_end = jnp.cumsum(pass_cnt)
    pass_start = pass_end - pass_cnt
    pid = jnp.arange(n_pass_max, dtype=I32)
    pass_e = jnp.minimum(jnp.searchsorted(pass_end, pid, side="right"), N_EXPERTS - 1).astype(I32)
    local = pid - pass_start[pass_e]
    pass_row0 = (blk_start[pass_e] + local * PASS_SUBS).astype(I32)
    pass_nsub = jnp.clip(blocks_e[pass_e] - local * PASS_SUBS, 0, PASS_SUBS).astype(I32)
    meta = jnp.stack([pass_end[-1], blk_end[-1]]).astype(I32)

    ar = jnp.arange(N_EXPERTS, dtype=I32)
    cand = jnp.concatenate([jnp.where(counts % MOE_BLOCK != 0, blk_end - 1, -1),
                            jnp.where(blk_end[-1] + ar < n_blocks, blk_end[-1] + ar, -1)])
    zero_blocks = cand[jnp.argsort(cand < 0, stable=True)].astype(I32)
    n_zero = jnp.sum(cand >= 0).astype(I32).reshape(1)

    xs = _dispatch(n_zero, zero_blocks, dest, hp, n_blocks * MOE_BLOCK, _pick_tile(n, 512))
    ys = _experts(meta, pass_e, pass_row0, pass_nsub, xs, w_gate_up[0], b_gate_up[0], w_down[0], b_down[0])
    out = _combine(dest, wt, h1, mod3, l, row(ln2_g[0]), row(ln2_b[0]), ys, _pick_tile(n, 512))
    return out.reshape(bsz, l, d)
```

```python
import math

import jax
import jax.numpy as jnp
from jax import lax
from jax.experimental import pallas as pl
from jax.experimental.pallas import tpu as pltpu

F32 = jnp.float32
BF16 = jnp.bfloat16
U32 = jnp.uint32
I32 = jnp.int32

CM_CHUNK = 128
CM_HEADS = 8
GLA_HEADS = 4
GLA_DK = 128
GLA_DV = 256
GLA_CHUNK = 64
GLA_RANK = 16
GLA_GATE_NORMALIZER = 16.0
N_EXPERTS = 32
TOP_K = 4
MOE_BLOCK = 256
SWIGLU_LIMIT = 7.0
SWIGLU_ALPHA = 1.702
N_MOD = 6
DEEPNORM_ALPHA = 2.0 ** 0.25
LN_EPS = 1e-5
RMS_EPS = 1e-6

LANES = 128
SUBLANES = 8
VMEM_LIMIT = 56 * 1024 * 1024
ROW_CHUNK = 256


def _cparams(n_axes, vmem=VMEM_LIMIT):
    return pltpu.CompilerParams(dimension_semantics=("arbitrary",) * n_axes,
                                vmem_limit_bytes=vmem)


def _layer_norm(t, g, b):
    mu = jnp.mean(t, axis=-1, keepdims=True)
    d = t - mu
    var = jnp.mean(d * d, axis=-1, keepdims=True)
    return d * lax.rsqrt(var + LN_EPS) * g + b


def _gelu(t):
    return 0.5 * t * (1.0 + lax.erf(t * (1.0 / math.sqrt(2.0))))


def _silu(t):
    return t * jax.nn.sigmoid(t)


def _ones_where(mask, dtype):
    return jnp.where(mask, 1.0, 0.0).astype(dtype)


def _dot(a, b):
    return jnp.dot(a, b, preferred_element_type=F32)


def _dot_nt(a, b):
    return lax.dot_general(a, b, (((1,), (1,)), ((), ())), preferred_element_type=F32)


def _dot_tn(a, b):
    return lax.dot_general(a, b, (((0,), (0,)), ((), ())), preferred_element_type=F32)


def _pack_bf16_pair(lo, hi):
    lo_b = lax.bitcast_convert_type(lo.astype(BF16).astype(F32), U32)
    hi_b = lax.bitcast_convert_type(hi.astype(BF16).astype(F32), U32)
    return hi_b | (lo_b >> 16)


def _unpack_pair_f32(p):
    lo = lax.bitcast_convert_type(p << 16, F32)
    hi = lax.bitcast_convert_type(p & jnp.uint32(0xFFFF0000), F32)
    return lo, hi


def _ada_body(c_ref, w_ref, b_ref, o_ref):
    a = _silu(c_ref[...]).astype(BF16)
    o_ref[...] = _dot(a, w_ref[...].astype(BF16)) + b_ref[...]


def _ada(cc, w, b):
    rows, d = cc.shape
    n = w.shape[1]
    tn = 1024
    return pl.pallas_call(
        _ada_body,
        grid=(n // tn,),
        in_specs=[pl.BlockSpec((rows, d), lambda j: (0, 0)),
                  pl.BlockSpec((d, tn), lambda j: (0, j)),
                  pl.BlockSpec((1, tn), lambda j: (0, j))],
        out_specs=pl.BlockSpec((rows, tn), lambda j: (0, j)),
        out_shape=jax.ShapeDtypeStruct((rows, n), F32),
        compiler_params=_cparams(1),
        name="ada",
    )(cc, w, b)


def _inproj_body(x_ref, g_ref, b_ref, sh_ref, sc_ref, w_ref, wlr_ref, o_ref, olr_ref, hm_ref):
    @pl.when(pl.program_id(1) == 0)
    def _():
        h = _layer_norm(x_ref[...], g_ref[...], b_ref[...])
        hm = (h * (1.0 + sc_ref[...]) + sh_ref[...]).astype(BF16)
        hm_ref[...] = hm
        olr_ref[...] = _dot(hm, wlr_ref[...])

    o_ref[...] = _dot(hm_ref[...], w_ref[...])


def _inproj(x2, mod3, mod_row, ln_g, ln_b, w_main, w_lr, col0, ncols, tm, tn):
    r, d = x2.shape
    return pl.pallas_call(
        _inproj_body,
        grid=(r // tm, ncols),
        in_specs=[pl.BlockSpec((tm, d), lambda i, j: (i, 0)),
                  pl.BlockSpec((1, d), lambda i, j: (0, 0)),
                  pl.BlockSpec((1, d), lambda i, j: (0, 0)),
                  pl.BlockSpec((None, 1, d), lambda i, j: (mod_row(i), 0, 0)),
                  pl.BlockSpec((None, 1, d), lambda i, j: (mod_row(i), 0, 1)),
                  pl.BlockSpec((d, tn), lambda i, j: (0, col0 + j)),
                  pl.BlockSpec((d, LANES), lambda i, j: (0, 0))],
        out_specs=[pl.BlockSpec((tm, tn), lambda i, j: (i, j)),
                   pl.BlockSpec((tm, LANES), lambda i, j: (i, 0))],
        out_shape=[jax.ShapeDtypeStruct((r, ncols * tn), F32),
                   jax.ShapeDtypeStruct((r, LANES), F32)],
        scratch_shapes=[pltpu.VMEM((tm, d), BF16)],
        compiler_params=_cparams(2),
        name="inproj",
    )(x2, ln_g, ln_b, mod3, mod3, w_main, w_lr)


def _inproj_cm_body(x_ref, g_ref, b_ref, sh_ref, sc_ref, wuv_ref, wr_ref, wlr_ref, ng_ref, nb_ref, ws_ref, bs_ref,
                    cm_ref, p_ref, lr_ref):
    tm = x_ref.shape[0]
    half = wuv_ref.shape[1] // 2
    hd = CM_CHUNK
    for r in range(0, tm, ROW_CHUNK):
        rows = slice(r, r + ROW_CHUNK)
        h = _layer_norm(x_ref[rows, :], g_ref[...], b_ref[...])
        hm = (h * (1.0 + sc_ref[...]) + sh_ref[...]).astype(BF16)
        uv = _dot(hm, wuv_ref[...])
        u = _gelu(uv[:, :half])
        vb = _layer_norm(_gelu(uv[:, half:]), ng_ref[...], nb_ref[...]).astype(BF16)
        for c in range(ROW_CHUNK // CM_CHUNK):
            crow = slice(c * CM_CHUNK, (c + 1) * CM_CHUNK)
            orow = slice(r + c * CM_CHUNK, r + (c + 1) * CM_CHUNK)
            for hh in range(CM_HEADS):
                cols = slice(hh * hd, (hh + 1) * hd)
                s = _dot(ws_ref[hh], vb[crow, cols]) + bs_ref[:, cols]
                cm_ref[orow, cols] = (u[crow, cols] * s).astype(BF16)
        p_ref[rows, :] = _dot(hm, wr_ref[...])
        lr_ref[rows, :] = _dot(hm, wlr_ref[...])


def _inproj_cm(x2, mod3, rows_per_batch, ln_g, ln_b, w_uv, w_rest, w_lr, ng, nb, ws, bs, tm):
    r, d = x2.shape
    n_uv, n_rest = w_uv.shape[1], w_rest.shape[1]
    row = lambda i: (i * tm) // rows_per_batch

    def const(shape):
        return pl.BlockSpec(shape, lambda i: (0,) * len(shape), pipeline_mode=pl.Buffered(1))

    return pl.pallas_call(
        _inproj_cm_body,
        grid=(r // tm,),
        in_specs=[pl.BlockSpec((tm, d), lambda i: (i, 0)),
                  const((1, d)), const((1, d)),
                  pl.BlockSpec((None, 1, d), lambda i: (row(i), 0, 0)),
                  pl.BlockSpec((None, 1, d), lambda i: (row(i), 0, 1)),
                  const((d, n_uv)), const((d, n_rest)), const((d, LANES)),
                  const((1, n_uv // 2)), const((1, n_uv // 2)),
                  const((CM_HEADS, CM_CHUNK, CM_CHUNK)), const((CM_CHUNK, n_uv // 2))],
        out_specs=[pl.BlockSpec((tm, n_uv // 2), lambda i: (i, 0)),
                   pl.BlockSpec((tm, n_rest), lambda i: (i, 0)),
                   pl.BlockSpec((tm, LANES), lambda i: (i, 0))],
        out_shape=[jax.ShapeDtypeStruct((r, n_uv // 2), BF16),
                   jax.ShapeDtypeStruct((r, n_rest), F32),
                   jax.ShapeDtypeStruct((r, LANES), F32)],
        compiler_params=_cparams(1),
        name="inproj_cm",
    )(x2, ln_g, ln_b, mod3, mod3, w_uv, w_rest, w_lr, ng, nb, ws, bs)


GLA_TILE = 256
GLA_TILES_PER_ITER = 8


def _gla_tile(q, k, v, lr, wg, bg, st_ref, forward, need_o):
    t = k.shape[0]
    n_chunks = t // GLA_CHUNK
    z = _dot(lr.astype(BF16), wg) + bg
    g = jax.nn.log_sigmoid(z) * (1.0 / GLA_GATE_NORMALIZER)
    r_id = lax.broadcasted_iota(I32, (t, t), 0)
    c_id = lax.broadcasted_iota(I32, (t, t), 1)
    shift = GLA_CHUNK.bit_length() - 1
    same = (r_id >> shift) == (c_id >> shift)
    lower = same & (c_id <= r_id)
    tri = _ones_where(lower, BF16)
    g_hi = g.astype(BF16)
    g_lo = (g - g_hi.astype(F32)).astype(BF16)
    csum2 = _dot(tri, jnp.concatenate([g_hi, g_lo], axis=1))
    csum = csum2[:, :GLA_DK] + csum2[:, GLA_DK:]
    g3 = g.reshape(n_chunks, GLA_CHUNK, GLA_DK)
    tot = jnp.broadcast_to(jnp.sum(g3, axis=1, keepdims=True), g3.shape).reshape(t, GLA_DK)
    bcum = csum if forward else tot - csum + g
    kd = (k * jnp.exp(tot - bcum)).astype(BF16)
    decay = jnp.exp(tot)
    vb = v.astype(BF16)
    row_chunk = lax.broadcasted_iota(I32, (t, GLA_DK), 0) >> shift
    kd_blocks = jnp.concatenate([jnp.where(row_chunk == c, kd, jnp.zeros_like(kd)) for c in range(n_chunks)], axis=1)
    u_all = _dot_tn(vb, kd_blocks)
    o = None
    if need_o:
        qe = ((q * (GLA_DK ** -0.5)) * jnp.exp(bcum)).astype(BF16)
        ke = (k * jnp.exp(-bcum)).astype(BF16)
        att = _dot_nt(qe, ke)
        mask = lower if forward else same & (c_id >= r_id)
        att = jnp.where(mask, att, 0.0).astype(BF16)
        o = _dot(att, vb)
    outs = [None] * n_chunks
    order = range(n_chunks) if forward else range(n_chunks - 1, -1, -1)
    for c in order:
        rows = slice(c * GLA_CHUNK, (c + 1) * GLA_CHUNK)
        s_t = st_ref[...]
        if need_o:
            outs[c] = o[rows] + _dot_nt(qe[rows], s_t.astype(BF16))
        u_t = u_all[:, c * GLA_DK:(c + 1) * GLA_DK]
        st_ref[...] = s_t * decay[c * GLA_CHUNK:c * GLA_CHUNK + 1, :] + u_t
    if need_o:
        return jnp.concatenate(outs, axis=0)
    return None


def _gla_body(q_ref, k_ref, v_ref, go_ref, lr_ref, kc_ref, vc_ref, lrc_ref,
              wgf_ref, bgf_ref, wgb_ref, bgb_ref, ng_ref, out_ref, o_scr, sf_ref, sb_ref):
    t = GLA_TILE
    n_x = q_ref.shape[0] // t
    n_c = kc_ref.shape[0] // t
    half = n_x // 2
    sf_ref[...] = jnp.zeros_like(sf_ref)
    sb_ref[...] = jnp.zeros_like(sb_ref)
    wgf, bgf, wgb, bgb = wgf_ref[...], bgf_ref[...], wgb_ref[...], bgb_ref[...]

    for i in range(n_c):
        rf = slice(i * t, (i + 1) * t)
        rb = slice((n_c - 1 - i) * t, (n_c - i) * t)
        _gla_tile(None, kc_ref[rf], vc_ref[rf], lrc_ref[rf], wgf, bgf, sf_ref, True, False)
        _gla_tile(None, kc_ref[rb], vc_ref[rb], lrc_ref[rb], wgb, bgb, sb_ref, False, False)

    def tile_out(i, forward):
        rows = pl.ds(pl.multiple_of(i * t, t), t)
        if forward:
            return rows, _gla_tile(q_ref[rows], k_ref[rows], v_ref[rows], lr_ref[rows],
                                   wgf, bgf, sf_ref, True, True)
        return rows, _gla_tile(q_ref[rows], k_ref[rows], v_ref[rows], lr_ref[rows],
                               wgb, bgb, sb_ref, False, True)

    def finish(rows, o):
        o = o + o_scr[rows]
        ms = jnp.mean(o * o, axis=-1, keepdims=True)
        on = o * lax.rsqrt(ms + RMS_EPS) * ng_ref[...]
        out_ref[rows] = (on * _silu(go_ref[rows])).astype(BF16)

    def keep(rows, o):
        o_scr[rows] = o

    u = math.gcd(GLA_TILES_PER_ITER, half)

    def make_step(sink):
        def step(it, carry):
            for w in range(u):
                i = it * u + w
                sink(*tile_out(i, True))
                sink(*tile_out(n_x - 1 - i, False))
            return carry
        return step

    lax.fori_loop(0, half // u, make_step(keep), 0)
    lax.fori_loop(half // u, n_x // u, make_step(finish), 0)


def _gla(p3, lr3, pc3, lrc3, wgf, bgf, wgb, bgb, ng):
    bsz, l, _ = p3.shape
    lc = pc3.shape[1]
    dk, dv = GLA_DK, GLA_DV
    kw = GLA_HEADS * dk
    q0, k0 = 0, kw // dk
    v0, go0 = 2 * kw // dv, (2 * kw + GLA_HEADS * dv) // dv
    kc0, vc0 = k0, v0
    return pl.pallas_call(
        _gla_body,
        grid=(bsz, GLA_HEADS),
        in_specs=[pl.BlockSpec((None, l, dk), lambda b, h: (b, 0, q0 + h)),
                  pl.BlockSpec((None, l, dk), lambda b, h: (b, 0, k0 + h)),
                  pl.BlockSpec((None, l, dv), lambda b, h: (b, 0, v0 + h)),
                  pl.BlockSpec((None, l, dv), lambda b, h: (b, 0, go0 + h)),
                  pl.BlockSpec((None, l, LANES), lambda b, h: (b, 0, 0)),
                  pl.BlockSpec((None, lc, dk), lambda b, h: (b, 0, kc0 + h)),
                  pl.BlockSpec((None, lc, dv), lambda b, h: (b, 0, vc0 + h)),
                  pl.BlockSpec((None, lc, LANES), lambda b, h: (b, 0, 0)),
                  pl.BlockSpec((LANES, dk), lambda b, h: (0, h)),
                  pl.BlockSpec((1, dk), lambda b, h: (0, h)),
                  pl.BlockSpec((LANES, dk), lambda b, h: (0, h)),
                  pl.BlockSpec((1, dk), lambda b, h: (0, h)),
                  pl.BlockSpec((1, dv), lambda b, h: (0, 0))],
        out_specs=pl.BlockSpec((None, l, dv), lambda b, h: (b, 0, h)),
        out_shape=jax.ShapeDtypeStruct((bsz, l, GLA_HEADS * dv), BF16),
        scratch_shapes=[pltpu.VMEM((l, dv), F32),
                        pltpu.VMEM((dv, dk), F32),
                        pltpu.VMEM((dv, dk), F32)],
        compiler_params=_cparams(2),
        name="gla",
    )(p3, p3, p3, p3, lr3, pc3, pc3, lrc3, wgf, bgf, wgb, bgb, ng)


def _post_body(cm_ref, gla_ref, x_ref, lng_ref, lnb_ref, g1_ref, sh2_ref, sc2_ref,
               wo_ref, l1g_ref, l1b_ref, wr_ref, br_ref, h1_ref, hp_ref, lg_ref):
    half = cm_ref.shape[1]
    d2 = x_ref.shape[1] // 2
    for r in range(0, x_ref.shape[0], ROW_CHUNK):
        rows = slice(r, r + ROW_CHUNK)
        y = _dot(cm_ref[rows, :], wo_ref[:half, :]) + _dot(gla_ref[rows, :], wo_ref[half:, :])
        hx = _layer_norm(x_ref[rows, :], lng_ref[...], lnb_ref[...])
        h1 = _layer_norm(DEEPNORM_ALPHA * hx + g1_ref[...] * y, l1g_ref[...], l1b_ref[...])
        h1_ref[rows, :] = h1
        hm = h1 * (1.0 + sc2_ref[...]) + sh2_ref[...]
        hp_ref[rows, :] = _pack_bf16_pair(hm[:, :d2], hm[:, d2:])
        lg_ref[rows, :] = _dot(hm.astype(BF16), wr_ref[...]) + br_ref[...]


def _post_attn(cm, gla, x2, mod3, rows_per_batch, ln_g, ln_b, w_out, l1g, l1b, w_r, b_r, tm):
    r, d = x2.shape
    half = d // 2
    row = lambda i: (i * tm) // rows_per_batch
    full = lambda shape: pl.BlockSpec(shape, lambda i: (0,) * len(shape))
    return pl.pallas_call(
        _post_body,
        grid=(r // tm,),
        in_specs=[pl.BlockSpec((tm, half), lambda i: (i, 0)),
                  pl.BlockSpec((tm, half), lambda i: (i, 0)),
                  pl.BlockSpec((tm, d), lambda i: (i, 0)),
                  full((1, d)), full((1, d)),
                  pl.BlockSpec((None, 1, d), lambda i: (row(i), 0, 2)),
                  pl.BlockSpec((None, 1, d), lambda i: (row(i), 0, 3)),
                  pl.BlockSpec((None, 1, d), lambda i: (row(i), 0, 4)),
                  full((d, d)), full((1, d)), full((1, d)),
                  full((d, LANES)), full((1, LANES))],
        out_specs=[pl.BlockSpec((tm, d), lambda i: (i, 0)),
                   pl.BlockSpec((tm, half), lambda i: (i, 0)),
                   pl.BlockSpec((tm, LANES), lambda i: (i, 0))],
        out_shape=[jax.ShapeDtypeStruct((r, d), F32),
                   jax.ShapeDtypeStruct((r, half), U32),
                   jax.ShapeDtypeStruct((r, LANES), F32)],
        compiler_params=_cparams(1),
        name="post_attn",
    )(cm, gla, x2, ln_g, ln_b, mod3, mod3, mod3, w_out, l1g, l1b, w_r, b_r)


def _route_body(lg_ref, idx_ref, wt_ref, rank_ref, cnt_ref, carry_ref):
    i = pl.program_id(0)
    t = lg_ref.shape[0]

    @pl.when(i == 0)
    def _():
        carry_ref[...] = jnp.zeros_like(carry_ref)

    lane = lax.broadcasted_iota(I32, (t, LANES), 1)
    lane_f = lane.astype(F32)
    neg = jnp.float32(-jnp.inf)
    l = jnp.where(lane < N_EXPERTS, lg_ref[...], neg)
    tops, onehots, idxs = [], [], []
    for _ in range(TOP_K):
        m = jnp.max(l, axis=-1, keepdims=True)
        idx = jnp.min(jnp.where(l == m, lane_f, float(LANES)), axis=-1, keepdims=True).astype(I32)
        oh = lane == idx
        l = jnp.where(oh, neg, l)
        tops.append(m)
        idxs.append(idx)
        onehots.append(oh)
    exps = [jnp.exp(m - tops[0]) for m in tops]
    denom = exps[0] + exps[1] + exps[2] + exps[3]
    sel = _ones_where(onehots[0] | onehots[1] | onehots[2] | onehots[3], F32)
    r_id = lax.broadcasted_iota(I32, (t, t), 0)
    c_id = lax.broadcasted_iota(I32, (t, t), 1)
    strict = _ones_where(c_id < r_id, BF16)
    before = _dot(strict, sel.astype(BF16)) + carry_ref[0:1, :]
    idx_out = jnp.zeros((t, LANES), I32)
    wt_out = jnp.zeros((t, LANES), F32)
    rank_out = jnp.zeros((t, LANES), I32)
    for k in range(TOP_K):
        rk = jnp.sum(jnp.where(onehots[k], before, 0.0), axis=-1, keepdims=True).astype(I32)
        idx_out = jnp.where(lane == k, idxs[k], idx_out)
        wt_out = jnp.where(lane == k, exps[k] / denom, wt_out)
        rank_out = jnp.where(lane == k, rk, rank_out)
    idx_ref[...] = jnp.transpose(idx_out)[:SUBLANES, :]
    wt_ref[...] = wt_out
    rank_ref[...] = jnp.transpose(rank_out)[:SUBLANES, :]
    total = carry_ref[0:1, :] + jnp.sum(sel, axis=0, keepdims=True)
    carry_ref[...] = jnp.broadcast_to(total, carry_ref.shape)
    cnt_ref[...] = jnp.broadcast_to(total, cnt_ref.shape)


def _route(logits, tm):
    n = logits.shape[0]
    blk = pl.BlockSpec((tm, LANES), lambda i: (i, 0))
    small = pl.BlockSpec((SUBLANES, tm), lambda i: (0, i))
    return pl.pallas_call(
        _route_body,
        grid=(n // tm,),
        in_specs=[blk],
        out_specs=[small, blk, small, pl.BlockSpec((8, LANES), lambda i: (0, 0))],
        out_shape=[jax.ShapeDtypeStruct((SUBLANES, n), I32),
                   jax.ShapeDtypeStruct((n, LANES), F32),
                   jax.ShapeDtypeStruct((SUBLANES, n), I32),
                   jax.ShapeDtypeStruct((8, LANES), F32)],
        scratch_shapes=[pltpu.VMEM((8, LANES), F32)],
        compiler_params=_cparams(1),
        name="route",
    )(logits)


def _dispatch_body(nz_ref, zl_ref, dest_ref, hp_ref, xs_ref, zero_ref, sem, zsem):
    t = hp_ref.shape[0] * hp_ref.shape[1]

    @pl.when(pl.program_id(0) == 0)
    def _():
        zero_ref[...] = jnp.zeros_like(zero_ref)

        def z_copy(b):
            r = pl.multiple_of(zl_ref[b] * MOE_BLOCK, MOE_BLOCK)
            return pltpu.make_async_copy(zero_ref, xs_ref.at[pl.ds(r, MOE_BLOCK)], zsem)

        def start(b, carry):
            z_copy(b).start()
            return carry

        def wait(b, carry):
            z_copy(b).wait()
            return carry

        lax.fori_loop(0, nz_ref[0], start, 0)
        lax.fori_loop(0, nz_ref[0], wait, 0)

    def issue(g, carry):
        for u in range(SUBLANES):
            for k in range(TOP_K):
                d = dest_ref[0, k * t + g * SUBLANES + u]
                pltpu.make_async_copy(hp_ref.at[g, pl.ds(u, 1)], xs_ref.at[pl.ds(d, 1)], sem).start(priority=k % 2)
        return carry

    lax.fori_loop(0, t // SUBLANES, issue, 0)
    for _ in range(TOP_K):
        pltpu.make_async_copy(xs_ref.at[pl.ds(0, t)], xs_ref.at[pl.ds(0, t)], sem).wait()


def _tile_major(slot, tm):
    n = slot.shape[1]
    return slot.reshape(TOP_K, n // tm, tm).transpose(1, 0, 2).reshape(n // tm, 1, TOP_K * tm)


def _dispatch(n_zero, zero_blocks, dest, hp, n_slots, tm):
    n, w = hp.shape
    dest3 = _tile_major(dest, tm)
    return pl.pallas_call(
        _dispatch_body,
        grid_spec=pltpu.PrefetchScalarGridSpec(
            num_scalar_prefetch=2,
            grid=(n // tm,),
            in_specs=[pl.BlockSpec((None, 1, tm * TOP_K), lambda i, nz, zl: (i, 0, 0), memory_space=pltpu.SMEM),
                      pl.BlockSpec((tm // SUBLANES, SUBLANES, w), lambda i, nz, zl: (i, 0, 0))],
            out_specs=pl.BlockSpec(memory_space=pl.ANY),
            scratch_shapes=[pltpu.VMEM((MOE_BLOCK, w), U32),
                            pltpu.SemaphoreType.DMA(()),
                            pltpu.SemaphoreType.DMA(())]),
        out_shape=jax.ShapeDtypeStruct((n_slots, w), U32),
        compiler_params=_cparams(1),
        name="dispatch",
    )(n_zero, zero_blocks, dest3, hp.reshape(n // SUBLANES, SUBLANES, w))


PASS_SUBS = 9
EXPERT_TN = 256
DOT_ROWS = 1024
W_SLOTS = 3
W_AHEAD = 2


def _expert_body(meta_ref, pe_ref, pr_ref, pn_ref, xs_hbm, wgu_hbm, wdn_hbm, bgu_ref, bdn_ref,
                 y_hbm, xraw, xb, h_ref, wg_buf, wu_buf, wd_buf, ybuf, zbuf,
                 sem_x, sem_y, sem_z, sem_w, sem_d):
    sub = MOE_BLOCK
    tn = EXPERT_TN
    tp = tn // 2
    nj = h_ref.shape[0]
    nc = y_hbm.shape[1] // tp
    de = nj * tn
    d2 = xraw.shape[1]
    p = pl.program_id(0)
    n_pass = meta_ref[0]
    nsub = pn_ref[p]
    row0 = pr_ref[p]

    def x_copy(i, pp):
        r = pl.multiple_of((pr_ref[pp] + i) * sub, sub)
        return pltpu.make_async_copy(xs_hbm.at[pl.ds(r, sub)], xraw.at[pl.ds(i * sub, sub)], sem_x)

    def y_copy(start, size, r0, c, slot):
        r = pl.multiple_of(r0 * sub + start, sub)
        col = pl.multiple_of(c * tp, tp)
        return pltpu.make_async_copy(ybuf.at[slot, pl.ds(start, size), :],
                                     y_hbm.at[pl.ds(r, size), pl.ds(col, tp)], sem_y.at[slot])

    def z_copy(b, c):
        r = pl.multiple_of(b * sub, sub)
        return pltpu.make_async_copy(zbuf, y_hbm.at[pl.ds(r, sub), pl.ds(c * tp, tp)], sem_z)

    def w1_copies(pp, j):
        e = pe_ref[pp]
        slot = j % W_SLOTS
        col = pl.multiple_of(j * tn, tn)
        return (pltpu.make_async_copy(wgu_hbm.at[e, :, pl.ds(col, tn)], wg_buf.at[slot], sem_w.at[slot]),
                pltpu.make_async_copy(wgu_hbm.at[e, :, pl.ds(de + col, tn)], wu_buf.at[slot], sem_w.at[slot]))

    def w2_copy(pp, c):
        slot = c % W_SLOTS
        col = pl.multiple_of(c * tn, tn)
        return pltpu.make_async_copy(wdn_hbm.at[pe_ref[pp], :, pl.ds(col, tn)], wd_buf.at[slot], sem_d.at[slot])

    def start_w1(pp, j):
        for cp in w1_copies(pp, j):
            cp.start()

    def for_subs(count, fn):
        for i in range(PASS_SUBS):
            pl.when(i < count)(lambda i=i: fn(i))

    def for_tail(fn):
        def body(b, carry):
            for c in range(nc):
                fn(b, c)
            return carry
        lax.fori_loop(meta_ref[1], y_hbm.shape[0] // sub, body, 0)

    def for_groups(count, fn):
        pl.when(count == PASS_SUBS)(lambda: fn(0, PASS_SUBS * sub))
        k = 1 << (PASS_SUBS.bit_length() - 1)
        while k:
            start = pl.multiple_of((count & (-2 * k)) * sub, sub)
            pl.when((count != PASS_SUBS) & ((count & k) != 0))(lambda start=start, k=k: fn(start, k * sub))
            k //= 2

    @pl.when(p == 0)
    def _():
        for_subs(nsub, lambda i: x_copy(i, p).start())
        for j in range(W_AHEAD):
            start_w1(p, j)
        zbuf[...] = jnp.zeros_like(zbuf)
        for_tail(lambda b, c: z_copy(b, c).start())

    for_subs(nsub, lambda i: x_copy(i, p).wait())

    def unpack(i):
        rows = slice(i * sub, (i + 1) * sub)
        lo, hi = _unpack_pair_f32(xraw[rows, :])
        xb[rows, :d2] = lo.astype(BF16)
        xb[rows, d2:] = hi.astype(BF16)

    for_subs(nsub, unpack)

    @pl.when(p + 1 < n_pass)
    def _():
        for_subs(pn_ref[p + 1], lambda i: x_copy(i, p + 1).start())

    def first_step(j, carry):
        slot = j % W_SLOTS
        for cp in w1_copies(p, j):
            cp.wait()
        nxt = j + W_AHEAD
        pl.when(nxt < nj)(lambda: start_w1(p, nxt))
        pl.when(nxt >= nj)(lambda: w2_copy(p, nxt - nj).start())

        def group(start, size):
            wg = wg_buf[slot].astype(BF16)
            wu = wu_buf[slot].astype(BF16)
            dr = DOT_ROWS if size % DOT_ROWS == 0 else size
            for r in range(0, size, dr):
                rows = pl.ds(start + r, dr)
                x = xb[rows, :]
                gate = jnp.minimum(_dot(x, wg) + bgu_ref[j], SWIGLU_LIMIT)
                up = jnp.clip(_dot(x, wu) + bgu_ref[nj + j], -SWIGLU_LIMIT, SWIGLU_LIMIT)
                h_ref[j, rows, :] = ((up + 1.0) * gate * jax.nn.sigmoid(SWIGLU_ALPHA * gate)).astype(BF16)

        for_groups(nsub, group)
        return carry

    lax.fori_loop(0, nj, first_step, 0)

    def second_step(c, carry):
        wslot = c % W_SLOTS
        slot = c % 2
        w2_copy(p, c).wait()
        nxt = c + W_AHEAD
        pl.when(nxt < nc)(lambda: w2_copy(p, nxt).start())
        pl.when((nxt >= nc) & (p + 1 < n_pass))(lambda: start_w1(p + 1, nxt - nc))

        @pl.when(c >= 2)
        def _():
            for_groups(nsub, lambda start, size: y_copy(start, size, row0, c - 2, slot).wait())

        @pl.when((c < 2) & (p > 0))
        def _():
            for_groups(pn_ref[p - 1],
                       lambda start, size: y_copy(start, size, pr_ref[p - 1], nc - 2 + c, slot).wait())

        def group(start, size):
            wd = wd_buf[wslot].astype(BF16)
            dr = DOT_ROWS if size % DOT_ROWS == 0 else size
            for r in range(0, size, dr):
                rows = pl.ds(start + r, dr)
                hx = jnp.concatenate([h_ref[j, rows, :] for j in range(nj)], axis=1)
                y = _dot(hx, wd) + bdn_ref[c]
                ybuf[slot, rows, :] = _pack_bf16_pair(y[:, :tp], y[:, tp:])
            y_copy(start, size, row0, c, slot).start()

        for_groups(nsub, group)
        return carry

    lax.fori_loop(0, nc, second_step, 0)

    @pl.when(p == n_pass - 1)
    def _():
        for c in (nc - 2, nc - 1):
            for_groups(nsub, lambda start, size, c=c: y_copy(start, size, row0, c, c % 2).wait())
        for_tail(lambda b, cc: z_copy(b, cc).wait())


def _experts(meta, pass_e, pass_row0, pass_nsub, xs, w_gu, b_gu, w_dn, b_dn):
    n_slots, d2 = xs.shape
    d = 2 * d2
    n_e, de = w_dn.shape[0], w_dn.shape[1]
    tn = EXPERT_TN
    nj = de // tn
    nc = d // tn
    assert W_AHEAD < W_SLOTS and W_AHEAD <= min(nj, nc) and nc % 2 == 0
    rmax = PASS_SUBS * MOE_BLOCK
    return pl.pallas_call(
        _expert_body,
        grid_spec=pltpu.PrefetchScalarGridSpec(
            num_scalar_prefetch=4,
            grid=(meta[0],),
            in_specs=[
                pl.BlockSpec(memory_space=pl.ANY),
                pl.BlockSpec(memory_space=pl.ANY),
                pl.BlockSpec(memory_space=pl.ANY),
                pl.BlockSpec((None, 2 * nj, 1, tn), lambda p, m, e, r, n: (e[p], 0, 0, 0)),
                pl.BlockSpec((None, nc, 1, tn), lambda p, m, e, r, n: (e[p], 0, 0, 0)),
            ],
            out_specs=pl.BlockSpec(memory_space=pl.ANY),
            scratch_shapes=[pltpu.VMEM((rmax, d2), U32),
                            pltpu.VMEM((rmax, d), BF16),
                            pltpu.VMEM((nj, rmax, tn), BF16),
                            pltpu.VMEM((W_SLOTS, d, tn), w_gu.dtype),
                            pltpu.VMEM((W_SLOTS, d, tn), w_gu.dtype),
                            pltpu.VMEM((W_SLOTS, de, tn), w_dn.dtype),
                            pltpu.VMEM((2, rmax, tn // 2), U32),
                            pltpu.VMEM((MOE_BLOCK, tn // 2), U32),
                            pltpu.SemaphoreType.DMA(()),
                            pltpu.SemaphoreType.DMA((2,)),
                            pltpu.SemaphoreType.DMA(()),
                            pltpu.SemaphoreType.DMA((W_SLOTS,)),
                            pltpu.SemaphoreType.DMA((W_SLOTS,))]),
        out_shape=jax.ShapeDtypeStruct((n_slots, d2), U32),
        compiler_params=_cparams(1),
        name="experts",
    )(meta, pass_e, pass_row0, pass_nsub, xs, w_gu, w_dn,
      b_gu.reshape(n_e, 2 * nj, 1, tn), b_dn.reshape(n_e, nc, 1, tn))


COMBINE_CHUNK = 128


def _combine_body(dcur_ref, dnxt_ref, wt_ref, h1_ref, g2_ref, l2g_ref, l2b_ref, y_ref, out_ref, buf_ref, sem):
    i = pl.program_id(0)
    n_tiles = pl.num_programs(0)
    t = h1_ref.shape[0]
    tp = EXPERT_TN // 2
    slot = i % 2

    def issue(dest_ref, sl, g):
        for u in range(SUBLANES):
            for k in range(TOP_K):
                d = dest_ref[0, k * t + g * SUBLANES + u]
                pltpu.make_async_copy(y_ref.at[pl.ds(d, 1)], buf_ref.at[sl, k, g, pl.ds(u, 1)],
                                      sem.at[sl]).start(priority=k % 2)

    gpc = COMBINE_CHUNK // SUBLANES

    def reduce_rows(j):
        rows = pl.ds(pl.multiple_of(j * COMBINE_CHUNK, COMBINE_CHUNK), COMBINE_CHUNK)
        groups = pl.ds(pl.multiple_of(j * gpc, gpc), gpc)
        wt = wt_ref[rows, :]
        f_lo, f_hi = None, None
        for k in range(TOP_K):
            lo, hi = _unpack_pair_f32(buf_ref[slot, k, groups].reshape(COMBINE_CHUNK, buf_ref.shape[-1]))
            w = wt[:, k:k + 1]
            f_lo = lo * w if f_lo is None else f_lo + lo * w
            f_hi = hi * w if f_hi is None else f_hi + hi * w
        parts = []
        for c in range(f_lo.shape[1] // tp):
            parts += [f_lo[:, c * tp:(c + 1) * tp], f_hi[:, c * tp:(c + 1) * tp]]
        f = jnp.concatenate(parts, axis=1)
        out_ref[rows, :] = _layer_norm(DEEPNORM_ALPHA * h1_ref[rows, :] + g2_ref[...] * f,
                                       l2g_ref[...], l2b_ref[...])

    def first_gather(g, carry):
        issue(dcur_ref, 0, g)
        return carry

    @pl.when(i == 0)
    def _():
        lax.fori_loop(0, t // SUBLANES, first_gather, 0)

    for k in range(TOP_K):
        pltpu.make_async_copy(buf_ref.at[1 - slot, k], buf_ref.at[slot, k], sem.at[slot]).wait()

    def both(j, carry):
        reduce_rows(j)
        for w in range(gpc):
            issue(dnxt_ref, 1 - slot, j * gpc + w)
        return carry

    def only_reduce(j, carry):
        reduce_rows(j)
        return carry

    @pl.when(i + 1 < n_tiles)
    def _():
        lax.fori_loop(0, t // COMBINE_CHUNK, both, 0)

    @pl.when(i + 1 == n_tiles)
    def _():
        lax.fori_loop(0, t // COMBINE_CHUNK, only_reduce, 0)


def _combine(dest, wt, h1, mod3, rows_per_batch, l2g, l2b, y, tm):
    n, d = h1.shape
    n_tiles = n // tm
    dest3 = _tile_major(dest, tm)
    row = lambda i: (i * tm) // rows_per_batch
    return pl.pallas_call(
        _combine_body,
        grid=(n_tiles,),
        in_specs=[pl.BlockSpec((None, 1, tm * TOP_K), lambda i: (i, 0, 0), memory_space=pltpu.SMEM),
                  pl.BlockSpec((None, 1, tm * TOP_K), lambda i: (jnp.minimum(i + 1, n_tiles - 1), 0, 0),
                               memory_space=pltpu.SMEM),
                  pl.BlockSpec((tm, LANES), lambda i: (i, 0)),
                  pl.BlockSpec((tm, d), lambda i: (i, 0)),
                  pl.BlockSpec((None, 1, d), lambda i: (row(i), 0, 5)),
                  pl.BlockSpec((1, d), lambda i: (0, 0)),
                  pl.BlockSpec((1, d), lambda i: (0, 0)),
                  pl.BlockSpec(memory_space=pl.ANY)],
        out_specs=pl.BlockSpec((tm, d), lambda i: (i, 0)),
        out_shape=jax.ShapeDtypeStruct((n, d), F32),
        scratch_shapes=[pltpu.VMEM((2, TOP_K, tm // SUBLANES, SUBLANES, y.shape[1]), U32),
                        pltpu.SemaphoreType.DMA((2,))],
        compiler_params=_cparams(1),
        name="combine",
    )(dest3, dest3, wt, h1, mod3, l2g, l2b, y)


def _pick_tile(n, pref):
    t = pref
    while n % t:
        t //= 2
    return t


def kernel(x, c, ctx, c_ctx, ln_in_g, ln_in_b, w_ada, b_ada, w_in, cm_norm_g, cm_norm_b, cm_w_s, cm_b_s,
           gla_w_gk_f, gla_b_gk_f, gla_w_gk_b, gla_b_gk_b, gla_norm_g, w_out, ln1_g, ln1_b,
           w_router, b_router, w_gate_up, b_gate_up, w_down, b_down, ln2_g, ln2_b):
    bsz, l, d = x.shape
    lc = ctx.shape[1]
    n, nc = bsz * l, bsz * lc
    assert w_ada.shape[0] == 1, "single-layer configuration"
    assert bsz + 1 <= 8 and l % (2 * GLA_TILE) == 0 and lc % GLA_TILE == 0
    row = lambda v: v.reshape(1, -1)

    cc = jnp.concatenate([c, c_ctx[None, :], jnp.zeros((8 - bsz - 1, d), F32)], axis=0)
    mod3 = _ada(cc, w_ada[0], row(b_ada[0])).reshape(8, 1, N_MOD * d)

    n_uv = 2 * CM_HEADS * CM_CHUNK
    n_main = n_uv + 2 * GLA_HEADS * GLA_DK + 2 * GLA_HEADS * GLA_DV
    w_uv = w_in[0][:, :n_uv].astype(BF16)
    w_rest = w_in[0][:, n_uv:n_main].astype(BF16)
    w_lr = jnp.pad(w_in[0][:, n_main:].astype(BF16), ((0, 0), (0, LANES - 2 * GLA_RANK)))
    x2 = x.reshape(n, d)
    tm_x = _pick_tile(l, 512)
    bs_tile = jnp.repeat(cm_b_s[0].T, CM_CHUNK, axis=1)
    cm, p, lr = _inproj_cm(x2, mod3, l, row(ln_in_g), row(ln_in_b), w_uv, w_rest, w_lr,
                           row(cm_norm_g[0]), row(cm_norm_b[0]), cm_w_s[0].astype(BF16), bs_tile, tm_x)
    tm_c = _pick_tile(nc, 512)
    pc, lrc = _inproj(ctx.reshape(nc, d), mod3, lambda i: bsz, row(ln_in_g), row(ln_in_b),
                      w_rest, w_lr, 0, 2, tm_c, 1024)

    kw = GLA_HEADS * GLA_DK
    wgf = jnp.zeros((LANES, kw), BF16).at[:GLA_RANK].set(gla_w_gk_f[0].astype(BF16))
    wgb = jnp.zeros((LANES, kw), BF16).at[GLA_RANK:2 * GLA_RANK].set(gla_w_gk_b[0].astype(BF16))
    gla = _gla(p.reshape(bsz, l, -1), lr.reshape(bsz, l, LANES), pc.reshape(bsz, lc, -1),
               lrc.reshape(bsz, lc, LANES), wgf, row(gla_b_gk_f[0]), wgb, row(gla_b_gk_b[0]),
               row(gla_norm_g[0])).reshape(n, -1)

    w_r = jnp.pad(w_router[0], ((0, 0), (0, LANES - N_EXPERTS))).astype(BF16)
    b_r = jnp.pad(b_router[0], (0, LANES - N_EXPERTS)).reshape(1, LANES)
    h1, hp, logits = _post_attn(cm, gla, x2, mod3, l, row(ln_in_g), row(ln_in_b), w_out[0].astype(BF16),
                                row(ln1_g[0]), row(ln1_b[0]), w_r, b_r, tm_x)

    idx, wt, rank, cnt = _route(logits, _pick_tile(n, 1024))
    counts = cnt[0, :N_EXPERTS].astype(I32)
    n_blocks = (n * TOP_K + N_EXPERTS * (MOE_BLOCK - 1)) // MOE_BLOCK
    blocks_e = (counts + MOE_BLOCK - 1) // MOE_BLOCK
    blk_end = jnp.cumsum(blocks_e)
    blk_start = blk_end - blocks_e
    dest = rank[:TOP_K]
    for e in range(N_EXPERTS):
        dest = dest + jnp.where(idx[:TOP_K] == e, blk_start[e] * MOE_BLOCK, 0)
    n_pass_max = n_blocks // PASS_SUBS + N_EXPERTS
    pass_cnt = (blocks_e + PASS_SUBS - 1) // PASS_SUBS
    pass_end = jnp.cumsum(pass_cnt)
    pass_start = pass_end - pass_cnt
    pid = jnp.arange(n_pass_max, dtype=I32)
    pass_e = jnp.minimum(jnp.searchsorted(pass_end, pid, side="right"), N_EXPERTS - 1).astype(I32)
    local = pid - pass_start[pass_e]
    pass_row0 = (blk_start[pass_e] + local * PASS_SUBS).astype(I32)
    pass_nsub = jnp.clip(blocks_e[pass_e] - local * PASS_SUBS, 0, PASS_SUBS).astype(I32)
    meta = jnp.stack([pass_end[-1], blk_end[-1]]).astype(I32)

    ar = jnp.arange(N_EXPERTS, dtype=I32)
    cand = jnp.concatenate([jnp.where(counts % MOE_BLOCK != 0, blk_end - 1, -1),
                            jnp.where(blk_end[-1] + ar < n_blocks, blk_end[-1] + ar, -1)])
    zero_blocks = cand[jnp.argsort(cand < 0, stable=True)].astype(I32)
    n_zero = jnp.sum(cand >= 0).astype(I32).reshape(1)

    xs = _dispatch(n_zero, zero_blocks, dest, hp, n_blocks * MOE_BLOCK, _pick_tile(n, 1024))
    ys = _experts(meta, pass_e, pass_row0, pass_nsub, xs, w_gate_up[0], b_gate_up[0], w_down[0], b_down[0])
    out = _combine(dest, wt, h1, mod3, l, row(ln2_g[0]), row(ln2_b[0]), ys, _pick_tile(n, 512))
    return out.reshape(bsz, l, d)
```

```python
import math

import jax
import jax.numpy as jnp
from jax import lax
from jax.experimental import pallas as pl
from jax.experimental.pallas import tpu as pltpu

F32 = jnp.float32
BF16 = jnp.bfloat16
U32 = jnp.uint32
I32 = jnp.int32

CM_CHUNK = 128
CM_HEADS = 8
GLA_HEADS = 4
GLA_DK = 128
GLA_DV = 256
GLA_CHUNK = 64
GLA_RANK = 16
GLA_GATE_NORMALIZER = 16.0
N_EXPERTS = 32
TOP_K = 4
MOE_BLOCK = 256
SWIGLU_LIMIT = 7.0
SWIGLU_ALPHA = 1.702
N_MOD = 6
DEEPNORM_ALPHA = 2.0 ** 0.25
LN_EPS = 1e-5
RMS_EPS = 1e-6

LANES = 128
SUBLANES = 8
VMEM_LIMIT = 56 * 1024 * 1024
ROW_CHUNK = 256


def _cparams(n_axes, vmem=VMEM_LIMIT):
    return pltpu.CompilerParams(dimension_semantics=("arbitrary",) * n_axes,
                                vmem_limit_bytes=vmem)


def _layer_norm(t, g, b):
    mu = jnp.mean(t, axis=-1, keepdims=True)
    d = t - mu
    var = jnp.mean(d * d, axis=-1, keepdims=True)
    return d * lax.rsqrt(var + LN_EPS) * g + b


def _gelu(t):
    return 0.5 * t * (1.0 + lax.erf(t * (1.0 / math.sqrt(2.0))))


def _silu(t):
    return t * jax.nn.sigmoid(t)


def _ones_where(mask, dtype):
    return jnp.where(mask, 1.0, 0.0).astype(dtype)


def _dot(a, b):
    return jnp.dot(a, b, preferred_element_type=F32)


def _dot_nt(a, b):
    return lax.dot_general(a, b, (((1,), (1,)), ((), ())), preferred_element_type=F32)


def _dot_tn(a, b):
    return lax.dot_general(a, b, (((0,), (0,)), ((), ())), preferred_element_type=F32)


def _pack_bf16_pair(lo, hi):
    lo_b = lax.bitcast_convert_type(lo.astype(BF16).astype(F32), U32)
    hi_b = lax.bitcast_convert_type(hi.astype(BF16).astype(F32), U32)
    return hi_b | (lo_b >> 16)


def _unpack_pair_f32(p):
    lo = lax.bitcast_convert_type(p << 16, F32)
    hi = lax.bitcast_convert_type(p & jnp.uint32(0xFFFF0000), F32)
    return lo, hi


def _ada_body(c_ref, w_ref, b_ref, o_ref):
    a = _silu(c_ref[...]).astype(BF16)
    o_ref[...] = _dot(a, w_ref[...].astype(BF16)) + b_ref[...]


def _ada(cc, w, b):
    rows, d = cc.shape
    n = w.shape[1]
    tn = 1024
    return pl.pallas_call(
        _ada_body,
        grid=(n // tn,),
        in_specs=[pl.BlockSpec((rows, d), lambda j: (0, 0)),
                  pl.BlockSpec((d, tn), lambda j: (0, j)),
                  pl.BlockSpec((1, tn), lambda j: (0, j))],
        out_specs=pl.BlockSpec((rows, tn), lambda j: (0, j)),
        out_shape=jax.ShapeDtypeStruct((rows, n), F32),
        compiler_params=_cparams(1),
        name="ada",
    )(cc, w, b)


def _inproj_body(x_ref, g_ref, b_ref, sh_ref, sc_ref, w_ref, wlr_ref, o_ref, olr_ref, hm_ref):
    @pl.when(pl.program_id(1) == 0)
    def _():
        h = _layer_norm(x_ref[...], g_ref[...], b_ref[...])
        hm = (h * (1.0 + sc_ref[...]) + sh_ref[...]).astype(BF16)
        hm_ref[...] = hm
        olr_ref[...] = _dot(hm, wlr_ref[...])

    o_ref[...] = _dot(hm_ref[...], w_ref[...])


def _inproj(x2, mod3, mod_row, ln_g, ln_b, w_main, w_lr, col0, ncols, tm, tn):
    r, d = x2.shape
    return pl.pallas_call(
        _inproj_body,
        grid=(r // tm, ncols),
        in_specs=[pl.BlockSpec((tm, d), lambda i, j: (i, 0)),
                  pl.BlockSpec((1, d), lambda i, j: (0, 0)),
                  pl.BlockSpec((1, d), lambda i, j: (0, 0)),
                  pl.BlockSpec((None, 1, d), lambda i, j: (mod_row(i), 0, 0)),
                  pl.BlockSpec((None, 1, d), lambda i, j: (mod_row(i), 0, 1)),
                  pl.BlockSpec((d, tn), lambda i, j: (0, col0 + j)),
                  pl.BlockSpec((d, LANES), lambda i, j: (0, 0))],
        out_specs=[pl.BlockSpec((tm, tn), lambda i, j: (i, j)),
                   pl.BlockSpec((tm, LANES), lambda i, j: (i, 0))],
        out_shape=[jax.ShapeDtypeStruct((r, ncols * tn), F32),
                   jax.ShapeDtypeStruct((r, LANES), F32)],
        scratch_shapes=[pltpu.VMEM((tm, d), BF16)],
        compiler_params=_cparams(2),
        name="inproj",
    )(x2, ln_g, ln_b, mod3, mod3, w_main, w_lr)


def _inproj_cm_body(x_ref, g_ref, b_ref, sh_ref, sc_ref, wuv_ref, wr_ref, wlr_ref, ng_ref, nb_ref, ws_ref, bs_ref,
                    cm_ref, p_ref, lr_ref):
    tm = x_ref.shape[0]
    half = wuv_ref.shape[1] // 2
    hd = CM_CHUNK
    for r in range(0, tm, ROW_CHUNK):
        rows = slice(r, r + ROW_CHUNK)
        h = _layer_norm(x_ref[rows, :], g_ref[...], b_ref[...])
        hm = (h * (1.0 + sc_ref[...]) + sh_ref[...]).astype(BF16)
        uv = _dot(hm, wuv_ref[...])
        u = _gelu(uv[:, :half])
        vb = _layer_norm(_gelu(uv[:, half:]), ng_ref[...], nb_ref[...]).astype(BF16)
        for c in range(ROW_CHUNK // CM_CHUNK):
            crow = slice(c * CM_CHUNK, (c + 1) * CM_CHUNK)
            orow = slice(r + c * CM_CHUNK, r + (c + 1) * CM_CHUNK)
            for hh in range(CM_HEADS):
                cols = slice(hh * hd, (hh + 1) * hd)
                s = _dot(ws_ref[hh], vb[crow, cols]) + bs_ref[:, cols]
                cm_ref[orow, cols] = (u[crow, cols] * s).astype(BF16)
        p_ref[rows, :] = _dot(hm, wr_ref[...])
        lr_ref[rows, :] = _dot(hm, wlr_ref[...])


def _inproj_cm(x2, mod3, rows_per_batch, ln_g, ln_b, w_uv, w_rest, w_lr, ng, nb, ws, bs, tm):
    r, d = x2.shape
    n_uv, n_rest = w_uv.shape[1], w_rest.shape[1]
    row = lambda i: (i * tm) // rows_per_batch

    def const(shape):
        return pl.BlockSpec(shape, lambda i: (0,) * len(shape), pipeline_mode=pl.Buffered(1))

    return pl.pallas_call(
        _inproj_cm_body,
        grid=(r // tm,),
        in_specs=[pl.BlockSpec((tm, d), lambda i: (i, 0)),
                  const((1, d)), const((1, d)),
                  pl.BlockSpec((None, 1, d), lambda i: (row(i), 0, 0)),
                  pl.BlockSpec((None, 1, d), lambda i: (row(i), 0, 1)),
                  const((d, n_uv)), const((d, n_rest)), const((d, LANES)),
                  const((1, n_uv // 2)), const((1, n_uv // 2)),
                  const((CM_HEADS, CM_CHUNK, CM_CHUNK)), const((CM_CHUNK, n_uv // 2))],
        out_specs=[pl.BlockSpec((tm, n_uv // 2), lambda i: (i, 0)),
                   pl.BlockSpec((tm, n_rest), lambda i: (i, 0)),
                   pl.BlockSpec((tm, LANES), lambda i: (i, 0))],
        out_shape=[jax.ShapeDtypeStruct((r, n_uv // 2), BF16),
                   jax.ShapeDtypeStruct((r, n_rest), F32),
                   jax.ShapeDtypeStruct((r, LANES), F32)],
        compiler_params=_cparams(1),
        name="inproj_cm",
    )(x2, ln_g, ln_b, mod3, mod3, w_uv, w_rest, w_lr, ng, nb, ws, bs)


GLA_TILE = 256
GLA_TILES_PER_ITER = 8


def _gla_tile(q, k, v, lr, wg, bg, st_ref, forward, need_o):
    t = k.shape[0]
    n_chunks = t // GLA_CHUNK
    z = _dot(lr.astype(BF16), wg) + bg
    g = jax.nn.log_sigmoid(z) * (1.0 / GLA_GATE_NORMALIZER)
    r_id = lax.broadcasted_iota(I32, (t, t), 0)
    c_id = lax.broadcasted_iota(I32, (t, t), 1)
    shift = GLA_CHUNK.bit_length() - 1
    same = (r_id >> shift) == (c_id >> shift)
    lower = same & (c_id <= r_id)
    tri = _ones_where(lower, BF16)
    g_hi = g.astype(BF16)
    g_lo = (g - g_hi.astype(F32)).astype(BF16)
    csum2 = _dot(tri, jnp.concatenate([g_hi, g_lo], axis=1))
    csum = csum2[:, :GLA_DK] + csum2[:, GLA_DK:]
    g3 = g.reshape(n_chunks, GLA_CHUNK, GLA_DK)
    tot = jnp.broadcast_to(jnp.sum(g3, axis=1, keepdims=True), g3.shape).reshape(t, GLA_DK)
    bcum = csum if forward else tot - csum + g
    kd = (k * jnp.exp(tot - bcum)).astype(BF16)
    decay = jnp.exp(tot)
    vb = v.astype(BF16)
    row_chunk = lax.broadcasted_iota(I32, (t, GLA_DK), 0) >> shift
    kd_blocks = jnp.concatenate([jnp.where(row_chunk == c, kd, jnp.zeros_like(kd)) for c in range(n_chunks)], axis=1)
    u_all = _dot_tn(vb, kd_blocks)
    o = None
    if need_o:
        qe = ((q * (GLA_DK ** -0.5)) * jnp.exp(bcum)).astype(BF16)
        ke = (k * jnp.exp(-bcum)).astype(BF16)
        att = _dot_nt(qe, ke)
        mask = lower if forward else same & (c_id >= r_id)
        att = jnp.where(mask, att, 0.0).astype(BF16)
        o = _dot(att, vb)
    outs = [None] * n_chunks
    order = range(n_chunks) if forward else range(n_chunks - 1, -1, -1)
    for c in order:
        rows = slice(c * GLA_CHUNK, (c + 1) * GLA_CHUNK)
        s_t = st_ref[...]
        if need_o:
            outs[c] = o[rows] + _dot_nt(qe[rows], s_t.astype(BF16))
        u_t = u_all[:, c * GLA_DK:(c + 1) * GLA_DK]
        st_ref[...] = s_t * decay[c * GLA_CHUNK:c * GLA_CHUNK + 1, :] + u_t
    if need_o:
        return jnp.concatenate(outs, axis=0)
    return None


def _gla_body(q_ref, k_ref, v_ref, go_ref, lr_ref, kc_ref, vc_ref, lrc_ref,
              wgf_ref, bgf_ref, wgb_ref, bgb_ref, ng_ref, out_ref, o_scr, sf_ref, sb_ref):
    t = GLA_TILE
    n_x = q_ref.shape[0] // t
    n_c = kc_ref.shape[0] // t
    half = n_x // 2
    sf_ref[...] = jnp.zeros_like(sf_ref)
    sb_ref[...] = jnp.zeros_like(sb_ref)
    wgf, bgf, wgb, bgb = wgf_ref[...], bgf_ref[...], wgb_ref[...], bgb_ref[...]

    for i in range(n_c):
        rf = slice(i * t, (i + 1) * t)
        rb = slice((n_c - 1 - i) * t, (n_c - i) * t)
        _gla_tile(None, kc_ref[rf], vc_ref[rf], lrc_ref[rf], wgf, bgf, sf_ref, True, False)
        _gla_tile(None, kc_ref[rb], vc_ref[rb], lrc_ref[rb], wgb, bgb, sb_ref, False, False)

    def tile_out(i, forward):
        rows = pl.ds(pl.multiple_of(i * t, t), t)
        if forward:
            return rows, _gla_tile(q_ref[rows], k_ref[rows], v_ref[rows], lr_ref[rows],
                                   wgf, bgf, sf_ref, True, True)
        return rows, _gla_tile(q_ref[rows], k_ref[rows], v_ref[rows], lr_ref[rows],
                               wgb, bgb, sb_ref, False, True)

    def finish(rows, o):
        o = o + o_scr[rows]
        ms = jnp.mean(o * o, axis=-1, keepdims=True)
        on = o * lax.rsqrt(ms + RMS_EPS) * ng_ref[...]
        out_ref[rows] = (on * _silu(go_ref[rows])).astype(BF16)

    def keep(rows, o):
        o_scr[rows] = o

    u = math.gcd(GLA_TILES_PER_ITER, half)

    def make_step(sink):
        def step(it, carry):
            for w in range(u):
                i = it * u + w
                sink(*tile_out(i, True))
                sink(*tile_out(n_x - 1 - i, False))
            return carry
        return step

    lax.fori_loop(0, half // u, make_step(keep), 0)
    lax.fori_loop(half // u, n_x // u, make_step(finish), 0)


def _gla(p3, lr3, pc3, lrc3, wgf, bgf, wgb, bgb, ng):
    bsz, l, _ = p3.shape
    lc = pc3.shape[1]
    dk, dv = GLA_DK, GLA_DV
    kw = GLA_HEADS * dk
    q0, k0 = 0, kw // dk
    v0, go0 = 2 * kw // dv, (2 * kw + GLA_HEADS * dv) // dv
    kc0, vc0 = k0, v0
    return pl.pallas_call(
        _gla_body,
        grid=(bsz, GLA_HEADS),
        in_specs=[pl.BlockSpec((None, l, dk), lambda b, h: (b, 0, q0 + h)),
                  pl.BlockSpec((None, l, dk), lambda b, h: (b, 0, k0 + h)),
                  pl.BlockSpec((None, l, dv), lambda b, h: (b, 0, v0 + h)),
                  pl.BlockSpec((None, l, dv), lambda b, h: (b, 0, go0 + h)),
                  pl.BlockSpec((None, l, LANES), lambda b, h: (b, 0, 0)),
                  pl.BlockSpec((None, lc, dk), lambda b, h: (b, 0, kc0 + h)),
                  pl.BlockSpec((None, lc, dv), lambda b, h: (b, 0, vc0 + h)),
                  pl.BlockSpec((None, lc, LANES), lambda b, h: (b, 0, 0)),
                  pl.BlockSpec((LANES, dk), lambda b, h: (0, h)),
                  pl.BlockSpec((1, dk), lambda b, h: (0, h)),
                  pl.BlockSpec((LANES, dk), lambda b, h: (0, h)),
                  pl.BlockSpec((1, dk), lambda b, h: (0, h)),
                  pl.BlockSpec((1, dv), lambda b, h: (0, 0))],
        out_specs=pl.BlockSpec((None, l, dv), lambda b, h: (b, 0, h)),
        out_shape=jax.ShapeDtypeStruct((bsz, l, GLA_HEADS * dv), BF16),
        scratch_shapes=[pltpu.VMEM((l, dv), F32),
                        pltpu.VMEM((dv, dk), F32),
                        pltpu.VMEM((dv, dk), F32)],
        compiler_params=_cparams(2),
        name="gla",
    )(p3, p3, p3, p3, lr3, pc3, pc3, lrc3, wgf, bgf, wgb, bgb, ng)


def _post_body(cm_ref, gla_ref, x_ref, lng_ref, lnb_ref, g1_ref, sh2_ref, sc2_ref,
               wo_ref, l1g_ref, l1b_ref, wr_ref, br_ref, h1_ref, hp_ref, idx_ref, wt_ref, rank_ref, cnt_ref,
               lg_ref, carry_ref):
    half = cm_ref.shape[1]
    d2 = x_ref.shape[1] // 2
    for r in range(0, x_ref.shape[0], ROW_CHUNK):
        rows = slice(r, r + ROW_CHUNK)
        y = _dot(cm_ref[rows, :], wo_ref[:half, :]) + _dot(gla_ref[rows, :], wo_ref[half:, :])
        hx = _layer_norm(x_ref[rows, :], lng_ref[...], lnb_ref[...])
        h1 = _layer_norm(DEEPNORM_ALPHA * hx + g1_ref[...] * y, l1g_ref[...], l1b_ref[...])
        h1_ref[rows, :] = h1
        hm = h1 * (1.0 + sc2_ref[...]) + sh2_ref[...]
        hp_ref[rows, :] = _pack_bf16_pair(hm[:, :d2], hm[:, d2:])
        lg_ref[rows, :] = _dot(hm.astype(BF16), wr_ref[...]) + br_ref[...]
    _route_body(lg_ref, idx_ref, wt_ref, rank_ref, cnt_ref, carry_ref)


def _post_attn(cm, gla, x2, mod3, rows_per_batch, ln_g, ln_b, w_out, l1g, l1b, w_r, b_r, tm):
    r, d = x2.shape
    half = d // 2
    row = lambda i: (i * tm) // rows_per_batch
    full = lambda shape: pl.BlockSpec(shape, lambda i: (0,) * len(shape))
    return pl.pallas_call(
        _post_body,
        grid=(r // tm,),
        in_specs=[pl.BlockSpec((tm, half), lambda i: (i, 0)),
                  pl.BlockSpec((tm, half), lambda i: (i, 0)),
                  pl.BlockSpec((tm, d), lambda i: (i, 0)),
                  full((1, d)), full((1, d)),
                  pl.BlockSpec((None, 1, d), lambda i: (row(i), 0, 2)),
                  pl.BlockSpec((None, 1, d), lambda i: (row(i), 0, 3)),
                  pl.BlockSpec((None, 1, d), lambda i: (row(i), 0, 4)),
                  full((d, d)), full((1, d)), full((1, d)),
                  full((d, LANES)), full((1, LANES))],
        out_specs=[pl.BlockSpec((tm, d), lambda i: (i, 0)),
                   pl.BlockSpec((tm, half), lambda i: (i, 0)),
                   pl.BlockSpec((SUBLANES, tm), lambda i: (0, i)),
                   pl.BlockSpec((tm, LANES), lambda i: (i, 0)),
                   pl.BlockSpec((SUBLANES, tm), lambda i: (0, i)),
                   pl.BlockSpec((8, LANES), lambda i: (0, 0))],
        out_shape=[jax.ShapeDtypeStruct((r, d), F32),
                   jax.ShapeDtypeStruct((r, half), U32),
                   jax.ShapeDtypeStruct((SUBLANES, r), I32),
                   jax.ShapeDtypeStruct((r, LANES), F32),
                   jax.ShapeDtypeStruct((SUBLANES, r), I32),
                   jax.ShapeDtypeStruct((8, LANES), F32)],
        scratch_shapes=[pltpu.VMEM((tm, LANES), F32), pltpu.VMEM((8, LANES), F32)],
        compiler_params=_cparams(1),
        name="post_attn",
    )(cm, gla, x2, ln_g, ln_b, mod3, mod3, mod3, w_out, l1g, l1b, w_r, b_r)


def _route_body(lg_ref, idx_ref, wt_ref, rank_ref, cnt_ref, carry_ref):
    i = pl.program_id(0)
    t = lg_ref.shape[0]

    @pl.when(i == 0)
    def _():
        carry_ref[...] = jnp.zeros_like(carry_ref)

    lane = lax.broadcasted_iota(I32, (t, LANES), 1)
    lane_f = lane.astype(F32)
    neg = jnp.float32(-jnp.inf)
    l = jnp.where(lane < N_EXPERTS, lg_ref[...], neg)
    tops, onehots, idxs = [], [], []
    for _ in range(TOP_K):
        m = jnp.max(l, axis=-1, keepdims=True)
        idx = jnp.min(jnp.where(l == m, lane_f, float(LANES)), axis=-1, keepdims=True).astype(I32)
        oh = lane == idx
        l = jnp.where(oh, neg, l)
        tops.append(m)
        idxs.append(idx)
        onehots.append(oh)
    exps = [jnp.exp(m - tops[0]) for m in tops]
    denom = exps[0] + exps[1] + exps[2] + exps[3]
    sel = _ones_where(onehots[0] | onehots[1] | onehots[2] | onehots[3], F32)
    r_id = lax.broadcasted_iota(I32, (t, t), 0)
    c_id = lax.broadcasted_iota(I32, (t, t), 1)
    strict = _ones_where(c_id < r_id, BF16)
    before = _dot(strict, sel.astype(BF16)) + carry_ref[0:1, :]
    idx_out = jnp.zeros((t, LANES), I32)
    wt_out = jnp.zeros((t, LANES), F32)
    rank_out = jnp.zeros((t, LANES), I32)
    for k in range(TOP_K):
        rk = jnp.sum(jnp.where(onehots[k], before, 0.0), axis=-1, keepdims=True).astype(I32)
        idx_out = jnp.where(lane == k, idxs[k], idx_out)
        wt_out = jnp.where(lane == k, exps[k] / denom, wt_out)
        rank_out = jnp.where(lane == k, rk, rank_out)
    idx_ref[...] = jnp.transpose(idx_out)[:SUBLANES, :]
    wt_ref[...] = wt_out
    rank_ref[...] = jnp.transpose(rank_out)[:SUBLANES, :]
    total = carry_ref[0:1, :] + jnp.sum(sel, axis=0, keepdims=True)
    carry_ref[...] = jnp.broadcast_to(total, carry_ref.shape)
    cnt_ref[...] = jnp.broadcast_to(total, cnt_ref.shape)


def _route(logits, tm):
    n = logits.shape[0]
    blk = pl.BlockSpec((tm, LANES), lambda i: (i, 0))
    small = pl.BlockSpec((SUBLANES, tm), lambda i: (0, i))
    return pl.pallas_call(
        _route_body,
        grid=(n // tm,),
        in_specs=[blk],
        out_specs=[small, blk, small, pl.BlockSpec((8, LANES), lambda i: (0, 0))],
        out_shape=[jax.ShapeDtypeStruct((SUBLANES, n), I32),
                   jax.ShapeDtypeStruct((n, LANES), F32),
                   jax.ShapeDtypeStruct((SUBLANES, n), I32),
                   jax.ShapeDtypeStruct((8, LANES), F32)],
        scratch_shapes=[pltpu.VMEM((8, LANES), F32)],
        compiler_params=_cparams(1),
        name="route",
    )(logits)


def _dispatch_body(nz_ref, zl_ref, dest_ref, hp_ref, xs_ref, zero_ref, sem, zsem):
    t = hp_ref.shape[0] * hp_ref.shape[1]

    @pl.when(pl.program_id(0) == 0)
    def _():
        zero_ref[...] = jnp.zeros_like(zero_ref)

        def z_copy(b):
            r = pl.multiple_of(zl_ref[b] * MOE_BLOCK, MOE_BLOCK)
            return pltpu.make_async_copy(zero_ref, xs_ref.at[pl.ds(r, MOE_BLOCK)], zsem)

        def start(b, carry):
            z_copy(b).start()
            return carry

        def wait(b, carry):
            z_copy(b).wait()
            return carry

        lax.fori_loop(0, nz_ref[0], start, 0)
        lax.fori_loop(0, nz_ref[0], wait, 0)

    def issue(g, carry):
        for u in range(SUBLANES):
            for k in range(TOP_K):
                d = dest_ref[0, k * t + g * SUBLANES + u]
                pltpu.make_async_copy(hp_ref.at[g, pl.ds(u, 1)], xs_ref.at[pl.ds(d, 1)], sem).start(priority=k % 2)
        return carry

    lax.fori_loop(0, t // SUBLANES, issue, 0)
    for _ in range(TOP_K):
        pltpu.make_async_copy(xs_ref.at[pl.ds(0, t)], xs_ref.at[pl.ds(0, t)], sem).wait()


def _tile_major(slot, tm):
    n = slot.shape[1]
    return slot.reshape(TOP_K, n // tm, tm).transpose(1, 0, 2).reshape(n // tm, 1, TOP_K * tm)


def _dispatch(n_zero, zero_blocks, dest, hp, n_slots, tm):
    n, w = hp.shape
    dest3 = _tile_major(dest, tm)
    return pl.pallas_call(
        _dispatch_body,
        grid_spec=pltpu.PrefetchScalarGridSpec(
            num_scalar_prefetch=2,
            grid=(n // tm,),
            in_specs=[pl.BlockSpec((None, 1, tm * TOP_K), lambda i, nz, zl: (i, 0, 0), memory_space=pltpu.SMEM),
                      pl.BlockSpec((tm // SUBLANES, SUBLANES, w), lambda i, nz, zl: (i, 0, 0))],
            out_specs=pl.BlockSpec(memory_space=pl.ANY),
            scratch_shapes=[pltpu.VMEM((MOE_BLOCK, w), U32),
                            pltpu.SemaphoreType.DMA(()),
                            pltpu.SemaphoreType.DMA(())]),
        out_shape=jax.ShapeDtypeStruct((n_slots, w), U32),
        compiler_params=_cparams(1),
        name="dispatch",
    )(n_zero, zero_blocks, dest3, hp.reshape(n // SUBLANES, SUBLANES, w))


PASS_SUBS = 9
EXPERT_TN = 256
DOT_ROWS = 1024
W_SLOTS = 3
W_AHEAD = 2


def _expert_body(meta_ref, pe_ref, pr_ref, pn_ref, xs_hbm, wgu_hbm, wdn_hbm, bgu_ref, bdn_ref,
                 y_hbm, xraw, xb, h_ref, wg_buf, wu_buf, wd_buf, ybuf, zbuf,
                 sem_x, sem_y, sem_z, sem_w, sem_d):
    sub = MOE_BLOCK
    tn = EXPERT_TN
    tp = tn // 2
    nj = h_ref.shape[0]
    nc = y_hbm.shape[1] // tp
    de = nj * tn
    d2 = xraw.shape[1]
    p = pl.program_id(0)
    n_pass = meta_ref[0]
    nsub = pn_ref[p]
    row0 = pr_ref[p]

    def x_copy(i, pp):
        r = pl.multiple_of((pr_ref[pp] + i) * sub, sub)
        return pltpu.make_async_copy(xs_hbm.at[pl.ds(r, sub)], xraw.at[pl.ds(i * sub, sub)], sem_x)

    def y_copy(start, size, r0, c, slot):
        r = pl.multiple_of(r0 * sub + start, sub)
        col = pl.multiple_of(c * tp, tp)
        return pltpu.make_async_copy(ybuf.at[slot, pl.ds(start, size), :],
                                     y_hbm.at[pl.ds(r, size), pl.ds(col, tp)], sem_y.at[slot])

    def z_copy(b, c):
        r = pl.multiple_of(b * sub, sub)
        return pltpu.make_async_copy(zbuf, y_hbm.at[pl.ds(r, sub), pl.ds(c * tp, tp)], sem_z)

    def w1_copies(pp, j):
        e = pe_ref[pp]
        slot = j % W_SLOTS
        col = pl.multiple_of(j * tn, tn)
        return (pltpu.make_async_copy(wgu_hbm.at[e, :, pl.ds(col, tn)], wg_buf.at[slot], sem_w.at[slot]),
                pltpu.make_async_copy(wgu_hbm.at[e, :, pl.ds(de + col, tn)], wu_buf.at[slot], sem_w.at[slot]))

    def w2_copy(pp, c):
        slot = c % W_SLOTS
        col = pl.multiple_of(c * tn, tn)
        return pltpu.make_async_copy(wdn_hbm.at[pe_ref[pp], :, pl.ds(col, tn)], wd_buf.at[slot], sem_d.at[slot])

    def start_w1(pp, j):
        for cp in w1_copies(pp, j):
            cp.start()

    def for_subs(count, fn):
        for i in range(PASS_SUBS):
            pl.when(i < count)(lambda i=i: fn(i))

    def for_tail(fn):
        def body(b, carry):
            for c in range(nc):
                fn(b, c)
            return carry
        lax.fori_loop(meta_ref[1], y_hbm.shape[0] // sub, body, 0)

    def for_groups(count, fn):
        pl.when(count == PASS_SUBS)(lambda: fn(0, PASS_SUBS * sub))
        k = 1 << (PASS_SUBS.bit_length() - 1)
        while k:
            start = pl.multiple_of((count & (-2 * k)) * sub, sub)
            pl.when((count != PASS_SUBS) & ((count & k) != 0))(lambda start=start, k=k: fn(start, k * sub))
            k //= 2

    @pl.when(p == 0)
    def _():
        for_subs(nsub, lambda i: x_copy(i, p).start())
        for j in range(W_AHEAD):
            start_w1(p, j)
        zbuf[...] = jnp.zeros_like(zbuf)
        for_tail(lambda b, c: z_copy(b, c).start())

    for_subs(nsub, lambda i: x_copy(i, p).wait())

    def unpack(i):
        rows = slice(i * sub, (i + 1) * sub)
        lo, hi = _unpack_pair_f32(xraw[rows, :])
        xb[rows, :d2] = lo.astype(BF16)
        xb[rows, d2:] = hi.astype(BF16)

    for_subs(nsub, unpack)

    @pl.when(p + 1 < n_pass)
    def _():
        for_subs(pn_ref[p + 1], lambda i: x_copy(i, p + 1).start())

    def first_step(j, carry):
        slot = j % W_SLOTS
        for cp in w1_copies(p, j):
            cp.wait()
        nxt = j + W_AHEAD
        pl.when(nxt < nj)(lambda: start_w1(p, nxt))
        pl.when(nxt >= nj)(lambda: w2_copy(p, nxt - nj).start())

        def group(start, size):
            wg = wg_buf[slot].astype(BF16)
            wu = wu_buf[slot].astype(BF16)
            dr = DOT_ROWS if size % DOT_ROWS == 0 else size
            for r in range(0, size, dr):
                rows = pl.ds(start + r, dr)
                x = xb[rows, :]
                gate = jnp.minimum(_dot(x, wg) + bgu_ref[j], SWIGLU_LIMIT)
                up = jnp.clip(_dot(x, wu) + bgu_ref[nj + j], -SWIGLU_LIMIT, SWIGLU_LIMIT)
                h_ref[j, rows, :] = ((up + 1.0) * gate * jax.nn.sigmoid(SWIGLU_ALPHA * gate)).astype(BF16)

        for_groups(nsub, group)
        return carry

    lax.fori_loop(0, nj, first_step, 0)

    def second_step(c, carry):
        wslot = c % W_SLOTS
        slot = c % 2
        w2_copy(p, c).wait()
        nxt = c + W_AHEAD
        pl.when(nxt < nc)(lambda: w2_copy(p, nxt).start())
        pl.when((nxt >= nc) & (p + 1 < n_pass))(lambda: start_w1(p + 1, nxt - nc))

        @pl.when(c >= 2)
        def _():
            for_groups(nsub, lambda start, size: y_copy(start, size, row0, c - 2, slot).wait())

        @pl.when((c < 2) & (p > 0))
        def _():
            for_groups(pn_ref[p - 1],
                       lambda start, size: y_copy(start, size, pr_ref[p - 1], nc - 2 + c, slot).wait())

        def group(start, size):
            wd = wd_buf[wslot].astype(BF16)
            dr = DOT_ROWS if size % DOT_ROWS == 0 else size
            for r in range(0, size, dr):
                rows = pl.ds(start + r, dr)
                hx = jnp.concatenate([h_ref[j, rows, :] for j in range(nj)], axis=1)
                y = _dot(hx, wd) + bdn_ref[c]
                ybuf[slot, rows, :] = _pack_bf16_pair(y[:, :tp], y[:, tp:])
            y_copy(start, size, row0, c, slot).start()

        for_groups(nsub, group)
        return carry

    lax.fori_loop(0, nc, second_step, 0)

    @pl.when(p == n_pass - 1)
    def _():
        for c in (nc - 2, nc - 1):
            for_groups(nsub, lambda start, size, c=c: y_copy(start, size, row0, c, c % 2).wait())
        for_tail(lambda b, cc: z_copy(b, cc).wait())


def _experts(meta, pass_e, pass_row0, pass_nsub, xs, w_gu, b_gu, w_dn, b_dn):
    n_slots, d2 = xs.shape
    d = 2 * d2
    n_e, de = w_dn.shape[0], w_dn.shape[1]
    tn = EXPERT_TN
    nj = de // tn
    nc = d // tn
    assert W_AHEAD < W_SLOTS and W_AHEAD <= min(nj, nc) and nc % 2 == 0
    rmax = PASS_SUBS * MOE_BLOCK
    return pl.pallas_call(
        _expert_body,
        grid_spec=pltpu.PrefetchScalarGridSpec(
            num_scalar_prefetch=4,
            grid=(meta[0],),
            in_specs=[
                pl.BlockSpec(memory_space=pl.ANY),
                pl.BlockSpec(memory_space=pl.ANY),
                pl.BlockSpec(memory_space=pl.ANY),
                pl.BlockSpec((None, 2 * nj, 1, tn), lambda p, m, e, r, n: (e[p], 0, 0, 0)),
                pl.BlockSpec((None, nc, 1, tn), lambda p, m, e, r, n: (e[p], 0, 0, 0)),
            ],
            out_specs=pl.BlockSpec(memory_space=pl.ANY),
            scratch_shapes=[pltpu.VMEM((rmax, d2), U32),
                            pltpu.VMEM((rmax, d), BF16),
                            pltpu.VMEM((nj, rmax, tn), BF16),
                            pltpu.VMEM((W_SLOTS, d, tn), w_gu.dtype),
                            pltpu.VMEM((W_SLOTS, d, tn), w_gu.dtype),
                            pltpu.VMEM((W_SLOTS, de, tn), w_dn.dtype),
                            pltpu.VMEM((2, rmax, tn // 2), U32),
                            pltpu.VMEM((MOE_BLOCK, tn // 2), U32),
                            pltpu.SemaphoreType.DMA(()),
                            pltpu.SemaphoreType.DMA((2,)),
                            pltpu.SemaphoreType.DMA(()),
                            pltpu.SemaphoreType.DMA((W_SLOTS,)),
                            pltpu.SemaphoreType.DMA((W_SLOTS,))]),
        out_shape=jax.ShapeDtypeStruct((n_slots, d2), U32),
        compiler_params=_cparams(1),
        name="experts",
    )(meta, pass_e, pass_row0, pass_nsub, xs, w_gu, w_dn,
      b_gu.reshape(n_e, 2 * nj, 1, tn), b_dn.reshape(n_e, nc, 1, tn))


COMBINE_CHUNK = 128


def _combine_body(dcur_ref, dnxt_ref, wt_ref, h1_ref, g2_ref, l2g_ref, l2b_ref, y_ref, out_ref, buf_ref, sem):
    i = pl.program_id(0)
    n_tiles = pl.num_programs(0)
    t = h1_ref.shape[0]
    tp = EXPERT_TN // 2
    slot = i % 2

    def issue(dest_ref, sl, g):
        for u in range(SUBLANES):
            for k in range(TOP_K):
                d = dest_ref[0, k * t + g * SUBLANES + u]
                pltpu.make_async_copy(y_ref.at[pl.ds(d, 1)], buf_ref.at[sl, k, g, pl.ds(u, 1)],
                                      sem.at[sl]).start(priority=k % 2)

    gpc = COMBINE_CHUNK // SUBLANES

    def reduce_rows(j):
        rows = pl.ds(pl.multiple_of(j * COMBINE_CHUNK, COMBINE_CHUNK), COMBINE_CHUNK)
        groups = pl.ds(pl.multiple_of(j * gpc, gpc), gpc)
        wt = wt_ref[rows, :]
        f_lo, f_hi = None, None
        for k in range(TOP_K):
            lo, hi = _unpack_pair_f32(buf_ref[slot, k, groups].reshape(COMBINE_CHUNK, buf_ref.shape[-1]))
            w = wt[:, k:k + 1]
            f_lo = lo * w if f_lo is None else f_lo + lo * w
            f_hi = hi * w if f_hi is None else f_hi + hi * w
        parts = []
        for c in range(f_lo.shape[1] // tp):
            parts += [f_lo[:, c * tp:(c + 1) * tp], f_hi[:, c * tp:(c + 1) * tp]]
        f = jnp.concatenate(parts, axis=1)
        out_ref[rows, :] = _layer_norm(DEEPNORM_ALPHA * h1_ref[rows, :] + g2_ref[...] * f,
                                       l2g_ref[...], l2b_ref[...])

    def first_gather(g, carry):
        issue(dcur_ref, 0, g)
        return carry

    @pl.when(i == 0)
    def _():
        lax.fori_loop(0, t // SUBLANES, first_gather, 0)

    for k in range(TOP_K):
        pltpu.make_async_copy(buf_ref.at[1 - slot, k], buf_ref.at[slot, k], sem.at[slot]).wait()

    def both(j, carry):
        reduce_rows(j)
        for w in range(gpc):
            issue(dnxt_ref, 1 - slot, j * gpc + w)
        return carry

    def only_reduce(j, carry):
        reduce_rows(j)
        return carry

    @pl.when(i + 1 < n_tiles)
    def _():
        lax.fori_loop(0, t // COMBINE_CHUNK, both, 0)

    @pl.when(i + 1 == n_tiles)
    def _():
        lax.fori_loop(0, t // COMBINE_CHUNK, only_reduce, 0)


def _combine(dest, wt, h1, mod3, rows_per_batch, l2g, l2b, y, tm):
    n, d = h1.shape
    n_tiles = n // tm
    dest3 = _tile_major(dest, tm)
    row = lambda i: (i * tm) // rows_per_batch
    return pl.pallas_call(
        _combine_body,
        grid=(n_tiles,),
        in_specs=[pl.BlockSpec((None, 1, tm * TOP_K), lambda i: (i, 0, 0), memory_space=pltpu.SMEM),
                  pl.BlockSpec((None, 1, tm * TOP_K), lambda i: (jnp.minimum(i + 1, n_tiles - 1), 0, 0),
                               memory_space=pltpu.SMEM),
                  pl.BlockSpec((tm, LANES), lambda i: (i, 0)),
                  pl.BlockSpec((tm, d), lambda i: (i, 0)),
                  pl.BlockSpec((None, 1, d), lambda i: (row(i), 0, 5)),
                  pl.BlockSpec((1, d), lambda i: (0, 0)),
                  pl.BlockSpec((1, d), lambda i: (0, 0)),
                  pl.BlockSpec(memory_space=pl.ANY)],
        out_specs=pl.BlockSpec((tm, d), lambda i: (i, 0)),
        out_shape=jax.ShapeDtypeStruct((n, d), F32),
        scratch_shapes=[pltpu.VMEM((2, TOP_K, tm // SUBLANES, SUBLANES, y.shape[1]), U32),
                        pltpu.SemaphoreType.DMA((2,))],
        compiler_params=_cparams(1),
        name="combine",
    )(dest3, dest3, wt, h1, mod3, l2g, l2b, y)


def _pick_tile(n, pref):
    t = pref
    while n % t:
        t //= 2
    return t


def kernel(x, c, ctx, c_ctx, ln_in_g, ln_in_b, w_ada, b_ada, w_in, cm_norm_g, cm_norm_b, cm_w_s, cm_b_s,
           gla_w_gk_f, gla_b_gk_f, gla_w_gk_b, gla_b_gk_b, gla_norm_g, w_out, ln1_g, ln1_b,
           w_router, b_router, w_gate_up, b_gate_up, w_down, b_down, ln2_g, ln2_b):
    bsz, l, d = x.shape
    lc = ctx.shape[1]
    n, nc = bsz * l, bsz * lc
    assert w_ada.shape[0] == 1, "single-layer configuration"
    assert bsz + 1 <= 8 and l % (2 * GLA_TILE) == 0 and lc % GLA_TILE == 0
    row = lambda v: v.reshape(1, -1)

    cc = jnp.concatenate([c, c_ctx[None, :], jnp.zeros((8 - bsz - 1, d), F32)], axis=0)
    mod3 = _ada(cc, w_ada[0], row(b_ada[0])).reshape(8, 1, N_MOD * d)

    n_uv = 2 * CM_HEADS * CM_CHUNK
    n_main = n_uv + 2 * GLA_HEADS * GLA_DK + 2 * GLA_HEADS * GLA_DV
    w_uv = w_in[0][:, :n_uv].astype(BF16)
    w_rest = w_in[0][:, n_uv:n_main].astype(BF16)
    w_lr = jnp.pad(w_in[0][:, n_main:].astype(BF16), ((0, 0), (0, LANES - 2 * GLA_RANK)))
    x2 = x.reshape(n, d)
    tm_x = _pick_tile(l, 512)
    bs_tile = jnp.repeat(cm_b_s[0].T, CM_CHUNK, axis=1)
    cm, p, lr = _inproj_cm(x2, mod3, l, row(ln_in_g), row(ln_in_b), w_uv, w_rest, w_lr,
                           row(cm_norm_g[0]), row(cm_norm_b[0]), cm_w_s[0].astype(BF16), bs_tile, tm_x)
    tm_c = _pick_tile(nc, 512)
    pc, lrc = _inproj(ctx.reshape(nc, d), mod3, lambda i: bsz, row(ln_in_g), row(ln_in_b),
                      w_rest, w_lr, 0, 2, tm_c, 1024)

    kw = GLA_HEADS * GLA_DK
    wgf = jnp.zeros((LANES, kw), BF16).at[:GLA_RANK].set(gla_w_gk_f[0].astype(BF16))
    wgb = jnp.zeros((LANES, kw), BF16).at[GLA_RANK:2 * GLA_RANK].set(gla_w_gk_b[0].astype(BF16))
    gla = _gla(p.reshape(bsz, l, -1), lr.reshape(bsz, l, LANES), pc.reshape(bsz, lc, -1),
               lrc.reshape(bsz, lc, LANES), wgf, row(gla_b_gk_f[0]), wgb, row(gla_b_gk_b[0]),
               row(gla_norm_g[0])).reshape(n, -1)

    w_r = jnp.pad(w_router[0], ((0, 0), (0, LANES - N_EXPERTS))).astype(BF16)
    b_r = jnp.pad(b_router[0], (0, LANES - N_EXPERTS)).reshape(1, LANES)
    h1, hp, idx, wt, rank, cnt = _post_attn(cm, gla, x2, mod3, l, row(ln_in_g), row(ln_in_b), w_out[0].astype(BF16),
                                row(ln1_g[0]), row(ln1_b[0]), w_r, b_r, tm_x)

    counts = cnt[0, :N_EXPERTS].astype(I32)
    n_blocks = (n * TOP_K + N_EXPERTS * (MOE_BLOCK - 1)) // MOE_BLOCK
    blocks_e = (counts + MOE_BLOCK - 1) // MOE_BLOCK
    blk_end = jnp.cumsum(blocks_e)
    blk_start = blk_end - blocks_e
    dest = rank[:TOP_K]
    for e in range(N_EXPERTS):
        dest = dest + jnp.where(idx[:TOP_K] == e, blk_start[e] * MOE_BLOCK, 0)
    n_pass_max = n_blocks // PASS_SUBS + N_EXPERTS
    pass_cnt = (blocks_e + PASS_SUBS - 1) // PASS_SUBS
    pass_end = jnp.cumsum(pass_cnt)
    pass_start = pass_end - pass_cnt
    pid = jnp.arange(n_pass_max, dtype=I32)
    pass_e = jnp.minimum(jnp.searchsorted(pass_end, pid, side="right"), N_EXPERTS - 1).astype(I32)
    local = pid - pass_start[pass_e]
    pass_row0 = (blk_start[pass_e] + local * PASS_SUBS).astype(I32)
    pass_nsub = jnp.clip(blocks_e[pass_e] - local * PASS_SUBS, 0, PASS_SUBS).astype(I32)
    meta = jnp.stack([pass_end[-1], blk_end[-1]]).astype(I32)

    ar = jnp.arange(N_EXPERTS, dtype=I32)
    cand = jnp.concatenate([jnp.where(counts % MOE_BLOCK != 0, blk_end - 1, -1),
                            jnp.where(blk_end[-1] + ar < n_blocks, blk_end[-1] + ar, -1)])
    zero_blocks = cand[jnp.argsort(cand < 0, stable=True)].astype(I32)
    n_zero = jnp.sum(cand >= 0).astype(I32).reshape(1)

    xs = _dispatch(n_zero, zero_blocks, dest, hp, n_blocks * MOE_BLOCK, _pick_tile(n, 512))
    ys = _experts(meta, pass_e, pass_row0, pass_nsub, xs, w_gate_up[0], b_gate_up[0], w_down[0], b_down[0])
    out = _combine(dest, wt, h1, mod3, l, row(ln2_g[0]), row(ln2_b[0]), ys, _pick_tile(n, 512))
    return out.reshape(bsz, l, d)
```
